```python
import math
import jax, jax.numpy as jnp
from jax import lax
import numpy as np

D_MODEL = 1024
BATCH = 8
SEQ = 4096
DEPTH = 4

N_MIXERS = 4
NORM_EPS = 1e-6
NEG_INF = -1e30
D_FF = 2816

RET_HEADS = 4
RET_DK = D_MODEL // RET_HEADS
RET_DV = 2 * RET_DK
RET_CHUNK = 128
ROPE_BASE = 10000.0

NSA_HEADS = 16
NSA_GROUPS = 4
NSA_HPG = NSA_HEADS // NSA_GROUPS
NSA_DH = D_MODEL // NSA_HEADS
NSA_CMP_LEN = 32
NSA_CMP_STRIDE = 16
NSA_CMP_HID = 4 * NSA_DH
NSA_SEL_LEN = 64
NSA_N_SEL = 16
NSA_WINDOW = 512
NSA_Q_BLOCK = 32
NSA_FORCE_BONUS = 1e4

SSD_D_INNER = 2 * D_MODEL
SSD_HEADDIM = 64
SSD_HEADS = SSD_D_INNER // SSD_HEADDIM
SSD_GROUPS = 4
SSD_HPG = SSD_HEADS // SSD_GROUPS
SSD_STATE = 128
SSD_CONV = 4
SSD_CONV_DIM = SSD_D_INNER + 2 * SSD_GROUPS * SSD_STATE
SSD_CHUNK = 256

DIL_PATTERN = ((128, 1), (512, 4), (2048, 16))
DIL_HEADS = 8
DIL_DH = D_MODEL // DIL_HEADS
DIL_Q_BLOCK = 128

kernel_name = "hybrid_interleaved_retnet_nsa_ssd_dilated"


def _rmsnorm(x, g):
    xf = x.astype(jnp.float32)
    y = xf * lax.rsqrt(jnp.mean(xf * xf, axis=-1, keepdims=True) + NORM_EPS)
    return (y * g.astype(jnp.float32)).astype(x.dtype)


def _swiglu(u, w_in, w_out):
    a, b = jnp.split(u @ w_in, 2, axis=-1)
    return (jax.nn.silu(a) * b) @ w_out


def _masked_softmax(s, mask):
    s = jnp.where(mask, s.astype(jnp.float32), NEG_INF)
    m = jnp.max(s, axis=-1, keepdims=True)
    e = jnp.where(mask, jnp.exp(s - m), 0.0)
    den = jnp.maximum(jnp.sum(e, axis=-1, keepdims=True), 1e-30)
    return e / den, (m + jnp.log(den))[..., 0]


def _rotary(t, pos):
    half = t.shape[-1] // 2
    inv = ROPE_BASE ** (-jnp.arange(half, dtype=jnp.float32) / half)
    ang = pos.astype(jnp.float32)[:, None] * inv[None, :]
    cos = jnp.cos(ang)[None, :, None, :]
    sin = jnp.sin(ang)[None, :, None, :]
    tf = t.astype(jnp.float32)
    t1, t2 = tf[..., :half], tf[..., half:]
    return jnp.concatenate([t1 * cos - t2 * sin, t1 * sin + t2 * cos], axis=-1).astype(t.dtype)


def _retention(u, w_in, gn_gain, w_out):
    b, s, _ = u.shape
    hk, hv = RET_HEADS * RET_DK, RET_HEADS * RET_DV
    q, k, v, g = jnp.split(u @ w_in, [hk, 2 * hk, 2 * hk + hv], axis=-1)
    pos = jnp.arange(s)
    q = _rotary(q.reshape(b, s, RET_HEADS, RET_DK), pos)
    k = _rotary(k.reshape(b, s, RET_HEADS, RET_DK), pos) * (RET_DK ** -0.5)
    v = v.reshape(b, s, RET_HEADS, RET_DV)
    n_ch = s // RET_CHUNK

    def to_chunks(t):
        return t.astype(jnp.float32).reshape(b, n_ch, RET_CHUNK, RET_HEADS, -1).transpose(1, 0, 3, 2, 4)

    log_gamma = jnp.log1p(-jnp.exp2(-5.0 - jnp.arange(RET_HEADS, dtype=jnp.float32)))
    idx = jnp.arange(RET_CHUNK, dtype=jnp.float32)
    rel = idx[:, None] - idx[None, :]
    causal = rel >= 0
    inner_decay = jnp.where(causal, jnp.exp(jnp.where(causal, rel, 0.0) * log_gamma[:, None, None]), 0.0)
    q_decay = jnp.exp((idx + 1.0) * log_gamma[:, None])[None, :, :, None]
    k_decay = jnp.exp((RET_CHUNK - 1.0 - idx) * log_gamma[:, None])[None, :, :, None]
    chunk_decay = jnp.exp(RET_CHUNK * log_gamma)[None, :, None, None]

    def step(state, qkv):
        qc, kc, vc = qkv
        sc = jnp.einsum('bhid,bhjd->bhij', qc, kc) * inner_decay
        o = jnp.einsum('bhij,bhje->bhie', sc, vc) + jnp.einsum('bhid,bhde->bhie', qc, state) * q_decay
        state = state * chunk_decay + jnp.einsum('bhjd,bhje->bhde', kc * k_decay, vc)
        return state, o

    state0 = jnp.zeros((b, RET_HEADS, RET_DK, RET_DV), jnp.float32)
    _, o = lax.scan(step, state0, (to_chunks(q), to_chunks(k), to_chunks(v)))
    o = o.transpose(1, 0, 3, 2, 4).reshape(b, s, RET_HEADS, RET_DV)
    o = _rmsnorm(o, gn_gain.reshape(RET_HEADS, RET_DV)).astype(u.dtype).reshape(b, s, hv)
    return (jax.nn.silu(g) * o) @ w_out


def _nsa(u, w_in, cmp_pos, cmp_w1, cmp_w2, w_out):
    b, s, _ = u.shape
    G, HPG, DH, QB = NSA_GROUPS, NSA_HPG, NSA_DH, NSA_Q_BLOCK
    hq, hkv = NSA_HEADS * DH, G * DH
    q, kv, gates = jnp.split(u @ w_in, [hq, hq + 6 * hkv], axis=-1)
    q = q.reshape(b, s, G, HPG, DH).transpose(0, 2, 3, 1, 4)
    kv = kv.reshape(b, s, 3, 2, G, DH).transpose(2, 3, 0, 4, 1, 5)
    gates = jax.nn.sigmoid(gates.reshape(b, s, 3, G, HPG).transpose(2, 0, 3, 4, 1))
    scale = DH ** -0.5

    n_cmp = (s - NSA_CMP_LEN) // NSA_CMP_STRIDE + 1
    blk = jnp.arange(n_cmp)[:, None] * NSA_CMP_STRIDE + jnp.arange(NSA_CMP_LEN)[None, :]

    def compress(t, c):
        tb = t[:, :, blk, :] + cmp_pos[c]
        hid = jax.nn.silu(tb.reshape(b, G, n_cmp, NSA_CMP_LEN * DH) @ cmp_w1[c])
        return hid @ cmp_w2[c]

    k_cmp, v_cmp = compress(kv[0, 0], 0), compress(kv[0, 1], 1)
    cmp_end = jnp.arange(n_cmp) * NSA_CMP_STRIDE + NSA_CMP_LEN - 1

    nb = s // NSA_SEL_LEN
    n_sel = min(NSA_N_SEL, nb)
    k_sel = kv[1, 0].reshape(b, G, nb, NSA_SEL_LEN, DH)
    v_sel = kv[1, 1].reshape(b, G, nb, NSA_SEL_LEN, DH)
    ratio = NSA_SEL_LEN // NSA_CMP_STRIDE
    span = NSA_CMP_LEN // NSA_CMP_STRIDE
    pad_r = ratio * nb + ratio + 1 - n_cmp
    gather = jax.vmap(jax.vmap(lambda tb, ib: tb[ib]))

    pad_w = ((0, 0), (0, 0), (NSA_WINDOW, 0), (0, 0))
    k_win, v_win = jnp.pad(kv[2, 0], pad_w), jnp.pad(kv[2, 1], pad_w)

    def block(c):
        t0 = c * QB
        t = t0 + jnp.arange(QB)
        qc = lax.dynamic_slice_in_dim(q, t0, QB, axis=3)
        gc = lax.dynamic_slice_in_dim(gates, t0, QB, axis=4)
        s_c = jnp.einsum('bghqd,bgnd->bghqn', qc, k_cmp) * scale
        p_c, _ = _masked_softmax(s_c, cmp_end[None, :] <= t[:, None])
        o_c = jnp.einsum('bghqn,bgnd->bghqd', p_c.astype(v_cmp.dtype), v_cmp)
        imp = jnp.pad(p_c.sum(axis=2), ((0, 0), (0, 0), (0, 0), (span - 1, pad_r)))
        imp_sel = 0.0
        for m in range(ratio):
            for n in range(span):
                imp_sel = imp_sel + imp[..., m + n: m + n + ratio * nb: ratio]
        cur = (t // NSA_SEL_LEN)[:, None]
        jb = jnp.arange(nb)[None, :]
        forced = (jb == 0) | (jb == cur) | (jb == cur - 1)
        score = jnp.where(jb <= cur, imp_sel + jnp.where(forced, NSA_FORCE_BONUS, 0.0), NEG_INF)
        top_val, top_idx = lax.top_k(score, n_sel)
        valid = top_val > 0.5 * NEG_INF
        k_g = gather(k_sel, top_idx).reshape(b, G, QB, n_sel * NSA_SEL_LEN, DH)
        v_g = gather(v_sel, top_idx).reshape(b, G, QB, n_sel * NSA_SEL_LEN, DH)
        key_pos = top_idx[..., None] * NSA_SEL_LEN + jnp.arange(NSA_SEL_LEN)
        m_sel = (valid[..., None] & (key_pos <= t[:, None, None])).reshape(b, G, 1, QB, n_sel * NSA_SEL_LEN)
        s_s = jnp.einsum('bghqd,bgqkd->bghqk', qc, k_g) * scale
        p_s, _ = _masked_softmax(s_s, m_sel)
        o_s = jnp.einsum('bghqk,bgqkd->bghqd', p_s.astype(v_g.dtype), v_g)
        kw = lax.dynamic_slice_in_dim(k_win, t0, NSA_WINDOW + QB, axis=2)
        vw = lax.dynamic_slice_in_dim(v_win, t0, NSA_WINDOW + QB, axis=2)
        kpos = (t0 - NSA_WINDOW + jnp.arange(NSA_WINDOW + QB))[None, :]
        m_w = (kpos <= t[:, None]) & (kpos > t[:, None] - NSA_WINDOW) & (kpos >= 0)
        s_w = jnp.einsum('bghqd,bgkd->bghqk', qc, kw) * scale
        p_w, _ = _masked_softmax(s_w, m_w)
        o_w = jnp.einsum('bghqk,bgkd->bghqd', p_w.astype(vw.dtype), vw)
        return gc[0][..., None] * o_c + gc[1][..., None] * o_s + gc[2][..., None] * o_w

    out = lax.map(block, jnp.arange(s // QB))
    out = out.transpose(1, 0, 4, 2, 3, 5).reshape(b, s, hq)
    return out.astype(u.dtype) @ w_out


def _ssd(u, w_in, conv_w, conv_b, dt_bias, a_log, d_skip, norm_g, w_out):
    b, s, _ = u.shape
    G, R, P, N, L = SSD_GROUPS, SSD_HPG, SSD_HEADDIM, SSD_STATE, SSD_CHUNK
    f32 = jnp.float32
    z, xbc, dt = jnp.split(u @ w_in, [SSD_D_INNER, SSD_D_INNER + SSD_CONV_DIM], axis=-1)
    xp = jnp.pad(xbc, ((0, 0), (SSD_CONV - 1, 0), (0, 0)))
    conv = conv_b
    for k in range(SSD_CONV):
        conv = conv + xp[:, k:k + s] * conv_w[k]
    xbc = jax.nn.silu(conv)
    xs, bm, cm = jnp.split(xbc, [SSD_D_INNER, SSD_D_INNER + G * N], axis=-1)
    xs = xs.astype(f32).reshape(b, s, G, R, P)
    bm = bm.astype(f32).reshape(b, s, G, N)
    cm = cm.astype(f32).reshape(b, s, G, N)
    dt = jax.nn.softplus(dt.astype(f32) + dt_bias.astype(f32)).reshape(b, s, G, R)
    A = -jnp.exp(a_log.astype(f32)).reshape(G, R)
    n_ch = -(-s // L)
    pad = n_ch * L - s

    def to_chunks(t):
        t = jnp.pad(t, ((0, 0), (0, pad)) + ((0, 0),) * (t.ndim - 2))
        return t.reshape((b, n_ch, L) + t.shape[2:]).swapaxes(0, 1)

    causal = jnp.tril(jnp.ones((L, L), bool))[None, :, :, None, None]

    def step(state, inp):
        xc, dtc, bc, cc = inp
        acs = jnp.cumsum(dtc * A, axis=1)
        seg = jnp.where(causal, jnp.exp(jnp.where(causal, acs[:, :, None] - acs[:, None, :], 0.0)), 0.0)
        cb = jnp.einsum('bign,bjgn->bijg', cc, bc)
        w = cb[..., None] * seg * dtc[:, None]
        y = jnp.einsum('bijgr,bjgrp->bigrp', w, xc)
        y = y + jnp.einsum('bign,bgrpn->bigrp', cc, state) * jnp.exp(acs)[..., None]
        to_end = jnp.exp(acs[:, -1:] - acs) * dtc
        state = state * jnp.exp(acs[:, -1])[..., None, None] + jnp.einsum('bjgn,bjgr,bjgrp->bgrpn', bc, to_end, xc)
        return state, y

    state0 = jnp.zeros((b, G, R, P, N), f32)
    _, y = lax.scan(step, state0, (to_chunks(xs), to_chunks(dt), to_chunks(bm), to_chunks(cm)))
    y = y.swapaxes(0, 1).reshape(b, n_ch * L, G, R, P)[:, :s]
    y = y + d_skip.astype(f32).reshape(G, R)[..., None] * xs
    yz = y.reshape(b, s, G, R * P) * jax.nn.silu(z.astype(f32)).reshape(b, s, G, R * P)
    y = _rmsnorm(yz, norm_g.reshape(G, R * P)).reshape(b, s, SSD_D_INNER).astype(u.dtype)
    return y @ w_out


def _dilated_group(q, k, v, r, n_back):
    b, s, h, dh = q.shape
    QB = DIL_Q_BLOCK
    L = s // r
    nb = -(-L // QB)
    Lp = nb * QB

    def to_sub(t):
        t = t.reshape(b, L, r, h, dh).transpose(0, 2, 3, 1, 4)
        return jnp.pad(t, ((0, 0), (0, 0), (0, 0), (0, Lp - L), (0, 0)))

    def band(t):
        tp = jnp.pad(to_sub(t), ((0, 0), (0, 0), (0, 0), (QB, 0), (0, 0))).reshape(b, r, h, nb + 1, QB, dh)
        return jnp.concatenate([tp[:, :, :, :-1], tp[:, :, :, 1:]], axis=4)

    qs = to_sub(q).reshape(b, r, h, nb, QB, dh)
    kb, vb = band(k), band(v)
    qi = jnp.arange(QB)[:, None]
    kj = jnp.arange(2 * QB)[None, :]
    dist = qi + QB - kj
    key_sub = jnp.arange(nb)[:, None, None] * QB - QB + kj[None]
    mask = (dist >= 0) & (dist <= n_back) & (key_sub >= 0)
    sc = jnp.einsum('bchnqd,bchnkd->bchnqk', qs, kb) * (dh ** -0.5)
    p, lse = _masked_softmax(sc, mask)
    o = jnp.einsum('bchnqk,bchnkd->bchnqd', p.astype(vb.dtype), vb)
    o = o.reshape(b, r, h, Lp, dh)[:, :, :, :L].transpose(0, 3, 1, 2, 4).reshape(b, s, h, dh)
    lse = lse.reshape(b, r, h, Lp)[:, :, :, :L].transpose(0, 3, 1, 2).reshape(b, s, h)
    return o, lse


def _dilated(u, w_in, w_out):
    b, s, _ = u.shape
    proj = (u @ w_in).reshape(b, s, len(DIL_PATTERN), 3, DIL_HEADS, DIL_DH)
    outs, lses = [], []
    for g, (win, r) in enumerate(DIL_PATTERN):
        o, lse = _dilated_group(proj[:, :, g, 0], proj[:, :, g, 1], proj[:, :, g, 2], r, win // r)
        outs.append(o)
        lses.append(lse)
    wts = jax.nn.softmax(jnp.stack(lses), axis=0)
    o = jnp.einsum('gbsh,gbshd->bshd', wts.astype(u.dtype), jnp.stack(outs))
    return o.reshape(b, s, DIL_HEADS * DIL_DH) @ w_out


def setup_inputs(seed: int = 0) -> dict:
    key = jax.random.key(seed)
    ks = iter(jax.random.split(key, 32))
    f32 = jnp.float32

    def nrm(shape, scale):
        return jax.random.normal(next(ks), shape, f32) * scale

    def gain(shape):
        return 1.0 + 0.01 * jax.random.normal(next(ks), shape, f32)

    n_a, n_b, n_c, n_d = (len(range(m, DEPTH, N_MIXERS)) for m in range(N_MIXERS))
    dt0 = jnp.exp(jax.random.uniform(next(ks), (n_c, SSD_HEADS), f32, math.log(1e-3), math.log(1e-1)))
    return {
        "x": nrm((BATCH, SEQ, D_MODEL), 1.0),
        "norm_ffn1": gain((DEPTH, D_MODEL)),
        "ffn1_w_in": nrm((DEPTH, D_MODEL, 2 * D_FF), D_MODEL ** -0.5),
        "ffn1_w_out": nrm((DEPTH, D_FF, D_MODEL), D_FF ** -0.5),
        "norm_mix": gain((DEPTH, D_MODEL)),
        "norm_ffn2": gain((DEPTH, D_MODEL)),
        "ffn2_w_in": nrm((DEPTH, D_MODEL, 2 * D_FF), D_MODEL ** -0.5),
        "ffn2_w_out": nrm((DEPTH, D_FF, D_MODEL), D_FF ** -0.5),
        "norm_final": gain((D_MODEL,)),
        "ret_w_in": nrm((n_a, D_MODEL, 2 * RET_HEADS * RET_DK + 2 * RET_HEADS * RET_DV), D_MODEL ** -0.5),
        "ret_gn_gain": gain((n_a, RET_HEADS * RET_DV)),
        "ret_w_out": nrm((n_a, RET_HEADS * RET_DV, D_MODEL), (RET_HEADS * RET_DV) ** -0.5),
        "nsa_w_in": nrm((n_b, D_MODEL, NSA_HEADS * NSA_DH + 6 * NSA_GROUPS * NSA_DH + 3 * NSA_HEADS), D_MODEL ** -0.5),
        "nsa_cmp_pos": nrm((n_b, 2, NSA_CMP_LEN, NSA_DH), 0.02),
        "nsa_cmp_w1": nrm((n_b, 2, NSA_CMP_LEN * NSA_DH, NSA_CMP_HID), (NSA_CMP_LEN * NSA_DH) ** -0.5),
        "nsa_cmp_w2": nrm((n_b, 2, NSA_CMP_HID, NSA_DH), NSA_CMP_HID ** -0.5),
        "nsa_w_out": nrm((n_b, NSA_HEADS * NSA_DH, D_MODEL), (NSA_HEADS * NSA_DH) ** -0.5),
        "ssd_w_in": nrm((n_c, D_MODEL, SSD_D_INNER + SSD_CONV_DIM + SSD_HEADS), D_MODEL ** -0.5),
        "ssd_conv_w": nrm((n_c, SSD_CONV, SSD_CONV_DIM), SSD_CONV ** -0.5),
        "ssd_conv_b": nrm((n_c, SSD_CONV_DIM), 0.01),
        "ssd_dt_bias": dt0 + jnp.log(-jnp.expm1(-dt0)),
        "ssd_a_log": jnp.log(jax.random.uniform(next(ks), (n_c, SSD_HEADS), f32, 1.0, 16.0)),
        "ssd_d": gain((n_c, SSD_HEADS)),
        "ssd_norm": gain((n_c, SSD_D_INNER)),
        "ssd_w_out": nrm((n_c, SSD_D_INNER, D_MODEL), SSD_D_INNER ** -0.5),
        "dil_w_in": nrm((n_d, D_MODEL, len(DIL_PATTERN) * 3 * DIL_HEADS * DIL_DH), D_MODEL ** -0.5),
        "dil_w_out": nrm((n_d, DIL_HEADS * DIL_DH, D_MODEL), (DIL_HEADS * DIL_DH) ** -0.5),
    }


def reference(x, norm_ffn1, ffn1_w_in, ffn1_w_out, norm_mix, norm_ffn2, ffn2_w_in, ffn2_w_out, norm_final,
              ret_w_in, ret_gn_gain, ret_w_out,
              nsa_w_in, nsa_cmp_pos, nsa_cmp_w1, nsa_cmp_w2, nsa_w_out,
              ssd_w_in, ssd_conv_w, ssd_conv_b, ssd_dt_bias, ssd_a_log, ssd_d, ssd_norm, ssd_w_out,
              dil_w_in, dil_w_out):
    h = x
    for i in range(DEPTH):
        h = h + 0.5 * _swiglu(_rmsnorm(h, norm_ffn1[i]), ffn1_w_in[i], ffn1_w_out[i])
        u = _rmsnorm(h, norm_mix[i])
        m, j = i % N_MIXERS, i // N_MIXERS
        if m == 0:
            y = _retention(u, ret_w_in[j], ret_gn_gain[j], ret_w_out[j])
        elif m == 1:
            y = _nsa(u, nsa_w_in[j], nsa_cmp_pos[j], nsa_cmp_w1[j], nsa_cmp_w2[j], nsa_w_out[j])
        elif m == 2:
            y = _ssd(u, ssd_w_in[j], ssd_conv_w[j], ssd_conv_b[j], ssd_dt_bias[j], ssd_a_log[j],
                     ssd_d[j], ssd_norm[j], ssd_w_out[j])
        else:
            y = _dilated(u, dil_w_in[j], dil_w_out[j])
        h = h + y
        h = h + 0.5 * _swiglu(_rmsnorm(h, norm_ffn2[i]), ffn2_w_in[i], ffn2_w_out[i])
    return _rmsnorm(h, norm_final)
```

```python
import functools
import math

import jax
import jax.numpy as jnp
from jax import lax
from jax.experimental import pallas as pl
from jax.experimental.pallas import tpu as pltpu

F32 = jnp.float32
BF16 = jnp.bfloat16
NORM_EPS = 1e-6
NEG_INF = -1e30
ROPE_BASE = 10000.0
VMEM_LIMIT_BYTES = 56 * 1024 * 1024

RET_HEADS = 4
RET_CHUNK = 128

NSA_GROUPS = 4
NSA_HPG = 4
NSA_DH = 64
NSA_CMP_LEN = 32
NSA_CMP_STRIDE = 16
NSA_SEL_LEN = 64
NSA_N_SEL = 16
NSA_WINDOW = 512
NSA_FORCE_BONUS = 1e4
NSA_QT = 128
NSA_KC = 512

SSD_GROUPS = 4
SSD_HPG = 8
SSD_HEADDIM = 64
SSD_STATE = 128
SSD_CONV = 4
SSD_CHUNK = 256

DIL_PATTERN = ((128, 1), (512, 4), (2048, 16))
DIL_HEADS = 8
DIL_DH = 128
DIL_QB = 128


def _params(*sem):
    return pltpu.CompilerParams(dimension_semantics=sem, vmem_limit_bytes=VMEM_LIMIT_BYTES)


def _rms(x, g):
    return x * lax.rsqrt(jnp.mean(x * x, axis=-1, keepdims=True) + NORM_EPS) * g


def _silu(x):
    return x * jax.nn.sigmoid(x)


def _dot(a, b):
    return jnp.dot(a, b, preferred_element_type=F32)


def _dot_nt(a, b):
    return lax.dot_general(a, b, (((1,), (1,)), ((), ())), preferred_element_type=F32)


def _split3(x):
    hi = x.astype(BF16)
    r1 = x - hi.astype(F32)
    mid = r1.astype(BF16)
    lo = (r1 - mid.astype(F32)).astype(BF16)
    return hi, mid, lo


def _ffn_body(h_ref, g_ref, wa_ref, wb_ref, wo_ref, gf_ref, o_ref, *, fc, final_norm):
    x = h_ref[...]
    xn = _rms(x, g_ref[...]).astype(BF16)
    acc = jnp.zeros(x.shape, F32)
    for j in range(wa_ref.shape[1] // fc):
        sl = slice(j * fc, (j + 1) * fc)
        a = _dot(xn, wa_ref[:, sl])
        b = _dot(xn, wb_ref[:, sl])
        acc = acc + _dot((_silu(a) * b).astype(BF16), wo_ref[sl, :])
    y = x + 0.5 * acc
    if final_norm:
        y = _rms(y, gf_ref[...])
    o_ref[...] = y


def _ffn(h, g, w_in, w_out, g_final=None, *, tm=512, fc=1408):
    m, d = h.shape
    f = w_out.shape[0]
    final_norm = g_final is not None
    gf = g_final if final_norm else g
    return pl.pallas_call(
        functools.partial(_ffn_body, fc=fc, final_norm=final_norm),
        grid=(m // tm,),
        in_specs=[
            pl.BlockSpec((tm, d), lambda i: (i, 0)),
            pl.BlockSpec((1, d), lambda i: (0, 0)),
            pl.BlockSpec((d, f), lambda i: (0, 0)),
            pl.BlockSpec((d, f), lambda i: (0, 1)),
            pl.BlockSpec((f, d), lambda i: (0, 0)),
            pl.BlockSpec((1, d), lambda i: (0, 0)),
        ],
        out_specs=pl.BlockSpec((tm, d), lambda i: (i, 0)),
        out_shape=jax.ShapeDtypeStruct((m, d), F32),
        compiler_params=_params("parallel"),
        name="ffn",
    )(h, g.reshape(1, d), w_in, w_in, w_out, gf.reshape(1, d))


def _norm_matmul_body(h_ref, g_ref, w_ref, o_ref, xn_ref, *, transposed):
    @pl.when(pl.program_id(1) == 0)
    def _():
        xn_ref[...] = _rms(h_ref[...], g_ref[...]).astype(BF16)

    if transposed:
        o_ref[0] = _dot_nt(w_ref[...], xn_ref[...]).astype(o_ref.dtype)
    else:
        o_ref[...] = _dot(xn_ref[...], w_ref[...]).astype(o_ref.dtype)


def _norm_matmul(h, g, w, *, tm, tn, transposed=False, out_dtype=BF16):
    m, d = h.shape
    n = w.shape[0] if transposed else w.shape[1]
    if transposed:
        w_spec = pl.BlockSpec((tn, d), lambda i, j: (j, 0))
        out_spec = pl.BlockSpec((1, tn, tm), lambda i, j: (i, j, 0))
        out_shape = jax.ShapeDtypeStruct((m // tm, n, tm), out_dtype)
    else:
        w_spec = pl.BlockSpec((d, tn), lambda i, j: (0, j))
        out_spec = pl.BlockSpec((tm, tn), lambda i, j: (i, j))
        out_shape = jax.ShapeDtypeStruct((m, n), out_dtype)
    return pl.pallas_call(
        functools.partial(_norm_matmul_body, transposed=transposed),
        grid=(m // tm, n // tn),
        in_specs=[
            pl.BlockSpec((tm, d), lambda i, j: (i, 0)),
            pl.BlockSpec((1, d), lambda i, j: (0, 0)),
            w_spec,
        ],
        out_specs=out_spec,
        out_shape=out_shape,
        scratch_shapes=[pltpu.VMEM((tm, d), BF16)],
        compiler_params=_params("parallel", "arbitrary"),
        name="norm_matmul_t" if transposed else "norm_matmul",
    )(h, g.reshape(1, d), w)


def _out_proj_body(h_ref, y_ref, w_ref, o_ref):
    o_ref[...] = h_ref[...] + _dot(y_ref[...], w_ref[...])


def _out_proj(h, y, w, *, tm=512):
    m, d = h.shape
    k = y.shape[1]
    return pl.pallas_call(
        _out_proj_body,
        grid=(m // tm,),
        in_specs=[
            pl.BlockSpec((tm, d), lambda i: (i, 0)),
            pl.BlockSpec((tm, k), lambda i: (i, 0)),
            pl.BlockSpec((k, d), lambda i: (0, 0)),
        ],
        out_specs=pl.BlockSpec((tm, d), lambda i: (i, 0)),
        out_shape=jax.ShapeDtypeStruct((m, d), F32),
        compiler_params=_params("parallel"),
        name="out_proj",
    )(h, y, w)


def _ret_body(q_ref, k_ref, v_ref, g_ref, cos_ref, sin_ref, gn_ref, o_ref, state_ref, *, ts):
    c_len = RET_CHUNK
    dk = q_ref.shape[2] // RET_HEADS
    dv = v_ref.shape[2] // RET_HEADS
    half = dk // 2

    @pl.when(pl.program_id(1) == 0)
    def _():
        state_ref[...] = jnp.zeros(state_ref.shape, F32)

    ii = lax.broadcasted_iota(jnp.int32, (c_len, c_len), 0)
    jj = lax.broadcasted_iota(jnp.int32, (c_len, c_len), 1)
    rel = (ii - jj).astype(F32)
    causal = ii >= jj
    idx = lax.broadcasted_iota(jnp.int32, (c_len, 1), 0).astype(F32)

    def rot(t, cos, sin):
        t1, t2 = t[:, :half], t[:, half:]
        return jnp.concatenate([t1 * cos - t2 * sin, t1 * sin + t2 * cos], axis=1)

    def chunk(c, carry):
        r0 = pl.multiple_of(c * c_len, c_len)
        rows = pl.ds(r0, c_len)
        cos = cos_ref[rows, :]
        sin = sin_ref[rows, :]
        for h in range(RET_HEADS):
            log_gamma = math.log1p(-(2.0 ** (-5.0 - h)))
            inner = jnp.where(causal, jnp.exp(jnp.where(causal, rel, 0.0) * log_gamma), 0.0)
            q_decay = jnp.exp((idx + 1.0) * log_gamma)
            k_decay = jnp.exp((c_len - 1.0 - idx) * log_gamma)
            chunk_decay = math.exp(c_len * log_gamma)
            q = rot(q_ref[0, rows, h * dk:(h + 1) * dk].astype(F32), cos, sin)
            k = rot(k_ref[0, rows, h * dk:(h + 1) * dk].astype(F32), cos, sin) * (dk ** -0.5)
            v = v_ref[0, rows, h * dv:(h + 1) * dv]
            qb = q.astype(BF16)
            sc = _dot_nt(qb, k.astype(BF16)) * inner
            st = state_ref[h]
            o = _dot(sc.astype(BF16), v) + _dot(qb, st.astype(BF16)) * q_decay
            kd_t = jnp.transpose(k * k_decay).astype(BF16)
            state_ref[h] = st * chunk_decay + _dot(kd_t, v)
            gn = gn_ref[:, h * dv:(h + 1) * dv]
            on = _rms(o, gn)
            gate = g_ref[0, rows, h * dv:(h + 1) * dv].astype(F32)
            o_ref[0, rows, h * dv:(h + 1) * dv] = (_silu(gate) * on).astype(o_ref.dtype)
        return carry

    lax.fori_loop(0, ts // c_len, chunk, 0)


def _retention_core(proj, cos, sin, gn_gain, *, b, s, ts=512):
    n = proj.shape[2]
    hk = n // 6
    hv = 2 * hk
    dk = hk // RET_HEADS
    return pl.pallas_call(
        functools.partial(_ret_body, ts=ts),
        grid=(b, s // ts),
        in_specs=[
            pl.BlockSpec((1, ts, hk), lambda i, j: (i, j, 0)),
            pl.BlockSpec((1, ts, hk), lambda i, j: (i, j, 1)),
            pl.BlockSpec((1, ts, hv), lambda i, j: (i, j, 1)),
            pl.BlockSpec((1, ts, hv), lambda i, j: (i, j, 2)),
            pl.BlockSpec((ts, dk // 2), lambda i, j: (j, 0)),
            pl.BlockSpec((ts, dk // 2), lambda i, j: (j, 0)),
            pl.BlockSpec((1, hv), lambda i, j: (0, 0)),
        ],
        out_specs=pl.BlockSpec((1, ts, hv), lambda i, j: (i, j, 0)),
        out_shape=jax.ShapeDtypeStruct((b, s, hv), BF16),
        scratch_shapes=[pltpu.VMEM((RET_HEADS, dk, hv // RET_HEADS), F32)],
        compiler_params=_params("parallel", "arbitrary"),
        name="retention",
    )(proj, proj, proj, proj, cos, sin, gn_gain.reshape(1, hv))


def _rope_tables(s, half):
    inv = ROPE_BASE ** (-jnp.arange(half, dtype=F32) / half)
    ang = jnp.arange(s, dtype=F32)[:, None] * inv[None, :]
    return jnp.cos(ang), jnp.sin(ang)


def _retention(h2, g_norm, w_in, gn_gain, w_out, *, b, s):
    d = h2.shape[1]
    proj = _norm_matmul(h2, g_norm, w_in.astype(BF16), tm=512, tn=1024)
    n = proj.shape[1]
    cos, sin = _rope_tables(s, n // 6 // RET_HEADS // 2)
    y = _retention_core(proj.reshape(b, s, n), cos, sin, gn_gain, b=b, s=s)
    return _out_proj(h2, y.reshape(b * s, -1), w_out.astype(BF16))


def _dil_body(q_ref, kc_ref, kp_ref, vc_ref, vp_ref, o_ref, lse_ref, kbuf, vbuf, *, rows, n_back):
    qb = DIL_QB
    dh = DIL_DH
    first_step = pl.program_id(2) == 0
    kbuf[0:qb, :] = kp_ref[0]
    kbuf[qb:, :] = kc_ref[0]
    vbuf[0:qb, :] = vp_ref[0]
    vbuf[qb:, :] = vc_ref[0]
    qi = lax.broadcasted_iota(jnp.int32, (qb, 2 * qb), 0)
    kj = lax.broadcasted_iota(jnp.int32, (qb, 2 * qb), 1)
    dist = qi + qb - kj
    band = (dist >= 0) & (dist <= n_back)
    lane = lax.broadcasted_iota(jnp.int32, (qb, 128), 1)
    scale = dh ** -0.5
    for i in range(rows // qb):
        mask = band & ((kj >= qb) | jnp.logical_not(first_step)) if i == 0 else band
        lse_tile = jnp.zeros((qb, 128), F32)
        for h in range(DIL_HEADS):
            cols = slice(h * dh, (h + 1) * dh)
            q = q_ref[0, i * qb:(i + 1) * qb, cols]
            k = kbuf[i * qb:(i + 2) * qb, cols]
            v = vbuf[i * qb:(i + 2) * qb, cols]
            s = jnp.where(mask, _dot_nt(q, k) * scale, NEG_INF)
            m = jnp.max(s, axis=1, keepdims=True)
            e = jnp.where(mask, jnp.exp(s - m), 0.0)
            den = jnp.maximum(jnp.sum(e, axis=1, keepdims=True), 1e-30)
            o = _dot((e / den).astype(BF16), v)
            o_ref[0, i * qb:(i + 1) * qb, cols] = o.astype(o_ref.dtype)
            lse_tile = jnp.where(lane == h, m + jnp.log(den), lse_tile)
        lse_ref[0, i * qb:(i + 1) * qb, :] = lse_tile


def _dilated_group(proj, g, win, r, *, b, s):
    hd = DIL_HEADS * DIL_DH
    n_all = proj.shape[2]
    length = s // r
    rows = min(length, 512)
    qb = DIL_QB
    pv = proj.reshape(b, length, r * n_all)
    cpb = n_all // hd
    base = g * 3

    def cur(which):
        return pl.BlockSpec((1, rows, hd), lambda i, c, n: (i, n, c * cpb + base + which))

    def prev(which):
        return pl.BlockSpec((1, qb, hd),
                            lambda i, c, n: (i, jnp.maximum(n * (rows // qb) - 1, 0), c * cpb + base + which))

    o, lse = pl.pallas_call(
        functools.partial(_dil_body, rows=rows, n_back=win // r),
        grid=(b, r, length // rows),
        in_specs=[cur(0), cur(1), prev(1), cur(2), prev(2)],
        out_specs=[
            pl.BlockSpec((1, rows, hd), lambda i, c, n: (i, n, c)),
            pl.BlockSpec((1, rows, 128), lambda i, c, n: (i, n, c)),
        ],
        out_shape=[
            jax.ShapeDtypeStruct((b, length, r * hd), BF16),
            jax.ShapeDtypeStruct((b, length, r * 128), F32),
        ],
        scratch_shapes=[pltpu.VMEM((rows + qb, hd), BF16), pltpu.VMEM((rows + qb, hd), BF16)],
        compiler_params=_params("parallel", "parallel", "arbitrary"),
        name=f"dilated_r{r}",
    )(pv, pv, pv, pv, pv)
    return o.reshape(b * s, hd), lse.reshape(b * s, 128)


def _dil_merge_body(h_ref, o0_ref, o1_ref, o2_ref, l0_ref, l1_ref, l2_ref, w_ref, out_ref):
    lses = [l0_ref[...], l1_ref[...], l2_ref[...]]
    mx = jnp.maximum(jnp.maximum(lses[0], lses[1]), lses[2])
    es = [jnp.exp(l - mx) for l in lses]
    tot = es[0] + es[1] + es[2]
    wts = [e / tot for e in es]
    o_refs = [o0_ref, o1_ref, o2_ref]
    tm = h_ref.shape[0]
    dh = DIL_DH
    parts = []
    for h in range(DIL_HEADS):
        acc = jnp.zeros((tm, dh), F32)
        for g in range(3):
            wg = jnp.broadcast_to(wts[g][:, h:h + 1], (tm, dh))
            acc = acc + wg * o_refs[g][:, h * dh:(h + 1) * dh].astype(F32)
        parts.append(acc)
    o = jnp.concatenate(parts, axis=1).astype(BF16)
    out_ref[...] = h_ref[...] + _dot(o, w_ref[...])


def _dil_merge(h, outs, lses, w, *, tm=512):
    m, d = h.shape
    hd = w.shape[0]
    row = lambda width: pl.BlockSpec((tm, width), lambda i: (i, 0))
    return pl.pallas_call(
        _dil_merge_body,
        grid=(m // tm,),
        in_specs=[row(d), row(hd), row(hd), row(hd), row(128), row(128), row(128),
                  pl.BlockSpec((hd, d), lambda i: (0, 0))],
        out_specs=row(d),
        out_shape=jax.ShapeDtypeStruct((m, d), F32),
        compiler_params=_params("parallel"),
        name="dilated_merge",
    )(h, *outs, *lses, w)


def _dilated(h2, g_norm, w_in, w_out, *, b, s):
    proj = _norm_matmul(h2, g_norm, w_in.astype(BF16), tm=512, tn=1024)
    proj = proj.reshape(b, s, -1)
    outs, lses = [], []
    for g, (win, r) in enumerate(DIL_PATTERN):
        o, lse = _dilated_group(proj, g, win, r, b=b, s=s)
        outs.append(o)
        lses.append(lse)
    return _dil_merge(h2, outs, lses, w_out.astype(BF16))


def _softplus(x):
    return jnp.maximum(x, 0.0) + jnp.log1p(jnp.exp(-jnp.abs(x)))


def _dot3(terms, rhs):
    return _dot(terms[0], rhs) + _dot(terms[1], rhs) + _dot(terms[2], rhs)


def _ssd_body(z_ref, x_ref, dt_ref, cw_ref, cb_ref, dtb_ref, alog_ref, dsk_ref, ng_ref, o_ref,
              xpad, state_ref):
    ln = SSD_CHUNK
    nst = SSD_STATE
    gw = SSD_HPG * SSD_HEADDIM
    d_inner = SSD_GROUPS * gw
    halo = 8

    @pl.when(pl.program_id(1) == 0)
    def _():
        xpad[0:halo, :] = jnp.zeros((halo, xpad.shape[1]), F32)
        state_ref[...] = jnp.zeros(state_ref.shape, F32)

    xpad[halo:halo + ln, :] = x_ref[0].astype(F32)
    conv = cb_ref[...]
    for k in range(SSD_CONV):
        off = halo - (SSD_CONV - 1) + k
        conv = conv + xpad[off:off + ln, :] * cw_ref[k:k + 1, :]
    xpad[0:halo, :] = xpad[ln:ln + halo, :]
    xbc = _silu(conv)
    xs = xbc[:, :d_inner]
    bm = xbc[:, d_inner:d_inner + SSD_GROUPS * nst]
    cm = xbc[:, d_inner + SSD_GROUPS * nst:]

    dt = _softplus(dt_ref[0] + dtb_ref[...])
    da = dt * (-jnp.exp(alog_ref[...]))
    ii = lax.broadcasted_iota(jnp.int32, (ln, ln), 0)
    jj = lax.broadcasted_iota(jnp.int32, (ln, ln), 1)
    causal = ii >= jj
    tril = jnp.where(causal, 1.0, 0.0).astype(BF16)
    da_terms = _split3(da)
    acs = _dot(tril, da_terms[0]) + _dot(tril, da_terms[1]) + _dot(tril, da_terms[2])
    acs_t = jnp.transpose(acs)
    dt_t = jnp.transpose(dt)

    erow = lax.broadcasted_iota(jnp.int32, (128, d_inner), 0)
    ecol = lax.broadcasted_iota(jnp.int32, (128, d_inner), 1)
    expand = jnp.where(ecol // SSD_HEADDIM == erow, 1.0, 0.0).astype(BF16)
    acs_e = _dot3(_split3(acs), expand)
    dt_e = _dot3(_split3(dt), expand)
    last = acs_e[ln - 1:ln, :]
    decay_in = jnp.exp(acs_e)
    xs_end = (xs * (jnp.exp(last - acs_e) * dt_e)).astype(BF16)
    xs_b = xs.astype(BF16)
    lane = lax.broadcasted_iota(jnp.int32, (ln, 128), 1)

    y_groups = []
    for g in range(SSD_GROUPS):
        bm_g = bm[:, g * nst:(g + 1) * nst]
        cm_g = cm[:, g * nst:(g + 1) * nst].astype(BF16)
        cb = _dot_nt(cm_g, bm_g.astype(BF16))
        st = state_ref[g]
        gcols = slice(g * gw, (g + 1) * gw)
        y_state = _dot(cm_g, st.astype(BF16)) * decay_in[:, gcols]
        pairs = []
        for p in range(SSD_HPG // 2):
            pair_cols = slice(g * gw + p * 128, g * gw + (p + 1) * 128)
            halves = []
            for e in range(2):
                hd = g * SSD_HPG + 2 * p + e
                diff = acs[:, hd:hd + 1] - acs_t[hd:hd + 1, :]
                seg = jnp.where(causal, jnp.exp(jnp.where(causal, diff, 0.0)), 0.0)
                w = (cb * seg * dt_t[hd:hd + 1, :]).astype(BF16)
                halves.append(_dot(w, xs_b[:, pair_cols]))
            pairs.append(jnp.where(lane < SSD_HEADDIM, halves[0], halves[1]))
        y_groups.append(jnp.concatenate(pairs, axis=1) + y_state)
        bm_t = jnp.transpose(bm_g).astype(BF16)
        state_ref[g] = st * jnp.exp(last[:, gcols]) + _dot(bm_t, xs_end[:, gcols])

    y = jnp.concatenate(y_groups, axis=1) + dsk_ref[...] * xs
    yz = y * _silu(z_ref[0].astype(F32))
    outs = [_rms(yz[:, g * gw:(g + 1) * gw], ng_ref[:, g * gw:(g + 1) * gw]) for g in range(SSD_GROUPS)]
    o_ref[0] = jnp.concatenate(outs, axis=1).astype(o_ref.dtype)


def _ssd_core(z, xbc, dt, conv_w, conv_b, dt_bias, a_log, d_skip, norm_g, *, b, s):
    ln = SSD_CHUNK
    d_inner = z.shape[2]
    conv_dim = xbc.shape[2]
    gw = SSD_HPG * SSD_HEADDIM
    full = lambda shape: pl.BlockSpec(shape, lambda i, j: (0,) * len(shape))
    return pl.pallas_call(
        _ssd_body,
        grid=(b, s // ln),
        in_specs=[
            pl.BlockSpec((1, ln, d_inner), lambda i, j: (i, j, 0)),
            pl.BlockSpec((1, ln, conv_dim), lambda i, j: (i, j, 0)),
            pl.BlockSpec((1, ln, 128), lambda i, j: (i, j, 0)),
            full((SSD_CONV, conv_dim)),
            full((1, conv_dim)),
            full((1, 128)),
            full((1, 128)),
            full((1, d_inner)),
            full((1, d_inner)),
        ],
        out_specs=pl.BlockSpec((1, ln, d_inner), lambda i, j: (i, j, 0)),
        out_shape=jax.ShapeDtypeStruct((b, s, d_inner), BF16),
        scratch_shapes=[pltpu.VMEM((ln + 8, conv_dim), F32),
                        pltpu.VMEM((SSD_GROUPS, SSD_STATE, gw), F32)],
        compiler_params=_params("parallel", "arbitrary"),
        name="ssd",
    )(z, xbc, dt, conv_w, conv_b, dt_bias, a_log, d_skip, norm_g)


def _ssd(h2, g_norm, w_in, conv_w, conv_b, dt_bias, a_log, d_skip, norm_g, w_out, *, b, s):
    heads = SSD_GROUPS * SSD_HPG
    d_inner = heads * SSD_HEADDIM
    conv_dim = conv_w.shape[1]
    wb = w_in.astype(BF16)
    z = _norm_matmul(h2, g_norm, wb[:, :d_inner], tm=512, tn=1024)
    xbc = _norm_matmul(h2, g_norm, wb[:, d_inner:d_inner + conv_dim], tm=512, tn=1024)
    w_dt = jnp.pad(wb[:, d_inner + conv_dim:], ((0, 0), (0, 128 - heads)))
    dt = _norm_matmul(h2, g_norm, w_dt, tm=512, tn=128, out_dtype=F32)
    pad_heads = lambda v: jnp.pad(v.astype(F32), (0, 128 - heads)).reshape(1, 128)
    y = _ssd_core(z.reshape(b, s, -1), xbc.reshape(b, s, -1), dt.reshape(b, s, 128),
                  conv_w.astype(F32), conv_b.reshape(1, -1).astype(F32), pad_heads(dt_bias), pad_heads(a_log),
                  jnp.repeat(d_skip.astype(F32), SSD_HEADDIM).reshape(1, d_inner), norm_g.reshape(1, d_inner),
                  b=b, s=s)
    return _out_proj(h2, y.reshape(b * s, d_inner), w_out.astype(BF16))


def _nsa_cmp_body(x_ref, pa_ref, pb_ref, w1a_ref, w1b_ref, w2_ref, o_ref, *, transposed):
    x = x_ref[0].astype(F32)
    nb = x.shape[0]
    ya = _dot((x + pa_ref[...]).astype(BF16), w1a_ref[...])
    yb = _dot((x + pb_ref[...]).astype(BF16), w1b_ref[...])
    hid = _silu(ya + pltpu.roll(yb, nb - 1, 0))
    out = _dot(hid.astype(BF16), w2_ref[...])
    o_ref[0] = (jnp.transpose(out) if transposed else out).astype(o_ref.dtype)


def _nsa_compress(x, pos, w1, w2, *, transposed):
    b, nb, width = x.shape
    g, dh = NSA_GROUPS, NSA_DH
    half = NSA_CMP_LEN // 2
    hid = w1.shape[1]
    eye = jnp.eye(g, dtype=F32)
    w1r = w1.reshape(NSA_CMP_LEN, dh, hid)
    big = jnp.einsum("ldj,gh->lgdhj", w1r, eye).reshape(NSA_CMP_LEN, g * dh, g * hid)
    w1a = big[:half].reshape(half * g * dh, g * hid).astype(BF16)
    w1b = big[half:].reshape(half * g * dh, g * hid).astype(BF16)
    w2big = jnp.einsum("jd,gh->gjhd", w2, eye).reshape(g * hid, g * dh).astype(BF16)
    posb = jnp.broadcast_to(pos[:, None, :], (NSA_CMP_LEN, g, dh)).reshape(NSA_CMP_LEN, g * dh)
    pa = posb[:half].reshape(1, width).astype(F32)
    pb = posb[half:].reshape(1, width).astype(F32)
    full = lambda shape: pl.BlockSpec(shape, lambda i: (0,) * len(shape))
    out_dims = (g * dh, nb) if transposed else (nb, g * dh)
    return pl.pallas_call(
        functools.partial(_nsa_cmp_body, transposed=transposed),
        grid=(b,),
        in_specs=[
            pl.BlockSpec((1, nb, width), lambda i: (i, 0, 0)),
            full((1, width)), full((1, width)),
            full((width, g * hid)), full((width, g * hid)), full((g * hid, g * dh)),
        ],
        out_specs=pl.BlockSpec((1,) + out_dims, lambda i: (i, 0, 0)),
        out_shape=jax.ShapeDtypeStruct((b,) + out_dims, BF16),
        compiler_params=_params("parallel"),
        name="nsa_compress",
    )(x, pa, pb, w1a, w1b, w2big)


def _nsa_body(q_ref, gt_ref, kc_ref, vct_ref, k1_ref, k2_ref, v1_ref, v2_ref, o_ref, sel_ref, *, s):
    qt, kc, dh = NSA_QT, NSA_KC, NSA_DH
    lanes = NSA_HPG * qt
    g = pl.program_id(1)
    t0 = pl.program_id(2) * qt
    nb = kc_ref.shape[1]
    nsb = s // NSA_SEL_LEN
    n_sel = min(NSA_N_SEL, nsb)

    q_t = q_ref[0, 0]
    qcat = jnp.concatenate([q_t[h * dh:(h + 1) * dh, :] for h in range(NSA_HPG)], axis=1)
    odd = (g % 2) == 1
    zero = jnp.zeros_like(qcat)
    qp = jnp.concatenate([jnp.where(odd, zero, qcat), jnp.where(odd, qcat, zero)], axis=0)
    tq = t0 + lax.broadcasted_iota(jnp.int32, (1, lanes), 1) % qt

    sc = _dot(kc_ref[0], qp)
    n_idx = lax.broadcasted_iota(jnp.int32, (nb, lanes), 0)
    cmask = (n_idx * NSA_CMP_STRIDE + NSA_CMP_LEN - 1 <= tq) & (n_idx < nb - 1)
    sc = jnp.where(cmask, sc, NEG_INF)
    e = jnp.where(cmask, jnp.exp(sc - jnp.max(sc, axis=0, keepdims=True)), 0.0)
    p_c = e / jnp.maximum(jnp.sum(e, axis=0, keepdims=True), 1e-30)
    o_c = _dot(vct_ref[0], p_c.astype(BF16))
    imp = p_c[:, 0:qt]
    for h in range(1, NSA_HPG):
        imp = imp + p_c[:, h * qt:(h + 1) * qt]

    ratio = NSA_SEL_LEN // NSA_CMP_STRIDE
    dd = lax.broadcasted_iota(jnp.int32, (nsb, nb), 1) - ratio * lax.broadcasted_iota(jnp.int32, (nsb, nb), 0)
    wsel = jnp.where((dd == -1) | (dd == ratio - 1), 1.0,
                     jnp.where((dd >= 0) & (dd < ratio - 1), 2.0, 0.0)).astype(BF16)
    terms = _split3(imp)
    imp_sel = _dot(wsel, terms[0]) + _dot(wsel, terms[1]) + _dot(wsel, terms[2])
    jq = lax.broadcasted_iota(jnp.int32, (nsb, qt), 0)
    cur = (t0 + lax.broadcasted_iota(jnp.int32, (nsb, qt), 1)) // NSA_SEL_LEN
    forced = (jq == 0) | (jq == cur) | (jq == cur - 1)
    score = jnp.where(jq <= cur, imp_sel + jnp.where(forced, NSA_FORCE_BONUS, 0.0), NEG_INF)
    rank = jnp.zeros((nsb, qt), F32)
    for jp in range(nsb):
        row = jnp.broadcast_to(score[jp:jp + 1, :], (nsb, qt))
        before = (row > score) | ((row == score) & (jq > jp))
        rank = rank + jnp.where(before, 1.0, 0.0)
    sel = jnp.where((rank < n_sel) & (score > 0.5 * NEG_INF), 1.0, 0.0)
    sel_ref[...] = jnp.concatenate([sel] * NSA_HPG, axis=1)

    kpos0 = lax.broadcasted_iota(jnp.int32, (kc, lanes), 0)
    blocks_per_chunk = kc // NSA_SEL_LEN

    def attend(k_ref, v_ref, c, carry, mask_fn):
        m, l, acc = carry
        k = k_ref[0, pl.ds(pl.multiple_of(c * kc, kc), kc), :]
        st = _dot(k, qp)
        mask = mask_fn(c, c * kc + kpos0)
        st = jnp.where(mask, st, NEG_INF)
        m_new = jnp.maximum(m, jnp.max(st, axis=0, keepdims=True))
        alpha = jnp.exp(m - m_new)
        p = jnp.where(mask, jnp.exp(st - m_new), 0.0)
        l = alpha * l + jnp.sum(p, axis=0, keepdims=True)
        acc = alpha * acc + _dot(v_ref[0, c], p.astype(BF16))
        return m_new, l, acc

    def sel_mask(c, kpos):
        selc = sel_ref[pl.ds(pl.multiple_of(c * blocks_per_chunk, blocks_per_chunk), blocks_per_chunk), :]
        rows = [jnp.broadcast_to(selc[r:r + 1, :], (NSA_SEL_LEN, lanes)) for r in range(blocks_per_chunk)]
        return (jnp.concatenate(rows, axis=0) > 0.5) & (kpos <= tq)

    def win_mask(c, kpos):
        return (kpos <= tq) & (kpos > tq - NSA_WINDOW)

    init = (jnp.full((1, lanes), NEG_INF, F32), jnp.zeros((1, lanes), F32), jnp.zeros((dh, lanes), F32))
    c_hi = t0 // kc
    _, l_s, acc_s = lax.fori_loop(0, c_hi + 1, lambda c, cr: attend(k1_ref, v1_ref, c, cr, sel_mask), init)
    _, l_w, acc_w = lax.fori_loop(jnp.maximum(c_hi - 1, 0), c_hi + 1,
                                  lambda c, cr: attend(k2_ref, v2_ref, c, cr, win_mask), init)
    o_s = acc_s / jnp.maximum(l_s, 1e-30)
    o_w = acc_w / jnp.maximum(l_w, 1e-30)

    gates = jax.nn.sigmoid(gt_ref[0, 0].astype(F32))
    def gate_row(br):
        return jnp.concatenate([gates[br * NSA_HPG + h:br * NSA_HPG + h + 1, :] for h in range(NSA_HPG)], axis=1)
    o = gate_row(0) * o_c + gate_row(1) * o_s + gate_row(2) * o_w
    o_hd = jnp.concatenate([o[:, h * qt:(h + 1) * qt] for h in range(NSA_HPG)], axis=0)
    o_ref[0] = jnp.transpose(o_hd).astype(o_ref.dtype)


def _nsa_attend(tr, k_cmp, v_cmp_t, k12, *, b, s):
    qt, kc, dh = NSA_QT, NSA_KC, NSA_DH
    gd = NSA_GROUPS * dh
    hq = NSA_GROUPS * NSA_HPG * dh
    nch = s // kc
    nb = k_cmp.shape[1]
    n_rows = tr.shape[1]
    tr4 = tr.reshape(b, nch, n_rows, kc)
    sub = kc // qt
    gate_rows = 4 * NSA_HPG
    v1_blk = hq // dh
    v2_blk = (hq + gd) // dh
    gate_blk = (hq + 2 * gd) // gate_rows
    return pl.pallas_call(
        functools.partial(_nsa_body, s=s),
        grid=(b, NSA_GROUPS, s // qt),
        in_specs=[
            pl.BlockSpec((1, 1, NSA_HPG * dh, qt), lambda i, g, j: (i, j // sub, g, j % sub)),
            pl.BlockSpec((1, 1, gate_rows, qt), lambda i, g, j: (i, j // sub, gate_blk + g, j % sub)),
            pl.BlockSpec((1, nb, 2 * dh), lambda i, g, j: (i, 0, g // 2)),
            pl.BlockSpec((1, dh, nb), lambda i, g, j: (i, g, 0)),
            pl.BlockSpec((1, s, 2 * dh), lambda i, g, j: (i, 0, g // 2)),
            pl.BlockSpec((1, s, 2 * dh), lambda i, g, j: (i, 0, NSA_GROUPS // 2 + g // 2)),
            pl.BlockSpec((1, nch, dh, kc), lambda i, g, j: (i, 0, v1_blk + g, 0)),
            pl.BlockSpec((1, nch, dh, kc), lambda i, g, j: (i, 0, v2_blk + g, 0)),
        ],
        out_specs=pl.BlockSpec((1, qt, NSA_HPG * dh), lambda i, g, j: (i, j, g)),
        out_shape=jax.ShapeDtypeStruct((b, s, hq), BF16),
        scratch_shapes=[pltpu.VMEM((s // NSA_SEL_LEN, NSA_HPG * qt), F32)],
        compiler_params=_params("parallel", "parallel", "arbitrary"),
        name="nsa_attend",
    )(tr4, tr4, k_cmp, v_cmp_t, k12, k12, tr4, tr4)


def _nsa(h2, g_norm, w_in, cmp_pos, cmp_w1, cmp_w2, w_out, *, b, s):
    g, hpg, dh = NSA_GROUPS, NSA_HPG, NSA_DH
    hq, gd = g * hpg * dh, g * dh
    kv_w = lambda br, kv: w_in[:, hq + (2 * br + kv) * gd:hq + (2 * br + kv + 1) * gd]
    w_q = w_in[:, :hq] * (dh ** -0.5)
    w_g = w_in[:, hq + 6 * gd:].reshape(-1, 3, g, hpg).transpose(2, 1, 3, 0)
    w_g = jnp.pad(w_g, ((0, 0), (0, 1), (0, 0), (0, 0))).reshape(g * 4 * hpg, -1)
    w_t = jnp.concatenate([w_q.T, kv_w(1, 1).T, kv_w(2, 1).T, w_g], axis=0).astype(BF16)
    tr = _norm_matmul(h2, g_norm, w_t, tm=NSA_KC, tn=w_t.shape[0], transposed=True)
    k0 = _norm_matmul(h2, g_norm, kv_w(0, 0).astype(BF16), tm=512, tn=gd)
    v0 = _norm_matmul(h2, g_norm, kv_w(0, 1).astype(BF16), tm=512, tn=gd)
    k12 = _norm_matmul(h2, g_norm, jnp.concatenate([kv_w(1, 0), kv_w(2, 0)], axis=1).astype(BF16), tm=512, tn=2 * gd)
    nb = s // NSA_CMP_STRIDE
    k_cmp = _nsa_compress(k0.reshape(b, nb, NSA_CMP_STRIDE * gd), cmp_pos[0], cmp_w1[0], cmp_w2[0], transposed=False)
    v_cmp_t = _nsa_compress(v0.reshape(b, nb, NSA_CMP_STRIDE * gd), cmp_pos[1], cmp_w1[1], cmp_w2[1], transposed=True)
    y = _nsa_attend(tr, k_cmp, v_cmp_t, k12.reshape(b, s, 2 * gd), b=b, s=s)
    return _out_proj(h2, y.reshape(b * s, hq), w_out.astype(BF16))


def kernel(x, norm_ffn1, ffn1_w_in, ffn1_w_out, norm_mix, norm_ffn2, ffn2_w_in, ffn2_w_out, norm_final,
           ret_w_in, ret_gn_gain, ret_w_out,
           nsa_w_in, nsa_cmp_pos, nsa_cmp_w1, nsa_cmp_w2, nsa_w_out,
           ssd_w_in, ssd_conv_w, ssd_conv_b, ssd_dt_bias, ssd_a_log, ssd_d, ssd_norm, ssd_w_out,
           dil_w_in, dil_w_out):
    b, s, d = x.shape
    depth = norm_mix.shape[0]
    h = x.reshape(b * s, d)
    for i in range(depth):
        h = _ffn(h, norm_ffn1[i], ffn1_w_in[i].astype(BF16), ffn1_w_out[i].astype(BF16))
        m, j = i % 4, i // 4
        if m == 0:
            h = _retention(h, norm_mix[i], ret_w_in[j], ret_gn_gain[j], ret_w_out[j], b=b, s=s)
        elif m == 1:
            h = _nsa(h, norm_mix[i], nsa_w_in[j], nsa_cmp_pos[j], nsa_cmp_w1[j], nsa_cmp_w2[j], nsa_w_out[j], b=b, s=s)
        elif m == 2:
            h = _ssd(h, norm_mix[i], ssd_w_in[j], ssd_conv_w[j], ssd_conv_b[j], ssd_dt_bias[j], ssd_a_log[j],
                     ssd_d[j], ssd_norm[j], ssd_w_out[j], b=b, s=s)
        else:
            h = _dilated(h, norm_mix[i], dil_w_in[j], dil_w_out[j], b=b, s=s)
        h = _ffn(h, norm_ffn2[i], ffn2_w_in[i].astype(BF16), ffn2_w_out[i].astype(BF16),
                 norm_final if i == depth - 1 else None)
    return h.reshape(b, s, d)
```

```python
import functools
import math

import jax
import jax.numpy as jnp
from jax import lax
from jax.experimental import pallas as pl
from jax.experimental.pallas import tpu as pltpu

F32 = jnp.float32
BF16 = jnp.bfloat16
NORM_EPS = 1e-6
NEG_INF = -1e30
ROPE_BASE = 10000.0
VMEM_LIMIT_BYTES = 56 * 1024 * 1024
LANES = 128
TOKEN_TILE = 512
MAX_DOT_COLS = 2048

RET_HEADS = 4
RET_CHUNK = 128

NSA_GROUPS = 4
NSA_HPG = 4
NSA_DH = 64
NSA_CMP_LEN = 32
NSA_CMP_STRIDE = 16
NSA_SEL_LEN = 64
NSA_N_SEL = 16
NSA_WINDOW = 512
NSA_FORCE_BONUS = 1e4
NSA_QT = 128
NSA_KC = 512

SSD_GROUPS = 4
SSD_HPG = 8
SSD_HEADDIM = 64
SSD_STATE = 128
SSD_CONV = 4
SSD_CHUNK = 256

DIL_PATTERN = ((128, 1), (512, 4), (2048, 16))
DIL_HEADS = 8
DIL_DH = 128
DIL_QB = 128


def _params(*sem):
    return pltpu.CompilerParams(dimension_semantics=sem, vmem_limit_bytes=VMEM_LIMIT_BYTES)


def _rms(x, g):
    return x * lax.rsqrt(jnp.mean(x * x, axis=-1, keepdims=True) + NORM_EPS) * g


def _silu(x):
    return x * jax.nn.sigmoid(x)


def _dot(a, b):
    return jnp.dot(a, b, preferred_element_type=F32)


def _dot_nt(a, b):
    return lax.dot_general(a, b, (((1,), (1,)), ((), ())), preferred_element_type=F32)


def _split3(x):
    hi = x.astype(BF16)
    r1 = x - hi.astype(F32)
    mid = r1.astype(BF16)
    lo = (r1 - mid.astype(F32)).astype(BF16)
    return hi, mid, lo


def _dot3(terms, rhs):
    return _dot(terms[0], rhs) + _dot(terms[1], rhs) + _dot(terms[2], rhs)


def _col_chunk(n):
    if n <= MAX_DOT_COLS:
        return n
    return max(c for c in range(LANES, MAX_DOT_COLS + 1, LANES) if n % c == 0)


def _ffn_body(h_ref, g_ref, wa_ref, wb_ref, wo_ref, gf_ref, o_ref, *, fc, final_norm):
    x = h_ref[...]
    xn = _rms(x, g_ref[...]).astype(BF16)
    acc = jnp.zeros(x.shape, F32)
    for j in range(wa_ref.shape[1] // fc):
        sl = slice(j * fc, (j + 1) * fc)
        a = _dot(xn, wa_ref[:, sl])
        b = _dot(xn, wb_ref[:, sl])
        acc = acc + _dot((_silu(a) * b).astype(BF16), wo_ref[sl, :])
    y = x + 0.5 * acc
    if final_norm:
        y = _rms(y, gf_ref[...])
    o_ref[...] = y


def _ffn(h, g, w_in, w_out, g_final=None, *, tm=TOKEN_TILE):
    m, d = h.shape
    f = w_out.shape[0]
    final_norm = g_final is not None
    gf = g_final if final_norm else g
    return pl.pallas_call(
        functools.partial(_ffn_body, fc=_col_chunk(f), final_norm=final_norm),
        grid=(m // tm,),
        in_specs=[
            pl.BlockSpec((tm, d), lambda i: (i, 0)),
            pl.BlockSpec((1, d), lambda i: (0, 0)),
            pl.BlockSpec((d, f), lambda i: (0, 0)),
            pl.BlockSpec((d, f), lambda i: (0, 1)),
            pl.BlockSpec((f, d), lambda i: (0, 0)),
            pl.BlockSpec((1, d), lambda i: (0, 0)),
        ],
        out_specs=pl.BlockSpec((tm, d), lambda i: (i, 0)),
        out_shape=jax.ShapeDtypeStruct((m, d), F32),
        compiler_params=_params("parallel"),
        name="ffn",
    )(h, g.reshape(1, d), w_in, w_in, w_out, gf.reshape(1, d))


def _norm_proj_body(h_ref, g_ref, *refs, n_out):
    xn = _rms(h_ref[...], g_ref[...]).astype(BF16)
    for w_ref, o_ref in zip(refs[:n_out], refs[n_out:]):
        n = w_ref.shape[1]
        nc = _col_chunk(n)
        for j in range(n // nc):
            sl = slice(j * nc, (j + 1) * nc)
            o_ref[:, sl] = _dot(xn, w_ref[:, sl]).astype(o_ref.dtype)


def _norm_proj(h, g, ws, out_dtypes, *, tm=TOKEN_TILE):
    m, d = h.shape
    return pl.pallas_call(
        functools.partial(_norm_proj_body, n_out=len(ws)),
        grid=(m // tm,),
        in_specs=[pl.BlockSpec((tm, d), lambda i: (i, 0)), pl.BlockSpec((1, d), lambda i: (0, 0))]
        + [pl.BlockSpec(w.shape, lambda i: (0, 0)) for w in ws],
        out_specs=[pl.BlockSpec((tm, w.shape[1]), lambda i: (i, 0)) for w in ws],
        out_shape=[jax.ShapeDtypeStruct((m, w.shape[1]), dt) for w, dt in zip(ws, out_dtypes)],
        compiler_params=_params("parallel"),
        name="norm_proj",
    )(h, g.reshape(1, d), *ws)


def _out_proj_body(h_ref, y_ref, w_ref, o_ref):
    o_ref[...] = h_ref[...] + _dot(y_ref[...], w_ref[...])


def _out_proj(h, y, w, *, tm=TOKEN_TILE):
    m, d = h.shape
    k = y.shape[1]
    return pl.pallas_call(
        _out_proj_body,
        grid=(m // tm,),
        in_specs=[
            pl.BlockSpec((tm, d), lambda i: (i, 0)),
            pl.BlockSpec((tm, k), lambda i: (i, 0)),
            pl.BlockSpec((k, d), lambda i: (0, 0)),
        ],
        out_specs=pl.BlockSpec((tm, d), lambda i: (i, 0)),
        out_shape=jax.ShapeDtypeStruct((m, d), F32),
        compiler_params=_params("parallel"),
        name="out_proj",
    )(h, y, w)


def _ret_body(q_ref, k_ref, v_ref, g_ref, cos_ref, sin_ref, gn_ref, o_ref, state_ref, *, ts):
    c_len = RET_CHUNK
    dk = q_ref.shape[2] // RET_HEADS
    dv = v_ref.shape[2] // RET_HEADS
    half = dk // 2

    @pl.when(pl.program_id(1) == 0)
    def _():
        state_ref[...] = jnp.zeros(state_ref.shape, F32)

    ii = lax.broadcasted_iota(jnp.int32, (c_len, c_len), 0)
    jj = lax.broadcasted_iota(jnp.int32, (c_len, c_len), 1)
    rel = (ii - jj).astype(F32)
    causal = ii >= jj
    idx = lax.broadcasted_iota(jnp.int32, (c_len, 1), 0).astype(F32)

    def rot(t, cos, sin):
        t1, t2 = t[:, :half], t[:, half:]
        return jnp.concatenate([t1 * cos - t2 * sin, t1 * sin + t2 * cos], axis=1)

    def chunk(c, carry):
        r0 = pl.multiple_of(c * c_len, c_len)
        rows = pl.ds(r0, c_len)
        cos = cos_ref[rows, :]
        sin = sin_ref[rows, :]
        for h in range(RET_HEADS):
            log_gamma = math.log1p(-(2.0 ** (-5.0 - h)))
            inner = jnp.where(causal, jnp.exp(jnp.where(causal, rel, 0.0) * log_gamma), 0.0)
            q_decay = jnp.exp((idx + 1.0) * log_gamma)
            k_decay = jnp.exp((c_len - 1.0 - idx) * log_gamma)
            chunk_decay = math.exp(c_len * log_gamma)
            q = rot(q_ref[0, rows, h * dk:(h + 1) * dk].astype(F32), cos, sin)
            k = rot(k_ref[0, rows, h * dk:(h + 1) * dk].astype(F32), cos, sin) * (dk ** -0.5)
            v = v_ref[0, rows, h * dv:(h + 1) * dv]
            qb = q.astype(BF16)
            sc = _dot_nt(qb, k.astype(BF16)) * inner
            st = state_ref[h]
            o = _dot(sc.astype(BF16), v) + _dot(qb, st.astype(BF16)) * q_decay
            kd_t = jnp.transpose(k * k_decay).astype(BF16)
            state_ref[h] = st * chunk_decay + _dot(kd_t, v)
            gn = gn_ref[:, h * dv:(h + 1) * dv]
            on = _rms(o, gn)
            gate = g_ref[0, rows, h * dv:(h + 1) * dv].astype(F32)
            o_ref[0, rows, h * dv:(h + 1) * dv] = (_silu(gate) * on).astype(o_ref.dtype)
        return carry

    lax.fori_loop(0, ts // c_len, chunk, 0)


def _retention_core(proj, cos, sin, gn_gain, *, b, s, ts=TOKEN_TILE):
    n = proj.shape[2]
    hk = n // 6
    hv = 2 * hk
    dk = hk // RET_HEADS
    return pl.pallas_call(
        functools.partial(_ret_body, ts=ts),
        grid=(b, s // ts),
        in_specs=[
            pl.BlockSpec((1, ts, hk), lambda i, j: (i, j, 0)),
            pl.BlockSpec((1, ts, hk), lambda i, j: (i, j, 1)),
            pl.BlockSpec((1, ts, hv), lambda i, j: (i, j, 1)),
            pl.BlockSpec((1, ts, hv), lambda i, j: (i, j, 2)),
            pl.BlockSpec((ts, dk // 2), lambda i, j: (j, 0)),
            pl.BlockSpec((ts, dk // 2), lambda i, j: (j, 0)),
            pl.BlockSpec((1, hv), lambda i, j: (0, 0)),
        ],
        out_specs=pl.BlockSpec((1, ts, hv), lambda i, j: (i, j, 0)),
        out_shape=jax.ShapeDtypeStruct((b, s, hv), BF16),
        scratch_shapes=[pltpu.VMEM((RET_HEADS, dk, hv // RET_HEADS), F32)],
        compiler_params=_params("parallel", "arbitrary"),
        name="retention",
    )(proj, proj, proj, proj, cos, sin, gn_gain.reshape(1, hv))


def _rope_tables(s, half):
    inv = ROPE_BASE ** (-jnp.arange(half, dtype=F32) / half)
    ang = jnp.arange(s, dtype=F32)[:, None] * inv[None, :]
    return jnp.cos(ang), jnp.sin(ang)


def _retention(h2, g_norm, w_in, gn_gain, w_out, *, b, s):
    (proj,) = _norm_proj(h2, g_norm, [w_in.astype(BF16)], [BF16])
    n = proj.shape[1]
    cos, sin = _rope_tables(s, n // 6 // RET_HEADS // 2)
    y = _retention_core(proj.reshape(b, s, n), cos, sin, gn_gain, b=b, s=s)
    return _out_proj(h2, y.reshape(b * s, -1), w_out.astype(BF16))


def _dil_proj_body(h_ref, g_ref, w_ref, o_ref, xn_ref, *, r):
    tm = h_ref.shape[0]
    n = tm // r
    xn = _rms(h_ref[...], g_ref[...])
    slabs = xn_ref.shape[0]
    for k in range(slabs):
        xn_ref[k] = xn[:, k * LANES:(k + 1) * LANES]
    xp = jnp.concatenate(
        [jnp.concatenate([xn_ref[k, pl.ds(c, n, stride=r), :] for c in range(r)], axis=0) for k in range(slabs)],
        axis=1).astype(BF16)
    nc = _col_chunk(w_ref.shape[1])
    for j in range(w_ref.shape[1] // nc):
        sl = slice(j * nc, (j + 1) * nc)
        res = _dot(xp, w_ref[:, sl]).astype(o_ref.dtype)
        for c in range(r):
            o_ref[0, c, :, sl] = res[c * n:(c + 1) * n, :]


def _dil_proj(h2, g_norm, w, r, *, b, s, tm=TOKEN_TILE):
    d = h2.shape[1]
    n_out = w.shape[1]
    tiles = s // tm
    return pl.pallas_call(
        functools.partial(_dil_proj_body, r=r),
        grid=(b * tiles,),
        in_specs=[
            pl.BlockSpec((tm, d), lambda i: (i, 0)),
            pl.BlockSpec((1, d), lambda i: (0, 0)),
            pl.BlockSpec((d, n_out), lambda i: (0, 0)),
        ],
        out_specs=pl.BlockSpec((1, r, tm // r, n_out), lambda i: (i // tiles, 0, i % tiles, 0)),
        out_shape=jax.ShapeDtypeStruct((b, r, s // r, n_out), BF16),
        scratch_shapes=[pltpu.VMEM((d // LANES, tm, LANES), F32)],
        compiler_params=_params("parallel"),
        name=f"dilated_proj_r{r}",
    )(h2, g_norm.reshape(1, d), w)


def _dil_body(q_ref, kc_ref, kp_ref, vc_ref, vp_ref, o_ref, lse_ref, kbuf, vbuf, *, rows, n_back):
    qb = DIL_QB
    dh = DIL_DH
    first_step = pl.program_id(2) == 0
    kbuf[0:qb, :] = kp_ref[0, 0]
    kbuf[qb:, :] = kc_ref[0, 0]
    vbuf[0:qb, :] = vp_ref[0, 0]
    vbuf[qb:, :] = vc_ref[0, 0]
    qi = lax.broadcasted_iota(jnp.int32, (qb, 2 * qb), 0)
    kj = lax.broadcasted_iota(jnp.int32, (qb, 2 * qb), 1)
    dist = qi + qb - kj
    band = (dist >= 0) & (dist <= n_back)
    lane = lax.broadcasted_iota(jnp.int32, (qb, LANES), 1)
    scale = dh ** -0.5
    for i in range(rows // qb):
        mask = band & ((kj >= qb) | jnp.logical_not(first_step)) if i == 0 else band
        lse_tile = jnp.zeros((qb, LANES), F32)
        for h in range(DIL_HEADS):
            cols = slice(h * dh, (h + 1) * dh)
            q = q_ref[0, 0, i * qb:(i + 1) * qb, cols]
            k = kbuf[i * qb:(i + 2) * qb, cols]
            v = vbuf[i * qb:(i + 2) * qb, cols]
            s = jnp.where(mask, _dot_nt(q, k) * scale, NEG_INF)
            m = jnp.max(s, axis=1, keepdims=True)
            e = jnp.where(mask, jnp.exp(s - m), 0.0)
            den = jnp.maximum(jnp.sum(e, axis=1, keepdims=True), 1e-30)
            o = _dot((e / den).astype(BF16), v)
            o_ref[0, 0, i * qb:(i + 1) * qb, cols] = o.astype(o_ref.dtype)
            lse_tile = jnp.where(lane == h, m + jnp.log(den), lse_tile)
        lse_ref[0, 0, i * qb:(i + 1) * qb, :] = lse_tile


def _dilated_group(proj, win, r, *, b, s):
    hd = DIL_HEADS * DIL_DH
    length = s // r
    rows = min(length, TOKEN_TILE)
    qb = DIL_QB

    def cur(which):
        return pl.BlockSpec((1, 1, rows, hd), lambda i, c, n: (i, c, n, which))

    def prev(which):
        return pl.BlockSpec((1, 1, qb, hd), lambda i, c, n: (i, c, jnp.maximum(n * (rows // qb) - 1, 0), which))

    return pl.pallas_call(
        functools.partial(_dil_body, rows=rows, n_back=win // r),
        grid=(b, r, length // rows),
        in_specs=[cur(0), cur(1), prev(1), cur(2), prev(2)],
        out_specs=[
            pl.BlockSpec((1, 1, rows, hd), lambda i, c, n: (i, c, n, 0)),
            pl.BlockSpec((1, 1, rows, LANES), lambda i, c, n: (i, c, n, 0)),
        ],
        out_shape=[
            jax.ShapeDtypeStruct((b, r, length, hd), BF16),
            jax.ShapeDtypeStruct((b, r, length, LANES), F32),
        ],
        scratch_shapes=[pltpu.VMEM((rows + qb, hd), BF16), pltpu.VMEM((rows + qb, hd), BF16)],
        compiler_params=_params("parallel", "parallel", "arbitrary"),
        name=f"dilated_r{r}",
    )(proj, proj, proj, proj, proj)


def _dil_merge_body(h_ref, *refs, dilations):
    ng = len(dilations)
    o_refs, l_refs = refs[:ng], refs[ng:2 * ng]
    w_ref, out_ref = refs[2 * ng], refs[2 * ng + 1]
    o_bufs, l_bufs = refs[2 * ng + 2:3 * ng + 2], refs[3 * ng + 2:]
    tm = h_ref.shape[1]
    for o_ref, l_ref, o_buf, l_buf, r in zip(o_refs, l_refs, o_bufs, l_bufs, dilations):
        n = tm // r
        for c in range(r):
            rows = pl.ds(c, n, stride=r)
            l_buf[rows, :] = l_ref[0, c]
            blk = o_ref[0, c].astype(F32)
            for h in range(DIL_HEADS):
                o_buf[h, rows, :] = blk[:, h * DIL_DH:(h + 1) * DIL_DH]
    lses = [l[...] for l in l_bufs]
    mx = functools.reduce(jnp.maximum, lses)
    es = [jnp.exp(l - mx) for l in lses]
    tot = functools.reduce(jnp.add, es)
    wts = [e / tot for e in es]
    dh = DIL_DH
    parts = []
    for h in range(DIL_HEADS):
        acc = jnp.zeros((tm, dh), F32)
        for g in range(ng):
            wg = jnp.broadcast_to(wts[g][:, h:h + 1], (tm, dh))
            acc = acc + wg * o_bufs[g][h]
        parts.append(acc)
    o = jnp.concatenate(parts, axis=1).astype(BF16)
    out_ref[0] = h_ref[0] + _dot(o, w_ref[...])


def _dil_merge(h3, outs, lses, w, dilations, *, tm=TOKEN_TILE):
    b, s, d = h3.shape
    hd = w.shape[0]
    res_major = lambda r, width: pl.BlockSpec((1, r, tm // r, width), lambda i, n: (i, 0, n, 0))
    return pl.pallas_call(
        functools.partial(_dil_merge_body, dilations=tuple(dilations)),
        grid=(b, s // tm),
        in_specs=[pl.BlockSpec((1, tm, d), lambda i, n: (i, n, 0))]
        + [res_major(r, hd) for r in dilations] + [res_major(r, LANES) for r in dilations]
        + [pl.BlockSpec((hd, d), lambda i, n: (0, 0))],
        out_specs=pl.BlockSpec((1, tm, d), lambda i, n: (i, n, 0)),
        out_shape=jax.ShapeDtypeStruct((b, s, d), F32),
        scratch_shapes=[pltpu.VMEM((DIL_HEADS, tm, DIL_DH), F32) for _ in dilations]
        + [pltpu.VMEM((tm, LANES), F32) for _ in dilations],
        compiler_params=_params("parallel", "parallel"),
        name="dilated_merge",
    )(h3, *outs, *lses, w)


def _dilated(h2, g_norm, w_in, w_out, *, b, s):
    d = h2.shape[1]
    gw = 3 * DIL_HEADS * DIL_DH
    wb = w_in.astype(BF16)
    outs, lses, dilations = [], [], []
    for g, (win, r) in enumerate(DIL_PATTERN):
        proj = _dil_proj(h2, g_norm, wb[:, g * gw:(g + 1) * gw], r, b=b, s=s)
        o, lse = _dilated_group(proj, win, r, b=b, s=s)
        outs.append(o)
        lses.append(lse)
        dilations.append(r)
    return _dil_merge(h2.reshape(b, s, d), outs, lses, w_out.astype(BF16), dilations).reshape(b * s, d)


def _softplus(x):
    return jnp.maximum(x, 0.0) + jnp.log1p(jnp.exp(-jnp.abs(x)))


def _ssd_body(z_ref, x_ref, dt_ref, cw_ref, cb_ref, dtb_ref, alog_ref, dsk_ref, ng_ref, o_ref,
              xpad, state_ref):
    ln = SSD_CHUNK
    nst = SSD_STATE
    gw = SSD_HPG * SSD_HEADDIM
    d_inner = SSD_GROUPS * gw
    halo = 8

    @pl.when(pl.program_id(1) == 0)
    def _():
        xpad[0:halo, :] = jnp.zeros((halo, xpad.shape[1]), F32)
        state_ref[...] = jnp.zeros(state_ref.shape, F32)

    xpad[halo:halo + ln, :] = x_ref[0].astype(F32)
    conv = cb_ref[...]
    for k in range(SSD_CONV):
        off = halo - (SSD_CONV - 1) + k
        conv = conv + xpad[off:off + ln, :] * cw_ref[k:k + 1, :]
    xpad[0:halo, :] = xpad[ln:ln + halo, :]
    xbc = _silu(conv)
    xs = xbc[:, :d_inner]
    bm = xbc[:, d_inner:d_inner + SSD_GROUPS * nst]
    cm = xbc[:, d_inner + SSD_GROUPS * nst:]

    dt = _softplus(dt_ref[0] + dtb_ref[...])
    da = dt * (-jnp.exp(alog_ref[...]))
    ii = lax.broadcasted_iota(jnp.int32, (ln, ln), 0)
    jj = lax.broadcasted_iota(jnp.int32, (ln, ln), 1)
    causal = ii >= jj
    tril = jnp.where(causal, 1.0, 0.0).astype(BF16)
    da_terms = _split3(da)
    acs = _dot(tril, da_terms[0]) + _dot(tril, da_terms[1]) + _dot(tril, da_terms[2])
    acs_t = jnp.transpose(acs)
    dt_t = jnp.transpose(dt)

    erow = lax.broadcasted_iota(jnp.int32, (LANES, d_inner), 0)
    ecol = lax.broadcasted_iota(jnp.int32, (LANES, d_inner), 1)
    expand = jnp.where(ecol // SSD_HEADDIM == erow, 1.0, 0.0).astype(BF16)
    acs_e = _dot3(_split3(acs), expand)
    dt_e = _dot3(_split3(dt), expand)
    last = acs_e[ln - 1:ln, :]
    decay_in = jnp.exp(acs_e)
    xs_end = (xs * (jnp.exp(last - acs_e) * dt_e)).astype(BF16)
    xs_b = xs.astype(BF16)
    lane = lax.broadcasted_iota(jnp.int32, (ln, LANES), 1)

    y_groups = []
    for g in range(SSD_GROUPS):
        bm_g = bm[:, g * nst:(g + 1) * nst]
        cm_g = cm[:, g * nst:(g + 1) * nst].astype(BF16)
        cb = _dot_nt(cm_g, bm_g.astype(BF16))
        st = state_ref[g]
        gcols = slice(g * gw, (g + 1) * gw)
        y_state = _dot(cm_g, st.astype(BF16)) * decay_in[:, gcols]
        pairs = []
        for p in range(SSD_HPG // 2):
            pair_cols = slice(g * gw + p * LANES, g * gw + (p + 1) * LANES)
            halves = []
            for e in range(2):
                hd = g * SSD_HPG + 2 * p + e
                diff = acs[:, hd:hd + 1] - acs_t[hd:hd + 1, :]
                seg = jnp.where(causal, jnp.exp(jnp.where(causal, diff, 0.0)), 0.0)
                w = (cb * seg * dt_t[hd:hd + 1, :]).astype(BF16)
                halves.append(_dot(w, xs_b[:, pair_cols]))
            pairs.append(jnp.where(lane < SSD_HEADDIM, halves[0], halves[1]))
        y_groups.append(jnp.concatenate(pairs, axis=1) + y_state)
        bm_t = jnp.transpose(bm_g).astype(BF16)
        state_ref[g] = st * jnp.exp(last[:, gcols]) + _dot(bm_t, xs_end[:, gcols])

    y = jnp.concatenate(y_groups, axis=1) + dsk_ref[...] * xs
    yz = y * _silu(z_ref[0].astype(F32))
    outs = [_rms(yz[:, g * gw:(g + 1) * gw], ng_ref[:, g * gw:(g + 1) * gw]) for g in range(SSD_GROUPS)]
    o_ref[0] = jnp.concatenate(outs, axis=1).astype(o_ref.dtype)


def _ssd_core(z, xbc, dt, conv_w, conv_b, dt_bias, a_log, d_skip, norm_g, *, b, s):
    ln = SSD_CHUNK
    d_inner = z.shape[2]
    conv_dim = xbc.shape[2]
    gw = SSD_HPG * SSD_HEADDIM
    full = lambda shape: pl.BlockSpec(shape, lambda i, j: (0,) * len(shape))
    return pl.pallas_call(
        _ssd_body,
        grid=(b, s // ln),
        in_specs=[
            pl.BlockSpec((1, ln, d_inner), lambda i, j: (i, j, 0)),
            pl.BlockSpec((1, ln, conv_dim), lambda i, j: (i, j, 0)),
            pl.BlockSpec((1, ln, LANES), lambda i, j: (i, j, 0)),
            full((SSD_CONV, conv_dim)),
            full((1, conv_dim)),
            full((1, LANES)),
            full((1, LANES)),
            full((1, d_inner)),
            full((1, d_inner)),
        ],
        out_specs=pl.BlockSpec((1, ln, d_inner), lambda i, j: (i, j, 0)),
        out_shape=jax.ShapeDtypeStruct((b, s, d_inner), BF16),
        scratch_shapes=[pltpu.VMEM((ln + 8, conv_dim), F32),
                        pltpu.VMEM((SSD_GROUPS, SSD_STATE, gw), F32)],
        compiler_params=_params("parallel", "arbitrary"),
        name="ssd",
    )(z, xbc, dt, conv_w, conv_b, dt_bias, a_log, d_skip, norm_g)


def _ssd(h2, g_norm, w_in, conv_w, conv_b, dt_bias, a_log, d_skip, norm_g, w_out, *, b, s):
    heads = SSD_GROUPS * SSD_HPG
    d_inner = heads * SSD_HEADDIM
    conv_dim = conv_w.shape[1]
    wb = w_in.astype(BF16)
    w_dt = jnp.pad(wb[:, d_inner + conv_dim:], ((0, 0), (0, LANES - heads)))
    z, xbc, dt = _norm_proj(h2, g_norm, [wb[:, :d_inner], wb[:, d_inner:d_inner + conv_dim], w_dt], [BF16, BF16, F32])
    pad_heads = lambda v: jnp.pad(v.astype(F32), (0, LANES - heads)).reshape(1, LANES)
    y = _ssd_core(z.reshape(b, s, -1), xbc.reshape(b, s, -1), dt.reshape(b, s, LANES),
                  conv_w.astype(F32), conv_b.reshape(1, -1).astype(F32), pad_heads(dt_bias), pad_heads(a_log),
                  jnp.repeat(d_skip.astype(F32), SSD_HEADDIM).reshape(1, d_inner), norm_g.reshape(1, d_inner),
                  b=b, s=s)
    return _out_proj(h2, y.reshape(b * s, d_inner), w_out.astype(BF16))


def _nsa_proj_body(h_ref, g_ref, wn_ref, wt_ref, k0_ref, v0_ref, k12_ref, tr_ref):
    xn = _rms(h_ref[...], g_ref[...]).astype(BF16)
    nat = _dot(xn, wn_ref[...])
    gd = k0_ref.shape[1]
    k0_ref[...] = nat[:, :gd].astype(k0_ref.dtype)
    v0_ref[...] = nat[:, gd:2 * gd].astype(v0_ref.dtype)
    k12_ref[...] = nat[:, 2 * gd:].astype(k12_ref.dtype)
    res = _dot_nt(wt_ref[...], xn)
    for j in range(tr_ref.shape[0]):
        tr_ref[j] = res[:, j * NSA_QT:(j + 1) * NSA_QT].astype(tr_ref.dtype)


def _nsa_proj(h2, g_norm, w_nat, w_t, *, tm=TOKEN_TILE):
    m, d = h2.shape
    gd = NSA_GROUPS * NSA_DH
    nt = w_t.shape[0]
    slabs = tm // NSA_QT
    return pl.pallas_call(
        _nsa_proj_body,
        grid=(m // tm,),
        in_specs=[
            pl.BlockSpec((tm, d), lambda i: (i, 0)),
            pl.BlockSpec((1, d), lambda i: (0, 0)),
            pl.BlockSpec(w_nat.shape, lambda i: (0, 0)),
            pl.BlockSpec(w_t.shape, lambda i: (0, 0)),
        ],
        out_specs=[
            pl.BlockSpec((tm, gd), lambda i: (i, 0)),
            pl.BlockSpec((tm, gd), lambda i: (i, 0)),
            pl.BlockSpec((tm, 2 * gd), lambda i: (i, 0)),
            pl.BlockSpec((slabs, nt, NSA_QT), lambda i: (i, 0, 0)),
        ],
        out_shape=[
            jax.ShapeDtypeStruct((m, gd), BF16),
            jax.ShapeDtypeStruct((m, gd), BF16),
            jax.ShapeDtypeStruct((m, 2 * gd), BF16),
            jax.ShapeDtypeStruct((m // NSA_QT, nt, NSA_QT), BF16),
        ],
        compiler_params=_params("parallel"),
        name="nsa_proj",
    )(h2, g_norm.reshape(1, d), w_nat, w_t)


def _nsa_cmp_body(x_ref, pa_ref, pb_ref, w1a_ref, w1b_ref, w2_ref, o_ref, *, transposed):
    x = x_ref[0].astype(F32)
    nb = x.shape[0]
    ya = _dot((x + pa_ref[...]).astype(BF16), w1a_ref[...])
    yb = _dot((x + pb_ref[...]).astype(BF16), w1b_ref[...])
    hid = _silu(ya + pltpu.roll(yb, nb - 1, 0))
    out = _dot(hid.astype(BF16), w2_ref[...])
    o_ref[0] = (jnp.transpose(out) if transposed else out).astype(o_ref.dtype)


def _nsa_compress(x, pos, w1, w2, *, transposed):
    b, nb, width = x.shape
    g, dh = NSA_GROUPS, NSA_DH
    half = NSA_CMP_LEN // 2
    hid = w1.shape[1]
    eye = jnp.eye(g, dtype=F32)
    w1r = w1.reshape(NSA_CMP_LEN, dh, hid)
    big = jnp.einsum("ldj,gh->lgdhj", w1r, eye).reshape(NSA_CMP_LEN, g * dh, g * hid)
    w1a = big[:half].reshape(half * g * dh, g * hid).astype(BF16)
    w1b = big[half:].reshape(half * g * dh, g * hid).astype(BF16)
    w2big = jnp.einsum("jd,gh->gjhd", w2, eye).reshape(g * hid, g * dh).astype(BF16)
    posb = jnp.broadcast_to(pos[:, None, :], (NSA_CMP_LEN, g, dh)).reshape(NSA_CMP_LEN, g * dh)
    pa = posb[:half].reshape(1, width).astype(F32)
    pb = posb[half:].reshape(1, width).astype(F32)
    full = lambda shape: pl.BlockSpec(shape, lambda i: (0,) * len(shape))
    out_dims = (g * dh, nb) if transposed else (nb, g * dh)
    return pl.pallas_call(
        functools.partial(_nsa_cmp_body, transposed=transposed),
        grid=(b,),
        in_specs=[
            pl.BlockSpec((1, nb, width), lambda i: (i, 0, 0)),
            full((1, width)), full((1, width)),
            full((width, g * hid)), full((width, g * hid)), full((g * hid, g * dh)),
        ],
        out_specs=pl.BlockSpec((1,) + out_dims, lambda i: (i, 0, 0)),
        out_shape=jax.ShapeDtypeStruct((b,) + out_dims, BF16),
        compiler_params=_params("parallel"),
        name="nsa_compress",
    )(x, pa, pb, w1a, w1b, w2big)


def _nsa_body(q_ref, gt_ref, kc_ref, vct_ref, k1_ref, k2_ref, v1_ref, v2_ref, o_ref, score_ref, bias_ref, *, s):
    qt, kc, dh = NSA_QT, NSA_KC, NSA_DH
    lanes = NSA_HPG * qt
    g = pl.program_id(1)
    j = pl.program_id(2)
    t0 = j * qt
    nb = kc_ref.shape[1]
    nsb = s // NSA_SEL_LEN
    n_sel = min(NSA_N_SEL, nsb)

    q_t = q_ref[0, 0]
    qcat = jnp.concatenate([q_t[h * dh:(h + 1) * dh, :] for h in range(NSA_HPG)], axis=1)
    odd = (g % 2) == 1
    zero = jnp.zeros_like(qcat)
    qp = jnp.concatenate([jnp.where(odd, zero, qcat), jnp.where(odd, qcat, zero)], axis=0)
    qlane = lax.broadcasted_iota(jnp.int32, (1, lanes), 1) % qt
    tq = t0 + qlane

    sc = _dot(kc_ref[0], qp)
    n_idx = lax.broadcasted_iota(jnp.int32, (nb, lanes), 0)
    cmask = (n_idx * NSA_CMP_STRIDE + NSA_CMP_LEN - 1 <= tq) & (n_idx < nb - 1)
    sc = jnp.where(cmask, sc, NEG_INF)
    e = jnp.where(cmask, jnp.exp(sc - jnp.max(sc, axis=0, keepdims=True)), 0.0)
    p_c = e / jnp.maximum(jnp.sum(e, axis=0, keepdims=True), 1e-30)
    o_c = _dot(vct_ref[0], p_c.astype(BF16))
    imp = p_c[:, 0:qt]
    for h in range(1, NSA_HPG):
        imp = imp + p_c[:, h * qt:(h + 1) * qt]

    ratio = NSA_SEL_LEN // NSA_CMP_STRIDE
    dd = lax.broadcasted_iota(jnp.int32, (nsb, nb), 1) - ratio * lax.broadcasted_iota(jnp.int32, (nsb, nb), 0)
    wsel = jnp.where((dd == -1) | (dd == ratio - 1), 1.0,
                     jnp.where((dd >= 0) & (dd < ratio - 1), 2.0, 0.0)).astype(BF16)
    terms = _split3(imp)
    imp_sel = _dot(wsel, terms[0]) + _dot(wsel, terms[1]) + _dot(wsel, terms[2])
    jq = lax.broadcasted_iota(jnp.int32, (nsb, qt), 0)
    cur = (t0 + lax.broadcasted_iota(jnp.int32, (nsb, qt), 1)) // NSA_SEL_LEN
    forced = (jq == 0) | (jq == cur) | (jq == cur - 1)
    score = jnp.where(jq <= cur, imp_sel + jnp.where(forced, NSA_FORCE_BONUS, 0.0), NEG_INF)
    score_ref[...] = score

    def rank_step(jp, rank):
        row = jnp.broadcast_to(score_ref[pl.ds(jp, 1), :], (nsb, qt))
        before = (row > score) | ((row == score) & (jq > jp))
        return rank + jnp.where(before, 1.0, 0.0)

    n_visible = (t0 + qt - 1) // NSA_SEL_LEN + 1
    rank = lax.fori_loop(0, n_visible, rank_step, jnp.zeros((nsb, qt), F32))
    sel_bias = jnp.where((rank < n_sel) & (score > 0.5 * NEG_INF), 0.0, NEG_INF)
    bias_ref[...] = jnp.concatenate([sel_bias] * NSA_HPG, axis=1)

    def online(carry, sts, pvs):
        m, l, acc = carry
        m_new = m
        for st in sts:
            m_new = jnp.maximum(m_new, jnp.max(st, axis=0, keepdims=True))
        alpha = jnp.exp(m - m_new)
        l = alpha * l
        acc = alpha * acc
        for st, pv in zip(sts, pvs):
            p = jnp.exp(st - m_new)
            l = l + jnp.sum(p, axis=0, keepdims=True)
            acc = acc + pv(p.astype(BF16))
        return m_new, l, acc

    init = (jnp.full((1, lanes), NEG_INF, F32), jnp.zeros((1, lanes), F32), jnp.zeros((dh, lanes), F32))

    blocks_per_chunk = kc // NSA_SEL_LEN
    slabs_per_chunk = kc // qt

    def sel_scores(c):
        k = k1_ref[0, pl.ds(pl.multiple_of(c * kc, kc), kc), :]
        bias = bias_ref[pl.ds(pl.multiple_of(c * blocks_per_chunk, blocks_per_chunk), blocks_per_chunk), :]
        rows = [jnp.broadcast_to(bias[r:r + 1, :], (NSA_SEL_LEN, lanes)) for r in range(blocks_per_chunk)]
        return _dot(k, qp) + jnp.concatenate(rows, axis=0)

    def sel_pv(c):
        def pv(p):
            out = _dot(v1_ref[0, c * slabs_per_chunk], p[0:qt, :])
            for i in range(1, slabs_per_chunk):
                out = out + _dot(v1_ref[0, c * slabs_per_chunk + i], p[i * qt:(i + 1) * qt, :])
            return out
        return pv

    c_hi = t0 // kc
    row_k = lax.broadcasted_iota(jnp.int32, (kc, lanes), 0)
    st = jnp.where(row_k <= (t0 - c_hi * kc) + qlane, sel_scores(c_hi), NEG_INF)
    carry = online(init, [st], [sel_pv(c_hi)])
    carry = lax.cond(c_hi % 2 == 1,
                     lambda cr: online(cr, [sel_scores(c_hi - 1)], [sel_pv(c_hi - 1)]),
                     lambda cr: cr, carry)
    _, l_s, acc_s = lax.fori_loop(
        0, c_hi // 2,
        lambda i, cr: online(cr, [sel_scores(2 * i), sel_scores(2 * i + 1)], [sel_pv(2 * i), sel_pv(2 * i + 1)]),
        carry)
    o_s = acc_s / jnp.maximum(l_s, 1e-30)

    wslabs = NSA_WINDOW // qt + 1
    jb0 = jnp.maximum(j + 1 - wslabs, 0)
    kw = k2_ref[0, pl.ds(pl.multiple_of(jb0 * qt, qt), wslabs * qt), :]
    kpos = jb0 * qt + lax.broadcasted_iota(jnp.int32, (wslabs * qt, lanes), 0)
    sw = jnp.where((kpos <= tq) & (kpos > tq - NSA_WINDOW), _dot(kw, qp), NEG_INF)
    pw = jnp.exp(sw - jnp.max(sw, axis=0, keepdims=True))
    l_w = jnp.sum(pw, axis=0, keepdims=True)
    pw = pw.astype(BF16)
    acc_w = _dot(v2_ref[0, jb0], pw[0:qt, :])
    for i in range(1, wslabs):
        acc_w = acc_w + _dot(v2_ref[0, jb0 + i], pw[i * qt:(i + 1) * qt, :])
    o_w = acc_w / jnp.maximum(l_w, 1e-30)

    gates = jax.nn.sigmoid(gt_ref[0, 0].astype(F32))

    def gate_row(br):
        return jnp.concatenate([gates[br * NSA_HPG + h:br * NSA_HPG + h + 1, :] for h in range(NSA_HPG)], axis=1)

    o = gate_row(0) * o_c + gate_row(1) * o_s + gate_row(2) * o_w
    o_hd = jnp.concatenate([o[:, h * qt:(h + 1) * qt] for h in range(NSA_HPG)], axis=0)
    o_ref[0] = jnp.transpose(o_hd).astype(o_ref.dtype)


def _nsa_attend(tr, k_cmp, v_cmp_t, k12, *, b, s):
    qt, dh = NSA_QT, NSA_DH
    gd = NSA_GROUPS * dh
    hq = NSA_GROUPS * NSA_HPG * dh
    nslab = s // qt
    nb = k_cmp.shape[1]
    tr4 = tr.reshape(b, nslab, tr.shape[1], qt)
    gate_rows = 4 * NSA_HPG
    v1_blk = hq // dh
    v2_blk = (hq + gd) // dh
    gate_blk = (hq + 2 * gd) // gate_rows
    nsb = s // NSA_SEL_LEN
    return pl.pallas_call(
        functools.partial(_nsa_body, s=s),
        grid=(b, NSA_GROUPS, nslab),
        in_specs=[
            pl.BlockSpec((1, 1, NSA_HPG * dh, qt), lambda i, g, j: (i, j, g, 0)),
            pl.BlockSpec((1, 1, gate_rows, qt), lambda i, g, j: (i, j, gate_blk + g, 0)),
            pl.BlockSpec((1, nb, 2 * dh), lambda i, g, j: (i, 0, g // 2)),
            pl.BlockSpec((1, dh, nb), lambda i, g, j: (i, g, 0)),
            pl.BlockSpec((1, s, 2 * dh), lambda i, g, j: (i, 0, g // 2)),
            pl.BlockSpec((1, s, 2 * dh), lambda i, g, j: (i, 0, NSA_GROUPS // 2 + g // 2)),
            pl.BlockSpec((1, nslab, dh, qt), lambda i, g, j: (i, 0, v1_blk + g, 0)),
            pl.BlockSpec((1, nslab, dh, qt), lambda i, g, j: (i, 0, v2_blk + g, 0)),
        ],
        out_specs=pl.BlockSpec((1, qt, NSA_HPG * dh), lambda i, g, j: (i, j, g)),
        out_shape=jax.ShapeDtypeStruct((b, s, hq), BF16),
        scratch_shapes=[pltpu.VMEM((nsb, qt), F32), pltpu.VMEM((nsb, NSA_HPG * qt), F32)],
        compiler_params=_params("parallel", "parallel", "arbitrary"),
        name="nsa_attend",
    )(tr4, tr4, k_cmp, v_cmp_t, k12, k12, tr4, tr4)


def _nsa(h2, g_norm, w_in, cmp_pos, cmp_w1, cmp_w2, w_out, *, b, s):
    g, hpg, dh = NSA_GROUPS, NSA_HPG, NSA_DH
    hq, gd = g * hpg * dh, g * dh
    kv_w = lambda br, kv: w_in[:, hq + (2 * br + kv) * gd:hq + (2 * br + kv + 1) * gd]
    w_q = w_in[:, :hq] * (dh ** -0.5)
    w_g = w_in[:, hq + 6 * gd:].reshape(-1, 3, g, hpg).transpose(2, 1, 3, 0)
    w_g = jnp.pad(w_g, ((0, 0), (0, 1), (0, 0), (0, 0))).reshape(g * 4 * hpg, -1)
    w_t = jnp.concatenate([w_q.T, kv_w(1, 1).T, kv_w(2, 1).T, w_g], axis=0).astype(BF16)
    w_nat = jnp.concatenate([kv_w(0, 0), kv_w(0, 1), kv_w(1, 0), kv_w(2, 0)], axis=1).astype(BF16)
    k0, v0, k12, tr = _nsa_proj(h2, g_norm, w_nat, w_t)
    nb = s // NSA_CMP_STRIDE
    k_cmp = _nsa_compress(k0.reshape(b, nb, NSA_CMP_STRIDE * gd), cmp_pos[0], cmp_w1[0], cmp_w2[0], transposed=False)
    v_cmp_t = _nsa_compress(v0.reshape(b, nb, NSA_CMP_STRIDE * gd), cmp_pos[1], cmp_w1[1], cmp_w2[1], transposed=True)
    y = _nsa_attend(tr, k_cmp, v_cmp_t, k12.reshape(b, s, 2 * gd), b=b, s=s)
    return _out_proj(h2, y.reshape(b * s, hq), w_out.astype(BF16))


def kernel(x, norm_ffn1, ffn1_w_in, ffn1_w_out, norm_mix, norm_ffn2, ffn2_w_in, ffn2_w_out, norm_final,
           ret_w_in, ret_gn_gain, ret_w_out,
           nsa_w_in, nsa_cmp_pos, nsa_cmp_w1, nsa_cmp_w2, nsa_w_out,
           ssd_w_in, ssd_conv_w, ssd_conv_b, ssd_dt_bias, ssd_a_log, ssd_d, ssd_norm, ssd_w_out,
           dil_w_in, dil_w_out):
    b, s, d = x.shape
    depth = norm_mix.shape[0]
    h = x.reshape(b * s, d)
    for i in range(depth):
        h = _ffn(h, norm_ffn1[i], ffn1_w_in[i].astype(BF16), ffn1_w_out[i].astype(BF16))
        m, j = i % 4, i // 4
        if m == 0:
            h = _retention(h, norm_mix[i], ret_w_in[j], ret_gn_gain[j], ret_w_out[j], b=b, s=s)
        elif m == 1:
            h = _nsa(h, norm_mix[i], nsa_w_in[j], nsa_cmp_pos[j], nsa_cmp_w1[j], nsa_cmp_w2[j], nsa_w_out[j], b=b, s=s)
        elif m == 2:
            h = _ssd(h, norm_mix[i], ssd_w_in[j], ssd_conv_w[j], ssd_conv_b[j], ssd_dt_bias[j], ssd_a_log[j],
                     ssd_d[j], ssd_norm[j], ssd_w_out[j], b=b, s=s)
        else:
            h = _dilated(h, norm_mix[i], dil_w_in[j], dil_w_out[j], b=b, s=s)
        h = _ffn(h, norm_ffn2[i], ffn2_w_in[i].astype(BF16), ffn2_w_out[i].astype(BF16),
                 norm_final if i == depth - 1 else None)
    return h.reshape(b, s, d)
```

```python
import functools
import math

import jax
import jax.numpy as jnp
from jax import lax
from jax.experimental import pallas as pl
from jax.experimental.pallas import tpu as pltpu

F32 = jnp.float32
BF16 = jnp.bfloat16
NORM_EPS = 1e-6
NEG_INF = -1e30
ROPE_BASE = 10000.0
VMEM_LIMIT_BYTES = 56 * 1024 * 1024
LANES = 128
TOKEN_TILE = 512
MAX_DOT_COLS = 2048

RET_HEADS = 4
RET_CHUNK = 128

NSA_GROUPS = 4
NSA_HPG = 4
NSA_DH = 64
NSA_CMP_LEN = 32
NSA_CMP_STRIDE = 16
NSA_SEL_LEN = 64
NSA_N_SEL = 16
NSA_WINDOW = 512
NSA_FORCE_BONUS = 1e4
NSA_QT = 128
NSA_KC = 512

SSD_GROUPS = 4
SSD_HPG = 8
SSD_HEADDIM = 64
SSD_STATE = 128
SSD_CONV = 4
SSD_CHUNK = 256

DIL_PATTERN = ((128, 1), (512, 4), (2048, 16))
DIL_HEADS = 8
DIL_DH = 128
DIL_QB = 128


def _params(*sem):
    return pltpu.CompilerParams(dimension_semantics=sem, vmem_limit_bytes=VMEM_LIMIT_BYTES)


def _rms(x, g):
    return x * lax.rsqrt(jnp.mean(x * x, axis=-1, keepdims=True) + NORM_EPS) * g


def _silu(x):
    return x * jax.nn.sigmoid(x)


def _dot(a, b):
    return jnp.dot(a, b, preferred_element_type=F32)


def _dot_nt(a, b):
    return lax.dot_general(a, b, (((1,), (1,)), ((), ())), preferred_element_type=F32)


def _split3(x):
    hi = x.astype(BF16)
    r1 = x - hi.astype(F32)
    mid = r1.astype(BF16)
    lo = (r1 - mid.astype(F32)).astype(BF16)
    return hi, mid, lo


def _dot3(terms, rhs):
    return _dot(terms[0], rhs) + _dot(terms[1], rhs) + _dot(terms[2], rhs)


def _col_chunk(n):
    if n <= MAX_DOT_COLS:
        return n
    return max(c for c in range(LANES, MAX_DOT_COLS + 1, LANES) if n % c == 0)


def _resident(shape, index=None):
    index = index if index is not None else (0,) * len(shape)
    return pl.BlockSpec(shape, lambda *_: index, pipeline_mode=pl.Buffered(1))


def _ffn_body(h_ref, *refs, fc, final_norm, mix):
    x = h_ref[...]
    if mix:
        y_ref, wm_ref, *refs = refs
        x = x + _dot(y_ref[...], wm_ref[...])
    g_ref, wa_ref, wb_ref, wo_ref, gf_ref, o_ref = refs
    xn = _rms(x, g_ref[...]).astype(BF16)
    acc = jnp.zeros(x.shape, F32)
    for j in range(wa_ref.shape[1] // fc):
        sl = slice(j * fc, (j + 1) * fc)
        a = _dot(xn, wa_ref[:, sl])
        b = _dot(xn, wb_ref[:, sl])
        acc = acc + _dot((_silu(a) * b).astype(BF16), wo_ref[sl, :])
    y = x + 0.5 * acc
    if final_norm:
        y = _rms(y, gf_ref[...])
    o_ref[...] = y


def _ffn(h, g, w_in, w_out, g_final=None, mix=None, *, tm=TOKEN_TILE):
    m, d = h.shape
    f = w_out.shape[0]
    final_norm = g_final is not None
    gf = g_final if final_norm else g
    mix_args = list(mix) if mix is not None else []
    mix_specs = [pl.BlockSpec((tm, mix[0].shape[1]), lambda i: (i, 0)), _resident(mix[1].shape)] if mix_args else []
    return pl.pallas_call(
        functools.partial(_ffn_body, fc=_col_chunk(f), final_norm=final_norm, mix=bool(mix_args)),
        grid=(m // tm,),
        in_specs=[pl.BlockSpec((tm, d), lambda i: (i, 0))] + mix_specs + [
            _resident((1, d)),
            _resident((d, f)),
            _resident((d, f), (0, 1)),
            _resident((f, d)),
            _resident((1, d)),
        ],
        out_specs=pl.BlockSpec((tm, d), lambda i: (i, 0)),
        out_shape=jax.ShapeDtypeStruct((m, d), F32),
        compiler_params=_params("parallel"),
        name="ffn",
    )(h, *mix_args, g.reshape(1, d), w_in, w_in, w_out, gf.reshape(1, d))


def _norm_proj_body(h_ref, g_ref, *refs, n_out):
    xn = _rms(h_ref[...], g_ref[...]).astype(BF16)
    for w_ref, o_ref in zip(refs[:n_out], refs[n_out:]):
        n = w_ref.shape[1]
        nc = _col_chunk(n)
        for j in range(n // nc):
            sl = slice(j * nc, (j + 1) * nc)
            o_ref[:, sl] = _dot(xn, w_ref[:, sl]).astype(o_ref.dtype)


def _norm_proj(h, g, ws, out_dtypes, *, tm=TOKEN_TILE):
    m, d = h.shape
    return pl.pallas_call(
        functools.partial(_norm_proj_body, n_out=len(ws)),
        grid=(m // tm,),
        in_specs=[pl.BlockSpec((tm, d), lambda i: (i, 0)), pl.BlockSpec((1, d), lambda i: (0, 0))]
        + [pl.BlockSpec(w.shape, lambda i: (0, 0)) for w in ws],
        out_specs=[pl.BlockSpec((tm, w.shape[1]), lambda i: (i, 0)) for w in ws],
        out_shape=[jax.ShapeDtypeStruct((m, w.shape[1]), dt) for w, dt in zip(ws, out_dtypes)],
        compiler_params=_params("parallel"),
        name="norm_proj",
    )(h, g.reshape(1, d), *ws)


def _out_proj_body(h_ref, y_ref, w_ref, o_ref):
    o_ref[...] = h_ref[...] + _dot(y_ref[...], w_ref[...])


def _out_proj(h, y, w, *, tm=TOKEN_TILE):
    m, d = h.shape
    k = y.shape[1]
    return pl.pallas_call(
        _out_proj_body,
        grid=(m // tm,),
        in_specs=[
            pl.BlockSpec((tm, d), lambda i: (i, 0)),
            pl.BlockSpec((tm, k), lambda i: (i, 0)),
            pl.BlockSpec((k, d), lambda i: (0, 0)),
        ],
        out_specs=pl.BlockSpec((tm, d), lambda i: (i, 0)),
        out_shape=jax.ShapeDtypeStruct((m, d), F32),
        compiler_params=_params("parallel"),
        name="out_proj",
    )(h, y, w)


def _ret_body(q_ref, k_ref, v_ref, g_ref, cos_ref, sin_ref, gn_ref, o_ref, state_ref, *, ts):
    c_len = RET_CHUNK
    dk = q_ref.shape[2] // RET_HEADS
    dv = v_ref.shape[2] // RET_HEADS
    half = dk // 2

    @pl.when(pl.program_id(1) == 0)
    def _():
        state_ref[...] = jnp.zeros(state_ref.shape, F32)

    ii = lax.broadcasted_iota(jnp.int32, (c_len, c_len), 0)
    jj = lax.broadcasted_iota(jnp.int32, (c_len, c_len), 1)
    rel = (ii - jj).astype(F32)
    causal = ii >= jj
    idx = lax.broadcasted_iota(jnp.int32, (c_len, 1), 0).astype(F32)

    def rot(t, cos, sin):
        t1, t2 = t[:, :half], t[:, half:]
        return jnp.concatenate([t1 * cos - t2 * sin, t1 * sin + t2 * cos], axis=1)

    def chunk(c, carry):
        r0 = pl.multiple_of(c * c_len, c_len)
        rows = pl.ds(r0, c_len)
        cos = cos_ref[rows, :]
        sin = sin_ref[rows, :]
        for h in range(RET_HEADS):
            log_gamma = math.log1p(-(2.0 ** (-5.0 - h)))
            inner = jnp.where(causal, jnp.exp(jnp.where(causal, rel, 0.0) * log_gamma), 0.0)
            q_decay = jnp.exp((idx + 1.0) * log_gamma)
            k_decay = jnp.exp((c_len - 1.0 - idx) * log_gamma)
            chunk_decay = math.exp(c_len * log_gamma)
            q = rot(q_ref[0, rows, h * dk:(h + 1) * dk].astype(F32), cos, sin)
            k = rot(k_ref[0, rows, h * dk:(h + 1) * dk].astype(F32), cos, sin) * (dk ** -0.5)
            v = v_ref[0, rows, h * dv:(h + 1) * dv]
            qb = q.astype(BF16)
            sc = _dot_nt(qb, k.astype(BF16)) * inner
            st = state_ref[h]
            o = _dot(sc.astype(BF16), v) + _dot(qb, st.astype(BF16)) * q_decay
            kd_t = jnp.transpose(k * k_decay).astype(BF16)
            state_ref[h] = st * chunk_decay + _dot(kd_t, v)
            gn = gn_ref[:, h * dv:(h + 1) * dv]
            on = _rms(o, gn)
            gate = g_ref[0, rows, h * dv:(h + 1) * dv].astype(F32)
            o_ref[0, rows, h * dv:(h + 1) * dv] = (_silu(gate) * on).astype(o_ref.dtype)
        return carry

    lax.fori_loop(0, ts // c_len, chunk, 0)


def _retention_core(proj, cos, sin, gn_gain, *, b, s, ts=TOKEN_TILE):
    n = proj.shape[2]
    hk = n // 6
    hv = 2 * hk
    dk = hk // RET_HEADS
    return pl.pallas_call(
        functools.partial(_ret_body, ts=ts),
        grid=(b, s // ts),
        in_specs=[
            pl.BlockSpec((1, ts, hk), lambda i, j: (i, j, 0)),
            pl.BlockSpec((1, ts, hk), lambda i, j: (i, j, 1)),
            pl.BlockSpec((1, ts, hv), lambda i, j: (i, j, 1)),
            pl.BlockSpec((1, ts, hv), lambda i, j: (i, j, 2)),
            pl.BlockSpec((ts, dk // 2), lambda i, j: (j, 0)),
            pl.BlockSpec((ts, dk // 2), lambda i, j: (j, 0)),
            pl.BlockSpec((1, hv), lambda i, j: (0, 0)),
        ],
        out_specs=pl.BlockSpec((1, ts, hv), lambda i, j: (i, j, 0)),
        out_shape=jax.ShapeDtypeStruct((b, s, hv), BF16),
        scratch_shapes=[pltpu.VMEM((RET_HEADS, dk, hv // RET_HEADS), F32)],
        compiler_params=_params("parallel", "arbitrary"),
        name="retention",
    )(proj, proj, proj, proj, cos, sin, gn_gain.reshape(1, hv))


def _rope_tables(s, half):
    inv = ROPE_BASE ** (-jnp.arange(half, dtype=F32) / half)
    ang = jnp.arange(s, dtype=F32)[:, None] * inv[None, :]
    return jnp.cos(ang), jnp.sin(ang)


def _retention(h2, g_norm, w_in, gn_gain, w_out, *, b, s):
    (proj,) = _norm_proj(h2, g_norm, [w_in.astype(BF16)], [BF16])
    n = proj.shape[1]
    cos, sin = _rope_tables(s, n // 6 // RET_HEADS // 2)
    y = _retention_core(proj.reshape(b, s, n), cos, sin, gn_gain, b=b, s=s)
    return y.reshape(b * s, -1), w_out.astype(BF16)


def _dil_proj_body(h_ref, g_ref, w_ref, o_ref, xn_ref, *, r):
    tm = h_ref.shape[0]
    n = tm // r
    xn = _rms(h_ref[...], g_ref[...])
    slabs = xn_ref.shape[0]
    for k in range(slabs):
        xn_ref[k] = xn[:, k * LANES:(k + 1) * LANES]
    xp = jnp.concatenate(
        [jnp.concatenate([xn_ref[k, pl.ds(c, n, stride=r), :] for c in range(r)], axis=0) for k in range(slabs)],
        axis=1).astype(BF16)
    nc = _col_chunk(w_ref.shape[1])
    for j in range(w_ref.shape[1] // nc):
        sl = slice(j * nc, (j + 1) * nc)
        res = _dot(xp, w_ref[:, sl]).astype(o_ref.dtype)
        for c in range(r):
            o_ref[0, c, :, sl] = res[c * n:(c + 1) * n, :]


def _dil_proj(h2, g_norm, w, r, *, b, s, tm=TOKEN_TILE):
    d = h2.shape[1]
    n_out = w.shape[1]
    tiles = s // tm
    return pl.pallas_call(
        functools.partial(_dil_proj_body, r=r),
        grid=(b * tiles,),
        in_specs=[
            pl.BlockSpec((tm, d), lambda i: (i, 0)),
            pl.BlockSpec((1, d), lambda i: (0, 0)),
            pl.BlockSpec((d, n_out), lambda i: (0, 0)),
        ],
        out_specs=pl.BlockSpec((1, r, tm // r, n_out), lambda i: (i // tiles, 0, i % tiles, 0)),
        out_shape=jax.ShapeDtypeStruct((b, r, s // r, n_out), BF16),
        scratch_shapes=[pltpu.VMEM((d // LANES, tm, LANES), F32)],
        compiler_params=_params("parallel"),
        name=f"dilated_proj_r{r}",
    )(h2, g_norm.reshape(1, d), w)


def _dil_body(q_ref, kc_ref, kp_ref, vc_ref, vp_ref, o_ref, lse_ref, kbuf, vbuf, *, rows, n_back):
    qb = DIL_QB
    dh = DIL_DH
    first_step = pl.program_id(2) == 0
    kbuf[0:qb, :] = kp_ref[0, 0]
    kbuf[qb:, :] = kc_ref[0, 0]
    vbuf[0:qb, :] = vp_ref[0, 0]
    vbuf[qb:, :] = vc_ref[0, 0]
    qi = lax.broadcasted_iota(jnp.int32, (qb, 2 * qb), 0)
    kj = lax.broadcasted_iota(jnp.int32, (qb, 2 * qb), 1)
    dist = qi + qb - kj
    band = (dist >= 0) & (dist <= n_back)
    lane = lax.broadcasted_iota(jnp.int32, (qb, LANES), 1)
    scale = dh ** -0.5
    for i in range(rows // qb):
        mask = band & ((kj >= qb) | jnp.logical_not(first_step)) if i == 0 else band
        lse_tile = jnp.zeros((qb, LANES), F32)
        for h in range(DIL_HEADS):
            cols = slice(h * dh, (h + 1) * dh)
            q = q_ref[0, 0, i * qb:(i + 1) * qb, cols]
            k = kbuf[i * qb:(i + 2) * qb, cols]
            v = vbuf[i * qb:(i + 2) * qb, cols]
            s = jnp.where(mask, _dot_nt(q, k) * scale, NEG_INF)
            m = jnp.max(s, axis=1, keepdims=True)
            e = jnp.where(mask, jnp.exp(s - m), 0.0)
            den = jnp.maximum(jnp.sum(e, axis=1, keepdims=True), 1e-30)
            o = _dot((e / den).astype(BF16), v)
            o_ref[0, 0, i * qb:(i + 1) * qb, cols] = o.astype(o_ref.dtype)
            lse_tile = jnp.where(lane == h, m + jnp.log(den), lse_tile)
        lse_ref[0, 0, i * qb:(i + 1) * qb, :] = lse_tile


def _dilated_group(proj, win, r, *, b, s):
    hd = DIL_HEADS * DIL_DH
    length = s // r
    rows = min(length, TOKEN_TILE)
    qb = DIL_QB

    def cur(which):
        return pl.BlockSpec((1, 1, rows, hd), lambda i, c, n: (i, c, n, which))

    def prev(which):
        return pl.BlockSpec((1, 1, qb, hd), lambda i, c, n: (i, c, jnp.maximum(n * (rows // qb) - 1, 0), which))

    return pl.pallas_call(
        functools.partial(_dil_body, rows=rows, n_back=win // r),
        grid=(b, r, length // rows),
        in_specs=[cur(0), cur(1), prev(1), cur(2), prev(2)],
        out_specs=[
            pl.BlockSpec((1, 1, rows, hd), lambda i, c, n: (i, c, n, 0)),
            pl.BlockSpec((1, 1, rows, LANES), lambda i, c, n: (i, c, n, 0)),
        ],
        out_shape=[
            jax.ShapeDtypeStruct((b, r, length, hd), BF16),
            jax.ShapeDtypeStruct((b, r, length, LANES), F32),
        ],
        scratch_shapes=[pltpu.VMEM((rows + qb, hd), BF16), pltpu.VMEM((rows + qb, hd), BF16)],
        compiler_params=_params("parallel", "parallel", "arbitrary"),
        name=f"dilated_r{r}",
    )(proj, proj, proj, proj, proj)


def _dil_merge_body(h_ref, *refs, dilations):
    ng = len(dilations)
    o_refs, l_refs = refs[:ng], refs[ng:2 * ng]
    w_ref, out_ref = refs[2 * ng], refs[2 * ng + 1]
    o_bufs, l_bufs = refs[2 * ng + 2:3 * ng + 2], refs[3 * ng + 2:]
    tm = h_ref.shape[1]
    for o_ref, l_ref, o_buf, l_buf, r in zip(o_refs, l_refs, o_bufs, l_bufs, dilations):
        n = tm // r
        for c in range(r):
            rows = pl.ds(c, n, stride=r)
            l_buf[rows, :] = l_ref[0, c]
            blk = o_ref[0, c].astype(F32)
            for h in range(DIL_HEADS):
                o_buf[h, rows, :] = blk[:, h * DIL_DH:(h + 1) * DIL_DH]
    lses = [l[...] for l in l_bufs]
    mx = functools.reduce(jnp.maximum, lses)
    es = [jnp.exp(l - mx) for l in lses]
    tot = functools.reduce(jnp.add, es)
    wts = [e / tot for e in es]
    dh = DIL_DH
    parts = []
    for h in range(DIL_HEADS):
        acc = jnp.zeros((tm, dh), F32)
        for g in range(ng):
            wg = jnp.broadcast_to(wts[g][:, h:h + 1], (tm, dh))
            acc = acc + wg * o_bufs[g][h]
        parts.append(acc)
    o = jnp.concatenate(parts, axis=1).astype(BF16)
    out_ref[0] = h_ref[0] + _dot(o, w_ref[...])


def _dil_merge(h3, outs, lses, w, dilations, *, tm=TOKEN_TILE):
    b, s, d = h3.shape
    hd = w.shape[0]
    res_major = lambda r, width: pl.BlockSpec((1, r, tm // r, width), lambda i, n: (i, 0, n, 0))
    return pl.pallas_call(
        functools.partial(_dil_merge_body, dilations=tuple(dilations)),
        grid=(b, s // tm),
        in_specs=[pl.BlockSpec((1, tm, d), lambda i, n: (i, n, 0))]
        + [res_major(r, hd) for r in dilations] + [res_major(r, LANES) for r in dilations]
        + [pl.BlockSpec((hd, d), lambda i, n: (0, 0))],
        out_specs=pl.BlockSpec((1, tm, d), lambda i, n: (i, n, 0)),
        out_shape=jax.ShapeDtypeStruct((b, s, d), F32),
        scratch_shapes=[pltpu.VMEM((DIL_HEADS, tm, DIL_DH), F32) for _ in dilations]
        + [pltpu.VMEM((tm, LANES), F32) for _ in dilations],
        compiler_params=_params("parallel", "parallel"),
        name="dilated_merge",
    )(h3, *outs, *lses, w)


def _dilated(h2, g_norm, w_in, w_out, *, b, s):
    d = h2.shape[1]
    gw = 3 * DIL_HEADS * DIL_DH
    wb = w_in.astype(BF16)
    outs, lses, dilations = [], [], []
    for g, (win, r) in enumerate(DIL_PATTERN):
        proj = _dil_proj(h2, g_norm, wb[:, g * gw:(g + 1) * gw], r, b=b, s=s)
        o, lse = _dilated_group(proj, win, r, b=b, s=s)
        outs.append(o)
        lses.append(lse)
        dilations.append(r)
    return _dil_merge(h2.reshape(b, s, d), outs, lses, w_out.astype(BF16), dilations).reshape(b * s, d)


def _softplus(x):
    return jnp.maximum(x, 0.0) + jnp.log1p(jnp.exp(-jnp.abs(x)))


def _ssd_body(z_ref, x_ref, dt_ref, cw_ref, cb_ref, dtb_ref, alog_ref, dsk_ref, ng_ref, o_ref,
              xpad, state_ref):
    ln = SSD_CHUNK
    nst = SSD_STATE
    gw = SSD_HPG * SSD_HEADDIM
    d_inner = SSD_GROUPS * gw
    halo = 8

    @pl.when(pl.program_id(1) == 0)
    def _():
        xpad[0:halo, :] = jnp.zeros((halo, xpad.shape[1]), F32)
        state_ref[...] = jnp.zeros(state_ref.shape, F32)

    xpad[halo:halo + ln, :] = x_ref[0].astype(F32)
    conv = cb_ref[...]
    for k in range(SSD_CONV):
        off = halo - (SSD_CONV - 1) + k
        conv = conv + xpad[off:off + ln, :] * cw_ref[k:k + 1, :]
    xpad[0:halo, :] = xpad[ln:ln + halo, :]
    xbc = _silu(conv)
    xs = xbc[:, :d_inner]
    bm = xbc[:, d_inner:d_inner + SSD_GROUPS * nst]
    cm = xbc[:, d_inner + SSD_GROUPS * nst:]

    dt = _softplus(dt_ref[0] + dtb_ref[...])
    da = dt * (-jnp.exp(alog_ref[...]))
    ii = lax.broadcasted_iota(jnp.int32, (ln, ln), 0)
    jj = lax.broadcasted_iota(jnp.int32, (ln, ln), 1)
    causal = ii >= jj
    tril = jnp.where(causal, 1.0, 0.0).astype(BF16)
    da_terms = _split3(da)
    acs = _dot(tril, da_terms[0]) + _dot(tril, da_terms[1]) + _dot(tril, da_terms[2])
    acs_t = jnp.transpose(acs)
    dt_t = jnp.transpose(dt)

    erow = lax.broadcasted_iota(jnp.int32, (LANES, d_inner), 0)
    ecol = lax.broadcasted_iota(jnp.int32, (LANES, d_inner), 1)
    expand = jnp.where(ecol // SSD_HEADDIM == erow, 1.0, 0.0).astype(BF16)
    acs_e = _dot3(_split3(acs), expand)
    dt_e = _dot3(_split3(dt), expand)
    last = acs_e[ln - 1:ln, :]
    decay_in = jnp.exp(acs_e)
    xs_end = (xs * (jnp.exp(last - acs_e) * dt_e)).astype(BF16)
    xs_b = xs.astype(BF16)
    lane = lax.broadcasted_iota(jnp.int32, (ln, LANES), 1)

    y_groups = []
    for g in range(SSD_GROUPS):
        bm_g = bm[:, g * nst:(g + 1) * nst]
        cm_g = cm[:, g * nst:(g + 1) * nst].astype(BF16)
        cb = _dot_nt(cm_g, bm_g.astype(BF16))
        st = state_ref[g]
        gcols = slice(g * gw, (g + 1) * gw)
        y_state = _dot(cm_g, st.astype(BF16)) * decay_in[:, gcols]
        pairs = []
        for p in range(SSD_HPG // 2):
            pair_cols = slice(g * gw + p * LANES, g * gw + (p + 1) * LANES)
            halves = []
            for e in range(2):
                hd = g * SSD_HPG + 2 * p + e
                diff = acs[:, hd:hd + 1] - acs_t[hd:hd + 1, :]
                seg = jnp.where(causal, jnp.exp(jnp.where(causal, diff, 0.0)), 0.0)
                w = (cb * seg * dt_t[hd:hd + 1, :]).astype(BF16)
                halves.append(_dot(w, xs_b[:, pair_cols]))
            pairs.append(jnp.where(lane < SSD_HEADDIM, halves[0], halves[1]))
        y_groups.append(jnp.concatenate(pairs, axis=1) + y_state)
        bm_t = jnp.transpose(bm_g).astype(BF16)
        state_ref[g] = st * jnp.exp(last[:, gcols]) + _dot(bm_t, xs_end[:, gcols])

    y = jnp.concatenate(y_groups, axis=1) + dsk_ref[...] * xs
    yz = y * _silu(z_ref[0].astype(F32))
    outs = [_rms(yz[:, g * gw:(g + 1) * gw], ng_ref[:, g * gw:(g + 1) * gw]) for g in range(SSD_GROUPS)]
    o_ref[0] = jnp.concatenate(outs, axis=1).astype(o_ref.dtype)


def _ssd_core(z, xbc, dt, conv_w, conv_b, dt_bias, a_log, d_skip, norm_g, *, b, s):
    ln = SSD_CHUNK
    d_inner = z.shape[2]
    conv_dim = xbc.shape[2]
    gw = SSD_HPG * SSD_HEADDIM
    full = lambda shape: pl.BlockSpec(shape, lambda i, j: (0,) * len(shape))
    return pl.pallas_call(
        _ssd_body,
        grid=(b, s // ln),
        in_specs=[
            pl.BlockSpec((1, ln, d_inner), lambda i, j: (i, j, 0)),
            pl.BlockSpec((1, ln, conv_dim), lambda i, j: (i, j, 0)),
            pl.BlockSpec((1, ln, LANES), lambda i, j: (i, j, 0)),
            full((SSD_CONV, conv_dim)),
            full((1, conv_dim)),
            full((1, LANES)),
            full((1, LANES)),
            full((1, d_inner)),
            full((1, d_inner)),
        ],
        out_specs=pl.BlockSpec((1, ln, d_inner), lambda i, j: (i, j, 0)),
        out_shape=jax.ShapeDtypeStruct((b, s, d_inner), BF16),
        scratch_shapes=[pltpu.VMEM((ln + 8, conv_dim), F32),
                        pltpu.VMEM((SSD_GROUPS, SSD_STATE, gw), F32)],
        compiler_params=_params("parallel", "arbitrary"),
        name="ssd",
    )(z, xbc, dt, conv_w, conv_b, dt_bias, a_log, d_skip, norm_g)


def _ssd(h2, g_norm, w_in, conv_w, conv_b, dt_bias, a_log, d_skip, norm_g, w_out, *, b, s):
    heads = SSD_GROUPS * SSD_HPG
    d_inner = heads * SSD_HEADDIM
    conv_dim = conv_w.shape[1]
    wb = w_in.astype(BF16)
    w_dt = jnp.pad(wb[:, d_inner + conv_dim:], ((0, 0), (0, LANES - heads)))
    z, xbc, dt = _norm_proj(h2, g_norm, [wb[:, :d_inner], wb[:, d_inner:d_inner + conv_dim], w_dt], [BF16, BF16, F32])
    pad_heads = lambda v: jnp.pad(v.astype(F32), (0, LANES - heads)).reshape(1, LANES)
    y = _ssd_core(z.reshape(b, s, -1), xbc.reshape(b, s, -1), dt.reshape(b, s, LANES),
                  conv_w.astype(F32), conv_b.reshape(1, -1).astype(F32), pad_heads(dt_bias), pad_heads(a_log),
                  jnp.repeat(d_skip.astype(F32), SSD_HEADDIM).reshape(1, d_inner), norm_g.reshape(1, d_inner),
                  b=b, s=s)
    return y.reshape(b * s, d_inner), w_out.astype(BF16)


def _nsa_proj_body(h_ref, g_ref, wn_ref, wt_ref, k0_ref, v0_ref, k12_ref, tr_ref):
    xn = _rms(h_ref[...], g_ref[...]).astype(BF16)
    nat = _dot(xn, wn_ref[...])
    gd = k0_ref.shape[1]
    k0_ref[...] = nat[:, :gd].astype(k0_ref.dtype)
    v0_ref[...] = nat[:, gd:2 * gd].astype(v0_ref.dtype)
    k12_ref[...] = nat[:, 2 * gd:].astype(k12_ref.dtype)
    res = _dot_nt(wt_ref[...], xn)
    for j in range(tr_ref.shape[0]):
        tr_ref[j] = res[:, j * NSA_QT:(j + 1) * NSA_QT].astype(tr_ref.dtype)


def _nsa_proj(h2, g_norm, w_nat, w_t, *, tm=TOKEN_TILE):
    m, d = h2.shape
    gd = NSA_GROUPS * NSA_DH
    nt = w_t.shape[0]
    slabs = tm // NSA_QT
    return pl.pallas_call(
        _nsa_proj_body,
        grid=(m // tm,),
        in_specs=[
            pl.BlockSpec((tm, d), lambda i: (i, 0)),
            pl.BlockSpec((1, d), lambda i: (0, 0)),
            pl.BlockSpec(w_nat.shape, lambda i: (0, 0)),
            pl.BlockSpec(w_t.shape, lambda i: (0, 0)),
        ],
        out_specs=[
            pl.BlockSpec((tm, gd), lambda i: (i, 0)),
            pl.BlockSpec((tm, gd), lambda i: (i, 0)),
            pl.BlockSpec((tm, 2 * gd), lambda i: (i, 0)),
            pl.BlockSpec((slabs, nt, NSA_QT), lambda i: (i, 0, 0)),
        ],
        out_shape=[
            jax.ShapeDtypeStruct((m, gd), BF16),
            jax.ShapeDtypeStruct((m, gd), BF16),
            jax.ShapeDtypeStruct((m, 2 * gd), BF16),
            jax.ShapeDtypeStruct((m // NSA_QT, nt, NSA_QT), BF16),
        ],
        compiler_params=_params("parallel"),
        name="nsa_proj",
    )(h2, g_norm.reshape(1, d), w_nat, w_t)


def _nsa_cmp_body(x_ref, pa_ref, pb_ref, w1a_ref, w1b_ref, w2_ref, o_ref, *, transposed):
    x = x_ref[0].astype(F32)
    nb = x.shape[0]
    ya = _dot((x + pa_ref[...]).astype(BF16), w1a_ref[...])
    yb = _dot((x + pb_ref[...]).astype(BF16), w1b_ref[...])
    hid = _silu(ya + pltpu.roll(yb, nb - 1, 0))
    out = _dot(hid.astype(BF16), w2_ref[...])
    o_ref[0] = (jnp.transpose(out) if transposed else out).astype(o_ref.dtype)


def _nsa_compress(x, pos, w1, w2, *, transposed):
    b, nb, width = x.shape
    g, dh = NSA_GROUPS, NSA_DH
    half = NSA_CMP_LEN // 2
    hid = w1.shape[1]
    eye = jnp.eye(g, dtype=F32)
    w1r = w1.reshape(NSA_CMP_LEN, dh, hid)
    big = jnp.einsum("ldj,gh->lgdhj", w1r, eye).reshape(NSA_CMP_LEN, g * dh, g * hid)
    w1a = big[:half].reshape(half * g * dh, g * hid).astype(BF16)
    w1b = big[half:].reshape(half * g * dh, g * hid).astype(BF16)
    w2big = jnp.einsum("jd,gh->gjhd", w2, eye).reshape(g * hid, g * dh).astype(BF16)
    posb = jnp.broadcast_to(pos[:, None, :], (NSA_CMP_LEN, g, dh)).reshape(NSA_CMP_LEN, g * dh)
    pa = posb[:half].reshape(1, width).astype(F32)
    pb = posb[half:].reshape(1, width).astype(F32)
    full = lambda shape: pl.BlockSpec(shape, lambda i: (0,) * len(shape))
    out_dims = (g * dh, nb) if transposed else (nb, g * dh)
    return pl.pallas_call(
        functools.partial(_nsa_cmp_body, transposed=transposed),
        grid=(b,),
        in_specs=[
            pl.BlockSpec((1, nb, width), lambda i: (i, 0, 0)),
            full((1, width)), full((1, width)),
            full((width, g * hid)), full((width, g * hid)), full((g * hid, g * dh)),
        ],
        out_specs=pl.BlockSpec((1,) + out_dims, lambda i: (i, 0, 0)),
        out_shape=jax.ShapeDtypeStruct((b,) + out_dims, BF16),
        compiler_params=_params("parallel"),
        name="nsa_compress",
    )(x, pa, pb, w1a, w1b, w2big)


def _nsa_body(q_ref, gt_ref, kc_ref, vct_ref, k1_ref, k2_ref, v1_ref, v2_ref, o_ref, score_ref, bias_ref, *, s):
    qt, kc, dh = NSA_QT, NSA_KC, NSA_DH
    lanes = NSA_HPG * qt
    g = pl.program_id(1)
    j = pl.program_id(2)
    t0 = j * qt
    nb = kc_ref.shape[1]
    nsb = s // NSA_SEL_LEN
    n_sel = min(NSA_N_SEL, nsb)

    q_t = q_ref[0, 0]
    qcat = jnp.concatenate([q_t[h * dh:(h + 1) * dh, :] for h in range(NSA_HPG)], axis=1)
    odd = (g % 2) == 1
    zero = jnp.zeros_like(qcat)
    qp = jnp.concatenate([jnp.where(odd, zero, qcat), jnp.where(odd, qcat, zero)], axis=0)
    qlane = lax.broadcasted_iota(jnp.int32, (1, lanes), 1) % qt
    tq = t0 + qlane

    sc = _dot(kc_ref[0], qp)
    n_idx = lax.broadcasted_iota(jnp.int32, (nb, lanes), 0)
    cmask = (n_idx * NSA_CMP_STRIDE + NSA_CMP_LEN - 1 <= tq) & (n_idx < nb - 1)
    sc = jnp.where(cmask, sc, NEG_INF)
    e = jnp.where(cmask, jnp.exp(sc - jnp.max(sc, axis=0, keepdims=True)), 0.0)
    p_c = e / jnp.maximum(jnp.sum(e, axis=0, keepdims=True), 1e-30)
    o_c = _dot(vct_ref[0], p_c.astype(BF16))
    imp = p_c[:, 0:qt]
    for h in range(1, NSA_HPG):
        imp = imp + p_c[:, h * qt:(h + 1) * qt]

    ratio = NSA_SEL_LEN // NSA_CMP_STRIDE
    dd = lax.broadcasted_iota(jnp.int32, (nsb, nb), 1) - ratio * lax.broadcasted_iota(jnp.int32, (nsb, nb), 0)
    wsel = jnp.where((dd == -1) | (dd == ratio - 1), 1.0,
                     jnp.where((dd >= 0) & (dd < ratio - 1), 2.0, 0.0)).astype(BF16)
    terms = _split3(imp)
    imp_sel = _dot(wsel, terms[0]) + _dot(wsel, terms[1]) + _dot(wsel, terms[2])
    jq = lax.broadcasted_iota(jnp.int32, (nsb, qt), 0)
    cur = (t0 + lax.broadcasted_iota(jnp.int32, (nsb, qt), 1)) // NSA_SEL_LEN
    forced = (jq == 0) | (jq == cur) | (jq == cur - 1)
    score = jnp.where(jq <= cur, imp_sel + jnp.where(forced, NSA_FORCE_BONUS, 0.0), NEG_INF)
    score_ref[...] = score

    def rank_step(jp, rank):
        row = jnp.broadcast_to(score_ref[pl.ds(jp, 1), :], (nsb, qt))
        before = (row > score) | ((row == score) & (jq > jp))
        return rank + jnp.where(before, 1.0, 0.0)

    n_visible = (t0 + qt - 1) // NSA_SEL_LEN + 1
    rank = lax.fori_loop(0, n_visible, rank_step, jnp.zeros((nsb, qt), F32))
    sel_bias = jnp.where((rank < n_sel) & (score > 0.5 * NEG_INF), 0.0, NEG_INF)
    bias_ref[...] = jnp.concatenate([sel_bias] * NSA_HPG, axis=1)

    ones_rows = jnp.ones((16, kc), BF16)

    def values(v_ref, slab0, n_slabs):
        v = jnp.concatenate([v_ref[0, slab0 + i] for i in range(n_slabs)], axis=1)
        return jnp.concatenate([v, ones_rows[:, :n_slabs * qt]], axis=0)

    def online(carry, st, v_aug):
        m, acc = carry
        m_new = jnp.maximum(m, jnp.max(st, axis=0, keepdims=True))
        p = jnp.exp(st - m_new).astype(BF16)
        return m_new, jnp.exp(m - m_new) * acc + _dot(v_aug, p)

    def normalise(acc):
        return acc[:dh, :] / jnp.maximum(acc[dh:dh + 1, :], 1e-30)

    init = (jnp.full((1, lanes), NEG_INF, F32), jnp.zeros((dh + 16, lanes), F32))

    blocks_per_chunk = kc // NSA_SEL_LEN
    slabs_per_chunk = kc // qt

    def sel_scores(c):
        k = k1_ref[0, pl.ds(pl.multiple_of(c * kc, kc), kc), :]
        bias = bias_ref[pl.ds(pl.multiple_of(c * blocks_per_chunk, blocks_per_chunk), blocks_per_chunk), :]
        rows = [jnp.broadcast_to(bias[r:r + 1, :], (NSA_SEL_LEN, lanes)) for r in range(blocks_per_chunk)]
        return _dot(k, qp) + jnp.concatenate(rows, axis=0)

    def sel_values(c):
        return values(v1_ref, c * slabs_per_chunk, slabs_per_chunk)

    c_hi = t0 // kc
    row_k = lax.broadcasted_iota(jnp.int32, (kc, lanes), 0)
    st = jnp.where(row_k <= (t0 - c_hi * kc) + qlane, sel_scores(c_hi), NEG_INF)
    chain_a = online(init, st, sel_values(c_hi))
    chain_a = lax.cond(c_hi % 2 == 1,
                       lambda cr: online(cr, sel_scores(c_hi - 1), sel_values(c_hi - 1)),
                       lambda cr: cr, chain_a)
    chain_b = (chain_a[0], jnp.zeros_like(chain_a[1]))

    def pair(i, chains):
        ca, cb = chains
        st_a = sel_scores(2 * i)
        st_b = sel_scores(2 * i + 1)
        return online(ca, st_a, sel_values(2 * i)), online(cb, st_b, sel_values(2 * i + 1))

    (m_a, acc_a), (m_b, acc_b) = lax.fori_loop(0, c_hi // 2, pair, (chain_a, chain_b))
    m_s = jnp.maximum(m_a, m_b)
    o_s = normalise(jnp.exp(m_a - m_s) * acc_a + jnp.exp(m_b - m_s) * acc_b)

    wslabs = NSA_WINDOW // qt + 1
    jb0 = jnp.maximum(j + 1 - wslabs, 0)
    kw = k2_ref[0, pl.ds(pl.multiple_of(jb0 * qt, qt), wslabs * qt), :]
    kpos = jb0 * qt + lax.broadcasted_iota(jnp.int32, (wslabs * qt, lanes), 0)
    sw = jnp.where((kpos <= tq) & (kpos > tq - NSA_WINDOW), _dot(kw, qp), NEG_INF)
    pw = jnp.exp(sw - jnp.max(sw, axis=0, keepdims=True)).astype(BF16)
    ones_w = jnp.ones((16, wslabs * qt), BF16)
    vw = jnp.concatenate([v2_ref[0, jb0 + i] for i in range(wslabs)], axis=1)
    o_w = normalise(_dot(jnp.concatenate([vw, ones_w], axis=0), pw))

    gates = jax.nn.sigmoid(gt_ref[0, 0].astype(F32))

    def gate_row(br):
        return jnp.concatenate([gates[br * NSA_HPG + h:br * NSA_HPG + h + 1, :] for h in range(NSA_HPG)], axis=1)

    o = gate_row(0) * o_c + gate_row(1) * o_s + gate_row(2) * o_w
    o_hd = jnp.concatenate([o[:, h * qt:(h + 1) * qt] for h in range(NSA_HPG)], axis=0)
    o_ref[0] = jnp.transpose(o_hd).astype(o_ref.dtype)


def _nsa_attend(tr, k_cmp, v_cmp_t, k12, *, b, s):
    qt, dh = NSA_QT, NSA_DH
    gd = NSA_GROUPS * dh
    hq = NSA_GROUPS * NSA_HPG * dh
    nslab = s // qt
    nb = k_cmp.shape[1]
    tr4 = tr.reshape(b, nslab, tr.shape[1], qt)
    gate_rows = 4 * NSA_HPG
    v1_blk = hq // dh
    v2_blk = (hq + gd) // dh
    gate_blk = (hq + 2 * gd) // gate_rows
    nsb = s // NSA_SEL_LEN
    return pl.pallas_call(
        functools.partial(_nsa_body, s=s),
        grid=(b, NSA_GROUPS, nslab),
        in_specs=[
            pl.BlockSpec((1, 1, NSA_HPG * dh, qt), lambda i, g, j: (i, j, g, 0)),
            pl.BlockSpec((1, 1, gate_rows, qt), lambda i, g, j: (i, j, gate_blk + g, 0)),
            pl.BlockSpec((1, nb, 2 * dh), lambda i, g, j: (i, 0, g // 2)),
            pl.BlockSpec((1, dh, nb), lambda i, g, j: (i, g, 0)),
            pl.BlockSpec((1, s, 2 * dh), lambda i, g, j: (i, 0, g // 2)),
            pl.BlockSpec((1, s, 2 * dh), lambda i, g, j: (i, 0, NSA_GROUPS // 2 + g // 2)),
            pl.BlockSpec((1, nslab, dh, qt), lambda i, g, j: (i, 0, v1_blk + g, 0)),
            pl.BlockSpec((1, nslab, dh, qt), lambda i, g, j: (i, 0, v2_blk + g, 0)),
        ],
        out_specs=pl.BlockSpec((1, qt, NSA_HPG * dh), lambda i, g, j: (i, j, g)),
        out_shape=jax.ShapeDtypeStruct((b, s, hq), BF16),
        scratch_shapes=[pltpu.VMEM((nsb, qt), F32), pltpu.VMEM((nsb, NSA_HPG * qt), F32)],
        compiler_params=_params("parallel", "parallel", "arbitrary"),
        name="nsa_attend",
    )(tr4, tr4, k_cmp, v_cmp_t, k12, k12, tr4, tr4)


def _nsa_pair_body(q_ref, gt_ref, kc_ref, vct_ref, k1_ref, k2_ref, v1_ref, v2_ref, o_ref, score_ref, bias_ref, *, s):
    qt, kc, dh = NSA_QT, NSA_KC, NSA_DH
    gl = 2
    glanes = NSA_HPG * qt
    lanes = gl * glanes
    vrows = gl * dh
    j = pl.program_id(2)
    t0 = j * qt
    nb = kc_ref.shape[1]
    nsb = s // NSA_SEL_LEN
    n_sel = min(NSA_N_SEL, nsb)

    q_t = q_ref[0, 0]
    qcat = [jnp.concatenate([q_t[(g * NSA_HPG + h) * dh:(g * NSA_HPG + h + 1) * dh, :] for h in range(NSA_HPG)],
                            axis=1) for g in range(gl)]
    zero = jnp.zeros_like(qcat[0])
    qp = jnp.concatenate([jnp.concatenate([qcat[0], zero], axis=1),
                          jnp.concatenate([zero, qcat[1]], axis=1)], axis=0)
    qlane = lax.broadcasted_iota(jnp.int32, (1, lanes), 1) % qt
    tq = t0 + qlane

    def normalise(acc):
        den = jnp.maximum(acc[vrows:vrows + 1, :], 1e-30)
        return [acc[g * dh:(g + 1) * dh, g * glanes:(g + 1) * glanes] / den[:, g * glanes:(g + 1) * glanes]
                for g in range(gl)]

    def values(v_ref, slab0, n_slabs):
        v = jnp.concatenate([v_ref[0, slab0 + i] for i in range(n_slabs)], axis=1)
        return jnp.concatenate([v, jnp.ones((16, n_slabs * qt), BF16)], axis=0)

    wslabs = NSA_WINDOW // qt + 1
    jb0 = jnp.maximum(j + 1 - wslabs, 0)
    kw = k2_ref[0, pl.ds(pl.multiple_of(jb0 * qt, qt), wslabs * qt), :]
    kpos = jb0 * qt + lax.broadcasted_iota(jnp.int32, (wslabs * qt, lanes), 0)
    sw = jnp.where((kpos <= tq) & (kpos > tq - NSA_WINDOW), _dot(kw, qp), NEG_INF)

    sc = _dot(kc_ref[0], qp)
    n_idx = lax.broadcasted_iota(jnp.int32, (nb, lanes), 0)
    cmask = (n_idx * NSA_CMP_STRIDE + NSA_CMP_LEN - 1 <= tq) & (n_idx < nb - 1)
    sc = jnp.where(cmask, sc, NEG_INF)
    e = jnp.where(cmask, jnp.exp(sc - jnp.max(sc, axis=0, keepdims=True)), 0.0)
    p_c = e / jnp.maximum(jnp.sum(e, axis=0, keepdims=True), 1e-30)
    acc_c = _dot(vct_ref[0], p_c.astype(BF16))
    o_c = [acc_c[g * dh:(g + 1) * dh, g * glanes:(g + 1) * glanes] for g in range(gl)]
    imps = []
    for g in range(gl):
        imp = p_c[:, g * glanes:g * glanes + qt]
        for h in range(1, NSA_HPG):
            imp = imp + p_c[:, g * glanes + h * qt:g * glanes + (h + 1) * qt]
        imps.append(imp)
    imp = jnp.concatenate(imps, axis=1)

    ratio = NSA_SEL_LEN // NSA_CMP_STRIDE
    dd = lax.broadcasted_iota(jnp.int32, (nsb, nb), 1) - ratio * lax.broadcasted_iota(jnp.int32, (nsb, nb), 0)
    wsel = jnp.where((dd == -1) | (dd == ratio - 1), 1.0,
                     jnp.where((dd >= 0) & (dd < ratio - 1), 2.0, 0.0)).astype(BF16)
    terms = _split3(imp)
    imp_sel = _dot(wsel, terms[0]) + _dot(wsel, terms[1]) + _dot(wsel, terms[2])
    jq = lax.broadcasted_iota(jnp.int32, (nsb, gl * qt), 0)
    cur = (t0 + lax.broadcasted_iota(jnp.int32, (nsb, gl * qt), 1) % qt) // NSA_SEL_LEN
    forced = (jq == 0) | (jq == cur) | (jq == cur - 1)
    score = jnp.where(jq <= cur, imp_sel + jnp.where(forced, NSA_FORCE_BONUS, 0.0), NEG_INF)
    score_ref[...] = score

    pw = jnp.exp(sw - jnp.max(sw, axis=0, keepdims=True)).astype(BF16)
    o_w = normalise(_dot(values(v2_ref, jb0, wslabs), pw))

    def rank_step(jp, rank):
        row = jnp.broadcast_to(score_ref[pl.ds(jp, 1), :], (nsb, gl * qt))
        before = (row > score) | ((row == score) & (jq > jp))
        return rank + jnp.where(before, 1.0, 0.0)

    n_visible = (t0 + qt - 1) // NSA_SEL_LEN + 1
    n_ranked = jnp.where(n_visible > n_sel, n_visible, 0)
    rank = lax.fori_loop(0, n_ranked, rank_step, jnp.zeros((nsb, gl * qt), F32))
    sel_bias = jnp.where((rank < n_sel) & (score > 0.5 * NEG_INF), 0.0, NEG_INF)
    bias_ref[...] = jnp.concatenate([sel_bias[:, g * qt:(g + 1) * qt] for g in range(gl) for _ in range(NSA_HPG)],
                                    axis=1)

    def online(carry, st, v_aug):
        m, acc = carry
        m_new = jnp.maximum(m, jnp.max(st, axis=0, keepdims=True))
        p = jnp.exp(st - m_new).astype(BF16)
        return m_new, jnp.exp(m - m_new) * acc + _dot(v_aug, p)

    blocks_per_chunk = kc // NSA_SEL_LEN
    slabs_per_chunk = kc // qt

    def sel_scores(c):
        k = k1_ref[0, pl.ds(pl.multiple_of(c * kc, kc), kc), :]
        bias = bias_ref[pl.ds(pl.multiple_of(c * blocks_per_chunk, blocks_per_chunk), blocks_per_chunk), :]
        rows = [jnp.broadcast_to(bias[r:r + 1, :], (NSA_SEL_LEN, lanes)) for r in range(blocks_per_chunk)]
        return _dot(k, qp) + jnp.concatenate(rows, axis=0)

    def sel_values(c):
        return values(v1_ref, c * slabs_per_chunk, slabs_per_chunk)

    init = (jnp.full((1, lanes), NEG_INF, F32), jnp.zeros((vrows + 16, lanes), F32))
    c_hi = t0 // kc
    row_k = lax.broadcasted_iota(jnp.int32, (kc, lanes), 0)
    st = jnp.where(row_k <= (t0 - c_hi * kc) + qlane, sel_scores(c_hi), NEG_INF)
    chain_a = online(init, st, sel_values(c_hi))
    chain_a = lax.cond(c_hi % 2 == 1,
                       lambda cr: online(cr, sel_scores(c_hi - 1), sel_values(c_hi - 1)),
                       lambda cr: cr, chain_a)
    chain_b = (chain_a[0], jnp.zeros_like(chain_a[1]))

    def pair(i, chains):
        ca, cb = chains
        st_a = sel_scores(2 * i)
        st_b = sel_scores(2 * i + 1)
        return online(ca, st_a, sel_values(2 * i)), online(cb, st_b, sel_values(2 * i + 1))

    (m_a, acc_a), (m_b, acc_b) = lax.fori_loop(0, c_hi // 2, pair, (chain_a, chain_b))
    m_s = jnp.maximum(m_a, m_b)
    o_s = normalise(jnp.exp(m_a - m_s) * acc_a + jnp.exp(m_b - m_s) * acc_b)

    gates = jax.nn.sigmoid(gt_ref[0, 0].astype(F32))
    outs = []
    for g in range(gl):
        def gate_row(br):
            r0 = (g * 4 + br) * NSA_HPG
            return jnp.concatenate([gates[r0 + h:r0 + h + 1, :] for h in range(NSA_HPG)], axis=1)
        o = gate_row(0) * o_c[g] + gate_row(1) * o_s[g] + gate_row(2) * o_w[g]
        outs.extend(o[:, h * qt:(h + 1) * qt] for h in range(NSA_HPG))
    o_ref[0] = jnp.transpose(jnp.concatenate(outs, axis=0)).astype(o_ref.dtype)


def _nsa_attend_pairs(tr, k_cmp, v_cmp_t, k12, *, b, s):
    qt, dh = NSA_QT, NSA_DH
    gl = 2
    gd = NSA_GROUPS * dh
    hq = NSA_GROUPS * NSA_HPG * dh
    pairs = NSA_GROUPS // gl
    nslab = s // qt
    assert nslab > NSA_WINDOW // qt
    nb = k_cmp.shape[1]
    tr4 = tr.reshape(b, nslab, tr.shape[1], qt)
    q_rows = gl * NSA_HPG * dh
    v_rows = gl * dh
    gate_rows = gl * 4 * NSA_HPG
    v1_blk = hq // v_rows
    v2_blk = (hq + gd) // v_rows
    gate_blk = (hq + 2 * gd) // gate_rows
    nsb = s // NSA_SEL_LEN
    return pl.pallas_call(
        functools.partial(_nsa_pair_body, s=s),
        grid=(b, pairs, nslab),
        in_specs=[
            pl.BlockSpec((1, 1, q_rows, qt), lambda i, p, j: (i, j, p, 0)),
            pl.BlockSpec((1, 1, gate_rows, qt), lambda i, p, j: (i, j, gate_blk + p, 0)),
            pl.BlockSpec((1, nb, v_rows), lambda i, p, j: (i, 0, p)),
            pl.BlockSpec((1, v_rows, nb), lambda i, p, j: (i, p, 0)),
            pl.BlockSpec((1, s, v_rows), lambda i, p, j: (i, 0, p)),
            pl.BlockSpec((1, s, v_rows), lambda i, p, j: (i, 0, pairs + p)),
            pl.BlockSpec((1, nslab, v_rows, qt), lambda i, p, j: (i, 0, v1_blk + p, 0)),
            pl.BlockSpec((1, nslab, v_rows, qt), lambda i, p, j: (i, 0, v2_blk + p, 0)),
        ],
        out_specs=pl.BlockSpec((1, qt, q_rows), lambda i, p, j: (i, j, p)),
        out_shape=jax.ShapeDtypeStruct((b, s, hq), BF16),
        scratch_shapes=[pltpu.VMEM((nsb, gl * qt), F32), pltpu.VMEM((nsb, gl * NSA_HPG * qt), F32)],
        compiler_params=_params("parallel", "parallel", "arbitrary"),
        name="nsa_attend",
    )(tr4, tr4, k_cmp, v_cmp_t, k12, k12, tr4, tr4)


def _nsa(h2, g_norm, w_in, cmp_pos, cmp_w1, cmp_w2, w_out, *, b, s):
    g, hpg, dh = NSA_GROUPS, NSA_HPG, NSA_DH
    hq, gd = g * hpg * dh, g * dh
    kv_w = lambda br, kv: w_in[:, hq + (2 * br + kv) * gd:hq + (2 * br + kv + 1) * gd]
    w_q = w_in[:, :hq] * (dh ** -0.5)
    w_g = w_in[:, hq + 6 * gd:].reshape(-1, 3, g, hpg).transpose(2, 1, 3, 0)
    w_g = jnp.pad(w_g, ((0, 0), (0, 1), (0, 0), (0, 0))).reshape(g * 4 * hpg, -1)
    w_t = jnp.concatenate([w_q.T, kv_w(1, 1).T, kv_w(2, 1).T, w_g], axis=0).astype(BF16)
    w_nat = jnp.concatenate([kv_w(0, 0), kv_w(0, 1), kv_w(1, 0), kv_w(2, 0)], axis=1).astype(BF16)
    k0, v0, k12, tr = _nsa_proj(h2, g_norm, w_nat, w_t)
    nb = s // NSA_CMP_STRIDE
    k_cmp = _nsa_compress(k0.reshape(b, nb, NSA_CMP_STRIDE * gd), cmp_pos[0], cmp_w1[0], cmp_w2[0], transposed=False)
    v_cmp_t = _nsa_compress(v0.reshape(b, nb, NSA_CMP_STRIDE * gd), cmp_pos[1], cmp_w1[1], cmp_w2[1], transposed=True)
    y = _nsa_attend_pairs(tr, k_cmp, v_cmp_t, k12.reshape(b, s, 2 * gd), b=b, s=s)
    return y.reshape(b * s, hq), w_out.astype(BF16)


def kernel(x, norm_ffn1, ffn1_w_in, ffn1_w_out, norm_mix, norm_ffn2, ffn2_w_in, ffn2_w_out, norm_final,
           ret_w_in, ret_gn_gain, ret_w_out,
           nsa_w_in, nsa_cmp_pos, nsa_cmp_w1, nsa_cmp_w2, nsa_w_out,
           ssd_w_in, ssd_conv_w, ssd_conv_b, ssd_dt_bias, ssd_a_log, ssd_d, ssd_norm, ssd_w_out,
           dil_w_in, dil_w_out):
    b, s, d = x.shape
    depth = norm_mix.shape[0]
    h = x.reshape(b * s, d)
    for i in range(depth):
        h = _ffn(h, norm_ffn1[i], ffn1_w_in[i].astype(BF16), ffn1_w_out[i].astype(BF16))
        m, j = i % 4, i // 4
        mix = None
        if m == 0:
            mix = _retention(h, norm_mix[i], ret_w_in[j], ret_gn_gain[j], ret_w_out[j], b=b, s=s)
        elif m == 1:
            mix = _nsa(h, norm_mix[i], nsa_w_in[j], nsa_cmp_pos[j], nsa_cmp_w1[j], nsa_cmp_w2[j], nsa_w_out[j], b=b, s=s)
        elif m == 2:
            mix = _ssd(h, norm_mix[i], ssd_w_in[j], ssd_conv_w[j], ssd_conv_b[j], ssd_dt_bias[j], ssd_a_log[j],
                       ssd_d[j], ssd_norm[j], ssd_w_out[j], b=b, s=s)
        else:
            h = _dilated(h, norm_mix[i], dil_w_in[j], dil_w_out[j], b=b, s=s)
        h = _ffn(h, norm_ffn2[i], ffn2_w_in[i].astype(BF16), ffn2_w_out[i].astype(BF16),
                 norm_final if i == depth - 1 else None, mix)
    return h.reshape(b, s, d)
```

```python
import functools
import math

import jax
import jax.numpy as jnp
from jax import lax
from jax.experimental import pallas as pl
from jax.experimental.pallas import tpu as pltpu

F32 = jnp.float32
BF16 = jnp.bfloat16
NORM_EPS = 1e-6
NEG_INF = -1e30
ROPE_BASE = 10000.0
VMEM_LIMIT_BYTES = 56 * 1024 * 1024
LANES = 128
TOKEN_TILE = 512
MAX_DOT_COLS = 2048

RET_HEADS = 4
RET_CHUNK = 128

NSA_GROUPS = 4
NSA_HPG = 4
NSA_DH = 64
NSA_CMP_LEN = 32
NSA_CMP_STRIDE = 16
NSA_SEL_LEN = 64
NSA_N_SEL = 16
NSA_WINDOW = 512
NSA_FORCE_BONUS = 1e4
NSA_QT = 256
NSA_KC = 512

SSD_GROUPS = 4
SSD_HPG = 8
SSD_HEADDIM = 64
SSD_STATE = 128
SSD_CONV = 4
SSD_CHUNK = 256

DIL_PATTERN = ((128, 1), (512, 4), (2048, 16))
DIL_HEADS = 8
DIL_DH = 128
DIL_QB = 128


def _params(*sem):
    return pltpu.CompilerParams(dimension_semantics=sem, vmem_limit_bytes=VMEM_LIMIT_BYTES)


def _rms(x, g):
    return x * lax.rsqrt(jnp.mean(x * x, axis=-1, keepdims=True) + NORM_EPS) * g


def _silu(x):
    return x * jax.nn.sigmoid(x)


def _dot(a, b):
    return jnp.dot(a, b, preferred_element_type=F32)


def _dot_nt(a, b):
    return lax.dot_general(a, b, (((1,), (1,)), ((), ())), preferred_element_type=F32)


def _split3(x):
    hi = x.astype(BF16)
    r1 = x - hi.astype(F32)
    mid = r1.astype(BF16)
    lo = (r1 - mid.astype(F32)).astype(BF16)
    return hi, mid, lo


def _dot3(terms, rhs):
    return _dot(terms[0], rhs) + _dot(terms[1], rhs) + _dot(terms[2], rhs)


def _col_chunk(n):
    if n <= MAX_DOT_COLS:
        return n
    return max(c for c in range(LANES, MAX_DOT_COLS + 1, LANES) if n % c == 0)


def _resident(shape, index=None):
    index = index if index is not None else (0,) * len(shape)
    return pl.BlockSpec(shape, lambda *_: index, pipeline_mode=pl.Buffered(1))


def _ffn_body(h_ref, *refs, fc, final_norm, mix):
    x = h_ref[...]
    if mix:
        y_ref, wm_ref, *refs = refs
        x = x + _dot(y_ref[...], wm_ref[...])
    g_ref, wa_ref, wb_ref, wo_ref, gf_ref, o_ref = refs
    xn = _rms(x, g_ref[...]).astype(BF16)
    acc = jnp.zeros(x.shape, F32)
    for j in range(wa_ref.shape[1] // fc):
        sl = slice(j * fc, (j + 1) * fc)
        a = _dot(xn, wa_ref[:, sl])
        b = _dot(xn, wb_ref[:, sl])
        acc = acc + _dot((_silu(a) * b).astype(BF16), wo_ref[sl, :])
    y = x + 0.5 * acc
    if final_norm:
        y = _rms(y, gf_ref[...])
    o_ref[...] = y


def _ffn(h, g, w_in, w_out, g_final=None, mix=None, *, tm=TOKEN_TILE):
    m, d = h.shape
    f = w_out.shape[0]
    final_norm = g_final is not None
    gf = g_final if final_norm else g
    mix_args = list(mix) if mix is not None else []
    mix_specs = [pl.BlockSpec((tm, mix[0].shape[1]), lambda i: (i, 0)), _resident(mix[1].shape)] if mix_args else []
    return pl.pallas_call(
        functools.partial(_ffn_body, fc=_col_chunk(f), final_norm=final_norm, mix=bool(mix_args)),
        grid=(m // tm,),
        in_specs=[pl.BlockSpec((tm, d), lambda i: (i, 0))] + mix_specs + [
            _resident((1, d)),
            _resident((d, f)),
            _resident((d, f), (0, 1)),
            _resident((f, d)),
            _resident((1, d)),
        ],
        out_specs=pl.BlockSpec((tm, d), lambda i: (i, 0)),
        out_shape=jax.ShapeDtypeStruct((m, d), F32),
        compiler_params=_params("parallel"),
        name="ffn",
    )(h, *mix_args, g.reshape(1, d), w_in, w_in, w_out, gf.reshape(1, d))


def _norm_proj_body(h_ref, g_ref, *refs, n_out):
    xn = _rms(h_ref[...], g_ref[...]).astype(BF16)
    for w_ref, o_ref in zip(refs[:n_out], refs[n_out:]):
        n = w_ref.shape[1]
        nc = _col_chunk(n)
        for j in range(n // nc):
            sl = slice(j * nc, (j + 1) * nc)
            o_ref[:, sl] = _dot(xn, w_ref[:, sl]).astype(o_ref.dtype)


def _norm_proj(h, g, ws, out_dtypes, *, tm=TOKEN_TILE):
    m, d = h.shape
    return pl.pallas_call(
        functools.partial(_norm_proj_body, n_out=len(ws)),
        grid=(m // tm,),
        in_specs=[pl.BlockSpec((tm, d), lambda i: (i, 0)), pl.BlockSpec((1, d), lambda i: (0, 0))]
        + [pl.BlockSpec(w.shape, lambda i: (0, 0)) for w in ws],
        out_specs=[pl.BlockSpec((tm, w.shape[1]), lambda i: (i, 0)) for w in ws],
        out_shape=[jax.ShapeDtypeStruct((m, w.shape[1]), dt) for w, dt in zip(ws, out_dtypes)],
        compiler_params=_params("parallel"),
        name="norm_proj",
    )(h, g.reshape(1, d), *ws)


def _out_proj_body(h_ref, y_ref, w_ref, o_ref):
    o_ref[...] = h_ref[...] + _dot(y_ref[...], w_ref[...])


def _out_proj(h, y, w, *, tm=TOKEN_TILE):
    m, d = h.shape
    k = y.shape[1]
    return pl.pallas_call(
        _out_proj_body,
        grid=(m // tm,),
        in_specs=[
            pl.BlockSpec((tm, d), lambda i: (i, 0)),
            pl.BlockSpec((tm, k), lambda i: (i, 0)),
            pl.BlockSpec((k, d), lambda i: (0, 0)),
        ],
        out_specs=pl.BlockSpec((tm, d), lambda i: (i, 0)),
        out_shape=jax.ShapeDtypeStruct((m, d), F32),
        compiler_params=_params("parallel"),
        name="out_proj",
    )(h, y, w)


def _ret_body(q_ref, k_ref, v_ref, g_ref, cos_ref, sin_ref, gn_ref, o_ref, state_ref, *, ts):
    c_len = RET_CHUNK
    dk = q_ref.shape[2] // RET_HEADS
    dv = v_ref.shape[2] // RET_HEADS
    half = dk // 2

    @pl.when(pl.program_id(1) == 0)
    def _():
        state_ref[...] = jnp.zeros(state_ref.shape, F32)

    ii = lax.broadcasted_iota(jnp.int32, (c_len, c_len), 0)
    jj = lax.broadcasted_iota(jnp.int32, (c_len, c_len), 1)
    rel = (ii - jj).astype(F32)
    causal = ii >= jj
    idx = lax.broadcasted_iota(jnp.int32, (c_len, 1), 0).astype(F32)

    def rot(t, cos, sin):
        t1, t2 = t[:, :half], t[:, half:]
        return jnp.concatenate([t1 * cos - t2 * sin, t1 * sin + t2 * cos], axis=1)

    def chunk(c, carry):
        r0 = pl.multiple_of(c * c_len, c_len)
        rows = pl.ds(r0, c_len)
        cos = cos_ref[rows, :]
        sin = sin_ref[rows, :]
        for h in range(RET_HEADS):
            log_gamma = math.log1p(-(2.0 ** (-5.0 - h)))
            inner = jnp.where(causal, jnp.exp(jnp.where(causal, rel, 0.0) * log_gamma), 0.0)
            q_decay = jnp.exp((idx + 1.0) * log_gamma)
            k_decay = jnp.exp((c_len - 1.0 - idx) * log_gamma)
            chunk_decay = math.exp(c_len * log_gamma)
            q = rot(q_ref[0, rows, h * dk:(h + 1) * dk].astype(F32), cos, sin)
            k = rot(k_ref[0, rows, h * dk:(h + 1) * dk].astype(F32), cos, sin) * (dk ** -0.5)
            v = v_ref[0, rows, h * dv:(h + 1) * dv]
            qb = q.astype(BF16)
            sc = _dot_nt(qb, k.astype(BF16)) * inner
            st = state_ref[h]
            o = _dot(sc.astype(BF16), v) + _dot(qb, st.astype(BF16)) * q_decay
            kd_t = jnp.transpose(k * k_decay).astype(BF16)
            state_ref[h] = st * chunk_decay + _dot(kd_t, v)
            gn = gn_ref[:, h * dv:(h + 1) * dv]
            on = _rms(o, gn)
            gate = g_ref[0, rows, h * dv:(h + 1) * dv].astype(F32)
            o_ref[0, rows, h * dv:(h + 1) * dv] = (_silu(gate) * on).astype(o_ref.dtype)
        return carry

    lax.fori_loop(0, ts // c_len, chunk, 0)


def _retention_core(proj, cos, sin, gn_gain, *, b, s, ts=TOKEN_TILE):
    n = proj.shape[2]
    hk = n // 6
    hv = 2 * hk
    dk = hk // RET_HEADS
    return pl.pallas_call(
        functools.partial(_ret_body, ts=ts),
        grid=(b, s // ts),
        in_specs=[
            pl.BlockSpec((1, ts, hk), lambda i, j: (i, j, 0)),
            pl.BlockSpec((1, ts, hk), lambda i, j: (i, j, 1)),
            pl.BlockSpec((1, ts, hv), lambda i, j: (i, j, 1)),
            pl.BlockSpec((1, ts, hv), lambda i, j: (i, j, 2)),
            pl.BlockSpec((ts, dk // 2), lambda i, j: (j, 0)),
            pl.BlockSpec((ts, dk // 2), lambda i, j: (j, 0)),
            pl.BlockSpec((1, hv), lambda i, j: (0, 0)),
        ],
        out_specs=pl.BlockSpec((1, ts, hv), lambda i, j: (i, j, 0)),
        out_shape=jax.ShapeDtypeStruct((b, s, hv), BF16),
        scratch_shapes=[pltpu.VMEM((RET_HEADS, dk, hv // RET_HEADS), F32)],
        compiler_params=_params("parallel", "arbitrary"),
        name="retention",
    )(proj, proj, proj, proj, cos, sin, gn_gain.reshape(1, hv))


def _rope_tables(s, half):
    inv = ROPE_BASE ** (-jnp.arange(half, dtype=F32) / half)
    ang = jnp.arange(s, dtype=F32)[:, None] * inv[None, :]
    return jnp.cos(ang), jnp.sin(ang)


def _retention(h2, g_norm, w_in, gn_gain, w_out, *, b, s):
    (proj,) = _norm_proj(h2, g_norm, [w_in.astype(BF16)], [BF16])
    n = proj.shape[1]
    cos, sin = _rope_tables(s, n // 6 // RET_HEADS // 2)
    y = _retention_core(proj.reshape(b, s, n), cos, sin, gn_gain, b=b, s=s)
    return y.reshape(b * s, -1), w_out.astype(BF16)


def _dil_proj_body(h_ref, g_ref, w_ref, o_ref, xn_ref, *, r):
    tm = h_ref.shape[0]
    n = tm // r
    xn = _rms(h_ref[...], g_ref[...])
    slabs = xn_ref.shape[0]
    for k in range(slabs):
        xn_ref[k] = xn[:, k * LANES:(k + 1) * LANES]
    xp = jnp.concatenate(
        [jnp.concatenate([xn_ref[k, pl.ds(c, n, stride=r), :] for c in range(r)], axis=0) for k in range(slabs)],
        axis=1).astype(BF16)
    nc = _col_chunk(w_ref.shape[1])
    for j in range(w_ref.shape[1] // nc):
        sl = slice(j * nc, (j + 1) * nc)
        res = _dot(xp, w_ref[:, sl]).astype(o_ref.dtype)
        for c in range(r):
            o_ref[0, c, :, sl] = res[c * n:(c + 1) * n, :]


def _dil_proj(h2, g_norm, w, r, *, b, s, tm=TOKEN_TILE):
    d = h2.shape[1]
    n_out = w.shape[1]
    tiles = s // tm
    return pl.pallas_call(
        functools.partial(_dil_proj_body, r=r),
        grid=(b * tiles,),
        in_specs=[
            pl.BlockSpec((tm, d), lambda i: (i, 0)),
            pl.BlockSpec((1, d), lambda i: (0, 0)),
            pl.BlockSpec((d, n_out), lambda i: (0, 0)),
        ],
        out_specs=pl.BlockSpec((1, r, tm // r, n_out), lambda i: (i // tiles, 0, i % tiles, 0)),
        out_shape=jax.ShapeDtypeStruct((b, r, s // r, n_out), BF16),
        scratch_shapes=[pltpu.VMEM((d // LANES, tm, LANES), F32)],
        compiler_params=_params("parallel"),
        name=f"dilated_proj_r{r}",
    )(h2, g_norm.reshape(1, d), w)


def _dil_body(q_ref, kc_ref, kp_ref, vc_ref, vp_ref, o_ref, lse_ref, kbuf, vbuf, *, rows, n_back):
    qb = DIL_QB
    dh = DIL_DH
    first_step = pl.program_id(2) == 0
    kbuf[0:qb, :] = kp_ref[0, 0]
    kbuf[qb:, :] = kc_ref[0, 0]
    vbuf[0:qb, :] = vp_ref[0, 0]
    vbuf[qb:, :] = vc_ref[0, 0]
    qi = lax.broadcasted_iota(jnp.int32, (qb, 2 * qb), 0)
    kj = lax.broadcasted_iota(jnp.int32, (qb, 2 * qb), 1)
    dist = qi + qb - kj
    band = (dist >= 0) & (dist <= n_back)
    lane = lax.broadcasted_iota(jnp.int32, (qb, LANES), 1)
    scale = dh ** -0.5
    for i in range(rows // qb):
        mask = band & ((kj >= qb) | jnp.logical_not(first_step)) if i == 0 else band
        lse_tile = jnp.zeros((qb, LANES), F32)
        for h in range(DIL_HEADS):
            cols = slice(h * dh, (h + 1) * dh)
            q = q_ref[0, 0, i * qb:(i + 1) * qb, cols]
            k = kbuf[i * qb:(i + 2) * qb, cols]
            v = vbuf[i * qb:(i + 2) * qb, cols]
            s = jnp.where(mask, _dot_nt(q, k) * scale, NEG_INF)
            m = jnp.max(s, axis=1, keepdims=True)
            e = jnp.where(mask, jnp.exp(s - m), 0.0)
            den = jnp.maximum(jnp.sum(e, axis=1, keepdims=True), 1e-30)
            o = _dot((e / den).astype(BF16), v)
            o_ref[0, 0, i * qb:(i + 1) * qb, cols] = o.astype(o_ref.dtype)
            lse_tile = jnp.where(lane == h, m + jnp.log(den), lse_tile)
        lse_ref[0, 0, i * qb:(i + 1) * qb, :] = lse_tile


def _dilated_group(proj, win, r, *, b, s):
    hd = DIL_HEADS * DIL_DH
    length = s // r
    rows = min(length, TOKEN_TILE)
    qb = DIL_QB

    def cur(which):
        return pl.BlockSpec((1, 1, rows, hd), lambda i, c, n: (i, c, n, which))

    def prev(which):
        return pl.BlockSpec((1, 1, qb, hd), lambda i, c, n: (i, c, jnp.maximum(n * (rows // qb) - 1, 0), which))

    return pl.pallas_call(
        functools.partial(_dil_body, rows=rows, n_back=win // r),
        grid=(b, r, length // rows),
        in_specs=[cur(0), cur(1), prev(1), cur(2), prev(2)],
        out_specs=[
            pl.BlockSpec((1, 1, rows, hd), lambda i, c, n: (i, c, n, 0)),
            pl.BlockSpec((1, 1, rows, LANES), lambda i, c, n: (i, c, n, 0)),
        ],
        out_shape=[
            jax.ShapeDtypeStruct((b, r, length, hd), BF16),
            jax.ShapeDtypeStruct((b, r, length, LANES), F32),
        ],
        scratch_shapes=[pltpu.VMEM((rows + qb, hd), BF16), pltpu.VMEM((rows + qb, hd), BF16)],
        compiler_params=_params("parallel", "parallel", "arbitrary"),
        name=f"dilated_r{r}",
    )(proj, proj, proj, proj, proj)


def _dil_merge_body(h_ref, *refs, dilations):
    ng = len(dilations)
    o_refs, l_refs = refs[:ng], refs[ng:2 * ng]
    w_ref, out_ref = refs[2 * ng], refs[2 * ng + 1]
    o_bufs, l_bufs = refs[2 * ng + 2:3 * ng + 2], refs[3 * ng + 2:]
    tm = h_ref.shape[1]
    for o_ref, l_ref, o_buf, l_buf, r in zip(o_refs, l_refs, o_bufs, l_bufs, dilations):
        n = tm // r
        for c in range(r):
            rows = pl.ds(c, n, stride=r)
            l_buf[rows, :] = l_ref[0, c]
            blk = o_ref[0, c].astype(F32)
            for h in range(DIL_HEADS):
                o_buf[h, rows, :] = blk[:, h * DIL_DH:(h + 1) * DIL_DH]
    lses = [l[...] for l in l_bufs]
    mx = functools.reduce(jnp.maximum, lses)
    es = [jnp.exp(l - mx) for l in lses]
    tot = functools.reduce(jnp.add, es)
    wts = [e / tot for e in es]
    dh = DIL_DH
    parts = []
    for h in range(DIL_HEADS):
        acc = jnp.zeros((tm, dh), F32)
        for g in range(ng):
            wg = jnp.broadcast_to(wts[g][:, h:h + 1], (tm, dh))
            acc = acc + wg * o_bufs[g][h]
        parts.append(acc)
    o = jnp.concatenate(parts, axis=1).astype(BF16)
    out_ref[0] = h_ref[0] + _dot(o, w_ref[...])


def _dil_merge(h3, outs, lses, w, dilations, *, tm=TOKEN_TILE):
    b, s, d = h3.shape
    hd = w.shape[0]
    res_major = lambda r, width: pl.BlockSpec((1, r, tm // r, width), lambda i, n: (i, 0, n, 0))
    return pl.pallas_call(
        functools.partial(_dil_merge_body, dilations=tuple(dilations)),
        grid=(b, s // tm),
        in_specs=[pl.BlockSpec((1, tm, d), lambda i, n: (i, n, 0))]
        + [res_major(r, hd) for r in dilations] + [res_major(r, LANES) for r in dilations]
        + [pl.BlockSpec((hd, d), lambda i, n: (0, 0))],
        out_specs=pl.BlockSpec((1, tm, d), lambda i, n: (i, n, 0)),
        out_shape=jax.ShapeDtypeStruct((b, s, d), F32),
        scratch_shapes=[pltpu.VMEM((DIL_HEADS, tm, DIL_DH), F32) for _ in dilations]
        + [pltpu.VMEM((tm, LANES), F32) for _ in dilations],
        compiler_params=_params("parallel", "parallel"),
        name="dilated_merge",
    )(h3, *outs, *lses, w)


def _dilated(h2, g_norm, w_in, w_out, *, b, s):
    d = h2.shape[1]
    gw = 3 * DIL_HEADS * DIL_DH
    wb = w_in.astype(BF16)
    outs, lses, dilations = [], [], []
    for g, (win, r) in enumerate(DIL_PATTERN):
        proj = _dil_proj(h2, g_norm, wb[:, g * gw:(g + 1) * gw], r, b=b, s=s)
        o, lse = _dilated_group(proj, win, r, b=b, s=s)
        outs.append(o)
        lses.append(lse)
        dilations.append(r)
    return _dil_merge(h2.reshape(b, s, d), outs, lses, w_out.astype(BF16), dilations).reshape(b * s, d)


def _softplus(x):
    return jnp.maximum(x, 0.0) + jnp.log1p(jnp.exp(-jnp.abs(x)))


def _ssd_body(z_ref, x_ref, dt_ref, cw_ref, cb_ref, dtb_ref, alog_ref, dsk_ref, ng_ref, o_ref,
              xpad, state_ref):
    ln = SSD_CHUNK
    nst = SSD_STATE
    gw = SSD_HPG * SSD_HEADDIM
    d_inner = SSD_GROUPS * gw
    halo = 8

    @pl.when(pl.program_id(1) == 0)
    def _():
        xpad[0:halo, :] = jnp.zeros((halo, xpad.shape[1]), F32)
        state_ref[...] = jnp.zeros(state_ref.shape, F32)

    xpad[halo:halo + ln, :] = x_ref[0].astype(F32)
    conv = cb_ref[...]
    for k in range(SSD_CONV):
        off = halo - (SSD_CONV - 1) + k
        conv = conv + xpad[off:off + ln, :] * cw_ref[k:k + 1, :]
    xpad[0:halo, :] = xpad[ln:ln + halo, :]
    xbc = _silu(conv)
    xs = xbc[:, :d_inner]
    bm = xbc[:, d_inner:d_inner + SSD_GROUPS * nst]
    cm = xbc[:, d_inner + SSD_GROUPS * nst:]

    dt = _softplus(dt_ref[0] + dtb_ref[...])
    da = dt * (-jnp.exp(alog_ref[...]))
    ii = lax.broadcasted_iota(jnp.int32, (ln, ln), 0)
    jj = lax.broadcasted_iota(jnp.int32, (ln, ln), 1)
    causal = ii >= jj
    tril = jnp.where(causal, 1.0, 0.0).astype(BF16)
    da_terms = _split3(da)
    acs = _dot(tril, da_terms[0]) + _dot(tril, da_terms[1]) + _dot(tril, da_terms[2])
    acs_t = jnp.transpose(acs)

    erow = lax.broadcasted_iota(jnp.int32, (LANES, d_inner), 0)
    ecol = lax.broadcasted_iota(jnp.int32, (LANES, d_inner), 1)
    expand = jnp.where(ecol // SSD_HEADDIM == erow, 1.0, 0.0).astype(BF16)
    acs_e = _dot3(_split3(acs), expand)
    dt_e = _dot3(_split3(dt), expand)
    last = acs_e[ln - 1:ln, :]
    decay_in = jnp.exp(acs_e)
    xs_dt = xs * dt_e
    xs_end = (xs_dt * jnp.exp(last - acs_e)).astype(BF16)
    xs_b = xs_dt.astype(BF16)
    lane = lax.broadcasted_iota(jnp.int32, (ln, LANES), 1)

    y_groups = []
    for g in range(SSD_GROUPS):
        bm_g = bm[:, g * nst:(g + 1) * nst]
        cm_g = cm[:, g * nst:(g + 1) * nst].astype(BF16)
        cb = jnp.where(causal, _dot_nt(cm_g, bm_g.astype(BF16)), 0.0)
        st = state_ref[g]
        gcols = slice(g * gw, (g + 1) * gw)
        y_state = _dot(cm_g, st.astype(BF16)) * decay_in[:, gcols]
        pairs = []
        for p in range(SSD_HPG // 2):
            pair_cols = slice(g * gw + p * LANES, g * gw + (p + 1) * LANES)
            halves = []
            for e in range(2):
                hd = g * SSD_HPG + 2 * p + e
                diff = acs[:, hd:hd + 1] - acs_t[hd:hd + 1, :]
                w = (cb * jnp.exp(jnp.minimum(diff, 0.0))).astype(BF16)
                halves.append(_dot(w, xs_b[:, pair_cols]))
            pairs.append(jnp.where(lane < SSD_HEADDIM, halves[0], halves[1]))
        y_groups.append(jnp.concatenate(pairs, axis=1) + y_state)
        bm_t = jnp.transpose(bm_g).astype(BF16)
        state_ref[g] = st * jnp.exp(last[:, gcols]) + _dot(bm_t, xs_end[:, gcols])

    y = jnp.concatenate(y_groups, axis=1) + dsk_ref[...] * xs
    yz = y * _silu(z_ref[0].astype(F32))
    outs = [_rms(yz[:, g * gw:(g + 1) * gw], ng_ref[:, g * gw:(g + 1) * gw]) for g in range(SSD_GROUPS)]
    o_ref[0] = jnp.concatenate(outs, axis=1).astype(o_ref.dtype)


def _ssd_core(z, xbc, dt, conv_w, conv_b, dt_bias, a_log, d_skip, norm_g, *, b, s):
    ln = SSD_CHUNK
    d_inner = z.shape[2]
    conv_dim = xbc.shape[2]
    gw = SSD_HPG * SSD_HEADDIM
    full = lambda shape: pl.BlockSpec(shape, lambda i, j: (0,) * len(shape))
    return pl.pallas_call(
        _ssd_body,
        grid=(b, s // ln),
        in_specs=[
            pl.BlockSpec((1, ln, d_inner), lambda i, j: (i, j, 0)),
            pl.BlockSpec((1, ln, conv_dim), lambda i, j: (i, j, 0)),
            pl.BlockSpec((1, ln, LANES), lambda i, j: (i, j, 0)),
            full((SSD_CONV, conv_dim)),
            full((1, conv_dim)),
            full((1, LANES)),
            full((1, LANES)),
            full((1, d_inner)),
            full((1, d_inner)),
        ],
        out_specs=pl.BlockSpec((1, ln, d_inner), lambda i, j: (i, j, 0)),
        out_shape=jax.ShapeDtypeStruct((b, s, d_inner), BF16),
        scratch_shapes=[pltpu.VMEM((ln + 8, conv_dim), F32),
                        pltpu.VMEM((SSD_GROUPS, SSD_STATE, gw), F32)],
        compiler_params=_params("parallel", "arbitrary"),
        name="ssd",
    )(z, xbc, dt, conv_w, conv_b, dt_bias, a_log, d_skip, norm_g)


def _ssd(h2, g_norm, w_in, conv_w, conv_b, dt_bias, a_log, d_skip, norm_g, w_out, *, b, s):
    heads = SSD_GROUPS * SSD_HPG
    d_inner = heads * SSD_HEADDIM
    conv_dim = conv_w.shape[1]
    wb = w_in.astype(BF16)
    w_dt = jnp.pad(wb[:, d_inner + conv_dim:], ((0, 0), (0, LANES - heads)))
    z, xbc, dt = _norm_proj(h2, g_norm, [wb[:, :d_inner], wb[:, d_inner:d_inner + conv_dim], w_dt], [BF16, BF16, F32])
    pad_heads = lambda v: jnp.pad(v.astype(F32), (0, LANES - heads)).reshape(1, LANES)
    y = _ssd_core(z.reshape(b, s, -1), xbc.reshape(b, s, -1), dt.reshape(b, s, LANES),
                  conv_w.astype(F32), conv_b.reshape(1, -1).astype(F32), pad_heads(dt_bias), pad_heads(a_log),
                  jnp.repeat(d_skip.astype(F32), SSD_HEADDIM).reshape(1, d_inner), norm_g.reshape(1, d_inner),
                  b=b, s=s)
    return y.reshape(b * s, d_inner), w_out.astype(BF16)


def _nsa_proj_body(h_ref, g_ref, wn_ref, wt_ref, k0_ref, v0_ref, k12_ref, tr_ref):
    xn = _rms(h_ref[...], g_ref[...]).astype(BF16)
    nat = _dot(xn, wn_ref[...])
    gd = k0_ref.shape[1]
    k0_ref[...] = nat[:, :gd].astype(k0_ref.dtype)
    v0_ref[...] = nat[:, gd:2 * gd].astype(v0_ref.dtype)
    k12_ref[...] = nat[:, 2 * gd:].astype(k12_ref.dtype)
    res = _dot_nt(wt_ref[...], xn)
    for j in range(tr_ref.shape[0]):
        tr_ref[j] = res[:, j * NSA_QT:(j + 1) * NSA_QT].astype(tr_ref.dtype)


def _nsa_proj(h2, g_norm, w_nat, w_t, *, tm=TOKEN_TILE):
    m, d = h2.shape
    gd = NSA_GROUPS * NSA_DH
    nt = w_t.shape[0]
    slabs = tm // NSA_QT
    return pl.pallas_call(
        _nsa_proj_body,
        grid=(m // tm,),
        in_specs=[
            pl.BlockSpec((tm, d), lambda i: (i, 0)),
            pl.BlockSpec((1, d), lambda i: (0, 0)),
            pl.BlockSpec(w_nat.shape, lambda i: (0, 0)),
            pl.BlockSpec(w_t.shape, lambda i: (0, 0)),
        ],
        out_specs=[
            pl.BlockSpec((tm, gd), lambda i: (i, 0)),
            pl.BlockSpec((tm, gd), lambda i: (i, 0)),
            pl.BlockSpec((tm, 2 * gd), lambda i: (i, 0)),
            pl.BlockSpec((slabs, nt, NSA_QT), lambda i: (i, 0, 0)),
        ],
        out_shape=[
            jax.ShapeDtypeStruct((m, gd), BF16),
            jax.ShapeDtypeStruct((m, gd), BF16),
            jax.ShapeDtypeStruct((m, 2 * gd), BF16),
            jax.ShapeDtypeStruct((m // NSA_QT, nt, NSA_QT), BF16),
        ],
        compiler_params=_params("parallel"),
        name="nsa_proj",
    )(h2, g_norm.reshape(1, d), w_nat, w_t)


def _nsa_cmp_body(x_ref, pa_ref, pb_ref, w1a_ref, w1b_ref, w2_ref, o_ref, *, transposed):
    x = x_ref[0].astype(F32)
    nb = x.shape[0]
    ya = _dot((x + pa_ref[...]).astype(BF16), w1a_ref[...])
    yb = _dot((x + pb_ref[...]).astype(BF16), w1b_ref[...])
    hid = _silu(ya + pltpu.roll(yb, nb - 1, 0))
    out = _dot(hid.astype(BF16), w2_ref[...])
    o_ref[0] = (jnp.transpose(out) if transposed else out).astype(o_ref.dtype)


def _nsa_compress(x, pos, w1, w2, *, transposed):
    b, nb, width = x.shape
    g, dh = NSA_GROUPS, NSA_DH
    half = NSA_CMP_LEN // 2
    hid = w1.shape[1]
    eye = jnp.eye(g, dtype=F32)
    w1r = w1.reshape(NSA_CMP_LEN, dh, hid)
    big = jnp.einsum("ldj,gh->lgdhj", w1r, eye).reshape(NSA_CMP_LEN, g * dh, g * hid)
    w1a = big[:half].reshape(half * g * dh, g * hid).astype(BF16)
    w1b = big[half:].reshape(half * g * dh, g * hid).astype(BF16)
    w2big = jnp.einsum("jd,gh->gjhd", w2, eye).reshape(g * hid, g * dh).astype(BF16)
    posb = jnp.broadcast_to(pos[:, None, :], (NSA_CMP_LEN, g, dh)).reshape(NSA_CMP_LEN, g * dh)
    pa = posb[:half].reshape(1, width).astype(F32)
    pb = posb[half:].reshape(1, width).astype(F32)
    full = lambda shape: pl.BlockSpec(shape, lambda i: (0,) * len(shape))
    out_dims = (g * dh, nb) if transposed else (nb, g * dh)
    return pl.pallas_call(
        functools.partial(_nsa_cmp_body, transposed=transposed),
        grid=(b,),
        in_specs=[
            pl.BlockSpec((1, nb, width), lambda i: (i, 0, 0)),
            full((1, width)), full((1, width)),
            full((width, g * hid)), full((width, g * hid)), full((g * hid, g * dh)),
        ],
        out_specs=pl.BlockSpec((1,) + out_dims, lambda i: (i, 0, 0)),
        out_shape=jax.ShapeDtypeStruct((b,) + out_dims, BF16),
        compiler_params=_params("parallel"),
        name="nsa_compress",
    )(x, pa, pb, w1a, w1b, w2big)


def _nsa_body(q_ref, gt_ref, kc_ref, vct_ref, k1_ref, k2_ref, v1_ref, v2_ref, o_ref, score_ref, bias_ref, *, s):
    qt, kc, dh = NSA_QT, NSA_KC, NSA_DH
    lanes = NSA_HPG * qt
    g = pl.program_id(1)
    j = pl.program_id(2)
    t0 = j * qt
    nb = kc_ref.shape[1]
    nsb = s // NSA_SEL_LEN
    n_sel = min(NSA_N_SEL, nsb)

    q_t = q_ref[0, 0]
    qcat = jnp.concatenate([q_t[h * dh:(h + 1) * dh, :] for h in range(NSA_HPG)], axis=1)
    odd = (g % 2) == 1
    zero = jnp.zeros_like(qcat)
    qp = jnp.concatenate([jnp.where(odd, zero, qcat), jnp.where(odd, qcat, zero)], axis=0)
    qlane = lax.broadcasted_iota(jnp.int32, (1, lanes), 1) % qt
    tq = t0 + qlane

    sc = _dot(kc_ref[0], qp)
    n_idx = lax.broadcasted_iota(jnp.int32, (nb, lanes), 0)
    cmask = (n_idx * NSA_CMP_STRIDE + NSA_CMP_LEN - 1 <= tq) & (n_idx < nb - 1)
    sc = jnp.where(cmask, sc, NEG_INF)
    e = jnp.where(cmask, jnp.exp(sc - jnp.max(sc, axis=0, keepdims=True)), 0.0)
    p_c = e / jnp.maximum(jnp.sum(e, axis=0, keepdims=True), 1e-30)
    o_c = _dot(vct_ref[0], p_c.astype(BF16))
    imp = p_c[:, 0:qt]
    for h in range(1, NSA_HPG):
        imp = imp + p_c[:, h * qt:(h + 1) * qt]

    ratio = NSA_SEL_LEN // NSA_CMP_STRIDE
    dd = lax.broadcasted_iota(jnp.int32, (nsb, nb), 1) - ratio * lax.broadcasted_iota(jnp.int32, (nsb, nb), 0)
    wsel = jnp.where((dd == -1) | (dd == ratio - 1), 1.0,
                     jnp.where((dd >= 0) & (dd < ratio - 1), 2.0, 0.0)).astype(BF16)
    terms = _split3(imp)
    imp_sel = _dot(wsel, terms[0]) + _dot(wsel, terms[1]) + _dot(wsel, terms[2])
    jq = lax.broadcasted_iota(jnp.int32, (nsb, qt), 0)
    cur = (t0 + lax.broadcasted_iota(jnp.int32, (nsb, qt), 1)) // NSA_SEL_LEN
    forced = (jq == 0) | (jq == cur) | (jq == cur - 1)
    score = jnp.where(jq <= cur, imp_sel + jnp.where(forced, NSA_FORCE_BONUS, 0.0), NEG_INF)
    score_ref[...] = score

    def rank_step(jp, rank):
        row = jnp.broadcast_to(score_ref[pl.ds(jp, 1), :], (nsb, qt))
        before = (row > score) | ((row == score) & (jq > jp))
        return rank + jnp.where(before, 1.0, 0.0)

    n_visible = (t0 + qt - 1) // NSA_SEL_LEN + 1
    rank = lax.fori_loop(0, n_visible, rank_step, jnp.zeros((nsb, qt), F32))
    sel_bias = jnp.where((rank < n_sel) & (score > 0.5 * NEG_INF), 0.0, NEG_INF)
    bias_ref[...] = jnp.concatenate([sel_bias] * NSA_HPG, axis=1)

    ones_rows = jnp.ones((16, kc), BF16)

    def values(v_ref, slab0, n_slabs):
        v = jnp.concatenate([v_ref[0, slab0 + i] for i in range(n_slabs)], axis=1)
        return jnp.concatenate([v, ones_rows[:, :n_slabs * qt]], axis=0)

    def online(carry, st, v_aug):
        m, acc = carry
        m_new = jnp.maximum(m, jnp.max(st, axis=0, keepdims=True))
        p = jnp.exp(st - m_new).astype(BF16)
        return m_new, jnp.exp(m - m_new) * acc + _dot(v_aug, p)

    def normalise(acc):
        return acc[:dh, :] / jnp.maximum(acc[dh:dh + 1, :], 1e-30)

    init = (jnp.full((1, lanes), NEG_INF, F32), jnp.zeros((dh + 16, lanes), F32))

    blocks_per_chunk = kc // NSA_SEL_LEN
    slabs_per_chunk = kc // qt

    def sel_scores(c):
        k = k1_ref[0, pl.ds(pl.multiple_of(c * kc, kc), kc), :]
        bias = bias_ref[pl.ds(pl.multiple_of(c * blocks_per_chunk, blocks_per_chunk), blocks_per_chunk), :]
        rows = [jnp.broadcast_to(bias[r:r + 1, :], (NSA_SEL_LEN, lanes)) for r in range(blocks_per_chunk)]
        return _dot(k, qp) + jnp.concatenate(rows, axis=0)

    def sel_values(c):
        return values(v1_ref, c * slabs_per_chunk, slabs_per_chunk)

    c_hi = t0 // kc
    row_k = lax.broadcasted_iota(jnp.int32, (kc, lanes), 0)
    st = jnp.where(row_k <= (t0 - c_hi * kc) + qlane, sel_scores(c_hi), NEG_INF)
    chain_a = online(init, st, sel_values(c_hi))
    chain_a = lax.cond(c_hi % 2 == 1,
                       lambda cr: online(cr, sel_scores(c_hi - 1), sel_values(c_hi - 1)),
                       lambda cr: cr, chain_a)
    chain_b = (chain_a[0], jnp.zeros_like(chain_a[1]))

    def pair(i, chains):
        ca, cb = chains
        st_a = sel_scores(2 * i)
        st_b = sel_scores(2 * i + 1)
        return online(ca, st_a, sel_values(2 * i)), online(cb, st_b, sel_values(2 * i + 1))

    (m_a, acc_a), (m_b, acc_b) = lax.fori_loop(0, c_hi // 2, pair, (chain_a, chain_b))
    m_s = jnp.maximum(m_a, m_b)
    o_s = normalise(jnp.exp(m_a - m_s) * acc_a + jnp.exp(m_b - m_s) * acc_b)

    wslabs = NSA_WINDOW // qt + 1
    jb0 = jnp.maximum(j + 1 - wslabs, 0)
    kw = k2_ref[0, pl.ds(pl.multiple_of(jb0 * qt, qt), wslabs * qt), :]
    back = (j - jb0) * qt + (qlane - lax.broadcasted_iota(jnp.int32, (wslabs * qt, lanes), 0))
    sw = jnp.where(lax.bitcast_convert_type(back, jnp.uint32) < jnp.uint32(NSA_WINDOW), _dot(kw, qp), NEG_INF)
    pw = jnp.exp(sw - jnp.max(sw, axis=0, keepdims=True)).astype(BF16)
    ones_w = jnp.ones((16, wslabs * qt), BF16)
    vw = jnp.concatenate([v2_ref[0, jb0 + i] for i in range(wslabs)], axis=1)
    o_w = normalise(_dot(jnp.concatenate([vw, ones_w], axis=0), pw))

    gates = jax.nn.sigmoid(gt_ref[0, 0].astype(F32))

    def gate_row(br):
        return jnp.concatenate([gates[br * NSA_HPG + h:br * NSA_HPG + h + 1, :] for h in range(NSA_HPG)], axis=1)

    o = gate_row(0) * o_c + gate_row(1) * o_s + gate_row(2) * o_w
    o_hd = jnp.concatenate([o[:, h * qt:(h + 1) * qt] for h in range(NSA_HPG)], axis=0)
    o_ref[0] = jnp.transpose(o_hd).astype(o_ref.dtype)


def _nsa_attend(tr, k_cmp, v_cmp_t, k12, *, b, s):
    qt, dh = NSA_QT, NSA_DH
    gd = NSA_GROUPS * dh
    hq = NSA_GROUPS * NSA_HPG * dh
    nslab = s // qt
    nb = k_cmp.shape[1]
    tr4 = tr.reshape(b, nslab, tr.shape[1], qt)
    gate_rows = 4 * NSA_HPG
    v1_blk = hq // dh
    v2_blk = (hq + gd) // dh
    gate_blk = (hq + 2 * gd) // gate_rows
    nsb = s // NSA_SEL_LEN
    return pl.pallas_call(
        functools.partial(_nsa_body, s=s),
        grid=(b, NSA_GROUPS, nslab),
        in_specs=[
            pl.BlockSpec((1, 1, NSA_HPG * dh, qt), lambda i, g, j: (i, j, g, 0)),
            pl.BlockSpec((1, 1, gate_rows, qt), lambda i, g, j: (i, j, gate_blk + g, 0)),
            pl.BlockSpec((1, nb, 2 * dh), lambda i, g, j: (i, 0, g // 2)),
            pl.BlockSpec((1, dh, nb), lambda i, g, j: (i, g, 0)),
            pl.BlockSpec((1, s, 2 * dh), lambda i, g, j: (i, 0, g // 2)),
            pl.BlockSpec((1, s, 2 * dh), lambda i, g, j: (i, 0, NSA_GROUPS // 2 + g // 2)),
            pl.BlockSpec((1, nslab, dh, qt), lambda i, g, j: (i, 0, v1_blk + g, 0)),
            pl.BlockSpec((1, nslab, dh, qt), lambda i, g, j: (i, 0, v2_blk + g, 0)),
        ],
        out_specs=pl.BlockSpec((1, qt, NSA_HPG * dh), lambda i, g, j: (i, j, g)),
        out_shape=jax.ShapeDtypeStruct((b, s, hq), BF16),
        scratch_shapes=[pltpu.VMEM((nsb, qt), F32), pltpu.VMEM((nsb, NSA_HPG * qt), F32)],
        compiler_params=_params("parallel", "parallel", "arbitrary"),
        name="nsa_attend",
    )(tr4, tr4, k_cmp, v_cmp_t, k12, k12, tr4, tr4)


def _nsa_pair_body(q_ref, gt_ref, kc_ref, vct_ref, k1_ref, k2_ref, v1_ref, v2_ref, o_ref, score_ref, bias_ref, *, s):
    qt, kc, dh = NSA_QT, NSA_KC, NSA_DH
    gl = 2
    glanes = NSA_HPG * qt
    lanes = gl * glanes
    vrows = gl * dh
    j = pl.program_id(2)
    t0 = j * qt
    nb = kc_ref.shape[1]
    nsb = s // NSA_SEL_LEN
    n_sel = min(NSA_N_SEL, nsb)

    q_t = q_ref[0, 0]
    qcat = [jnp.concatenate([q_t[(g * NSA_HPG + h) * dh:(g * NSA_HPG + h + 1) * dh, :] for h in range(NSA_HPG)],
                            axis=1) for g in range(gl)]
    zero = jnp.zeros_like(qcat[0])
    qp = jnp.concatenate([jnp.concatenate([qcat[0], zero], axis=1),
                          jnp.concatenate([zero, qcat[1]], axis=1)], axis=0)
    qlane = lax.broadcasted_iota(jnp.int32, (1, lanes), 1) % qt
    tq = t0 + qlane

    def values(v_ref, slab0, n_slabs):
        v = jnp.concatenate([v_ref[0, slab0 + i] for i in range(n_slabs)], axis=1)
        return jnp.concatenate([v, jnp.ones((16, n_slabs * qt), BF16)], axis=0)

    def weighted_values(v_aug, p):
        return jnp.concatenate(
            [_dot(jnp.concatenate([v_aug[g * dh:(g + 1) * dh, :], v_aug[vrows:, :]], axis=0),
                  p[:, g * glanes:(g + 1) * glanes]) for g in range(gl)], axis=1)

    def normalise(acc):
        o = acc[:dh, :] / jnp.maximum(acc[dh:dh + 1, :], 1e-30)
        return [o[:, g * glanes:(g + 1) * glanes] for g in range(gl)]

    wslabs = NSA_WINDOW // qt + 1
    jb0 = jnp.maximum(j + 1 - wslabs, 0)
    kw = k2_ref[0, pl.ds(pl.multiple_of(jb0 * qt, qt), wslabs * qt), :]
    back = (j - jb0) * qt + (qlane - lax.broadcasted_iota(jnp.int32, (wslabs * qt, lanes), 0))
    sw = jnp.where(lax.bitcast_convert_type(back, jnp.uint32) < jnp.uint32(NSA_WINDOW), _dot(kw, qp), NEG_INF)

    sc = _dot(kc_ref[0], qp)
    n_idx = lax.broadcasted_iota(jnp.int32, (nb, lanes), 0)
    cmask = (n_idx * NSA_CMP_STRIDE + NSA_CMP_LEN - 1 <= tq) & (n_idx < nb - 1)
    sc = jnp.where(cmask, sc, NEG_INF)
    e = jnp.where(cmask, jnp.exp(sc - jnp.max(sc, axis=0, keepdims=True)), 0.0)
    p_c = e / jnp.maximum(jnp.sum(e, axis=0, keepdims=True), 1e-30)
    acc_c = _dot(vct_ref[0], p_c.astype(BF16))
    o_c = [acc_c[g * dh:(g + 1) * dh, g * glanes:(g + 1) * glanes] for g in range(gl)]
    imps = []
    for g in range(gl):
        imp = p_c[:, g * glanes:g * glanes + qt]
        for h in range(1, NSA_HPG):
            imp = imp + p_c[:, g * glanes + h * qt:g * glanes + (h + 1) * qt]
        imps.append(imp)
    imp = jnp.concatenate(imps, axis=1)

    ratio = NSA_SEL_LEN // NSA_CMP_STRIDE
    dd = lax.broadcasted_iota(jnp.int32, (nsb, nb), 1) - ratio * lax.broadcasted_iota(jnp.int32, (nsb, nb), 0)
    wsel = jnp.where((dd == -1) | (dd == ratio - 1), 1.0,
                     jnp.where((dd >= 0) & (dd < ratio - 1), 2.0, 0.0)).astype(BF16)
    terms = _split3(imp)
    imp_sel = _dot(wsel, terms[0]) + _dot(wsel, terms[1]) + _dot(wsel, terms[2])
    jq = lax.broadcasted_iota(jnp.int32, (nsb, gl * qt), 0)
    cur = (t0 + lax.broadcasted_iota(jnp.int32, (nsb, gl * qt), 1) % qt) // NSA_SEL_LEN
    forced = (jq == 0) | (jq == cur) | (jq == cur - 1)
    score = jnp.where(jq <= cur, imp_sel + jnp.where(forced, NSA_FORCE_BONUS, 0.0), NEG_INF)
    score_ref[...] = score

    pw = jnp.exp(sw - jnp.max(sw, axis=0, keepdims=True)).astype(BF16)
    o_w = normalise(weighted_values(values(v2_ref, jb0, wslabs), pw))

    def rank_step(jp, rank):
        row = jnp.broadcast_to(score_ref[pl.ds(jp, 1), :], (nsb, gl * qt))
        before = (row > score) | ((row == score) & (jq > jp))
        return rank + jnp.where(before, 1.0, 0.0)

    n_visible = (t0 + qt - 1) // NSA_SEL_LEN + 1
    n_ranked = jnp.where(n_visible > n_sel, n_visible, 0)
    rank = lax.fori_loop(0, n_ranked, rank_step, jnp.zeros((nsb, gl * qt), F32))
    sel_bias = jnp.where((rank < n_sel) & (score > 0.5 * NEG_INF), 0.0, NEG_INF)
    bias_ref[...] = jnp.concatenate([sel_bias[:, g * qt:(g + 1) * qt] for g in range(gl) for _ in range(NSA_HPG)],
                                    axis=1)

    blocks_per_chunk = kc // NSA_SEL_LEN
    slabs_per_chunk = kc // qt

    def online(carry, st, bias, v_aug):
        m, acc = carry
        blks = [st[r * NSA_SEL_LEN:(r + 1) * NSA_SEL_LEN, :] for r in range(blocks_per_chunk)]
        part = None
        for r in range(blocks_per_chunk):
            pr = blks[r][0:8, :]
            for i in range(1, NSA_SEL_LEN // 8):
                pr = jnp.maximum(pr, blks[r][8 * i:8 * (i + 1), :])
            pr = pr + bias[r:r + 1, :]
            part = pr if part is None else jnp.maximum(part, pr)
        m_new = jnp.maximum(m, jnp.max(part, axis=0, keepdims=True))
        p = jnp.concatenate([jnp.exp(blks[r] - (m_new - bias[r:r + 1, :])) for r in range(blocks_per_chunk)],
                            axis=0).astype(BF16)
        return m_new, jnp.exp(m - m_new) * acc + weighted_values(v_aug, p)

    def sel_scores(c):
        return _dot(k1_ref[0, pl.ds(pl.multiple_of(c * kc, kc), kc), :], qp)

    def sel_bias(c):
        return bias_ref[pl.ds(pl.multiple_of(c * blocks_per_chunk, blocks_per_chunk), blocks_per_chunk), :]

    def sel_values(c):
        return values(v1_ref, c * slabs_per_chunk, slabs_per_chunk)

    init = (jnp.full((1, lanes), NEG_INF, F32), jnp.zeros((dh + 16, lanes), F32))
    c_hi = t0 // kc
    row_k = lax.broadcasted_iota(jnp.int32, (kc, lanes), 0)
    st = jnp.where(row_k <= (t0 - c_hi * kc) + qlane, sel_scores(c_hi), NEG_INF)
    chain_a = online(init, st, sel_bias(c_hi), sel_values(c_hi))
    chain_a = lax.cond(c_hi % 2 == 1,
                       lambda cr: online(cr, sel_scores(c_hi - 1), sel_bias(c_hi - 1), sel_values(c_hi - 1)),
                       lambda cr: cr, chain_a)
    chain_b = (chain_a[0], jnp.zeros_like(chain_a[1]))

    def pair(i, chains):
        ca, cb = chains
        st_a = sel_scores(2 * i)
        st_b = sel_scores(2 * i + 1)
        return (online(ca, st_a, sel_bias(2 * i), sel_values(2 * i)),
                online(cb, st_b, sel_bias(2 * i + 1), sel_values(2 * i + 1)))

    (m_a, acc_a), (m_b, acc_b) = lax.fori_loop(0, c_hi // 2, pair, (chain_a, chain_b))
    m_s = jnp.maximum(m_a, m_b)
    o_s = normalise(jnp.exp(m_a - m_s) * acc_a + jnp.exp(m_b - m_s) * acc_b)

    gates = jax.nn.sigmoid(gt_ref[0, 0].astype(F32))
    outs = []
    for g in range(gl):
        def gate_row(br):
            r0 = (g * 4 + br) * NSA_HPG
            return jnp.concatenate([gates[r0 + h:r0 + h + 1, :] for h in range(NSA_HPG)], axis=1)
        o = gate_row(0) * o_c[g] + gate_row(1) * o_s[g] + gate_row(2) * o_w[g]
        outs.extend(o[:, h * qt:(h + 1) * qt] for h in range(NSA_HPG))
    o_ref[0] = jnp.transpose(jnp.concatenate(outs, axis=0)).astype(o_ref.dtype)


def _nsa_attend_pairs(tr, k_cmp, v_cmp_t, k12, *, b, s):
    qt, dh = NSA_QT, NSA_DH
    gl = 2
    gd = NSA_GROUPS * dh
    hq = NSA_GROUPS * NSA_HPG * dh
    pairs = NSA_GROUPS // gl
    nslab = s // qt
    assert nslab > NSA_WINDOW // qt
    nb = k_cmp.shape[1]
    tr4 = tr.reshape(b, nslab, tr.shape[1], qt)
    q_rows = gl * NSA_HPG * dh
    v_rows = gl * dh
    gate_rows = gl * 4 * NSA_HPG
    v1_blk = hq // v_rows
    v2_blk = (hq + gd) // v_rows
    gate_blk = (hq + 2 * gd) // gate_rows
    nsb = s // NSA_SEL_LEN
    return pl.pallas_call(
        functools.partial(_nsa_pair_body, s=s),
        grid=(b, pairs, nslab),
        in_specs=[
            pl.BlockSpec((1, 1, q_rows, qt), lambda i, p, j: (i, j, p, 0)),
            pl.BlockSpec((1, 1, gate_rows, qt), lambda i, p, j: (i, j, gate_blk + p, 0)),
            pl.BlockSpec((1, nb, v_rows), lambda i, p, j: (i, 0, p)),
            pl.BlockSpec((1, v_rows, nb), lambda i, p, j: (i, p, 0)),
            pl.BlockSpec((1, s, v_rows), lambda i, p, j: (i, 0, p)),
            pl.BlockSpec((1, s, v_rows), lambda i, p, j: (i, 0, pairs + p)),
            pl.BlockSpec((1, nslab, v_rows, qt), lambda i, p, j: (i, 0, v1_blk + p, 0)),
            pl.BlockSpec((1, nslab, v_rows, qt), lambda i, p, j: (i, 0, v2_blk + p, 0)),
        ],
        out_specs=pl.BlockSpec((1, qt, q_rows), lambda i, p, j: (i, j, p)),
        out_shape=jax.ShapeDtypeStruct((b, s, hq), BF16),
        scratch_shapes=[pltpu.VMEM((nsb, gl * qt), F32), pltpu.VMEM((nsb, gl * NSA_HPG * qt), F32)],
        compiler_params=_params("parallel", "parallel", "arbitrary"),
        name="nsa_attend",
    )(tr4, tr4, k_cmp, v_cmp_t, k12, k12, tr4, tr4)


def _nsa(h2, g_norm, w_in, cmp_pos, cmp_w1, cmp_w2, w_out, *, b, s):
    g, hpg, dh = NSA_GROUPS, NSA_HPG, NSA_DH
    hq, gd = g * hpg * dh, g * dh
    kv_w = lambda br, kv: w_in[:, hq + (2 * br + kv) * gd:hq + (2 * br + kv + 1) * gd]
    w_q = w_in[:, :hq] * (dh ** -0.5)
    w_g = w_in[:, hq + 6 * gd:].reshape(-1, 3, g, hpg).transpose(2, 1, 3, 0)
    w_g = jnp.pad(w_g, ((0, 0), (0, 1), (0, 0), (0, 0))).reshape(g * 4 * hpg, -1)
    w_t = jnp.concatenate([w_q.T, kv_w(1, 1).T, kv_w(2, 1).T, w_g], axis=0).astype(BF16)
    w_nat = jnp.concatenate([kv_w(0, 0), kv_w(0, 1), kv_w(1, 0), kv_w(2, 0)], axis=1).astype(BF16)
    k0, v0, k12, tr = _nsa_proj(h2, g_norm, w_nat, w_t)
    nb = s // NSA_CMP_STRIDE
    k_cmp = _nsa_compress(k0.reshape(b, nb, NSA_CMP_STRIDE * gd), cmp_pos[0], cmp_w1[0], cmp_w2[0], transposed=False)
    v_cmp_t = _nsa_compress(v0.reshape(b, nb, NSA_CMP_STRIDE * gd), cmp_pos[1], cmp_w1[1], cmp_w2[1], transposed=True)
    y = _nsa_attend_pairs(tr, k_cmp, v_cmp_t, k12.reshape(b, s, 2 * gd), b=b, s=s)
    return y.reshape(b * s, hq), w_out.astype(BF16)


def kernel(x, norm_ffn1, ffn1_w_in, ffn1_w_out, norm_mix, norm_ffn2, ffn2_w_in, ffn2_w_out, norm_final,
           ret_w_in, ret_gn_gain, ret_w_out,
           nsa_w_in, nsa_cmp_pos, nsa_cmp_w1, nsa_cmp_w2, nsa_w_out,
           ssd_w_in, ssd_conv_w, ssd_conv_b, ssd_dt_bias, ssd_a_log, ssd_d, ssd_norm, ssd_w_out,
           dil_w_in, dil_w_out):
    b, s, d = x.shape
    depth = norm_mix.shape[0]
    h = x.reshape(b * s, d)
    for i in range(depth):
        h = _ffn(h, norm_ffn1[i], ffn1_w_in[i].astype(BF16), ffn1_w_out[i].astype(BF16))
        m, j = i % 4, i // 4
        mix = None
        if m == 0:
            mix = _retention(h, norm_mix[i], ret_w_in[j], ret_gn_gain[j], ret_w_out[j], b=b, s=s)
        elif m == 1:
            mix = _nsa(h, norm_mix[i], nsa_w_in[j], nsa_cmp_pos[j], nsa_cmp_w1[j], nsa_cmp_w2[j], nsa_w_out[j], b=b, s=s)
        elif m == 2:
            mix = _ssd(h, norm_mix[i], ssd_w_in[j], ssd_conv_w[j], ssd_conv_b[j], ssd_dt_bias[j], ssd_a_log[j],
                       ssd_d[j], ssd_norm[j], ssd_w_out[j], b=b, s=s)
        else:
            h = _dilated(h, norm_mix[i], dil_w_in[j], dil_w_out[j], b=b, s=s)
        h = _ffn(h, norm_ffn2[i], ffn2_w_in[i].astype(BF16), ffn2_w_out[i].astype(BF16),
                 norm_final if i == depth - 1 else None, mix)
    return h.reshape(b, s, d)
```

```python
import functools
import math

import jax
import jax.numpy as jnp
from jax import lax
from jax.experimental import pallas as pl
from jax.experimental.pallas import tpu as pltpu

F32 = jnp.float32
BF16 = jnp.bfloat16
NORM_EPS = 1e-6
NEG_INF = -1e30
ROPE_BASE = 10000.0
VMEM_LIMIT_BYTES = 56 * 1024 * 1024
LANES = 128
TOKEN_TILE = 512
MAX_DOT_COLS = 2048

RET_HEADS = 4
RET_CHUNK = 128

NSA_GROUPS = 4
NSA_HPG = 4
NSA_DH = 64
NSA_CMP_LEN = 32
NSA_CMP_STRIDE = 16
NSA_SEL_LEN = 64
NSA_N_SEL = 16
NSA_WINDOW = 512
NSA_FORCE_BONUS = 1e4
NSA_QT = 256
NSA_KC = 512

SSD_GROUPS = 4
SSD_HPG = 8
SSD_HEADDIM = 64
SSD_STATE = 128
SSD_CONV = 4
SSD_CHUNK = 256

DIL_PATTERN = ((128, 1), (512, 4), (2048, 16))
DIL_HEADS = 8
DIL_DH = 128
DIL_QB = 128


def _params(*sem):
    return pltpu.CompilerParams(dimension_semantics=sem, vmem_limit_bytes=VMEM_LIMIT_BYTES)


def _rms(x, g):
    return x * lax.rsqrt(jnp.mean(x * x, axis=-1, keepdims=True) + NORM_EPS) * g


def _silu(x):
    return x * jax.nn.sigmoid(x)


def _dot(a, b):
    return jnp.dot(a, b, preferred_element_type=F32)


def _dot_nt(a, b):
    return lax.dot_general(a, b, (((1,), (1,)), ((), ())), preferred_element_type=F32)


def _split3(x):
    hi = x.astype(BF16)
    r1 = x - hi.astype(F32)
    mid = r1.astype(BF16)
    lo = (r1 - mid.astype(F32)).astype(BF16)
    return hi, mid, lo


def _dot3(terms, rhs):
    return _dot(terms[0], rhs) + _dot(terms[1], rhs) + _dot(terms[2], rhs)


def _col_chunk(n):
    if n <= MAX_DOT_COLS:
        return n
    return max(c for c in range(LANES, MAX_DOT_COLS + 1, LANES) if n % c == 0)


def _resident(shape, index=None):
    index = index if index is not None else (0,) * len(shape)
    return pl.BlockSpec(shape, lambda *_: index, pipeline_mode=pl.Buffered(1))


def _ffn_body(h_ref, *refs, fc, final_norm, mix):
    x = h_ref[...]
    if mix:
        y_ref, wm_ref, *refs = refs
        x = x + _dot(y_ref[...], wm_ref[...])
    g_ref, wa_ref, wb_ref, wo_ref, gf_ref, o_ref = refs
    xn = _rms(x, g_ref[...]).astype(BF16)
    acc = jnp.zeros(x.shape, F32)
    for j in range(wa_ref.shape[1] // fc):
        sl = slice(j * fc, (j + 1) * fc)
        a = _dot(xn, wa_ref[:, sl])
        b = _dot(xn, wb_ref[:, sl])
        acc = acc + _dot((_silu(a) * b).astype(BF16), wo_ref[sl, :])
    y = x + 0.5 * acc
    if final_norm:
        y = _rms(y, gf_ref[...])
    o_ref[...] = y


def _ffn(h, g, w_in, w_out, g_final=None, mix=None, *, tm=TOKEN_TILE):
    m, d = h.shape
    f = w_out.shape[0]
    final_norm = g_final is not None
    gf = g_final if final_norm else g
    mix_args = list(mix) if mix is not None else []
    mix_specs = [pl.BlockSpec((tm, mix[0].shape[1]), lambda i: (i, 0)), _resident(mix[1].shape)] if mix_args else []
    return pl.pallas_call(
        functools.partial(_ffn_body, fc=_col_chunk(f), final_norm=final_norm, mix=bool(mix_args)),
        grid=(m // tm,),
        in_specs=[pl.BlockSpec((tm, d), lambda i: (i, 0))] + mix_specs + [
            _resident((1, d)),
            _resident((d, f)),
            _resident((d, f), (0, 1)),
            _resident((f, d)),
            _resident((1, d)),
        ],
        out_specs=pl.BlockSpec((tm, d), lambda i: (i, 0)),
        out_shape=jax.ShapeDtypeStruct((m, d), F32),
        compiler_params=_params("parallel"),
        name="ffn",
    )(h, *mix_args, g.reshape(1, d), w_in, w_in, w_out, gf.reshape(1, d))


def _norm_proj_body(h_ref, g_ref, *refs, n_out):
    xn = _rms(h_ref[...], g_ref[...]).astype(BF16)
    for w_ref, o_ref in zip(refs[:n_out], refs[n_out:]):
        n = w_ref.shape[1]
        nc = _col_chunk(n)
        for j in range(n // nc):
            sl = slice(j * nc, (j + 1) * nc)
            o_ref[:, sl] = _dot(xn, w_ref[:, sl]).astype(o_ref.dtype)


def _norm_proj(h, g, ws, out_dtypes, *, tm=TOKEN_TILE):
    m, d = h.shape
    return pl.pallas_call(
        functools.partial(_norm_proj_body, n_out=len(ws)),
        grid=(m // tm,),
        in_specs=[pl.BlockSpec((tm, d), lambda i: (i, 0)), pl.BlockSpec((1, d), lambda i: (0, 0))]
        + [pl.BlockSpec(w.shape, lambda i: (0, 0)) for w in ws],
        out_specs=[pl.BlockSpec((tm, w.shape[1]), lambda i: (i, 0)) for w in ws],
        out_shape=[jax.ShapeDtypeStruct((m, w.shape[1]), dt) for w, dt in zip(ws, out_dtypes)],
        compiler_params=_params("parallel"),
        name="norm_proj",
    )(h, g.reshape(1, d), *ws)


def _out_proj_body(h_ref, y_ref, w_ref, o_ref):
    o_ref[...] = h_ref[...] + _dot(y_ref[...], w_ref[...])


def _out_proj(h, y, w, *, tm=TOKEN_TILE):
    m, d = h.shape
    k = y.shape[1]
    return pl.pallas_call(
        _out_proj_body,
        grid=(m // tm,),
        in_specs=[
            pl.BlockSpec((tm, d), lambda i: (i, 0)),
            pl.BlockSpec((tm, k), lambda i: (i, 0)),
            pl.BlockSpec((k, d), lambda i: (0, 0)),
        ],
        out_specs=pl.BlockSpec((tm, d), lambda i: (i, 0)),
        out_shape=jax.ShapeDtypeStruct((m, d), F32),
        compiler_params=_params("parallel"),
        name="out_proj",
    )(h, y, w)


def _ret_body(q_ref, k_ref, v_ref, g_ref, cos_ref, sin_ref, gn_ref, o_ref, state_ref, *, ts):
    c_len = RET_CHUNK
    dk = q_ref.shape[2] // RET_HEADS
    dv = v_ref.shape[2] // RET_HEADS
    half = dk // 2

    @pl.when(pl.program_id(1) == 0)
    def _():
        state_ref[...] = jnp.zeros(state_ref.shape, F32)

    ii = lax.broadcasted_iota(jnp.int32, (c_len, c_len), 0)
    jj = lax.broadcasted_iota(jnp.int32, (c_len, c_len), 1)
    rel = (ii - jj).astype(F32)
    causal = ii >= jj
    idx = lax.broadcasted_iota(jnp.int32, (c_len, 1), 0).astype(F32)

    def rot(t, cos, sin):
        t1, t2 = t[:, :half], t[:, half:]
        return jnp.concatenate([t1 * cos - t2 * sin, t1 * sin + t2 * cos], axis=1)

    def chunk(c, carry):
        r0 = pl.multiple_of(c * c_len, c_len)
        rows = pl.ds(r0, c_len)
        cos = cos_ref[rows, :]
        sin = sin_ref[rows, :]
        for h in range(RET_HEADS):
            log_gamma = math.log1p(-(2.0 ** (-5.0 - h)))
            inner = jnp.where(causal, jnp.exp(jnp.where(causal, rel, 0.0) * log_gamma), 0.0)
            q_decay = jnp.exp((idx + 1.0) * log_gamma)
            k_decay = jnp.exp((c_len - 1.0 - idx) * log_gamma)
            chunk_decay = math.exp(c_len * log_gamma)
            q = rot(q_ref[0, rows, h * dk:(h + 1) * dk].astype(F32), cos, sin)
            k = rot(k_ref[0, rows, h * dk:(h + 1) * dk].astype(F32), cos, sin) * (dk ** -0.5)
            v = v_ref[0, rows, h * dv:(h + 1) * dv]
            qb = q.astype(BF16)
            sc = _dot_nt(qb, k.astype(BF16)) * inner
            st = state_ref[h]
            o = _dot(sc.astype(BF16), v) + _dot(qb, st.astype(BF16)) * q_decay
            kd_t = jnp.transpose(k * k_decay).astype(BF16)
            state_ref[h] = st * chunk_decay + _dot(kd_t, v)
            gn = gn_ref[:, h * dv:(h + 1) * dv]
            on = _rms(o, gn)
            gate = g_ref[0, rows, h * dv:(h + 1) * dv].astype(F32)
            o_ref[0, rows, h * dv:(h + 1) * dv] = (_silu(gate) * on).astype(o_ref.dtype)
        return carry

    lax.fori_loop(0, ts // c_len, chunk, 0)


def _retention_core(proj, cos, sin, gn_gain, *, b, s, ts=TOKEN_TILE):
    n = proj.shape[2]
    hk = n // 6
    hv = 2 * hk
    dk = hk // RET_HEADS
    return pl.pallas_call(
        functools.partial(_ret_body, ts=ts),
        grid=(b, s // ts),
        in_specs=[
            pl.BlockSpec((1, ts, hk), lambda i, j: (i, j, 0)),
            pl.BlockSpec((1, ts, hk), lambda i, j: (i, j, 1)),
            pl.BlockSpec((1, ts, hv), lambda i, j: (i, j, 1)),
            pl.BlockSpec((1, ts, hv), lambda i, j: (i, j, 2)),
            pl.BlockSpec((ts, dk // 2), lambda i, j: (j, 0)),
            pl.BlockSpec((ts, dk // 2), lambda i, j: (j, 0)),
            pl.BlockSpec((1, hv), lambda i, j: (0, 0)),
        ],
        out_specs=pl.BlockSpec((1, ts, hv), lambda i, j: (i, j, 0)),
        out_shape=jax.ShapeDtypeStruct((b, s, hv), BF16),
        scratch_shapes=[pltpu.VMEM((RET_HEADS, dk, hv // RET_HEADS), F32)],
        compiler_params=_params("parallel", "arbitrary"),
        name="retention",
    )(proj, proj, proj, proj, cos, sin, gn_gain.reshape(1, hv))


def _rope_tables(s, half):
    inv = ROPE_BASE ** (-jnp.arange(half, dtype=F32) / half)
    ang = jnp.arange(s, dtype=F32)[:, None] * inv[None, :]
    return jnp.cos(ang), jnp.sin(ang)


def _retention(h2, g_norm, w_in, gn_gain, w_out, *, b, s):
    (proj,) = _norm_proj(h2, g_norm, [w_in.astype(BF16)], [BF16])
    n = proj.shape[1]
    cos, sin = _rope_tables(s, n // 6 // RET_HEADS // 2)
    y = _retention_core(proj.reshape(b, s, n), cos, sin, gn_gain, b=b, s=s)
    return y.reshape(b * s, -1), w_out.astype(BF16)


def _dil_proj_body(h_ref, g_ref, w_ref, o_ref, xn_ref, *, r):
    tm = h_ref.shape[0]
    n = tm // r
    xn = _rms(h_ref[...], g_ref[...])
    slabs = xn_ref.shape[0]
    for k in range(slabs):
        xn_ref[k] = xn[:, k * LANES:(k + 1) * LANES]
    xp = jnp.concatenate(
        [jnp.concatenate([xn_ref[k, pl.ds(c, n, stride=r), :] for c in range(r)], axis=0) for k in range(slabs)],
        axis=1).astype(BF16)
    nc = _col_chunk(w_ref.shape[1])
    for j in range(w_ref.shape[1] // nc):
        sl = slice(j * nc, (j + 1) * nc)
        res = _dot(xp, w_ref[:, sl]).astype(o_ref.dtype)
        for c in range(r):
            o_ref[0, c, :, sl] = res[c * n:(c + 1) * n, :]


def _dil_proj(h2, g_norm, w, r, *, b, s, tm=TOKEN_TILE):
    d = h2.shape[1]
    n_out = w.shape[1]
    tiles = s // tm
    return pl.pallas_call(
        functools.partial(_dil_proj_body, r=r),
        grid=(b * tiles,),
        in_specs=[
            pl.BlockSpec((tm, d), lambda i: (i, 0)),
            pl.BlockSpec((1, d), lambda i: (0, 0)),
            pl.BlockSpec((d, n_out), lambda i: (0, 0)),
        ],
        out_specs=pl.BlockSpec((1, r, tm // r, n_out), lambda i: (i // tiles, 0, i % tiles, 0)),
        out_shape=jax.ShapeDtypeStruct((b, r, s // r, n_out), BF16),
        scratch_shapes=[pltpu.VMEM((d // LANES, tm, LANES), F32)],
        compiler_params=_params("parallel"),
        name=f"dilated_proj_r{r}",
    )(h2, g_norm.reshape(1, d), w)


def _dil_body(q_ref, kc_ref, kp_ref, vc_ref, vp_ref, o_ref, lse_ref, kbuf, vbuf, *, rows, n_back):
    qb = DIL_QB
    dh = DIL_DH
    first_step = pl.program_id(2) == 0
    kbuf[0:qb, :] = kp_ref[0, 0]
    kbuf[qb:, :] = kc_ref[0, 0]
    vbuf[0:qb, :] = vp_ref[0, 0]
    vbuf[qb:, :] = vc_ref[0, 0]
    qi = lax.broadcasted_iota(jnp.int32, (qb, 2 * qb), 0)
    kj = lax.broadcasted_iota(jnp.int32, (qb, 2 * qb), 1)
    dist = qi + qb - kj
    band = (dist >= 0) & (dist <= n_back)
    lane = lax.broadcasted_iota(jnp.int32, (qb, LANES), 1)
    scale = dh ** -0.5
    ones = jnp.ones((2 * qb, dh), BF16)
    for i in range(rows // qb):
        mask = band & ((kj >= qb) | jnp.logical_not(first_step)) if i == 0 else band
        lse_tile = jnp.zeros((qb, LANES), F32)
        for h in range(DIL_HEADS):
            cols = slice(h * dh, (h + 1) * dh)
            q = q_ref[0, 0, i * qb:(i + 1) * qb, cols]
            k = kbuf[i * qb:(i + 2) * qb, cols]
            v = vbuf[i * qb:(i + 2) * qb, cols]
            s = jnp.where(mask, _dot_nt(q, k), NEG_INF)
            m = jnp.max(s, axis=1, keepdims=True)
            e = jnp.exp2((s - m) * (scale * math.log2(math.e))).astype(BF16)
            acc = _dot(e, jnp.concatenate([v, ones], axis=1))
            den = jnp.maximum(acc[:, dh:], 1e-30)
            o_ref[0, 0, i * qb:(i + 1) * qb, cols] = (acc[:, :dh] / den).astype(o_ref.dtype)
            lse_tile = jnp.where(lane == h, m * scale + jnp.log(den), lse_tile)
        lse_ref[0, 0, i * qb:(i + 1) * qb, :] = lse_tile


def _dilated_group(proj, win, r, *, b, s):
    hd = DIL_HEADS * DIL_DH
    length = s // r
    rows = min(length, TOKEN_TILE)
    qb = DIL_QB

    def cur(which):
        return pl.BlockSpec((1, 1, rows, hd), lambda i, c, n: (i, c, n, which))

    def prev(which):
        return pl.BlockSpec((1, 1, qb, hd), lambda i, c, n: (i, c, jnp.maximum(n * (rows // qb) - 1, 0), which))

    return pl.pallas_call(
        functools.partial(_dil_body, rows=rows, n_back=win // r),
        grid=(b, r, length // rows),
        in_specs=[cur(0), cur(1), prev(1), cur(2), prev(2)],
        out_specs=[
            pl.BlockSpec((1, 1, rows, hd), lambda i, c, n: (i, c, n, 0)),
            pl.BlockSpec((1, 1, rows, LANES), lambda i, c, n: (i, c, n, 0)),
        ],
        out_shape=[
            jax.ShapeDtypeStruct((b, r, length, hd), BF16),
            jax.ShapeDtypeStruct((b, r, length, LANES), F32),
        ],
        scratch_shapes=[pltpu.VMEM((rows + qb, hd), BF16), pltpu.VMEM((rows + qb, hd), BF16)],
        compiler_params=_params("parallel", "parallel", "arbitrary"),
        name=f"dilated_r{r}",
    )(proj, proj, proj, proj, proj)


def _dil_merge_body(h_ref, *refs, dilations):
    ng = len(dilations)
    o_refs, l_refs = refs[:ng], refs[ng:2 * ng]
    w_ref, out_ref = refs[2 * ng], refs[2 * ng + 1]
    o_bufs, l_bufs = refs[2 * ng + 2:3 * ng + 2], refs[3 * ng + 2:]
    tm = h_ref.shape[1]
    for o_ref, l_ref, o_buf, l_buf, r in zip(o_refs, l_refs, o_bufs, l_bufs, dilations):
        n = tm // r
        for c in range(r):
            rows = pl.ds(c, n, stride=r)
            l_buf[rows, :] = l_ref[0, c]
            blk = o_ref[0, c].astype(F32)
            for h in range(DIL_HEADS):
                o_buf[h, rows, :] = blk[:, h * DIL_DH:(h + 1) * DIL_DH]
    lses = [l[...] for l in l_bufs]
    mx = functools.reduce(jnp.maximum, lses)
    es = [jnp.exp(l - mx) for l in lses]
    tot = functools.reduce(jnp.add, es)
    wts = [e / tot for e in es]
    dh = DIL_DH
    parts = []
    for h in range(DIL_HEADS):
        acc = jnp.zeros((tm, dh), F32)
        for g in range(ng):
            wg = jnp.broadcast_to(wts[g][:, h:h + 1], (tm, dh))
            acc = acc + wg * o_bufs[g][h]
        parts.append(acc)
    o = jnp.concatenate(parts, axis=1).astype(BF16)
    out_ref[0] = h_ref[0] + _dot(o, w_ref[...])


def _dil_merge(h3, outs, lses, w, dilations, *, tm=TOKEN_TILE):
    b, s, d = h3.shape
    hd = w.shape[0]
    res_major = lambda r, width: pl.BlockSpec((1, r, tm // r, width), lambda i, n: (i, 0, n, 0))
    return pl.pallas_call(
        functools.partial(_dil_merge_body, dilations=tuple(dilations)),
        grid=(b, s // tm),
        in_specs=[pl.BlockSpec((1, tm, d), lambda i, n: (i, n, 0))]
        + [res_major(r, hd) for r in dilations] + [res_major(r, LANES) for r in dilations]
        + [pl.BlockSpec((hd, d), lambda i, n: (0, 0))],
        out_specs=pl.BlockSpec((1, tm, d), lambda i, n: (i, n, 0)),
        out_shape=jax.ShapeDtypeStruct((b, s, d), F32),
        scratch_shapes=[pltpu.VMEM((DIL_HEADS, tm, DIL_DH), F32) for _ in dilations]
        + [pltpu.VMEM((tm, LANES), F32) for _ in dilations],
        compiler_params=_params("parallel", "parallel"),
        name="dilated_merge",
    )(h3, *outs, *lses, w)


def _dilated(h2, g_norm, w_in, w_out, *, b, s):
    d = h2.shape[1]
    gw = 3 * DIL_HEADS * DIL_DH
    wb = w_in.astype(BF16)
    outs, lses, dilations = [], [], []
    for g, (win, r) in enumerate(DIL_PATTERN):
        proj = _dil_proj(h2, g_norm, wb[:, g * gw:(g + 1) * gw], r, b=b, s=s)
        o, lse = _dilated_group(proj, win, r, b=b, s=s)
        outs.append(o)
        lses.append(lse)
        dilations.append(r)
    return _dil_merge(h2.reshape(b, s, d), outs, lses, w_out.astype(BF16), dilations).reshape(b * s, d)


def _softplus(x):
    return jnp.maximum(x, 0.0) + jnp.log1p(jnp.exp(-jnp.abs(x)))


def _ssd_body(z_ref, x_ref, dt_ref, cw_ref, cb_ref, dtb_ref, alog_ref, dsk_ref, ng_ref, o_ref,
              xpad, state_ref):
    ln = SSD_CHUNK
    nst = SSD_STATE
    gw = SSD_HPG * SSD_HEADDIM
    d_inner = SSD_GROUPS * gw
    halo = 8

    @pl.when(pl.program_id(1) == 0)
    def _():
        xpad[0:halo, :] = jnp.zeros((halo, xpad.shape[1]), F32)
        state_ref[...] = jnp.zeros(state_ref.shape, F32)

    xpad[halo:halo + ln, :] = x_ref[0].astype(F32)
    conv = cb_ref[...]
    for k in range(SSD_CONV):
        off = halo - (SSD_CONV - 1) + k
        conv = conv + xpad[off:off + ln, :] * cw_ref[k:k + 1, :]
    xpad[0:halo, :] = xpad[ln:ln + halo, :]
    xbc = _silu(conv)
    xs = xbc[:, :d_inner]
    bm = xbc[:, d_inner:d_inner + SSD_GROUPS * nst]
    cm = xbc[:, d_inner + SSD_GROUPS * nst:]

    dt = _softplus(dt_ref[0] + dtb_ref[...])
    da = dt * (-jnp.exp(alog_ref[...]))
    ii = lax.broadcasted_iota(jnp.int32, (ln, ln), 0)
    jj = lax.broadcasted_iota(jnp.int32, (ln, ln), 1)
    causal = ii >= jj
    tril = jnp.where(causal, 1.0, 0.0).astype(BF16)
    da_terms = _split3(da)
    acs = _dot(tril, da_terms[0]) + _dot(tril, da_terms[1]) + _dot(tril, da_terms[2])
    acs_t = jnp.transpose(acs)

    erow = lax.broadcasted_iota(jnp.int32, (LANES, d_inner), 0)
    ecol = lax.broadcasted_iota(jnp.int32, (LANES, d_inner), 1)
    expand = jnp.where(ecol // SSD_HEADDIM == erow, 1.0, 0.0).astype(BF16)
    acs_e = _dot3(_split3(acs), expand)
    dt_e = _dot3(_split3(dt), expand)
    last = acs_e[ln - 1:ln, :]
    decay_in = jnp.exp(acs_e)
    xs_dt = xs * dt_e
    xs_end = (xs_dt * jnp.exp(last - acs_e)).astype(BF16)
    xs_b = xs_dt.astype(BF16)
    lane = lax.broadcasted_iota(jnp.int32, (ln, LANES), 1)

    y_groups = []
    for g in range(SSD_GROUPS):
        bm_g = bm[:, g * nst:(g + 1) * nst]
        cm_g = cm[:, g * nst:(g + 1) * nst].astype(BF16)
        cb = jnp.where(causal, _dot_nt(cm_g, bm_g.astype(BF16)), 0.0)
        st = state_ref[g]
        gcols = slice(g * gw, (g + 1) * gw)
        y_state = _dot(cm_g, st.astype(BF16)) * decay_in[:, gcols]
        pairs = []
        for p in range(SSD_HPG // 2):
            pair_cols = slice(g * gw + p * LANES, g * gw + (p + 1) * LANES)
            halves = []
            for e in range(2):
                hd = g * SSD_HPG + 2 * p + e
                diff = acs[:, hd:hd + 1] - acs_t[hd:hd + 1, :]
                w = (cb * jnp.exp(jnp.minimum(diff, 0.0))).astype(BF16)
                halves.append(_dot(w, xs_b[:, pair_cols]))
            pairs.append(jnp.where(lane < SSD_HEADDIM, halves[0], halves[1]))
        y_groups.append(jnp.concatenate(pairs, axis=1) + y_state)
        bm_t = jnp.transpose(bm_g).astype(BF16)
        state_ref[g] = st * jnp.exp(last[:, gcols]) + _dot(bm_t, xs_end[:, gcols])

    y = jnp.concatenate(y_groups, axis=1) + dsk_ref[...] * xs
    yz = y * _silu(z_ref[0].astype(F32))
    outs = [_rms(yz[:, g * gw:(g + 1) * gw], ng_ref[:, g * gw:(g + 1) * gw]) for g in range(SSD_GROUPS)]
    o_ref[0] = jnp.concatenate(outs, axis=1).astype(o_ref.dtype)


def _ssd_core(z, xbc, dt, conv_w, conv_b, dt_bias, a_log, d_skip, norm_g, *, b, s):
    ln = SSD_CHUNK
    d_inner = z.shape[2]
    conv_dim = xbc.shape[2]
    gw = SSD_HPG * SSD_HEADDIM
    full = lambda shape: pl.BlockSpec(shape, lambda i, j: (0,) * len(shape))
    return pl.pallas_call(
        _ssd_body,
        grid=(b, s // ln),
        in_specs=[
            pl.BlockSpec((1, ln, d_inner), lambda i, j: (i, j, 0)),
            pl.BlockSpec((1, ln, conv_dim), lambda i, j: (i, j, 0)),
            pl.BlockSpec((1, ln, LANES), lambda i, j: (i, j, 0)),
            full((SSD_CONV, conv_dim)),
            full((1, conv_dim)),
            full((1, LANES)),
            full((1, LANES)),
            full((1, d_inner)),
            full((1, d_inner)),
        ],
        out_specs=pl.BlockSpec((1, ln, d_inner), lambda i, j: (i, j, 0)),
        out_shape=jax.ShapeDtypeStruct((b, s, d_inner), BF16),
        scratch_shapes=[pltpu.VMEM((ln + 8, conv_dim), F32),
                        pltpu.VMEM((SSD_GROUPS, SSD_STATE, gw), F32)],
        compiler_params=_params("parallel", "arbitrary"),
        name="ssd",
    )(z, xbc, dt, conv_w, conv_b, dt_bias, a_log, d_skip, norm_g)


def _ssd(h2, g_norm, w_in, conv_w, conv_b, dt_bias, a_log, d_skip, norm_g, w_out, *, b, s):
    heads = SSD_GROUPS * SSD_HPG
    d_inner = heads * SSD_HEADDIM
    conv_dim = conv_w.shape[1]
    wb = w_in.astype(BF16)
    w_dt = jnp.pad(wb[:, d_inner + conv_dim:], ((0, 0), (0, LANES - heads)))
    z, xbc, dt = _norm_proj(h2, g_norm, [wb[:, :d_inner], wb[:, d_inner:d_inner + conv_dim], w_dt], [BF16, BF16, F32])
    pad_heads = lambda v: jnp.pad(v.astype(F32), (0, LANES - heads)).reshape(1, LANES)
    y = _ssd_core(z.reshape(b, s, -1), xbc.reshape(b, s, -1), dt.reshape(b, s, LANES),
                  conv_w.astype(F32), conv_b.reshape(1, -1).astype(F32), pad_heads(dt_bias), pad_heads(a_log),
                  jnp.repeat(d_skip.astype(F32), SSD_HEADDIM).reshape(1, d_inner), norm_g.reshape(1, d_inner),
                  b=b, s=s)
    return y.reshape(b * s, d_inner), w_out.astype(BF16)


def _nsa_proj_body(h_ref, g_ref, wn_ref, wt_ref, k0_ref, v0_ref, k12_ref, tr_ref):
    xn = _rms(h_ref[...], g_ref[...]).astype(BF16)
    nat = _dot(xn, wn_ref[...])
    gd = k0_ref.shape[1]
    k0_ref[...] = nat[:, :gd].astype(k0_ref.dtype)
    v0_ref[...] = nat[:, gd:2 * gd].astype(v0_ref.dtype)
    k12_ref[...] = nat[:, 2 * gd:].astype(k12_ref.dtype)
    res = _dot_nt(wt_ref[...], xn)
    for j in range(tr_ref.shape[0]):
        tr_ref[j] = res[:, j * NSA_QT:(j + 1) * NSA_QT].astype(tr_ref.dtype)


def _nsa_proj(h2, g_norm, w_nat, w_t, *, tm=TOKEN_TILE):
    m, d = h2.shape
    gd = NSA_GROUPS * NSA_DH
    nt = w_t.shape[0]
    slabs = tm // NSA_QT
    return pl.pallas_call(
        _nsa_proj_body,
        grid=(m // tm,),
        in_specs=[
            pl.BlockSpec((tm, d), lambda i: (i, 0)),
            pl.BlockSpec((1, d), lambda i: (0, 0)),
            pl.BlockSpec(w_nat.shape, lambda i: (0, 0)),
            pl.BlockSpec(w_t.shape, lambda i: (0, 0)),
        ],
        out_specs=[
            pl.BlockSpec((tm, gd), lambda i: (i, 0)),
            pl.BlockSpec((tm, gd), lambda i: (i, 0)),
            pl.BlockSpec((tm, 2 * gd), lambda i: (i, 0)),
            pl.BlockSpec((slabs, nt, NSA_QT), lambda i: (i, 0, 0)),
        ],
        out_shape=[
            jax.ShapeDtypeStruct((m, gd), BF16),
            jax.ShapeDtypeStruct((m, gd), BF16),
            jax.ShapeDtypeStruct((m, 2 * gd), BF16),
            jax.ShapeDtypeStruct((m // NSA_QT, nt, NSA_QT), BF16),
        ],
        compiler_params=_params("parallel"),
        name="nsa_proj",
    )(h2, g_norm.reshape(1, d), w_nat, w_t)


def _nsa_cmp_body(x_ref, pa_ref, pb_ref, w1a_ref, w1b_ref, w2_ref, o_ref, *, transposed):
    x = x_ref[0].astype(F32)
    nb = x.shape[0]
    ya = _dot((x + pa_ref[...]).astype(BF16), w1a_ref[...])
    yb = _dot((x + pb_ref[...]).astype(BF16), w1b_ref[...])
    hid = _silu(ya + pltpu.roll(yb, nb - 1, 0))
    out = _dot(hid.astype(BF16), w2_ref[...])
    o_ref[0] = (jnp.transpose(out) if transposed else out).astype(o_ref.dtype)


def _nsa_compress(x, pos, w1, w2, *, transposed):
    b, nb, width = x.shape
    g, dh = NSA_GROUPS, NSA_DH
    half = NSA_CMP_LEN // 2
    hid = w1.shape[1]
    eye = jnp.eye(g, dtype=F32)
    w1r = w1.reshape(NSA_CMP_LEN, dh, hid)
    big = jnp.einsum("ldj,gh->lgdhj", w1r, eye).reshape(NSA_CMP_LEN, g * dh, g * hid)
    w1a = big[:half].reshape(half * g * dh, g * hid).astype(BF16)
    w1b = big[half:].reshape(half * g * dh, g * hid).astype(BF16)
    w2big = jnp.einsum("jd,gh->gjhd", w2, eye).reshape(g * hid, g * dh).astype(BF16)
    posb = jnp.broadcast_to(pos[:, None, :], (NSA_CMP_LEN, g, dh)).reshape(NSA_CMP_LEN, g * dh)
    pa = posb[:half].reshape(1, width).astype(F32)
    pb = posb[half:].reshape(1, width).astype(F32)
    full = lambda shape: pl.BlockSpec(shape, lambda i: (0,) * len(shape))
    out_dims = (g * dh, nb) if transposed else (nb, g * dh)
    return pl.pallas_call(
        functools.partial(_nsa_cmp_body, transposed=transposed),
        grid=(b,),
        in_specs=[
            pl.BlockSpec((1, nb, width), lambda i: (i, 0, 0)),
            full((1, width)), full((1, width)),
            full((width, g * hid)), full((width, g * hid)), full((g * hid, g * dh)),
        ],
        out_specs=pl.BlockSpec((1,) + out_dims, lambda i: (i, 0, 0)),
        out_shape=jax.ShapeDtypeStruct((b,) + out_dims, BF16),
        compiler_params=_params("parallel"),
        name="nsa_compress",
    )(x, pa, pb, w1a, w1b, w2big)


def _nsa_body(q_ref, gt_ref, kc_ref, vct_ref, k1_ref, k2_ref, v1_ref, v2_ref, o_ref, score_ref, bias_ref, *, s):
    qt, kc, dh = NSA_QT, NSA_KC, NSA_DH
    lanes = NSA_HPG * qt
    g = pl.program_id(1)
    j = pl.program_id(2)
    t0 = j * qt
    nb = kc_ref.shape[1]
    nsb = s // NSA_SEL_LEN
    n_sel = min(NSA_N_SEL, nsb)

    q_t = q_ref[0, 0]
    qcat = jnp.concatenate([q_t[h * dh:(h + 1) * dh, :] for h in range(NSA_HPG)], axis=1)
    odd = (g % 2) == 1
    zero = jnp.zeros_like(qcat)
    qp = jnp.concatenate([jnp.where(odd, zero, qcat), jnp.where(odd, qcat, zero)], axis=0)
    qlane = lax.broadcasted_iota(jnp.int32, (1, lanes), 1) % qt
    tq = t0 + qlane

    sc = _dot(kc_ref[0], qp)
    n_idx = lax.broadcasted_iota(jnp.int32, (nb, lanes), 0)
    cmask = (n_idx * NSA_CMP_STRIDE + NSA_CMP_LEN - 1 <= tq) & (n_idx < nb - 1)
    sc = jnp.where(cmask, sc, NEG_INF)
    e = jnp.where(cmask, jnp.exp(sc - jnp.max(sc, axis=0, keepdims=True)), 0.0)
    p_c = e / jnp.maximum(jnp.sum(e, axis=0, keepdims=True), 1e-30)
    o_c = _dot(vct_ref[0], p_c.astype(BF16))
    imp = p_c[:, 0:qt]
    for h in range(1, NSA_HPG):
        imp = imp + p_c[:, h * qt:(h + 1) * qt]

    ratio = NSA_SEL_LEN // NSA_CMP_STRIDE
    dd = lax.broadcasted_iota(jnp.int32, (nsb, nb), 1) - ratio * lax.broadcasted_iota(jnp.int32, (nsb, nb), 0)
    wsel = jnp.where((dd == -1) | (dd == ratio - 1), 1.0,
                     jnp.where((dd >= 0) & (dd < ratio - 1), 2.0, 0.0)).astype(BF16)
    terms = _split3(imp)
    imp_sel = _dot(wsel, terms[0]) + _dot(wsel, terms[1]) + _dot(wsel, terms[2])
    jq = lax.broadcasted_iota(jnp.int32, (nsb, qt), 0)
    cur = (t0 + lax.broadcasted_iota(jnp.int32, (nsb, qt), 1)) // NSA_SEL_LEN
    forced = (jq == 0) | (jq == cur) | (jq == cur - 1)
    score = jnp.where(jq <= cur, imp_sel + jnp.where(forced, NSA_FORCE_BONUS, 0.0), NEG_INF)
    score_ref[...] = score

    def rank_step(jp, rank):
        row = jnp.broadcast_to(score_ref[pl.ds(jp, 1), :], (nsb, qt))
        before = (row > score) | ((row == score) & (jq > jp))
        return rank + jnp.where(before, 1.0, 0.0)

    n_visible = (t0 + qt - 1) // NSA_SEL_LEN + 1
    rank = lax.fori_loop(0, n_visible, rank_step, jnp.zeros((nsb, qt), F32))
    sel_bias = jnp.where((rank < n_sel) & (score > 0.5 * NEG_INF), 0.0, NEG_INF)
    bias_ref[...] = jnp.concatenate([sel_bias] * NSA_HPG, axis=1)

    ones_rows = jnp.ones((16, kc), BF16)

    def values(v_ref, slab0, n_slabs):
        v = jnp.concatenate([v_ref[0, slab0 + i] for i in range(n_slabs)], axis=1)
        return jnp.concatenate([v, ones_rows[:, :n_slabs * qt]], axis=0)

    def online(carry, st, v_aug):
        m, acc = carry
        m_new = jnp.maximum(m, jnp.max(st, axis=0, keepdims=True))
        p = jnp.exp(st - m_new).astype(BF16)
        return m_new, jnp.exp(m - m_new) * acc + _dot(v_aug, p)

    def normalise(acc):
        return acc[:dh, :] / jnp.maximum(acc[dh:dh + 1, :], 1e-30)

    init = (jnp.full((1, lanes), NEG_INF, F32), jnp.zeros((dh + 16, lanes), F32))

    blocks_per_chunk = kc // NSA_SEL_LEN
    slabs_per_chunk = kc // qt

    def sel_scores(c):
        k = k1_ref[0, pl.ds(pl.multiple_of(c * kc, kc), kc), :]
        bias = bias_ref[pl.ds(pl.multiple_of(c * blocks_per_chunk, blocks_per_chunk), blocks_per_chunk), :]
        rows = [jnp.broadcast_to(bias[r:r + 1, :], (NSA_SEL_LEN, lanes)) for r in range(blocks_per_chunk)]
        return _dot(k, qp) + jnp.concatenate(rows, axis=0)

    def sel_values(c):
        return values(v1_ref, c * slabs_per_chunk, slabs_per_chunk)

    c_hi = t0 // kc
    row_k = lax.broadcasted_iota(jnp.int32, (kc, lanes), 0)
    st = jnp.where(row_k <= (t0 - c_hi * kc) + qlane, sel_scores(c_hi), NEG_INF)
    chain_a = online(init, st, sel_values(c_hi))
    chain_a = lax.cond(c_hi % 2 == 1,
                       lambda cr: online(cr, sel_scores(c_hi - 1), sel_values(c_hi - 1)),
                       lambda cr: cr, chain_a)
    chain_b = (chain_a[0], jnp.zeros_like(chain_a[1]))

    def pair(i, chains):
        ca, cb = chains
        st_a = sel_scores(2 * i)
        st_b = sel_scores(2 * i + 1)
        return online(ca, st_a, sel_values(2 * i)), online(cb, st_b, sel_values(2 * i + 1))

    (m_a, acc_a), (m_b, acc_b) = lax.fori_loop(0, c_hi // 2, pair, (chain_a, chain_b))
    m_s = jnp.maximum(m_a, m_b)
    o_s = normalise(jnp.exp(m_a - m_s) * acc_a + jnp.exp(m_b - m_s) * acc_b)

    wslabs = NSA_WINDOW // qt + 1
    jb0 = jnp.maximum(j + 1 - wslabs, 0)
    kw = k2_ref[0, pl.ds(pl.multiple_of(jb0 * qt, qt), wslabs * qt), :]
    back = (j - jb0) * qt + (qlane - lax.broadcasted_iota(jnp.int32, (wslabs * qt, lanes), 0))
    sw = jnp.where(lax.bitcast_convert_type(back, jnp.uint32) < jnp.uint32(NSA_WINDOW), _dot(kw, qp), NEG_INF)
    pw = jnp.exp(sw - jnp.max(sw, axis=0, keepdims=True)).astype(BF16)
    ones_w = jnp.ones((16, wslabs * qt), BF16)
    vw = jnp.concatenate([v2_ref[0, jb0 + i] for i in range(wslabs)], axis=1)
    o_w = normalise(_dot(jnp.concatenate([vw, ones_w], axis=0), pw))

    gates = jax.nn.sigmoid(gt_ref[0, 0].astype(F32))

    def gate_row(br):
        return jnp.concatenate([gates[br * NSA_HPG + h:br * NSA_HPG + h + 1, :] for h in range(NSA_HPG)], axis=1)

    o = gate_row(0) * o_c + gate_row(1) * o_s + gate_row(2) * o_w
    o_hd = jnp.concatenate([o[:, h * qt:(h + 1) * qt] for h in range(NSA_HPG)], axis=0)
    o_ref[0] = jnp.transpose(o_hd).astype(o_ref.dtype)


def _nsa_attend(tr, k_cmp, v_cmp_t, k12, *, b, s):
    qt, dh = NSA_QT, NSA_DH
    gd = NSA_GROUPS * dh
    hq = NSA_GROUPS * NSA_HPG * dh
    nslab = s // qt
    nb = k_cmp.shape[1]
    tr4 = tr.reshape(b, nslab, tr.shape[1], qt)
    gate_rows = 4 * NSA_HPG
    v1_blk = hq // dh
    v2_blk = (hq + gd) // dh
    gate_blk = (hq + 2 * gd) // gate_rows
    nsb = s // NSA_SEL_LEN
    return pl.pallas_call(
        functools.partial(_nsa_body, s=s),
        grid=(b, NSA_GROUPS, nslab),
        in_specs=[
            pl.BlockSpec((1, 1, NSA_HPG * dh, qt), lambda i, g, j: (i, j, g, 0)),
            pl.BlockSpec((1, 1, gate_rows, qt), lambda i, g, j: (i, j, gate_blk + g, 0)),
            pl.BlockSpec((1, nb, 2 * dh), lambda i, g, j: (i, 0, g // 2)),
            pl.BlockSpec((1, dh, nb), lambda i, g, j: (i, g, 0)),
            pl.BlockSpec((1, s, 2 * dh), lambda i, g, j: (i, 0, g // 2)),
            pl.BlockSpec((1, s, 2 * dh), lambda i, g, j: (i, 0, NSA_GROUPS // 2 + g // 2)),
            pl.BlockSpec((1, nslab, dh, qt), lambda i, g, j: (i, 0, v1_blk + g, 0)),
            pl.BlockSpec((1, nslab, dh, qt), lambda i, g, j: (i, 0, v2_blk + g, 0)),
        ],
        out_specs=pl.BlockSpec((1, qt, NSA_HPG * dh), lambda i, g, j: (i, j, g)),
        out_shape=jax.ShapeDtypeStruct((b, s, hq), BF16),
        scratch_shapes=[pltpu.VMEM((nsb, qt), F32), pltpu.VMEM((nsb, NSA_HPG * qt), F32)],
        compiler_params=_params("parallel", "parallel", "arbitrary"),
        name="nsa_attend",
    )(tr4, tr4, k_cmp, v_cmp_t, k12, k12, tr4, tr4)


def _nsa_pair_body(q_ref, gt_ref, kc_ref, vct_ref, k1_ref, k2_ref, v1_ref, v2_ref, o_ref, bias_ref, wbias_ref, *, s):
    qt, kc, dh = NSA_QT, NSA_KC, NSA_DH
    gl = 2
    glanes = NSA_HPG * qt
    lanes = gl * glanes
    vrows = gl * dh
    j = pl.program_id(2)
    t0 = j * qt
    nb = kc_ref.shape[1]
    nsb = s // NSA_SEL_LEN
    n_sel = min(NSA_N_SEL, nsb)

    q_t = q_ref[0, 0]
    qcat = [jnp.concatenate([q_t[(g * NSA_HPG + h) * dh:(g * NSA_HPG + h + 1) * dh, :] for h in range(NSA_HPG)],
                            axis=1) for g in range(gl)]
    zero = jnp.zeros_like(qcat[0])
    qp = jnp.concatenate([jnp.concatenate([qcat[0], zero], axis=1),
                          jnp.concatenate([zero, qcat[1]], axis=1)], axis=0)
    qlane = lax.broadcasted_iota(jnp.int32, (1, lanes), 1) % qt
    tq = t0 + qlane

    def values(v_ref, slab0, n_slabs):
        v = jnp.concatenate([v_ref[0, slab0 + i] for i in range(n_slabs)], axis=1)
        return jnp.concatenate([v, jnp.ones((16, n_slabs * qt), BF16)], axis=0)

    def weighted_values(v_aug, p):
        return jnp.concatenate(
            [_dot(jnp.concatenate([v_aug[g * dh:(g + 1) * dh, :], v_aug[vrows:, :]], axis=0),
                  p[:, g * glanes:(g + 1) * glanes]) for g in range(gl)], axis=1)

    def normalise(acc):
        o = acc[:dh, :] / jnp.maximum(acc[dh:dh + 1, :], 1e-30)
        return [o[:, g * glanes:(g + 1) * glanes] for g in range(gl)]

    wslabs = NSA_WINDOW // qt + 1
    row_w = lax.broadcasted_iota(jnp.int32, (qt, lanes), 0)

    @pl.when(j == 0)
    def _():
        wbias_ref[0] = jnp.where(row_w > qlane, 0.0, NEG_INF)
        wbias_ref[1] = jnp.zeros((qt, lanes), F32)
        wbias_ref[2] = jnp.where(row_w <= qlane, 0.0, NEG_INF)
        wbias_ref[3] = jnp.full((qt, lanes), NEG_INF, F32)

    jb0 = jnp.maximum(j + 1 - wslabs, 0)
    kw = k2_ref[0, pl.ds(pl.multiple_of(jb0 * qt, qt), wslabs * qt), :]
    sw_raw = _dot(kw, qp)
    slabs_back = [j - jb0 - i for i in range(wslabs)]
    sw = jnp.concatenate(
        [sw_raw[i * qt:(i + 1) * qt, :]
         + wbias_ref[jnp.where(back < 0, 3, jnp.where(back == wslabs - 1, 0, jnp.where(back == 0, 2, 1)))]
         for i, back in enumerate(slabs_back)], axis=0)

    sc = _dot(kc_ref[0], qp)
    n_idx = lax.broadcasted_iota(jnp.int32, (nb, lanes), 0)
    cmask = (n_idx * NSA_CMP_STRIDE + NSA_CMP_LEN - 1 <= tq) & (n_idx < nb - 1)
    sc = jnp.where(cmask, sc, NEG_INF)
    e = jnp.where(cmask, jnp.exp(sc - jnp.max(sc, axis=0, keepdims=True)), 0.0)
    p_c = e / jnp.maximum(jnp.sum(e, axis=0, keepdims=True), 1e-30)
    acc_c = _dot(vct_ref[0], p_c.astype(BF16))
    o_c = [acc_c[g * dh:(g + 1) * dh, g * glanes:(g + 1) * glanes] for g in range(gl)]
    imps = []
    for g in range(gl):
        imp = p_c[:, g * glanes:g * glanes + qt]
        for h in range(1, NSA_HPG):
            imp = imp + p_c[:, g * glanes + h * qt:g * glanes + (h + 1) * qt]
        imps.append(imp)
    imp = jnp.concatenate(imps, axis=1)

    ratio = NSA_SEL_LEN // NSA_CMP_STRIDE
    dd = lax.broadcasted_iota(jnp.int32, (nsb, nb), 1) - ratio * lax.broadcasted_iota(jnp.int32, (nsb, nb), 0)
    wsel = jnp.where((dd == -1) | (dd == ratio - 1), 1.0,
                     jnp.where((dd >= 0) & (dd < ratio - 1), 2.0, 0.0)).astype(BF16)
    terms = _split3(imp)
    imp_sel = _dot(wsel, terms[0]) + _dot(wsel, terms[1]) + _dot(wsel, terms[2])
    jq = lax.broadcasted_iota(jnp.int32, (nsb, gl * qt), 0)
    cur = (t0 + lax.broadcasted_iota(jnp.int32, (nsb, gl * qt), 1) % qt) // NSA_SEL_LEN
    forced = (jq == 0) | (jq == cur) | (jq == cur - 1)
    score = jnp.where(jq <= cur, imp_sel + jnp.where(forced, NSA_FORCE_BONUS, 0.0), NEG_INF)

    pw = jnp.exp(sw - jnp.max(sw, axis=0, keepdims=True)).astype(BF16)
    o_w = normalise(weighted_values(values(v2_ref, jb0, wslabs), pw))

    jqf = jq.astype(F32)
    work = score
    sel_bias = jnp.full((nsb, gl * qt), NEG_INF, F32)
    for _ in range(n_sel):
        top = jnp.max(work, axis=0, keepdims=True)
        first = jnp.min(jnp.where(work == top, jqf, float(nsb)), axis=0, keepdims=True)
        pick = jqf == first
        sel_bias = jnp.where(pick & (work > 0.5 * NEG_INF), 0.0, sel_bias)
        work = jnp.where(pick, 3.0 * NEG_INF, work)
    bias_ref[...] = jnp.concatenate([sel_bias[:, g * qt:(g + 1) * qt] for g in range(gl) for _ in range(NSA_HPG)],
                                    axis=1)

    blocks_per_chunk = kc // NSA_SEL_LEN
    slabs_per_chunk = kc // qt

    def online(carry, st, bias, v_aug):
        m, acc = carry
        blks = [st[r * NSA_SEL_LEN:(r + 1) * NSA_SEL_LEN, :] for r in range(blocks_per_chunk)]
        part = None
        for r in range(blocks_per_chunk):
            pr = blks[r][0:8, :]
            for i in range(1, NSA_SEL_LEN // 8):
                pr = jnp.maximum(pr, blks[r][8 * i:8 * (i + 1), :])
            pr = pr + bias[r:r + 1, :]
            part = pr if part is None else jnp.maximum(part, pr)
        m_new = jnp.maximum(m, jnp.max(part, axis=0, keepdims=True))
        p = jnp.concatenate([jnp.exp(blks[r] - (m_new - bias[r:r + 1, :])) for r in range(blocks_per_chunk)],
                            axis=0).astype(BF16)
        return m_new, jnp.exp(m - m_new) * acc + weighted_values(v_aug, p)

    def sel_scores(c):
        return _dot(k1_ref[0, pl.ds(pl.multiple_of(c * kc, kc), kc), :], qp)

    def sel_bias(c):
        return bias_ref[pl.ds(pl.multiple_of(c * blocks_per_chunk, blocks_per_chunk), blocks_per_chunk), :]

    def sel_values(c):
        return values(v1_ref, c * slabs_per_chunk, slabs_per_chunk)

    init = (jnp.full((1, lanes), NEG_INF, F32), jnp.zeros((dh + 16, lanes), F32))
    c_hi = t0 // kc
    row_k = lax.broadcasted_iota(jnp.int32, (kc, lanes), 0)
    st = jnp.where(row_k <= (t0 - c_hi * kc) + qlane, sel_scores(c_hi), NEG_INF)
    chain_a = online(init, st, sel_bias(c_hi), sel_values(c_hi))
    chain_a = lax.cond(c_hi % 2 == 1,
                       lambda cr: online(cr, sel_scores(c_hi - 1), sel_bias(c_hi - 1), sel_values(c_hi - 1)),
                       lambda cr: cr, chain_a)
    chain_b = (chain_a[0], jnp.zeros_like(chain_a[1]))

    def pair(i, chains):
        ca, cb = chains
        st_a = sel_scores(2 * i)
        st_b = sel_scores(2 * i + 1)
        return (online(ca, st_a, sel_bias(2 * i), sel_values(2 * i)),
                online(cb, st_b, sel_bias(2 * i + 1), sel_values(2 * i + 1)))

    (m_a, acc_a), (m_b, acc_b) = lax.fori_loop(0, c_hi // 2, pair, (chain_a, chain_b))
    m_s = jnp.maximum(m_a, m_b)
    o_s = normalise(jnp.exp(m_a - m_s) * acc_a + jnp.exp(m_b - m_s) * acc_b)

    gates = jax.nn.sigmoid(gt_ref[0, 0].astype(F32))
    outs = []
    for g in range(gl):
        def gate_row(br):
            r0 = (g * 4 + br) * NSA_HPG
            return jnp.concatenate([gates[r0 + h:r0 + h + 1, :] for h in range(NSA_HPG)], axis=1)
        o = gate_row(0) * o_c[g] + gate_row(1) * o_s[g] + gate_row(2) * o_w[g]
        outs.extend(o[:, h * qt:(h + 1) * qt] for h in range(NSA_HPG))
    o_ref[0] = jnp.transpose(jnp.concatenate(outs, axis=0)).astype(o_ref.dtype)


def _nsa_attend_pairs(tr, k_cmp, v_cmp_t, k12, *, b, s):
    qt, dh = NSA_QT, NSA_DH
    gl = 2
    gd = NSA_GROUPS * dh
    hq = NSA_GROUPS * NSA_HPG * dh
    pairs = NSA_GROUPS // gl
    nslab = s // qt
    assert nslab > NSA_WINDOW // qt
    nb = k_cmp.shape[1]
    tr4 = tr.reshape(b, nslab, tr.shape[1], qt)
    q_rows = gl * NSA_HPG * dh
    v_rows = gl * dh
    gate_rows = gl * 4 * NSA_HPG
    v1_blk = hq // v_rows
    v2_blk = (hq + gd) // v_rows
    gate_blk = (hq + 2 * gd) // gate_rows
    nsb = s // NSA_SEL_LEN
    return pl.pallas_call(
        functools.partial(_nsa_pair_body, s=s),
        grid=(b, pairs, nslab),
        in_specs=[
            pl.BlockSpec((1, 1, q_rows, qt), lambda i, p, j: (i, j, p, 0)),
            pl.BlockSpec((1, 1, gate_rows, qt), lambda i, p, j: (i, j, gate_blk + p, 0)),
            pl.BlockSpec((1, nb, v_rows), lambda i, p, j: (i, 0, p)),
            pl.BlockSpec((1, v_rows, nb), lambda i, p, j: (i, p, 0)),
            pl.BlockSpec((1, s, v_rows), lambda i, p, j: (i, 0, p)),
            pl.BlockSpec((1, s, v_rows), lambda i, p, j: (i, 0, pairs + p)),
            pl.BlockSpec((1, nslab, v_rows, qt), lambda i, p, j: (i, 0, v1_blk + p, 0)),
            pl.BlockSpec((1, nslab, v_rows, qt), lambda i, p, j: (i, 0, v2_blk + p, 0)),
        ],
        out_specs=pl.BlockSpec((1, qt, q_rows), lambda i, p, j: (i, j, p)),
        out_shape=jax.ShapeDtypeStruct((b, s, hq), BF16),
        scratch_shapes=[pltpu.VMEM((nsb, gl * NSA_HPG * qt), F32), pltpu.VMEM((4, qt, gl * NSA_HPG * qt), F32)],
        compiler_params=_params("parallel", "parallel", "arbitrary"),
        name="nsa_attend",
    )(tr4, tr4, k_cmp, v_cmp_t, k12, k12, tr4, tr4)


def _nsa(h2, g_norm, w_in, cmp_pos, cmp_w1, cmp_w2, w_out, *, b, s):
    g, hpg, dh = NSA_GROUPS, NSA_HPG, NSA_DH
    hq, gd = g * hpg * dh, g * dh
    kv_w = lambda br, kv: w_in[:, hq + (2 * br + kv) * gd:hq + (2 * br + kv + 1) * gd]
    w_q = w_in[:, :hq] * (dh ** -0.5)
    w_g = w_in[:, hq + 6 * gd:].reshape(-1, 3, g, hpg).transpose(2, 1, 3, 0)
    w_g = jnp.pad(w_g, ((0, 0), (0, 1), (0, 0), (0, 0))).reshape(g * 4 * hpg, -1)
    w_t = jnp.concatenate([w_q.T, kv_w(1, 1).T, kv_w(2, 1).T, w_g], axis=0).astype(BF16)
    w_nat = jnp.concatenate([kv_w(0, 0), kv_w(0, 1), kv_w(1, 0), kv_w(2, 0)], axis=1).astype(BF16)
    k0, v0, k12, tr = _nsa_proj(h2, g_norm, w_nat, w_t)
    nb = s // NSA_CMP_STRIDE
    k_cmp = _nsa_compress(k0.reshape(b, nb, NSA_CMP_STRIDE * gd), cmp_pos[0], cmp_w1[0], cmp_w2[0], transposed=False)
    v_cmp_t = _nsa_compress(v0.reshape(b, nb, NSA_CMP_STRIDE * gd), cmp_pos[1], cmp_w1[1], cmp_w2[1], transposed=True)
    y = _nsa_attend_pairs(tr, k_cmp, v_cmp_t, k12.reshape(b, s, 2 * gd), b=b, s=s)
    return y.reshape(b * s, hq), w_out.astype(BF16)


def kernel(x, norm_ffn1, ffn1_w_in, ffn1_w_out, norm_mix, norm_ffn2, ffn2_w_in, ffn2_w_out, norm_final,
           ret_w_in, ret_gn_gain, ret_w_out,
           nsa_w_in, nsa_cmp_pos, nsa_cmp_w1, nsa_cmp_w2, nsa_w_out,
           ssd_w_in, ssd_conv_w, ssd_conv_b, ssd_dt_bias, ssd_a_log, ssd_d, ssd_norm, ssd_w_out,
           dil_w_in, dil_w_out):
    b, s, d = x.shape
    depth = norm_mix.shape[0]
    h = x.reshape(b * s, d)
    for i in range(depth):
        h = _ffn(h, norm_ffn1[i], ffn1_w_in[i].astype(BF16), ffn1_w_out[i].astype(BF16))
        m, j = i % 4, i // 4
        mix = None
        if m == 0:
            mix = _retention(h, norm_mix[i], ret_w_in[j], ret_gn_gain[j], ret_w_out[j], b=b, s=s)
        elif m == 1:
            mix = _nsa(h, norm_mix[i], nsa_w_in[j], nsa_cmp_pos[j], nsa_cmp_w1[j], nsa_cmp_w2[j], nsa_w_out[j], b=b, s=s)
        elif m == 2:
            mix = _ssd(h, norm_mix[i], ssd_w_in[j], ssd_conv_w[j], ssd_conv_b[j], ssd_dt_bias[j], ssd_a_log[j],
                       ssd_d[j], ssd_norm[j], ssd_w_out[j], b=b, s=s)
        else:
            h = _dilated(h, norm_mix[i], dil_w_in[j], dil_w_out[j], b=b, s=s)
        h = _ffn(h, norm_ffn2[i], ffn2_w_in[i].astype(BF16), ffn2_w_out[i].astype(BF16),
                 norm_final if i == depth - 1 else None, mix)
    return h.reshape(b, s, d)
```

```python
import functools
import math

import jax
import jax.numpy as jnp
from jax import lax
from jax.experimental import pallas as pl
from jax.experimental.pallas import tpu as pltpu

F32 = jnp.float32
BF16 = jnp.bfloat16
NORM_EPS = 1e-6
NEG_INF = -1e30
ROPE_BASE = 10000.0
VMEM_LIMIT_BYTES = 56 * 1024 * 1024
LANES = 128
MXU_WIDTH = 256
TOKEN_TILE = 512
MAX_DOT_COLS = 2048

RET_HEADS = 4
RET_CHUNK = 128

NSA_GROUPS = 4
NSA_HPG = 4
NSA_DH = 64
NSA_CMP_LEN = 32
NSA_CMP_STRIDE = 16
NSA_SEL_LEN = 64
NSA_N_SEL = 16
NSA_WINDOW = 512
NSA_FORCE_BONUS = 1e4
NSA_QT = 256
NSA_KC = 512

SSD_GROUPS = 4
SSD_HPG = 8
SSD_HEADDIM = 64
SSD_STATE = 128
SSD_CONV = 4
SSD_CHUNK = 256

DIL_PATTERN = ((128, 1), (512, 4), (2048, 16))
DIL_HEADS = 8
DIL_DH = 128
DIL_QB = 128


def _params(*sem):
    return pltpu.CompilerParams(dimension_semantics=sem, vmem_limit_bytes=VMEM_LIMIT_BYTES)


def _rms(x, g):
    return x * lax.rsqrt(jnp.mean(x * x, axis=-1, keepdims=True) + NORM_EPS) * g


def _silu(x):
    return x * jax.nn.sigmoid(x)


def _dot(a, b):
    return jnp.dot(a, b, preferred_element_type=F32)


def _dot_nt(a, b):
    return lax.dot_general(a, b, (((1,), (1,)), ((), ())), preferred_element_type=F32)


def _split3(x):
    hi = x.astype(BF16)
    r1 = x - hi.astype(F32)
    mid = r1.astype(BF16)
    lo = (r1 - mid.astype(F32)).astype(BF16)
    return hi, mid, lo


def _dot3(terms, rhs):
    return _dot(terms[0], rhs) + _dot(terms[1], rhs) + _dot(terms[2], rhs)


def _col_slices(n):
    if n <= MAX_DOT_COLS:
        return [slice(0, n)]
    tiles = -(-n // MXU_WIDTH)
    parts = -(-tiles * MXU_WIDTH // MAX_DOT_COLS)
    bounds = [min(n, (tiles * p // parts) * MXU_WIDTH) for p in range(parts + 1)]
    return [slice(lo, hi) for lo, hi in zip(bounds[:-1], bounds[1:])]


def _resident(shape, index=None):
    index = index if index is not None else (0,) * len(shape)
    return pl.BlockSpec(shape, lambda *_: index, pipeline_mode=pl.Buffered(1))


def _ffn_body(h_ref, *refs, final_norm, mix):
    x = h_ref[...]
    if mix:
        y_ref, wm_ref, *refs = refs
        x = x + _dot(y_ref[...], wm_ref[...])
    g_ref, wa_ref, wb_ref, wo_ref, gf_ref, o_ref = refs
    xn = _rms(x, g_ref[...]).astype(BF16)
    acc = jnp.zeros(x.shape, F32)
    for sl in _col_slices(wa_ref.shape[1]):
        a = _dot(xn, wa_ref[:, sl])
        b = _dot(xn, wb_ref[:, sl])
        acc = acc + _dot((_silu(a) * b).astype(BF16), wo_ref[sl, :])
    y = x + 0.5 * acc
    if final_norm:
        y = _rms(y, gf_ref[...])
    o_ref[...] = y


def _ffn(h, g, w_in, w_out, g_final=None, mix=None, *, tm=TOKEN_TILE):
    m, d = h.shape
    f = w_out.shape[0]
    final_norm = g_final is not None
    gf = g_final if final_norm else g
    mix_args = list(mix) if mix is not None else []
    mix_specs = [pl.BlockSpec((tm, mix[0].shape[1]), lambda i: (i, 0)), _resident(mix[1].shape)] if mix_args else []
    return pl.pallas_call(
        functools.partial(_ffn_body, final_norm=final_norm, mix=bool(mix_args)),
        grid=(m // tm,),
        in_specs=[pl.BlockSpec((tm, d), lambda i: (i, 0))] + mix_specs + [
            _resident((1, d)),
            _resident((d, f)),
            _resident((d, f), (0, 1)),
            _resident((f, d)),
            _resident((1, d)),
        ],
        out_specs=pl.BlockSpec((tm, d), lambda i: (i, 0)),
        out_shape=jax.ShapeDtypeStruct((m, d), F32),
        compiler_params=_params("parallel"),
        name="ffn",
    )(h, *mix_args, g.reshape(1, d), w_in, w_in, w_out, gf.reshape(1, d))


def _norm_proj_body(h_ref, g_ref, *refs, n_out):
    xn = _rms(h_ref[...], g_ref[...]).astype(BF16)
    for w_ref, o_ref in zip(refs[:n_out], refs[n_out:]):
        for sl in _col_slices(w_ref.shape[1]):
            o_ref[:, sl] = _dot(xn, w_ref[:, sl]).astype(o_ref.dtype)


def _norm_proj(h, g, ws, out_dtypes, *, tm=TOKEN_TILE):
    m, d = h.shape
    return pl.pallas_call(
        functools.partial(_norm_proj_body, n_out=len(ws)),
        grid=(m // tm,),
        in_specs=[pl.BlockSpec((tm, d), lambda i: (i, 0)), pl.BlockSpec((1, d), lambda i: (0, 0))]
        + [pl.BlockSpec(w.shape, lambda i: (0, 0)) for w in ws],
        out_specs=[pl.BlockSpec((tm, w.shape[1]), lambda i: (i, 0)) for w in ws],
        out_shape=[jax.ShapeDtypeStruct((m, w.shape[1]), dt) for w, dt in zip(ws, out_dtypes)],
        compiler_params=_params("parallel"),
        name="norm_proj",
    )(h, g.reshape(1, d), *ws)


def _ret_body(q_ref, k_ref, v_ref, g_ref, cos_ref, sin_ref, gn_ref, o_ref, state_ref, *, ts):
    c_len = RET_CHUNK
    dk = q_ref.shape[2] // RET_HEADS
    dv = v_ref.shape[2] // RET_HEADS
    half = dk // 2

    @pl.when(pl.program_id(1) == 0)
    def _():
        state_ref[...] = jnp.zeros(state_ref.shape, F32)

    ii = lax.broadcasted_iota(jnp.int32, (c_len, c_len), 0)
    jj = lax.broadcasted_iota(jnp.int32, (c_len, c_len), 1)
    rel = (ii - jj).astype(F32)
    causal = ii >= jj
    idx = lax.broadcasted_iota(jnp.int32, (c_len, 1), 0).astype(F32)

    def rot(t, cos, sin):
        t1, t2 = t[:, :half], t[:, half:]
        return jnp.concatenate([t1 * cos - t2 * sin, t1 * sin + t2 * cos], axis=1)

    decays = []
    for h in range(RET_HEADS):
        log_gamma = math.log1p(-(2.0 ** (-5.0 - h)))
        decays.append((jnp.where(causal, jnp.exp(jnp.where(causal, rel, 0.0) * log_gamma), 0.0),
                       jnp.exp((idx + 1.0) * log_gamma),
                       jnp.exp((c_len - 1.0 - idx) * log_gamma),
                       math.exp(c_len * log_gamma)))

    def chunk(c, carry):
        r0 = pl.multiple_of(c * c_len, c_len)
        rows = pl.ds(r0, c_len)
        cos = cos_ref[rows, :]
        sin = sin_ref[rows, :]
        for h in range(RET_HEADS):
            inner, q_decay, k_decay, chunk_decay = decays[h]
            q = rot(q_ref[0, rows, h * dk:(h + 1) * dk].astype(F32), cos, sin)
            k = rot(k_ref[0, rows, h * dk:(h + 1) * dk].astype(F32), cos, sin) * (dk ** -0.5)
            v = v_ref[0, rows, h * dv:(h + 1) * dv]
            qb = q.astype(BF16)
            sc = _dot_nt(qb, k.astype(BF16)) * inner
            st = state_ref[h]
            o = _dot(sc.astype(BF16), v) + _dot(qb, st.astype(BF16)) * q_decay
            kd_t = jnp.transpose(k * k_decay).astype(BF16)
            state_ref[h] = st * chunk_decay + _dot(kd_t, v)
            gn = gn_ref[:, h * dv:(h + 1) * dv]
            on = _rms(o, gn)
            gate = g_ref[0, rows, h * dv:(h + 1) * dv].astype(F32)
            o_ref[0, rows, h * dv:(h + 1) * dv] = (_silu(gate) * on).astype(o_ref.dtype)
        return carry

    lax.fori_loop(0, ts // c_len, chunk, 0)


def _retention_core(proj, cos, sin, gn_gain, *, b, s, ts=TOKEN_TILE):
    n = proj.shape[2]
    hk = n // 6
    hv = 2 * hk
    dk = hk // RET_HEADS
    return pl.pallas_call(
        functools.partial(_ret_body, ts=ts),
        grid=(b, s // ts),
        in_specs=[
            pl.BlockSpec((1, ts, hk), lambda i, j: (i, j, 0)),
            pl.BlockSpec((1, ts, hk), lambda i, j: (i, j, 1)),
            pl.BlockSpec((1, ts, hv), lambda i, j: (i, j, 1)),
            pl.BlockSpec((1, ts, hv), lambda i, j: (i, j, 2)),
            pl.BlockSpec((ts, dk // 2), lambda i, j: (j, 0)),
            pl.BlockSpec((ts, dk // 2), lambda i, j: (j, 0)),
            pl.BlockSpec((1, hv), lambda i, j: (0, 0)),
        ],
        out_specs=pl.BlockSpec((1, ts, hv), lambda i, j: (i, j, 0)),
        out_shape=jax.ShapeDtypeStruct((b, s, hv), BF16),
        scratch_shapes=[pltpu.VMEM((RET_HEADS, dk, hv // RET_HEADS), F32)],
        compiler_params=_params("parallel", "arbitrary"),
        name="retention",
    )(proj, proj, proj, proj, cos, sin, gn_gain.reshape(1, hv))


def _rope_tables(s, half):
    inv = ROPE_BASE ** (-jnp.arange(half, dtype=F32) / half)
    ang = jnp.arange(s, dtype=F32)[:, None] * inv[None, :]
    return jnp.cos(ang), jnp.sin(ang)


def _retention(h2, g_norm, w_in, gn_gain, w_out, *, b, s):
    (proj,) = _norm_proj(h2, g_norm, [w_in.astype(BF16)], [BF16])
    n = proj.shape[1]
    cos, sin = _rope_tables(s, n // 6 // RET_HEADS // 2)
    y = _retention_core(proj.reshape(b, s, n), cos, sin, gn_gain, b=b, s=s)
    return y.reshape(b * s, -1), w_out.astype(BF16)


def _dil_proj_body(h_ref, g_ref, w_ref, o_ref, xn_ref, *, r):
    tm = h_ref.shape[0]
    n = tm // r
    xn = _rms(h_ref[...], g_ref[...])
    slabs = xn_ref.shape[0]
    for k in range(slabs):
        xn_ref[k] = xn[:, k * LANES:(k + 1) * LANES]
    xp = jnp.concatenate(
        [jnp.concatenate([xn_ref[k, pl.ds(c, n, stride=r), :] for c in range(r)], axis=0) for k in range(slabs)],
        axis=1).astype(BF16)
    for sl in _col_slices(w_ref.shape[1]):
        res = _dot(xp, w_ref[:, sl]).astype(o_ref.dtype)
        for c in range(r):
            o_ref[0, c, :, sl] = res[c * n:(c + 1) * n, :]


def _dil_proj(h2, g_norm, w, r, *, b, s, tm=TOKEN_TILE):
    d = h2.shape[1]
    n_out = w.shape[1]
    tiles = s // tm
    return pl.pallas_call(
        functools.partial(_dil_proj_body, r=r),
        grid=(b * tiles,),
        in_specs=[
            pl.BlockSpec((tm, d), lambda i: (i, 0)),
            pl.BlockSpec((1, d), lambda i: (0, 0)),
            pl.BlockSpec((d, n_out), lambda i: (0, 0)),
        ],
        out_specs=pl.BlockSpec((1, r, tm // r, n_out), lambda i: (i // tiles, 0, i % tiles, 0)),
        out_shape=jax.ShapeDtypeStruct((b, r, s // r, n_out), BF16),
        scratch_shapes=[pltpu.VMEM((d // LANES, tm, LANES), F32)],
        compiler_params=_params("parallel"),
        name=f"dilated_proj_r{r}",
    )(h2, g_norm.reshape(1, d), w)


def _dil_body(q_ref, kc_ref, kp_ref, vc_ref, vp_ref, o_ref, lse_ref, kbuf, vbuf, *, rows, n_back):
    qb = DIL_QB
    dh = DIL_DH
    first_step = pl.program_id(2) == 0
    kbuf[0:qb, :] = kp_ref[0, 0]
    kbuf[qb:, :] = kc_ref[0, 0]
    vbuf[0:qb, :] = vp_ref[0, 0]
    vbuf[qb:, :] = vc_ref[0, 0]
    qi = lax.broadcasted_iota(jnp.int32, (qb, 2 * qb), 0)
    kj = lax.broadcasted_iota(jnp.int32, (qb, 2 * qb), 1)
    dist = qi + qb - kj
    band = (dist >= 0) & (dist <= n_back)
    lane = lax.broadcasted_iota(jnp.int32, (qb, LANES), 1)
    scale = dh ** -0.5
    ones = jnp.ones((2 * qb, dh), BF16)
    for i in range(rows // qb):
        mask = band & ((kj >= qb) | jnp.logical_not(first_step)) if i == 0 else band
        lse_tile = jnp.zeros((qb, LANES), F32)
        for h in range(DIL_HEADS):
            cols = slice(h * dh, (h + 1) * dh)
            q = q_ref[0, 0, i * qb:(i + 1) * qb, cols]
            k = kbuf[i * qb:(i + 2) * qb, cols]
            v = vbuf[i * qb:(i + 2) * qb, cols]
            s = jnp.where(mask, _dot_nt(q, k), NEG_INF)
            m = jnp.max(s, axis=1, keepdims=True)
            e = jnp.exp2((s - m) * (scale * math.log2(math.e))).astype(BF16)
            acc = _dot(e, jnp.concatenate([v, ones], axis=1))
            den = jnp.maximum(acc[:, dh:], 1e-30)
            o_ref[0, 0, i * qb:(i + 1) * qb, cols] = (acc[:, :dh] / den).astype(o_ref.dtype)
            lse_tile = jnp.where(lane == h, m * scale + jnp.log(den), lse_tile)
        lse_ref[0, 0, i * qb:(i + 1) * qb, :] = lse_tile


def _dilated_group(proj, win, r, *, b, s):
    hd = DIL_HEADS * DIL_DH
    length = s // r
    rows = min(length, TOKEN_TILE)
    qb = DIL_QB

    def cur(which):
        return pl.BlockSpec((1, 1, rows, hd), lambda i, c, n: (i, c, n, which))

    def prev(which):
        return pl.BlockSpec((1, 1, qb, hd), lambda i, c, n: (i, c, jnp.maximum(n * (rows // qb) - 1, 0), which))

    return pl.pallas_call(
        functools.partial(_dil_body, rows=rows, n_back=win // r),
        grid=(b, r, length // rows),
        in_specs=[cur(0), cur(1), prev(1), cur(2), prev(2)],
        out_specs=[
            pl.BlockSpec((1, 1, rows, hd), lambda i, c, n: (i, c, n, 0)),
            pl.BlockSpec((1, 1, rows, LANES), lambda i, c, n: (i, c, n, 0)),
        ],
        out_shape=[
            jax.ShapeDtypeStruct((b, r, length, hd), BF16),
            jax.ShapeDtypeStruct((b, r, length, LANES), F32),
        ],
        scratch_shapes=[pltpu.VMEM((rows + qb, hd), BF16), pltpu.VMEM((rows + qb, hd), BF16)],
        compiler_params=_params("parallel", "parallel", "arbitrary"),
        name=f"dilated_r{r}",
    )(proj, proj, proj, proj, proj)


def _dil_merge_body(h_ref, *refs, dilations):
    ng = len(dilations)
    o_refs, l_refs = refs[:ng], refs[ng:2 * ng]
    w_ref, out_ref = refs[2 * ng], refs[2 * ng + 1]
    o_bufs, l_bufs = refs[2 * ng + 2:3 * ng + 2], refs[3 * ng + 2:]
    tm = h_ref.shape[1]
    for o_ref, l_ref, o_buf, l_buf, r in zip(o_refs, l_refs, o_bufs, l_bufs, dilations):
        n = tm // r
        for c in range(r):
            rows = pl.ds(c, n, stride=r)
            l_buf[rows, :] = l_ref[0, c]
            blk = o_ref[0, c].astype(F32)
            for h in range(DIL_HEADS):
                o_buf[h, rows, :] = blk[:, h * DIL_DH:(h + 1) * DIL_DH]
    lses = [l[...] for l in l_bufs]
    mx = functools.reduce(jnp.maximum, lses)
    es = [jnp.exp(l - mx) for l in lses]
    tot = functools.reduce(jnp.add, es)
    wts = [e / tot for e in es]
    dh = DIL_DH
    parts = []
    for h in range(DIL_HEADS):
        acc = jnp.zeros((tm, dh), F32)
        for g in range(ng):
            wg = jnp.broadcast_to(wts[g][:, h:h + 1], (tm, dh))
            acc = acc + wg * o_bufs[g][h]
        parts.append(acc)
    o = jnp.concatenate(parts, axis=1).astype(BF16)
    out_ref[0] = h_ref[0] + _dot(o, w_ref[...])


def _dil_merge(h3, outs, lses, w, dilations, *, tm=TOKEN_TILE):
    b, s, d = h3.shape
    hd = w.shape[0]
    res_major = lambda r, width: pl.BlockSpec((1, r, tm // r, width), lambda i, n: (i, 0, n, 0))
    return pl.pallas_call(
        functools.partial(_dil_merge_body, dilations=tuple(dilations)),
        grid=(b, s // tm),
        in_specs=[pl.BlockSpec((1, tm, d), lambda i, n: (i, n, 0))]
        + [res_major(r, hd) for r in dilations] + [res_major(r, LANES) for r in dilations]
        + [pl.BlockSpec((hd, d), lambda i, n: (0, 0))],
        out_specs=pl.BlockSpec((1, tm, d), lambda i, n: (i, n, 0)),
        out_shape=jax.ShapeDtypeStruct((b, s, d), F32),
        scratch_shapes=[pltpu.VMEM((DIL_HEADS, tm, DIL_DH), F32) for _ in dilations]
        + [pltpu.VMEM((tm, LANES), F32) for _ in dilations],
        compiler_params=_params("parallel", "parallel"),
        name="dilated_merge",
    )(h3, *outs, *lses, w)


def _dilated(h2, g_norm, w_in, w_out, *, b, s):
    d = h2.shape[1]
    gw = 3 * DIL_HEADS * DIL_DH
    wb = w_in.astype(BF16)
    outs, lses, dilations = [], [], []
    for g, (win, r) in enumerate(DIL_PATTERN):
        proj = _dil_proj(h2, g_norm, wb[:, g * gw:(g + 1) * gw], r, b=b, s=s)
        o, lse = _dilated_group(proj, win, r, b=b, s=s)
        outs.append(o)
        lses.append(lse)
        dilations.append(r)
    return _dil_merge(h2.reshape(b, s, d), outs, lses, w_out.astype(BF16), dilations).reshape(b * s, d)


def _softplus(x):
    return jnp.maximum(x, 0.0) + jnp.log1p(jnp.exp(-jnp.abs(x)))


def _ssd_body(z_ref, x_ref, dt_ref, cw_ref, cb_ref, dtb_ref, alog_ref, dsk_ref, ng_ref, o_ref,
              xpad, state_ref):
    ln = SSD_CHUNK
    nst = SSD_STATE
    gw = SSD_HPG * SSD_HEADDIM
    d_inner = SSD_GROUPS * gw
    halo = 8

    @pl.when(pl.program_id(1) == 0)
    def _():
        xpad[0:halo, :] = jnp.zeros((halo, xpad.shape[1]), F32)
        state_ref[...] = jnp.zeros(state_ref.shape, F32)

    x_b = x_ref[0]
    x_f = x_b.astype(F32)
    ii = lax.broadcasted_iota(jnp.int32, (ln, ln), 0)
    jj = lax.broadcasted_iota(jnp.int32, (ln, ln), 1)
    conv = cb_ref[...] + x_f * cw_ref[SSD_CONV - 1:SSD_CONV, :]
    for shift in range(1, SSD_CONV):
        shifted = _dot(jnp.where(ii - jj == shift, 1.0, 0.0).astype(BF16), x_b)
        conv = conv + shifted * cw_ref[SSD_CONV - 1 - shift:SSD_CONV - shift, :]
    xpad[halo:2 * halo, :] = x_f[0:halo, :]
    head = cb_ref[...]
    for k in range(SSD_CONV):
        off = halo - (SSD_CONV - 1) + k
        head = head + xpad[off:off + halo, :] * cw_ref[k:k + 1, :]
    xpad[0:halo, :] = x_f[ln - halo:ln, :]
    xbc = _silu(jnp.concatenate([head, conv[halo:, :]], axis=0))
    xs = xbc[:, :d_inner]
    bm = xbc[:, d_inner:d_inner + SSD_GROUPS * nst]
    cm = xbc[:, d_inner + SSD_GROUPS * nst:]

    dt = _softplus(dt_ref[0] + dtb_ref[...])
    da = dt * (-jnp.exp(alog_ref[...]))
    ii = lax.broadcasted_iota(jnp.int32, (ln, ln), 0)
    jj = lax.broadcasted_iota(jnp.int32, (ln, ln), 1)
    causal = ii >= jj
    tril = jnp.where(causal, 1.0, 0.0).astype(BF16)
    da_terms = _split3(da)
    acs = _dot(tril, da_terms[0]) + _dot(tril, da_terms[1]) + _dot(tril, da_terms[2])
    acs_t = jnp.transpose(acs)

    erow = lax.broadcasted_iota(jnp.int32, (LANES, d_inner), 0)
    ecol = lax.broadcasted_iota(jnp.int32, (LANES, d_inner), 1)
    expand = jnp.where(ecol // SSD_HEADDIM == erow, 1.0, 0.0).astype(BF16)
    acs_e = _dot3(_split3(acs), expand)
    dt_e = _dot3(_split3(dt), expand)
    last = acs_e[ln - 1:ln, :]
    decay_in = jnp.exp(acs_e)
    xs_dt = xs * dt_e
    xs_end = (xs_dt * jnp.exp(last - acs_e)).astype(BF16)
    xs_b = xs_dt.astype(BF16)
    lane = lax.broadcasted_iota(jnp.int32, (ln, LANES), 1)

    y_groups = []
    for g in range(SSD_GROUPS):
        bm_g = bm[:, g * nst:(g + 1) * nst]
        cm_g = cm[:, g * nst:(g + 1) * nst].astype(BF16)
        cb = jnp.where(causal, _dot_nt(cm_g, bm_g.astype(BF16)), 0.0)
        st = state_ref[g]
        gcols = slice(g * gw, (g + 1) * gw)
        y_state = _dot(cm_g, st.astype(BF16)) * decay_in[:, gcols]
        pairs = []
        for p in range(SSD_HPG // 2):
            pair_cols = slice(g * gw + p * LANES, g * gw + (p + 1) * LANES)
            halves = []
            for e in range(2):
                hd = g * SSD_HPG + 2 * p + e
                diff = acs[:, hd:hd + 1] - acs_t[hd:hd + 1, :]
                w = (cb * jnp.exp(jnp.minimum(diff, 0.0))).astype(BF16)
                halves.append(_dot(w, xs_b[:, pair_cols]))
            pairs.append(jnp.where(lane < SSD_HEADDIM, halves[0], halves[1]))
        y_groups.append(jnp.concatenate(pairs, axis=1) + y_state)
        bm_t = jnp.transpose(bm_g).astype(BF16)
        state_ref[g] = st * jnp.exp(last[:, gcols]) + _dot(bm_t, xs_end[:, gcols])

    y = jnp.concatenate(y_groups, axis=1) + dsk_ref[...] * xs
    yz = y * _silu(z_ref[0].astype(F32))
    outs = [_rms(yz[:, g * gw:(g + 1) * gw], ng_ref[:, g * gw:(g + 1) * gw]) for g in range(SSD_GROUPS)]
    o_ref[0] = jnp.concatenate(outs, axis=1).astype(o_ref.dtype)


def _ssd_core(z, xbc, dt, conv_w, conv_b, dt_bias, a_log, d_skip, norm_g, *, b, s):
    ln = SSD_CHUNK
    d_inner = z.shape[2]
    conv_dim = xbc.shape[2]
    gw = SSD_HPG * SSD_HEADDIM
    full = lambda shape: pl.BlockSpec(shape, lambda i, j: (0,) * len(shape))
    return pl.pallas_call(
        _ssd_body,
        grid=(b, s // ln),
        in_specs=[
            pl.BlockSpec((1, ln, d_inner), lambda i, j: (i, j, 0)),
            pl.BlockSpec((1, ln, conv_dim), lambda i, j: (i, j, 0)),
            pl.BlockSpec((1, ln, LANES), lambda i, j: (i, j, 0)),
            full((SSD_CONV, conv_dim)),
            full((1, conv_dim)),
            full((1, LANES)),
            full((1, LANES)),
            full((1, d_inner)),
            full((1, d_inner)),
        ],
        out_specs=pl.BlockSpec((1, ln, d_inner), lambda i, j: (i, j, 0)),
        out_shape=jax.ShapeDtypeStruct((b, s, d_inner), BF16),
        scratch_shapes=[pltpu.VMEM((16, conv_dim), F32),
                        pltpu.VMEM((SSD_GROUPS, SSD_STATE, gw), F32)],
        compiler_params=_params("parallel", "arbitrary"),
        name="ssd",
    )(z, xbc, dt, conv_w, conv_b, dt_bias, a_log, d_skip, norm_g)


def _ssd(h2, g_norm, w_in, conv_w, conv_b, dt_bias, a_log, d_skip, norm_g, w_out, *, b, s):
    heads = SSD_GROUPS * SSD_HPG
    d_inner = heads * SSD_HEADDIM
    conv_dim = conv_w.shape[1]
    wb = w_in.astype(BF16)
    w_dt = jnp.pad(wb[:, d_inner + conv_dim:], ((0, 0), (0, LANES - heads)))
    z, xbc, dt = _norm_proj(h2, g_norm, [wb[:, :d_inner], wb[:, d_inner:d_inner + conv_dim], w_dt], [BF16, BF16, F32])
    pad_heads = lambda v: jnp.pad(v.astype(F32), (0, LANES - heads)).reshape(1, LANES)
    y = _ssd_core(z.reshape(b, s, -1), xbc.reshape(b, s, -1), dt.reshape(b, s, LANES),
                  conv_w.astype(F32), conv_b.reshape(1, -1).astype(F32), pad_heads(dt_bias), pad_heads(a_log),
                  jnp.repeat(d_skip.astype(F32), SSD_HEADDIM).reshape(1, d_inner), norm_g.reshape(1, d_inner),
                  b=b, s=s)
    return y.reshape(b * s, d_inner), w_out.astype(BF16)


def _nsa_proj_body(h_ref, g_ref, wn_ref, wt_ref, k0_ref, v0_ref, k12_ref, tr_ref):
    xn = _rms(h_ref[...], g_ref[...]).astype(BF16)
    nat = _dot(xn, wn_ref[...])
    gd = k0_ref.shape[1]
    k0_ref[...] = nat[:, :gd].astype(k0_ref.dtype)
    v0_ref[...] = nat[:, gd:2 * gd].astype(v0_ref.dtype)
    k12_ref[...] = nat[:, 2 * gd:].astype(k12_ref.dtype)
    res = _dot_nt(wt_ref[...], xn)
    for j in range(tr_ref.shape[0]):
        tr_ref[j] = res[:, j * NSA_QT:(j + 1) * NSA_QT].astype(tr_ref.dtype)


def _nsa_proj(h2, g_norm, w_nat, w_t, *, tm=TOKEN_TILE):
    m, d = h2.shape
    gd = NSA_GROUPS * NSA_DH
    nt = w_t.shape[0]
    slabs = tm // NSA_QT
    return pl.pallas_call(
        _nsa_proj_body,
        grid=(m // tm,),
        in_specs=[
            pl.BlockSpec((tm, d), lambda i: (i, 0)),
            pl.BlockSpec((1, d), lambda i: (0, 0)),
            pl.BlockSpec(w_nat.shape, lambda i: (0, 0)),
            pl.BlockSpec(w_t.shape, lambda i: (0, 0)),
        ],
        out_specs=[
            pl.BlockSpec((tm, gd), lambda i: (i, 0)),
            pl.BlockSpec((tm, gd), lambda i: (i, 0)),
            pl.BlockSpec((tm, 2 * gd), lambda i: (i, 0)),
            pl.BlockSpec((slabs, nt, NSA_QT), lambda i: (i, 0, 0)),
        ],
        out_shape=[
            jax.ShapeDtypeStruct((m, gd), BF16),
            jax.ShapeDtypeStruct((m, gd), BF16),
            jax.ShapeDtypeStruct((m, 2 * gd), BF16),
            jax.ShapeDtypeStruct((m // NSA_QT, nt, NSA_QT), BF16),
        ],
        compiler_params=_params("parallel"),
        name="nsa_proj",
    )(h2, g_norm.reshape(1, d), w_nat, w_t)


def _nsa_cmp_body(x_ref, pa_ref, pb_ref, w1a_ref, w1b_ref, w2_ref, o_ref, *, transposed):
    x = x_ref[0].astype(F32)
    nb = x.shape[0]
    ya = _dot((x + pa_ref[...]).astype(BF16), w1a_ref[...])
    yb = _dot((x + pb_ref[...]).astype(BF16), w1b_ref[...])
    hid = _silu(ya + pltpu.roll(yb, nb - 1, 0))
    out = _dot(hid.astype(BF16), w2_ref[...])
    o_ref[0] = (jnp.transpose(out) if transposed else out).astype(o_ref.dtype)


def _nsa_compress(x, pos, w1, w2, *, transposed):
    b, nb, width = x.shape
    g, dh = NSA_GROUPS, NSA_DH
    half = NSA_CMP_LEN // 2
    hid = w1.shape[1]
    eye = jnp.eye(g, dtype=F32)
    w1r = w1.reshape(NSA_CMP_LEN, dh, hid)
    big = jnp.einsum("ldj,gh->lgdhj", w1r, eye).reshape(NSA_CMP_LEN, g * dh, g * hid)
    w1a = big[:half].reshape(half * g * dh, g * hid).astype(BF16)
    w1b = big[half:].reshape(half * g * dh, g * hid).astype(BF16)
    w2big = jnp.einsum("jd,gh->gjhd", w2, eye).reshape(g * hid, g * dh).astype(BF16)
    posb = jnp.broadcast_to(pos[:, None, :], (NSA_CMP_LEN, g, dh)).reshape(NSA_CMP_LEN, g * dh)
    pa = posb[:half].reshape(1, width).astype(F32)
    pb = posb[half:].reshape(1, width).astype(F32)
    full = lambda shape: pl.BlockSpec(shape, lambda i: (0,) * len(shape))
    out_dims = (g * dh, nb) if transposed else (nb, g * dh)
    return pl.pallas_call(
        functools.partial(_nsa_cmp_body, transposed=transposed),
        grid=(b,),
        in_specs=[
            pl.BlockSpec((1, nb, width), lambda i: (i, 0, 0)),
            full((1, width)), full((1, width)),
            full((width, g * hid)), full((width, g * hid)), full((g * hid, g * dh)),
        ],
        out_specs=pl.BlockSpec((1,) + out_dims, lambda i: (i, 0, 0)),
        out_shape=jax.ShapeDtypeStruct((b,) + out_dims, BF16),
        compiler_params=_params("parallel"),
        name="nsa_compress",
    )(x, pa, pb, w1a, w1b, w2big)


def _nsa_pair_body(q_ref, gt_ref, kc_ref, vct_ref, k1_ref, k2_ref, v1_ref, v2_ref, o_ref, bias_ref, wbias_ref, *, s):
    qt, kc, dh = NSA_QT, NSA_KC, NSA_DH
    gl = 2
    glanes = NSA_HPG * qt
    lanes = gl * glanes
    vrows = gl * dh
    j = pl.program_id(2)
    t0 = j * qt
    nb = kc_ref.shape[1]
    nsb = s // NSA_SEL_LEN
    n_sel = min(NSA_N_SEL, nsb)

    q_t = q_ref[0, 0]
    qcat = [jnp.concatenate([q_t[(g * NSA_HPG + h) * dh:(g * NSA_HPG + h + 1) * dh, :] for h in range(NSA_HPG)],
                            axis=1) for g in range(gl)]
    zero = jnp.zeros_like(qcat[0])
    qp = jnp.concatenate([jnp.concatenate([qcat[0], zero], axis=1),
                          jnp.concatenate([zero, qcat[1]], axis=1)], axis=0)
    qlane = lax.broadcasted_iota(jnp.int32, (1, lanes), 1) % qt
    tq = t0 + qlane

    def values(v_ref, slab0, n_slabs):
        v = jnp.concatenate([v_ref[0, slab0 + i] for i in range(n_slabs)], axis=1)
        return jnp.concatenate([v, jnp.ones((16, n_slabs * qt), BF16)], axis=0)

    def weighted_values(v_aug, p):
        return jnp.concatenate(
            [_dot(jnp.concatenate([v_aug[g * dh:(g + 1) * dh, :], v_aug[vrows:, :]], axis=0),
                  p[:, g * glanes:(g + 1) * glanes]) for g in range(gl)], axis=1)

    def normalise(acc):
        o = acc[:dh, :] / jnp.maximum(acc[dh:dh + 1, :], 1e-30)
        return [o[:, g * glanes:(g + 1) * glanes] for g in range(gl)]

    wslabs = NSA_WINDOW // qt + 1
    row_w = lax.broadcasted_iota(jnp.int32, (qt, lanes), 0)

    @pl.when(j == 0)
    def _():
        wbias_ref[0] = jnp.where(row_w > qlane, 0.0, NEG_INF)
        wbias_ref[1] = jnp.zeros((qt, lanes), F32)
        wbias_ref[2] = jnp.where(row_w <= qlane, 0.0, NEG_INF)
        wbias_ref[3] = jnp.full((qt, lanes), NEG_INF, F32)

    jb0 = jnp.maximum(j + 1 - wslabs, 0)
    kw = k2_ref[0, pl.ds(pl.multiple_of(jb0 * qt, qt), wslabs * qt), :]
    sw_raw = _dot(kw, qp)
    slabs_back = [j - jb0 - i for i in range(wslabs)]
    sw = jnp.concatenate(
        [sw_raw[i * qt:(i + 1) * qt, :]
         + wbias_ref[jnp.where(back < 0, 3, jnp.where(back == wslabs - 1, 0, jnp.where(back == 0, 2, 1)))]
         for i, back in enumerate(slabs_back)], axis=0)

    sc = _dot(kc_ref[0], qp)
    n_idx = lax.broadcasted_iota(jnp.int32, (nb, lanes), 0)
    cmask = (n_idx * NSA_CMP_STRIDE + NSA_CMP_LEN - 1 <= tq) & (n_idx < nb - 1)
    sc = jnp.where(cmask, sc, NEG_INF)
    e = jnp.where(cmask, jnp.exp(sc - jnp.max(sc, axis=0, keepdims=True)), 0.0)
    p_c = e / jnp.maximum(jnp.sum(e, axis=0, keepdims=True), 1e-30)
    acc_c = _dot(vct_ref[0], p_c.astype(BF16))
    o_c = [acc_c[g * dh:(g + 1) * dh, g * glanes:(g + 1) * glanes] for g in range(gl)]
    imps = []
    for g in range(gl):
        imp = p_c[:, g * glanes:g * glanes + qt]
        for h in range(1, NSA_HPG):
            imp = imp + p_c[:, g * glanes + h * qt:g * glanes + (h + 1) * qt]
        imps.append(imp)
    imp = jnp.concatenate(imps, axis=1)

    ratio = NSA_SEL_LEN // NSA_CMP_STRIDE
    dd = lax.broadcasted_iota(jnp.int32, (nsb, nb), 1) - ratio * lax.broadcasted_iota(jnp.int32, (nsb, nb), 0)
    wsel = jnp.where((dd == -1) | (dd == ratio - 1), 1.0,
                     jnp.where((dd >= 0) & (dd < ratio - 1), 2.0, 0.0)).astype(BF16)
    terms = _split3(imp)
    imp_sel = _dot(wsel, terms[0]) + _dot(wsel, terms[1]) + _dot(wsel, terms[2])
    jq = lax.broadcasted_iota(jnp.int32, (nsb, gl * qt), 0)
    cur = (t0 + lax.broadcasted_iota(jnp.int32, (nsb, gl * qt), 1) % qt) // NSA_SEL_LEN
    forced = (jq == 0) | (jq == cur) | (jq == cur - 1)
    score = jnp.where(jq <= cur, imp_sel + jnp.where(forced, NSA_FORCE_BONUS, 0.0), NEG_INF)

    pw = jnp.exp(sw - jnp.max(sw, axis=0, keepdims=True)).astype(BF16)
    o_w = normalise(weighted_values(values(v2_ref, jb0, wslabs), pw))

    jqf = jq.astype(F32)
    work = score
    sel_bias = jnp.full((nsb, gl * qt), NEG_INF, F32)
    for _ in range(n_sel):
        top = jnp.max(work, axis=0, keepdims=True)
        first = jnp.min(jnp.where(work == top, jqf, float(nsb)), axis=0, keepdims=True)
        pick = jqf == first
        sel_bias = jnp.where(pick & (work > 0.5 * NEG_INF), 0.0, sel_bias)
        work = jnp.where(pick, 3.0 * NEG_INF, work)
    bias_ref[...] = jnp.concatenate([sel_bias[:, g * qt:(g + 1) * qt] for g in range(gl) for _ in range(NSA_HPG)],
                                    axis=1)

    blocks_per_chunk = kc // NSA_SEL_LEN
    slabs_per_chunk = kc // qt

    def online(carry, st, bias, v_aug):
        m, acc = carry
        blks = [st[r * NSA_SEL_LEN:(r + 1) * NSA_SEL_LEN, :] for r in range(blocks_per_chunk)]
        part = None
        for r in range(blocks_per_chunk):
            pr = blks[r][0:8, :]
            for i in range(1, NSA_SEL_LEN // 8):
                pr = jnp.maximum(pr, blks[r][8 * i:8 * (i + 1), :])
            pr = pr + bias[r:r + 1, :]
            part = pr if part is None else jnp.maximum(part, pr)
        m_new = jnp.maximum(m, jnp.max(part, axis=0, keepdims=True))
        p = jnp.concatenate([jnp.exp(blks[r] - (m_new - bias[r:r + 1, :])) for r in range(blocks_per_chunk)],
                            axis=0).astype(BF16)
        return m_new, jnp.exp(m - m_new) * acc + weighted_values(v_aug, p)

    def sel_scores(c):
        return _dot(k1_ref[0, pl.ds(pl.multiple_of(c * kc, kc), kc), :], qp)

    def sel_bias(c):
        return bias_ref[pl.ds(pl.multiple_of(c * blocks_per_chunk, blocks_per_chunk), blocks_per_chunk), :]

    def sel_values(c):
        return values(v1_ref, c * slabs_per_chunk, slabs_per_chunk)

    init = (jnp.full((1, lanes), NEG_INF, F32), jnp.zeros((dh + 16, lanes), F32))
    c_hi = t0 // kc
    row_k = lax.broadcasted_iota(jnp.int32, (kc, lanes), 0)
    st = jnp.where(row_k <= (t0 - c_hi * kc) + qlane, sel_scores(c_hi), NEG_INF)
    chain_a = online(init, st, sel_bias(c_hi), sel_values(c_hi))
    chain_a = lax.cond(c_hi % 2 == 1,
                       lambda cr: online(cr, sel_scores(c_hi - 1), sel_bias(c_hi - 1), sel_values(c_hi - 1)),
                       lambda cr: cr, chain_a)
    chain_b = (chain_a[0], jnp.zeros_like(chain_a[1]))

    def pair(i, chains):
        ca, cb = chains
        st_a = sel_scores(2 * i)
        st_b = sel_scores(2 * i + 1)
        return (online(ca, st_a, sel_bias(2 * i), sel_values(2 * i)),
                online(cb, st_b, sel_bias(2 * i + 1), sel_values(2 * i + 1)))

    (m_a, acc_a), (m_b, acc_b) = lax.fori_loop(0, c_hi // 2, pair, (chain_a, chain_b))
    m_s = jnp.maximum(m_a, m_b)
    o_s = normalise(jnp.exp(m_a - m_s) * acc_a + jnp.exp(m_b - m_s) * acc_b)

    gates = jax.nn.sigmoid(gt_ref[0, 0].astype(F32))
    outs = []
    for g in range(gl):
        def gate_row(br):
            r0 = (g * 4 + br) * NSA_HPG
            return jnp.concatenate([gates[r0 + h:r0 + h + 1, :] for h in range(NSA_HPG)], axis=1)
        o = gate_row(0) * o_c[g] + gate_row(1) * o_s[g] + gate_row(2) * o_w[g]
        outs.extend(o[:, h * qt:(h + 1) * qt] for h in range(NSA_HPG))
    o_ref[0] = jnp.transpose(jnp.concatenate(outs, axis=0)).astype(o_ref.dtype)


def _nsa_attend_pairs(tr, k_cmp, v_cmp_t, k12, *, b, s):
    qt, dh = NSA_QT, NSA_DH
    gl = 2
    gd = NSA_GROUPS * dh
    hq = NSA_GROUPS * NSA_HPG * dh
    pairs = NSA_GROUPS // gl
    nslab = s // qt
    assert nslab > NSA_WINDOW // qt
    nb = k_cmp.shape[1]
    tr4 = tr.reshape(b, nslab, tr.shape[1], qt)
    q_rows = gl * NSA_HPG * dh
    v_rows = gl * dh
    gate_rows = gl * 4 * NSA_HPG
    v1_blk = hq // v_rows
    v2_blk = (hq + gd) // v_rows
    gate_blk = (hq + 2 * gd) // gate_rows
    nsb = s // NSA_SEL_LEN
    return pl.pallas_call(
        functools.partial(_nsa_pair_body, s=s),
        grid=(b, pairs, nslab),
        in_specs=[
            pl.BlockSpec((1, 1, q_rows, qt), lambda i, p, j: (i, j, p, 0)),
            pl.BlockSpec((1, 1, gate_rows, qt), lambda i, p, j: (i, j, gate_blk + p, 0)),
            pl.BlockSpec((1, nb, v_rows), lambda i, p, j: (i, 0, p)),
            pl.BlockSpec((1, v_rows, nb), lambda i, p, j: (i, p, 0)),
            pl.BlockSpec((1, s, v_rows), lambda i, p, j: (i, 0, p)),
            pl.BlockSpec((1, s, v_rows), lambda i, p, j: (i, 0, pairs + p)),
            pl.BlockSpec((1, nslab, v_rows, qt), lambda i, p, j: (i, 0, v1_blk + p, 0)),
            pl.BlockSpec((1, nslab, v_rows, qt), lambda i, p, j: (i, 0, v2_blk + p, 0)),
        ],
        out_specs=pl.BlockSpec((1, qt, q_rows), lambda i, p, j: (i, j, p)),
        out_shape=jax.ShapeDtypeStruct((b, s, hq), BF16),
        scratch_shapes=[pltpu.VMEM((nsb, gl * NSA_HPG * qt), F32), pltpu.VMEM((4, qt, gl * NSA_HPG * qt), F32)],
        compiler_params=_params("parallel", "parallel", "arbitrary"),
        name="nsa_attend",
    )(tr4, tr4, k_cmp, v_cmp_t, k12, k12, tr4, tr4)


def _nsa(h2, g_norm, w_in, cmp_pos, cmp_w1, cmp_w2, w_out, *, b, s):
    g, hpg, dh = NSA_GROUPS, NSA_HPG, NSA_DH
    hq, gd = g * hpg * dh, g * dh
    kv_w = lambda br, kv: w_in[:, hq + (2 * br + kv) * gd:hq + (2 * br + kv + 1) * gd]
    w_q = w_in[:, :hq] * (dh ** -0.5)
    w_g = w_in[:, hq + 6 * gd:].reshape(-1, 3, g, hpg).transpose(2, 1, 3, 0)
    w_g = jnp.pad(w_g, ((0, 0), (0, 1), (0, 0), (0, 0))).reshape(g * 4 * hpg, -1)
    w_t = jnp.concatenate([w_q.T, kv_w(1, 1).T, kv_w(2, 1).T, w_g], axis=0).astype(BF16)
    w_nat = jnp.concatenate([kv_w(0, 0), kv_w(0, 1), kv_w(1, 0), kv_w(2, 0)], axis=1).astype(BF16)
    k0, v0, k12, tr = _nsa_proj(h2, g_norm, w_nat, w_t)
    nb = s // NSA_CMP_STRIDE
    k_cmp = _nsa_compress(k0.reshape(b, nb, NSA_CMP_STRIDE * gd), cmp_pos[0], cmp_w1[0], cmp_w2[0], transposed=False)
    v_cmp_t = _nsa_compress(v0.reshape(b, nb, NSA_CMP_STRIDE * gd), cmp_pos[1], cmp_w1[1], cmp_w2[1], transposed=True)
    y = _nsa_attend_pairs(tr, k_cmp, v_cmp_t, k12.reshape(b, s, 2 * gd), b=b, s=s)
    return y.reshape(b * s, hq), w_out.astype(BF16)


def kernel(x, norm_ffn1, ffn1_w_in, ffn1_w_out, norm_mix, norm_ffn2, ffn2_w_in, ffn2_w_out, norm_final,
           ret_w_in, ret_gn_gain, ret_w_out,
           nsa_w_in, nsa_cmp_pos, nsa_cmp_w1, nsa_cmp_w2, nsa_w_out,
           ssd_w_in, ssd_conv_w, ssd_conv_b, ssd_dt_bias, ssd_a_log, ssd_d, ssd_norm, ssd_w_out,
           dil_w_in, dil_w_out):
    b, s, d = x.shape
    depth = norm_mix.shape[0]
    h = x.reshape(b * s, d)
    for i in range(depth):
        h = _ffn(h, norm_ffn1[i], ffn1_w_in[i].astype(BF16), ffn1_w_out[i].astype(BF16))
        m, j = i % 4, i // 4
        mix = None
        if m == 0:
            mix = _retention(h, norm_mix[i], ret_w_in[j], ret_gn_gain[j], ret_w_out[j], b=b, s=s)
        elif m == 1:
            mix = _nsa(h, norm_mix[i], nsa_w_in[j], nsa_cmp_pos[j], nsa_cmp_w1[j], nsa_cmp_w2[j], nsa_w_out[j], b=b, s=s)
        elif m == 2:
            mix = _ssd(h, norm_mix[i], ssd_w_in[j], ssd_conv_w[j], ssd_conv_b[j], ssd_dt_bias[j], ssd_a_log[j],
                       ssd_d[j], ssd_norm[j], ssd_w_out[j], b=b, s=s)
        else:
            h = _dilated(h, norm_mix[i], dil_w_in[j], dil_w_out[j], b=b, s=s)
        h = _ffn(h, norm_ffn2[i], ffn2_w_in[i].astype(BF16), ffn2_w_out[i].astype(BF16),
                 norm_final if i == depth - 1 else None, mix)
    return h.reshape(b, s, d)
```

```python
import functools
import math

import jax
import jax.numpy as jnp
from jax import lax
from jax.experimental import pallas as pl
from jax.experimental.pallas import tpu as pltpu

F32 = jnp.float32
BF16 = jnp.bfloat16
NORM_EPS = 1e-6
NEG_INF = -1e30
ROPE_BASE = 10000.0
VMEM_LIMIT_BYTES = 56 * 1024 * 1024
LANES = 128
MXU_WIDTH = 256
TOKEN_TILE = 512
MAX_DOT_COLS = 2048

RET_HEADS = 4
RET_CHUNK = 128

NSA_GROUPS = 4
NSA_HPG = 4
NSA_DH = 64
NSA_CMP_LEN = 32
NSA_CMP_STRIDE = 16
NSA_SEL_LEN = 64
NSA_N_SEL = 16
NSA_WINDOW = 512
NSA_FORCE_BONUS = 1e4
NSA_QT = 256
NSA_KC = 512

SSD_GROUPS = 4
SSD_HPG = 8
SSD_HEADDIM = 64
SSD_STATE = 128
SSD_CONV = 4
SSD_CHUNK = 256

DIL_PATTERN = ((128, 1), (512, 4), (2048, 16))
DIL_HEADS = 8
DIL_DH = 128
DIL_QB = 128


def _params(*sem):
    return pltpu.CompilerParams(dimension_semantics=sem, vmem_limit_bytes=VMEM_LIMIT_BYTES)


def _rms(x, g):
    return x * lax.rsqrt(jnp.mean(x * x, axis=-1, keepdims=True) + NORM_EPS) * g


def _silu(x):
    return x * jax.nn.sigmoid(x)


def _dot(a, b):
    return jnp.dot(a, b, preferred_element_type=F32)


def _dot_nt(a, b):
    return lax.dot_general(a, b, (((1,), (1,)), ((), ())), preferred_element_type=F32)


def _split3(x):
    hi = x.astype(BF16)
    r1 = x - hi.astype(F32)
    mid = r1.astype(BF16)
    lo = (r1 - mid.astype(F32)).astype(BF16)
    return hi, mid, lo


def _dot3(terms, rhs):
    return _dot(terms[0], rhs) + _dot(terms[1], rhs) + _dot(terms[2], rhs)


def _col_slices(n):
    if n <= MAX_DOT_COLS:
        return [slice(0, n)]
    tiles = -(-n // MXU_WIDTH)
    parts = -(-tiles * MXU_WIDTH // MAX_DOT_COLS)
    bounds = [min(n, (tiles * p // parts) * MXU_WIDTH) for p in range(parts + 1)]
    return [slice(lo, hi) for lo, hi in zip(bounds[:-1], bounds[1:])]


def _resident(shape, index=None):
    index = index if index is not None else (0,) * len(shape)
    return pl.BlockSpec(shape, lambda *_: index, pipeline_mode=pl.Buffered(1))


def _ffn_body(h_ref, *refs, final_norm, mix):
    x = h_ref[...]
    if mix:
        y_ref, wm_ref, *refs = refs
        x = x + _dot(y_ref[...], wm_ref[...])
    g_ref, wa_ref, wb_ref, wo_ref, gf_ref, o_ref = refs
    xn = _rms(x, g_ref[...]).astype(BF16)
    acc = jnp.zeros(x.shape, F32)
    for sl in _col_slices(wa_ref.shape[1]):
        a = _dot(xn, wa_ref[:, sl])
        b = _dot(xn, wb_ref[:, sl])
        acc = acc + _dot((_silu(a) * b).astype(BF16), wo_ref[sl, :])
    y = x + 0.5 * acc
    if final_norm:
        y = _rms(y, gf_ref[...])
    o_ref[...] = y


def _ffn(h, g, w_in, w_out, g_final=None, mix=None, *, tm=TOKEN_TILE):
    m, d = h.shape
    f = w_out.shape[0]
    final_norm = g_final is not None
    gf = g_final if final_norm else g
    mix_args = list(mix) if mix is not None else []
    mix_specs = [pl.BlockSpec((tm, mix[0].shape[1]), lambda i: (i, 0)), _resident(mix[1].shape)] if mix_args else []
    return pl.pallas_call(
        functools.partial(_ffn_body, final_norm=final_norm, mix=bool(mix_args)),
        grid=(m // tm,),
        in_specs=[pl.BlockSpec((tm, d), lambda i: (i, 0))] + mix_specs + [
            _resident((1, d)),
            _resident((d, f)),
            _resident((d, f), (0, 1)),
            _resident((f, d)),
            _resident((1, d)),
        ],
        out_specs=pl.BlockSpec((tm, d), lambda i: (i, 0)),
        out_shape=jax.ShapeDtypeStruct((m, d), F32),
        compiler_params=_params("parallel"),
        name="ffn",
    )(h, *mix_args, g.reshape(1, d), w_in, w_in, w_out, gf.reshape(1, d))


def _norm_proj_body(h_ref, g_ref, *refs, n_out):
    xn = _rms(h_ref[...], g_ref[...]).astype(BF16)
    for w_ref, o_ref in zip(refs[:n_out], refs[n_out:]):
        for sl in _col_slices(w_ref.shape[1]):
            o_ref[:, sl] = _dot(xn, w_ref[:, sl]).astype(o_ref.dtype)


def _norm_proj(h, g, ws, out_dtypes, *, tm=TOKEN_TILE):
    m, d = h.shape
    return pl.pallas_call(
        functools.partial(_norm_proj_body, n_out=len(ws)),
        grid=(m // tm,),
        in_specs=[pl.BlockSpec((tm, d), lambda i: (i, 0)), pl.BlockSpec((1, d), lambda i: (0, 0))]
        + [pl.BlockSpec(w.shape, lambda i: (0, 0)) for w in ws],
        out_specs=[pl.BlockSpec((tm, w.shape[1]), lambda i: (i, 0)) for w in ws],
        out_shape=[jax.ShapeDtypeStruct((m, w.shape[1]), dt) for w, dt in zip(ws, out_dtypes)],
        compiler_params=_params("parallel"),
        name="norm_proj",
    )(h, g.reshape(1, d), *ws)


def _ret_body(q_ref, k_ref, v_ref, g_ref, cos_ref, sin_ref, gn_ref, o_ref, state_ref, *, ts):
    c_len = RET_CHUNK
    dk = q_ref.shape[2] // RET_HEADS
    dv = v_ref.shape[2] // RET_HEADS
    half = dk // 2

    @pl.when(pl.program_id(1) == 0)
    def _():
        state_ref[...] = jnp.zeros(state_ref.shape, F32)

    ii = lax.broadcasted_iota(jnp.int32, (c_len, c_len), 0)
    jj = lax.broadcasted_iota(jnp.int32, (c_len, c_len), 1)
    rel = (ii - jj).astype(F32)
    causal = ii >= jj
    idx = lax.broadcasted_iota(jnp.int32, (c_len, 1), 0).astype(F32)

    def rot(t, cos, sin):
        t1, t2 = t[:, :half], t[:, half:]
        return jnp.concatenate([t1 * cos - t2 * sin, t1 * sin + t2 * cos], axis=1)

    decays = []
    for h in range(RET_HEADS):
        log_gamma = math.log1p(-(2.0 ** (-5.0 - h)))
        decays.append((jnp.where(causal, jnp.exp(jnp.where(causal, rel, 0.0) * log_gamma), 0.0),
                       jnp.exp((idx + 1.0) * log_gamma),
                       jnp.exp((c_len - 1.0 - idx) * log_gamma),
                       math.exp(c_len * log_gamma)))

    def chunk(c, carry):
        r0 = pl.multiple_of(c * c_len, c_len)
        rows = pl.ds(r0, c_len)
        cos = cos_ref[rows, :]
        sin = sin_ref[rows, :]
        for h in range(RET_HEADS):
            inner, q_decay, k_decay, chunk_decay = decays[h]
            q = rot(q_ref[0, rows, h * dk:(h + 1) * dk].astype(F32), cos, sin)
            k = rot(k_ref[0, rows, h * dk:(h + 1) * dk].astype(F32), cos, sin) * (dk ** -0.5)
            v = v_ref[0, rows, h * dv:(h + 1) * dv]
            qb = q.astype(BF16)
            sc = _dot_nt(qb, k.astype(BF16)) * inner
            st = state_ref[h]
            o = _dot(sc.astype(BF16), v) + _dot(qb, st.astype(BF16)) * q_decay
            kd_t = jnp.transpose(k * k_decay).astype(BF16)
            state_ref[h] = st * chunk_decay + _dot(kd_t, v)
            gn = gn_ref[:, h * dv:(h + 1) * dv]
            on = _rms(o, gn)
            gate = g_ref[0, rows, h * dv:(h + 1) * dv].astype(F32)
            o_ref[0, rows, h * dv:(h + 1) * dv] = (_silu(gate) * on).astype(o_ref.dtype)
        return carry

    lax.fori_loop(0, ts // c_len, chunk, 0)


def _retention_core(proj, cos, sin, gn_gain, *, b, s, ts=TOKEN_TILE):
    n = proj.shape[2]
    hk = n // 6
    hv = 2 * hk
    dk = hk // RET_HEADS
    return pl.pallas_call(
        functools.partial(_ret_body, ts=ts),
        grid=(b, s // ts),
        in_specs=[
            pl.BlockSpec((1, ts, hk), lambda i, j: (i, j, 0)),
            pl.BlockSpec((1, ts, hk), lambda i, j: (i, j, 1)),
            pl.BlockSpec((1, ts, hv), lambda i, j: (i, j, 1)),
            pl.BlockSpec((1, ts, hv), lambda i, j: (i, j, 2)),
            pl.BlockSpec((ts, dk // 2), lambda i, j: (j, 0)),
            pl.BlockSpec((ts, dk // 2), lambda i, j: (j, 0)),
            pl.BlockSpec((1, hv), lambda i, j: (0, 0)),
        ],
        out_specs=pl.BlockSpec((1, ts, hv), lambda i, j: (i, j, 0)),
        out_shape=jax.ShapeDtypeStruct((b, s, hv), BF16),
        scratch_shapes=[pltpu.VMEM((RET_HEADS, dk, hv // RET_HEADS), F32)],
        compiler_params=_params("parallel", "arbitrary"),
        name="retention",
    )(proj, proj, proj, proj, cos, sin, gn_gain.reshape(1, hv))


def _rope_tables(s, half):
    inv = ROPE_BASE ** (-jnp.arange(half, dtype=F32) / half)
    ang = jnp.arange(s, dtype=F32)[:, None] * inv[None, :]
    return jnp.cos(ang), jnp.sin(ang)


def _retention(h2, g_norm, w_in, gn_gain, w_out, *, b, s):
    (proj,) = _norm_proj(h2, g_norm, [w_in.astype(BF16)], [BF16])
    n = proj.shape[1]
    cos, sin = _rope_tables(s, n // 6 // RET_HEADS // 2)
    y = _retention_core(proj.reshape(b, s, n), cos, sin, gn_gain, b=b, s=s)
    return y.reshape(b * s, -1), w_out.astype(BF16)


def _dil_proj_body(h_ref, g_ref, w_ref, o_ref, xn_ref, *, r):
    tm = h_ref.shape[0]
    n = tm // r
    xn = _rms(h_ref[...], g_ref[...])
    slabs = xn_ref.shape[0]
    for k in range(slabs):
        xn_ref[k] = xn[:, k * LANES:(k + 1) * LANES]
    xp = jnp.concatenate(
        [jnp.concatenate([xn_ref[k, pl.ds(c, n, stride=r), :] for c in range(r)], axis=0) for k in range(slabs)],
        axis=1).astype(BF16)
    for sl in _col_slices(w_ref.shape[1]):
        res = _dot(xp, w_ref[:, sl]).astype(o_ref.dtype)
        for c in range(r):
            o_ref[0, c, :, sl] = res[c * n:(c + 1) * n, :]


def _dil_proj(h2, g_norm, w, r, *, b, s, tm=TOKEN_TILE):
    d = h2.shape[1]
    n_out = w.shape[1]
    tiles = s // tm
    return pl.pallas_call(
        functools.partial(_dil_proj_body, r=r),
        grid=(b * tiles,),
        in_specs=[
            pl.BlockSpec((tm, d), lambda i: (i, 0)),
            pl.BlockSpec((1, d), lambda i: (0, 0)),
            pl.BlockSpec((d, n_out), lambda i: (0, 0)),
        ],
        out_specs=pl.BlockSpec((1, r, tm // r, n_out), lambda i: (i // tiles, 0, i % tiles, 0)),
        out_shape=jax.ShapeDtypeStruct((b, r, s // r, n_out), BF16),
        scratch_shapes=[pltpu.VMEM((d // LANES, tm, LANES), F32)],
        compiler_params=_params("parallel"),
        name=f"dilated_proj_r{r}",
    )(h2, g_norm.reshape(1, d), w)


def _dil_body(q_ref, kc_ref, kp_ref, vc_ref, vp_ref, o_ref, lse_ref, kbuf, vbuf, *, rows, n_back):
    qb = DIL_QB
    dh = DIL_DH
    first_step = pl.program_id(2) == 0
    kbuf[0:qb, :] = kp_ref[0, 0]
    kbuf[qb:, :] = kc_ref[0, 0]
    vbuf[0:qb, :] = vp_ref[0, 0]
    vbuf[qb:, :] = vc_ref[0, 0]
    qi = lax.broadcasted_iota(jnp.int32, (qb, 2 * qb), 0)
    kj = lax.broadcasted_iota(jnp.int32, (qb, 2 * qb), 1)
    dist = qi + qb - kj
    band = (dist >= 0) & (dist <= n_back)
    lane = lax.broadcasted_iota(jnp.int32, (qb, LANES), 1)
    scale = dh ** -0.5
    ones = jnp.ones((2 * qb, dh), BF16)
    for i in range(rows // qb):
        mask = band & ((kj >= qb) | jnp.logical_not(first_step)) if i == 0 else band
        lse_tile = jnp.zeros((qb, LANES), F32)
        for h in range(DIL_HEADS):
            cols = slice(h * dh, (h + 1) * dh)
            q = q_ref[0, 0, i * qb:(i + 1) * qb, cols]
            k = kbuf[i * qb:(i + 2) * qb, cols]
            v = vbuf[i * qb:(i + 2) * qb, cols]
            s = jnp.where(mask, _dot_nt(q, k), NEG_INF)
            m = jnp.max(s, axis=1, keepdims=True)
            e = jnp.exp2((s - m) * (scale * math.log2(math.e))).astype(BF16)
            acc = _dot(e, jnp.concatenate([v, ones], axis=1))
            den = jnp.maximum(acc[:, dh:], 1e-30)
            o_ref[0, 0, i * qb:(i + 1) * qb, cols] = (acc[:, :dh] / den).astype(o_ref.dtype)
            lse_tile = jnp.where(lane == h, m * scale + jnp.log(den), lse_tile)
        lse_ref[0, 0, i * qb:(i + 1) * qb, :] = lse_tile


def _dilated_group(proj, win, r, *, b, s):
    hd = DIL_HEADS * DIL_DH
    length = s // r
    rows = min(length, TOKEN_TILE)
    qb = DIL_QB

    def cur(which):
        return pl.BlockSpec((1, 1, rows, hd), lambda i, c, n: (i, c, n, which))

    def prev(which):
        return pl.BlockSpec((1, 1, qb, hd), lambda i, c, n: (i, c, jnp.maximum(n * (rows // qb) - 1, 0), which))

    return pl.pallas_call(
        functools.partial(_dil_body, rows=rows, n_back=win // r),
        grid=(b, r, length // rows),
        in_specs=[cur(0), cur(1), prev(1), cur(2), prev(2)],
        out_specs=[
            pl.BlockSpec((1, 1, rows, hd), lambda i, c, n: (i, c, n, 0)),
            pl.BlockSpec((1, 1, rows, LANES), lambda i, c, n: (i, c, n, 0)),
        ],
        out_shape=[
            jax.ShapeDtypeStruct((b, r, length, hd), BF16),
            jax.ShapeDtypeStruct((b, r, length, LANES), F32),
        ],
        scratch_shapes=[pltpu.VMEM((rows + qb, hd), BF16), pltpu.VMEM((rows + qb, hd), BF16)],
        compiler_params=_params("parallel", "parallel", "arbitrary"),
        name=f"dilated_r{r}",
    )(proj, proj, proj, proj, proj)


def _dil_merge_body(h_ref, *refs, dilations):
    ng = len(dilations)
    o_refs, l_refs = refs[:ng], refs[ng:2 * ng]
    w_ref, out_ref = refs[2 * ng], refs[2 * ng + 1]
    o_bufs, l_bufs = refs[2 * ng + 2:3 * ng + 2], refs[3 * ng + 2:]
    tm = h_ref.shape[1]
    for o_ref, l_ref, o_buf, l_buf, r in zip(o_refs, l_refs, o_bufs, l_bufs, dilations):
        n = tm // r
        for c in range(r):
            rows = pl.ds(c, n, stride=r)
            l_buf[rows, :] = l_ref[0, c]
            blk = o_ref[0, c].astype(F32)
            for h in range(DIL_HEADS):
                o_buf[h, rows, :] = blk[:, h * DIL_DH:(h + 1) * DIL_DH]
    lses = [l[...] for l in l_bufs]
    mx = functools.reduce(jnp.maximum, lses)
    es = [jnp.exp(l - mx) for l in lses]
    tot = functools.reduce(jnp.add, es)
    wts = [e / tot for e in es]
    dh = DIL_DH
    parts = []
    for h in range(DIL_HEADS):
        acc = jnp.zeros((tm, dh), F32)
        for g in range(ng):
            wg = jnp.broadcast_to(wts[g][:, h:h + 1], (tm, dh))
            acc = acc + wg * o_bufs[g][h]
        parts.append(acc)
    o = jnp.concatenate(parts, axis=1).astype(BF16)
    out_ref[0] = h_ref[0] + _dot(o, w_ref[...])


def _dil_merge(h3, outs, lses, w, dilations, *, tm=TOKEN_TILE):
    b, s, d = h3.shape
    hd = w.shape[0]
    res_major = lambda r, width: pl.BlockSpec((1, r, tm // r, width), lambda i, n: (i, 0, n, 0))
    return pl.pallas_call(
        functools.partial(_dil_merge_body, dilations=tuple(dilations)),
        grid=(b, s // tm),
        in_specs=[pl.BlockSpec((1, tm, d), lambda i, n: (i, n, 0))]
        + [res_major(r, hd) for r in dilations] + [res_major(r, LANES) for r in dilations]
        + [pl.BlockSpec((hd, d), lambda i, n: (0, 0))],
        out_specs=pl.BlockSpec((1, tm, d), lambda i, n: (i, n, 0)),
        out_shape=jax.ShapeDtypeStruct((b, s, d), F32),
        scratch_shapes=[pltpu.VMEM((DIL_HEADS, tm, DIL_DH), F32) for _ in dilations]
        + [pltpu.VMEM((tm, LANES), F32) for _ in dilations],
        compiler_params=_params("parallel", "parallel"),
        name="dilated_merge",
    )(h3, *outs, *lses, w)


def _dilated(h2, g_norm, w_in, w_out, *, b, s):
    d = h2.shape[1]
    gw = 3 * DIL_HEADS * DIL_DH
    wb = w_in.astype(BF16)
    outs, lses, dilations = [], [], []
    for g, (win, r) in enumerate(DIL_PATTERN):
        proj = _dil_proj(h2, g_norm, wb[:, g * gw:(g + 1) * gw], r, b=b, s=s)
        o, lse = _dilated_group(proj, win, r, b=b, s=s)
        outs.append(o)
        lses.append(lse)
        dilations.append(r)
    return _dil_merge(h2.reshape(b, s, d), outs, lses, w_out.astype(BF16), dilations).reshape(b * s, d)


def _softplus(x):
    return jnp.maximum(x, 0.0) + jnp.log1p(jnp.exp(-jnp.abs(x)))


def _ssd_body(z_ref, x_ref, dt_ref, cw_ref, cb_ref, dtb_ref, alog_ref, dsk_ref, ng_ref, o_ref,
              xpad, state_ref):
    ln = SSD_CHUNK
    nst = SSD_STATE
    gw = SSD_HPG * SSD_HEADDIM
    d_inner = SSD_GROUPS * gw
    halo = 8

    @pl.when(pl.program_id(1) == 0)
    def _():
        xpad[0:halo, :] = jnp.zeros((halo, xpad.shape[1]), F32)
        state_ref[...] = jnp.zeros(state_ref.shape, F32)

    x_b = x_ref[0]
    x_f = x_b.astype(F32)
    ii = lax.broadcasted_iota(jnp.int32, (ln, ln), 0)
    jj = lax.broadcasted_iota(jnp.int32, (ln, ln), 1)
    conv = cb_ref[...] + x_f * cw_ref[SSD_CONV - 1:SSD_CONV, :]
    for shift in range(1, SSD_CONV):
        shifted = _dot(jnp.where(ii - jj == shift, 1.0, 0.0).astype(BF16), x_b)
        conv = conv + shifted * cw_ref[SSD_CONV - 1 - shift:SSD_CONV - shift, :]
    xpad[halo:2 * halo, :] = x_f[0:halo, :]
    head = cb_ref[...]
    for k in range(SSD_CONV):
        off = halo - (SSD_CONV - 1) + k
        head = head + xpad[off:off + halo, :] * cw_ref[k:k + 1, :]
    xpad[0:halo, :] = x_f[ln - halo:ln, :]
    xbc = _silu(jnp.concatenate([head, conv[halo:, :]], axis=0))
    xs = xbc[:, :d_inner]
    bm = xbc[:, d_inner:d_inner + SSD_GROUPS * nst]
    cm = xbc[:, d_inner + SSD_GROUPS * nst:]

    dt = _softplus(dt_ref[0] + dtb_ref[...])
    da = dt * (-jnp.exp(alog_ref[...]))
    ii = lax.broadcasted_iota(jnp.int32, (ln, ln), 0)
    jj = lax.broadcasted_iota(jnp.int32, (ln, ln), 1)
    causal = ii >= jj
    tril = jnp.where(causal, 1.0, 0.0).astype(BF16)
    da_terms = _split3(da)
    acs = _dot(tril, da_terms[0]) + _dot(tril, da_terms[1]) + _dot(tril, da_terms[2])
    acs_t = jnp.transpose(acs)

    erow = lax.broadcasted_iota(jnp.int32, (LANES, d_inner), 0)
    ecol = lax.broadcasted_iota(jnp.int32, (LANES, d_inner), 1)
    expand = jnp.where(ecol // SSD_HEADDIM == erow, 1.0, 0.0).astype(BF16)
    acs_e = _dot3(_split3(acs), expand)
    dt_e = _dot3(_split3(dt), expand)
    last = acs_e[ln - 1:ln, :]
    decay_in = jnp.exp(acs_e)
    xs_dt = xs * dt_e
    xs_end = (xs_dt * jnp.exp(last - acs_e)).astype(BF16)
    xs_b = xs_dt.astype(BF16)
    lane = lax.broadcasted_iota(jnp.int32, (ln, LANES), 1)

    y_groups = []
    for g in range(SSD_GROUPS):
        bm_g = bm[:, g * nst:(g + 1) * nst]
        cm_g = cm[:, g * nst:(g + 1) * nst].astype(BF16)
        cb = jnp.where(causal, _dot_nt(cm_g, bm_g.astype(BF16)), 0.0)
        st = state_ref[g]
        gcols = slice(g * gw, (g + 1) * gw)
        y_state = _dot(cm_g, st.astype(BF16)) * decay_in[:, gcols]
        pairs = []
        for p in range(SSD_HPG // 2):
            pair_cols = slice(g * gw + p * LANES, g * gw + (p + 1) * LANES)
            halves = []
            for e in range(2):
                hd = g * SSD_HPG + 2 * p + e
                diff = acs[:, hd:hd + 1] - acs_t[hd:hd + 1, :]
                w = (cb * jnp.exp(jnp.minimum(diff, 0.0))).astype(BF16)
                halves.append(_dot(w, xs_b[:, pair_cols]))
            pairs.append(jnp.where(lane < SSD_HEADDIM, halves[0], halves[1]))
        y_groups.append(jnp.concatenate(pairs, axis=1) + y_state)
        bm_t = jnp.transpose(bm_g).astype(BF16)
        state_ref[g] = st * jnp.exp(last[:, gcols]) + _dot(bm_t, xs_end[:, gcols])

    y = jnp.concatenate(y_groups, axis=1) + dsk_ref[...] * xs
    yz = y * _silu(z_ref[0].astype(F32))
    outs = [_rms(yz[:, g * gw:(g + 1) * gw], ng_ref[:, g * gw:(g + 1) * gw]) for g in range(SSD_GROUPS)]
    o_ref[0] = jnp.concatenate(outs, axis=1).astype(o_ref.dtype)


def _ssd_core(z, xbc, dt, conv_w, conv_b, dt_bias, a_log, d_skip, norm_g, *, b, s):
    ln = SSD_CHUNK
    d_inner = z.shape[2]
    conv_dim = xbc.shape[2]
    gw = SSD_HPG * SSD_HEADDIM
    full = lambda shape: pl.BlockSpec(shape, lambda i, j: (0,) * len(shape))
    return pl.pallas_call(
        _ssd_body,
        grid=(b, s // ln),
        in_specs=[
            pl.BlockSpec((1, ln, d_inner), lambda i, j: (i, j, 0)),
            pl.BlockSpec((1, ln, conv_dim), lambda i, j: (i, j, 0)),
            pl.BlockSpec((1, ln, LANES), lambda i, j: (i, j, 0)),
            full((SSD_CONV, conv_dim)),
            full((1, conv_dim)),
            full((1, LANES)),
            full((1, LANES)),
            full((1, d_inner)),
            full((1, d_inner)),
        ],
        out_specs=pl.BlockSpec((1, ln, d_inner), lambda i, j: (i, j, 0)),
        out_shape=jax.ShapeDtypeStruct((b, s, d_inner), BF16),
        scratch_shapes=[pltpu.VMEM((16, conv_dim), F32),
                        pltpu.VMEM((SSD_GROUPS, SSD_STATE, gw), F32)],
        compiler_params=_params("parallel", "arbitrary"),
        name="ssd",
    )(z, xbc, dt, conv_w, conv_b, dt_bias, a_log, d_skip, norm_g)


def _ssd(h2, g_norm, w_in, conv_w, conv_b, dt_bias, a_log, d_skip, norm_g, w_out, *, b, s):
    heads = SSD_GROUPS * SSD_HPG
    d_inner = heads * SSD_HEADDIM
    conv_dim = conv_w.shape[1]
    wb = w_in.astype(BF16)
    w_dt = jnp.pad(wb[:, d_inner + conv_dim:], ((0, 0), (0, LANES - heads)))
    z, xbc, dt = _norm_proj(h2, g_norm, [wb[:, :d_inner], wb[:, d_inner:d_inner + conv_dim], w_dt], [BF16, BF16, F32])
    pad_heads = lambda v: jnp.pad(v.astype(F32), (0, LANES - heads)).reshape(1, LANES)
    y = _ssd_core(z.reshape(b, s, -1), xbc.reshape(b, s, -1), dt.reshape(b, s, LANES),
                  conv_w.astype(F32), conv_b.reshape(1, -1).astype(F32), pad_heads(dt_bias), pad_heads(a_log),
                  jnp.repeat(d_skip.astype(F32), SSD_HEADDIM).reshape(1, d_inner), norm_g.reshape(1, d_inner),
                  b=b, s=s)
    return y.reshape(b * s, d_inner), w_out.astype(BF16)


def _nsa_proj_body(h_ref, g_ref, wn_ref, wt_ref, k0_ref, v0_ref, k12_ref, tr_ref):
    xn = _rms(h_ref[...], g_ref[...]).astype(BF16)
    nat = _dot(xn, wn_ref[...])
    gd = k0_ref.shape[1]
    k0_ref[...] = nat[:, :gd].astype(k0_ref.dtype)
    v0_ref[...] = nat[:, gd:2 * gd].astype(v0_ref.dtype)
    k12_ref[...] = nat[:, 2 * gd:].astype(k12_ref.dtype)
    res = _dot_nt(wt_ref[...], xn)
    for j in range(tr_ref.shape[0]):
        tr_ref[j] = res[:, j * NSA_QT:(j + 1) * NSA_QT].astype(tr_ref.dtype)


def _nsa_proj(h2, g_norm, w_nat, w_t, *, tm=TOKEN_TILE):
    m, d = h2.shape
    gd = NSA_GROUPS * NSA_DH
    nt = w_t.shape[0]
    slabs = tm // NSA_QT
    return pl.pallas_call(
        _nsa_proj_body,
        grid=(m // tm,),
        in_specs=[
            pl.BlockSpec((tm, d), lambda i: (i, 0)),
            pl.BlockSpec((1, d), lambda i: (0, 0)),
            pl.BlockSpec(w_nat.shape, lambda i: (0, 0)),
            pl.BlockSpec(w_t.shape, lambda i: (0, 0)),
        ],
        out_specs=[
            pl.BlockSpec((tm, gd), lambda i: (i, 0)),
            pl.BlockSpec((tm, gd), lambda i: (i, 0)),
            pl.BlockSpec((tm, 2 * gd), lambda i: (i, 0)),
            pl.BlockSpec((slabs, nt, NSA_QT), lambda i: (i, 0, 0)),
        ],
        out_shape=[
            jax.ShapeDtypeStruct((m, gd), BF16),
            jax.ShapeDtypeStruct((m, gd), BF16),
            jax.ShapeDtypeStruct((m, 2 * gd), BF16),
            jax.ShapeDtypeStruct((m // NSA_QT, nt, NSA_QT), BF16),
        ],
        compiler_params=_params("parallel"),
        name="nsa_proj",
    )(h2, g_norm.reshape(1, d), w_nat, w_t)


def _nsa_cmp_body(x_ref, pa_ref, pb_ref, w1a_ref, w1b_ref, w2_ref, o_ref, *, transposed):
    x = x_ref[0].astype(F32)
    nb = x.shape[0]
    ya = _dot((x + pa_ref[...]).astype(BF16), w1a_ref[...])
    yb = _dot((x + pb_ref[...]).astype(BF16), w1b_ref[...])
    hid = _silu(ya + pltpu.roll(yb, nb - 1, 0))
    out = _dot(hid.astype(BF16), w2_ref[...])
    o_ref[0] = (jnp.transpose(out) if transposed else out).astype(o_ref.dtype)


def _nsa_compress(x, pos, w1, w2, *, transposed):
    b, nb, width = x.shape
    g, dh = NSA_GROUPS, NSA_DH
    half = NSA_CMP_LEN // 2
    hid = w1.shape[1]
    eye = jnp.eye(g, dtype=F32)
    w1r = w1.reshape(NSA_CMP_LEN, dh, hid)
    big = jnp.einsum("ldj,gh->lgdhj", w1r, eye).reshape(NSA_CMP_LEN, g * dh, g * hid)
    w1a = big[:half].reshape(half * g * dh, g * hid).astype(BF16)
    w1b = big[half:].reshape(half * g * dh, g * hid).astype(BF16)
    w2big = jnp.einsum("jd,gh->gjhd", w2, eye).reshape(g * hid, g * dh).astype(BF16)
    posb = jnp.broadcast_to(pos[:, None, :], (NSA_CMP_LEN, g, dh)).reshape(NSA_CMP_LEN, g * dh)
    pa = posb[:half].reshape(1, width).astype(F32)
    pb = posb[half:].reshape(1, width).astype(F32)
    full = lambda shape: pl.BlockSpec(shape, lambda i: (0,) * len(shape))
    out_dims = (g * dh, nb) if transposed else (nb, g * dh)
    return pl.pallas_call(
        functools.partial(_nsa_cmp_body, transposed=transposed),
        grid=(b,),
        in_specs=[
            pl.BlockSpec((1, nb, width), lambda i: (i, 0, 0)),
            full((1, width)), full((1, width)),
            full((width, g * hid)), full((width, g * hid)), full((g * hid, g * dh)),
        ],
        out_specs=pl.BlockSpec((1,) + out_dims, lambda i: (i, 0, 0)),
        out_shape=jax.ShapeDtypeStruct((b,) + out_dims, BF16),
        compiler_params=_params("parallel"),
        name="nsa_compress",
    )(x, pa, pb, w1a, w1b, w2big)


def _nsa_pair_body(q_ref, gt_ref, kc_ref, vct_ref, k1_ref, k2_ref, v1_ref, v2_ref, o_ref, bias_ref, wbias_ref, *, s):
    qt, kc, dh = NSA_QT, NSA_KC, NSA_DH
    gl = 2
    glanes = NSA_HPG * qt
    lanes = gl * glanes
    vrows = gl * dh
    j = pl.program_id(2)
    t0 = j * qt
    nb = kc_ref.shape[1]
    nsb = s // NSA_SEL_LEN
    n_sel = min(NSA_N_SEL, nsb)

    q_t = q_ref[0, 0]
    qcat = [jnp.concatenate([q_t[(g * NSA_HPG + h) * dh:(g * NSA_HPG + h + 1) * dh, :] for h in range(NSA_HPG)],
                            axis=1) for g in range(gl)]
    zero = jnp.zeros_like(qcat[0])
    qp = jnp.concatenate([jnp.concatenate([qcat[0], zero], axis=1),
                          jnp.concatenate([zero, qcat[1]], axis=1)], axis=0)
    qlane = lax.broadcasted_iota(jnp.int32, (1, lanes), 1) % qt
    tq = t0 + qlane

    def values(v_ref, slab0, n_slabs):
        v = jnp.concatenate([v_ref[0, slab0 + i] for i in range(n_slabs)], axis=1)
        return jnp.concatenate([v, jnp.ones((16, n_slabs * qt), BF16)], axis=0)

    def weighted_values(v_aug, p):
        return jnp.concatenate(
            [_dot(jnp.concatenate([v_aug[g * dh:(g + 1) * dh, :], v_aug[vrows:, :]], axis=0),
                  p[:, g * glanes:(g + 1) * glanes]) for g in range(gl)], axis=1)

    def normalise(acc):
        o = acc[:dh, :] / jnp.maximum(acc[dh:dh + 1, :], 1e-30)
        return [o[:, g * glanes:(g + 1) * glanes] for g in range(gl)]

    wslabs = NSA_WINDOW // qt + 1
    row_w = lax.broadcasted_iota(jnp.int32, (qt, lanes), 0)

    @pl.when(j == 0)
    def _():
        wbias_ref[0] = jnp.where(row_w > qlane, 0.0, NEG_INF)
        wbias_ref[1] = jnp.zeros((qt, lanes), F32)
        wbias_ref[2] = jnp.where(row_w <= qlane, 0.0, NEG_INF)
        wbias_ref[3] = jnp.full((qt, lanes), NEG_INF, F32)

    jb0 = jnp.maximum(j + 1 - wslabs, 0)
    kw = k2_ref[0, pl.ds(pl.multiple_of(jb0 * qt, qt), wslabs * qt), :]
    sw_raw = _dot(kw, qp)
    slabs_back = [j - jb0 - i for i in range(wslabs)]
    sw = jnp.concatenate(
        [sw_raw[i * qt:(i + 1) * qt, :]
         + wbias_ref[jnp.where(back < 0, 3, jnp.where(back == wslabs - 1, 0, jnp.where(back == 0, 2, 1)))]
         for i, back in enumerate(slabs_back)], axis=0)

    sc = _dot(kc_ref[0], qp)
    n_idx = lax.broadcasted_iota(jnp.int32, (nb, lanes), 0)
    cmask = (n_idx * NSA_CMP_STRIDE + NSA_CMP_LEN - 1 <= tq) & (n_idx < nb - 1)
    sc = jnp.where(cmask, sc, NEG_INF)
    e = jnp.where(cmask, jnp.exp(sc - jnp.max(sc, axis=0, keepdims=True)), 0.0)
    p_c = e / jnp.maximum(jnp.sum(e, axis=0, keepdims=True), 1e-30)
    acc_c = _dot(vct_ref[0], p_c.astype(BF16))
    o_c = [acc_c[g * dh:(g + 1) * dh, g * glanes:(g + 1) * glanes] for g in range(gl)]
    imps = []
    for g in range(gl):
        imp = p_c[:, g * glanes:g * glanes + qt]
        for h in range(1, NSA_HPG):
            imp = imp + p_c[:, g * glanes + h * qt:g * glanes + (h + 1) * qt]
        imps.append(imp)
    imp = jnp.concatenate(imps, axis=1)

    ratio = NSA_SEL_LEN // NSA_CMP_STRIDE
    dd = lax.broadcasted_iota(jnp.int32, (nsb, nb), 1) - ratio * lax.broadcasted_iota(jnp.int32, (nsb, nb), 0)
    wsel = jnp.where((dd == -1) | (dd == ratio - 1), 1.0,
                     jnp.where((dd >= 0) & (dd < ratio - 1), 2.0, 0.0)).astype(BF16)
    terms = _split3(imp)
    imp_sel = _dot(wsel, terms[0]) + _dot(wsel, terms[1]) + _dot(wsel, terms[2])
    jq = lax.broadcasted_iota(jnp.int32, (nsb, gl * qt), 0)
    cur = (t0 + lax.broadcasted_iota(jnp.int32, (nsb, gl * qt), 1) % qt) // NSA_SEL_LEN
    forced = (jq == 0) | (jq == cur) | (jq == cur - 1)
    score = jnp.where(jq <= cur, imp_sel + jnp.where(forced, NSA_FORCE_BONUS, 0.0), NEG_INF)

    pw = jnp.exp(sw - jnp.max(sw, axis=0, keepdims=True)).astype(BF16)
    o_w = normalise(weighted_values(values(v2_ref, jb0, wslabs), pw))

    jqf = jq.astype(F32)
    work = score
    sel_bias = jnp.full((nsb, gl * qt), NEG_INF, F32)
    for _ in range(n_sel):
        top = jnp.max(work, axis=0, keepdims=True)
        first = jnp.min(jnp.where(work == top, jqf, float(nsb)), axis=0, keepdims=True)
        pick = jqf == first
        sel_bias = jnp.where(pick & (work > 0.5 * NEG_INF), 0.0, sel_bias)
        work = jnp.where(pick, 3.0 * NEG_INF, work)
    bias_ref[...] = jnp.concatenate([sel_bias[:, g * qt:(g + 1) * qt] for g in range(gl) for _ in range(NSA_HPG)],
                                    axis=1)

    blocks_per_chunk = kc // NSA_SEL_LEN
    slabs_per_chunk = kc // qt

    def online(carry, st, bias, v_aug):
        m, acc = carry
        blks = [st[r * NSA_SEL_LEN:(r + 1) * NSA_SEL_LEN, :] for r in range(blocks_per_chunk)]
        part = None
        for r in range(blocks_per_chunk):
            pr = blks[r][0:8, :]
            for i in range(1, NSA_SEL_LEN // 8):
                pr = jnp.maximum(pr, blks[r][8 * i:8 * (i + 1), :])
            pr = pr + bias[r:r + 1, :]
            part = pr if part is None else jnp.maximum(part, pr)
        m_new = jnp.maximum(m, jnp.max(part, axis=0, keepdims=True))
        p = jnp.concatenate([jnp.exp(blks[r] - (m_new - bias[r:r + 1, :])) for r in range(blocks_per_chunk)],
                            axis=0).astype(BF16)
        return m_new, jnp.exp(m - m_new) * acc + weighted_values(v_aug, p)

    def sel_scores(c):
        return _dot(k1_ref[0, pl.ds(pl.multiple_of(c * kc, kc), kc), :], qp)

    def sel_bias(c):
        return bias_ref[pl.ds(pl.multiple_of(c * blocks_per_chunk, blocks_per_chunk), blocks_per_chunk), :]

    def sel_values(c):
        return values(v1_ref, c * slabs_per_chunk, slabs_per_chunk)

    init = (jnp.full((1, lanes), NEG_INF, F32), jnp.zeros((dh + 16, lanes), F32))
    c_hi = t0 // kc
    row_k = lax.broadcasted_iota(jnp.int32, (kc, lanes), 0)
    st = jnp.where(row_k <= (t0 - c_hi * kc) + qlane, sel_scores(c_hi), NEG_INF)
    chain_a = online(init, st, sel_bias(c_hi), sel_values(c_hi))
    chain_a = lax.cond(c_hi % 2 == 1,
                       lambda cr: online(cr, sel_scores(c_hi - 1), sel_bias(c_hi - 1), sel_values(c_hi - 1)),
                       lambda cr: cr, chain_a)
    chain_b = (chain_a[0], jnp.zeros_like(chain_a[1]))

    def pair(i, chains):
        ca, cb = chains
        st_a = sel_scores(2 * i)
        st_b = sel_scores(2 * i + 1)
        return (online(ca, st_a, sel_bias(2 * i), sel_values(2 * i)),
                online(cb, st_b, sel_bias(2 * i + 1), sel_values(2 * i + 1)))

    (m_a, acc_a), (m_b, acc_b) = lax.fori_loop(0, c_hi // 2, pair, (chain_a, chain_b))
    m_s = jnp.maximum(m_a, m_b)
    o_s = normalise(jnp.exp(m_a - m_s) * acc_a + jnp.exp(m_b - m_s) * acc_b)

    gates = jax.nn.sigmoid(gt_ref[0, 0].astype(F32))
    outs = []
    for g in range(gl):
        def gate_row(br):
            r0 = (g * 4 + br) * NSA_HPG
            return jnp.concatenate([gates[r0 + h:r0 + h + 1, :] for h in range(NSA_HPG)], axis=1)
        o = gate_row(0) * o_c[g] + gate_row(1) * o_s[g] + gate_row(2) * o_w[g]
        outs.extend(o[:, h * qt:(h + 1) * qt] for h in range(NSA_HPG))
    o_ref[0] = jnp.transpose(jnp.concatenate(outs, axis=0)).astype(o_ref.dtype)


def _nsa_attend_pairs(tr, k_cmp, v_cmp_t, k12, *, b, s):
    qt, dh = NSA_QT, NSA_DH
    gl = 2
    gd = NSA_GROUPS * dh
    hq = NSA_GROUPS * NSA_HPG * dh
    pairs = NSA_GROUPS // gl
    nslab = s // qt
    assert nslab > NSA_WINDOW // qt
    nb = k_cmp.shape[1]
    tr4 = tr.reshape(b, nslab, tr.shape[1], qt)
    q_rows = gl * NSA_HPG * dh
    v_rows = gl * dh
    gate_rows = gl * 4 * NSA_HPG
    v1_blk = hq // v_rows
    v2_blk = (hq + gd) // v_rows
    gate_blk = (hq + 2 * gd) // gate_rows
    nsb = s // NSA_SEL_LEN
    return pl.pallas_call(
        functools.partial(_nsa_pair_body, s=s),
        grid=(b, pairs, nslab),
        in_specs=[
            pl.BlockSpec((1, 1, q_rows, qt), lambda i, p, j: (i, j, p, 0)),
            pl.BlockSpec((1, 1, gate_rows, qt), lambda i, p, j: (i, j, gate_blk + p, 0)),
            pl.BlockSpec((1, nb, v_rows), lambda i, p, j: (i, 0, p)),
            pl.BlockSpec((1, v_rows, nb), lambda i, p, j: (i, p, 0)),
            pl.BlockSpec((1, s, v_rows), lambda i, p, j: (i, 0, p)),
            pl.BlockSpec((1, s, v_rows), lambda i, p, j: (i, 0, pairs + p)),
            pl.BlockSpec((1, nslab, v_rows, qt), lambda i, p, j: (i, 0, v1_blk + p, 0)),
            pl.BlockSpec((1, nslab, v_rows, qt), lambda i, p, j: (i, 0, v2_blk + p, 0)),
        ],
        out_specs=pl.BlockSpec((1, qt, q_rows), lambda i, p, j: (i, j, p)),
        out_shape=jax.ShapeDtypeStruct((b, s, hq), BF16),
        scratch_shapes=[pltpu.VMEM((nsb, gl * NSA_HPG * qt), F32), pltpu.VMEM((4, qt, gl * NSA_HPG * qt), F32)],
        compiler_params=_params("parallel", "parallel", "arbitrary"),
        name="nsa_attend",
    )(tr4, tr4, k_cmp, v_cmp_t, k12, k12, tr4, tr4)


def _run_interleaved(gens):
    live = list(gens)
    msgs = [next(g) for g in live]
    while live:
        kind = msgs[0][0] if msgs[0] is not None else None
        if kind == "cond":
            fns = [m[2] for m in msgs]
            replies = lax.cond(msgs[0][1], lambda ops: tuple(f(o) for f, o in zip(fns, ops)), lambda ops: ops,
                               tuple(m[3] for m in msgs))
        elif kind == "loop":
            score_fns, update_fns = [m[2] for m in msgs], [m[3] for m in msgs]

            def body(i, carries):
                scores = [f(i) for f in score_fns]
                return tuple(f(i, c, st) for f, c, st in zip(update_fns, carries, scores))

            replies = lax.fori_loop(0, msgs[0][1], body, tuple(m[4] for m in msgs))
        else:
            replies = [None] * len(live)
        nxt_live, nxt_msgs = [], []
        for g, r in zip(live, replies):
            try:
                nxt_msgs.append(g.send(r))
                nxt_live.append(g)
            except StopIteration:
                pass
        live, msgs = nxt_live, nxt_msgs


def _nsa_stream(p, q_ref, gt_ref, kc_ref, vct_ref, k1_ref, k2_ref, v1_ref, v2_ref, o_ref, bias_ref, wbias_ref, *, s):
    qt, kc, dh = NSA_QT, NSA_KC, NSA_DH
    gl = 2
    glanes = NSA_HPG * qt
    lanes = gl * glanes
    vrows = gl * dh
    kcols = slice(p * vrows, (p + 1) * vrows)
    j = pl.program_id(1)
    t0 = j * qt
    nb = kc_ref.shape[1]
    nsb = s // NSA_SEL_LEN
    n_sel = min(NSA_N_SEL, nsb)

    q_t = q_ref[0, 0, p * gl * NSA_HPG * dh:(p + 1) * gl * NSA_HPG * dh, :]
    qcat = [jnp.concatenate([q_t[(g * NSA_HPG + h) * dh:(g * NSA_HPG + h + 1) * dh, :] for h in range(NSA_HPG)],
                            axis=1) for g in range(gl)]
    zero = jnp.zeros_like(qcat[0])
    qp = jnp.concatenate([jnp.concatenate([qcat[0], zero], axis=1),
                          jnp.concatenate([zero, qcat[1]], axis=1)], axis=0)
    qlane = lax.broadcasted_iota(jnp.int32, (1, lanes), 1) % qt
    tq = t0 + qlane

    def values(v_ref, slab0, n_slabs):
        v = jnp.concatenate([v_ref[0, slab0 + i, kcols, :] for i in range(n_slabs)], axis=1)
        return jnp.concatenate([v, jnp.ones((16, n_slabs * qt), BF16)], axis=0)

    def weighted_values(v_aug, pr):
        return jnp.concatenate(
            [_dot(jnp.concatenate([v_aug[g * dh:(g + 1) * dh, :], v_aug[vrows:, :]], axis=0),
                  pr[:, g * glanes:(g + 1) * glanes]) for g in range(gl)], axis=1)

    def normalise(acc):
        o = acc[:dh, :] / jnp.maximum(acc[dh:dh + 1, :], 1e-30)
        return [o[:, g * glanes:(g + 1) * glanes] for g in range(gl)]

    wslabs = NSA_WINDOW // qt + 1
    jb0 = jnp.maximum(j + 1 - wslabs, 0)
    kw = k2_ref[0, pl.ds(pl.multiple_of(jb0 * qt, qt), wslabs * qt), kcols]
    sw_raw = _dot(kw, qp)
    slabs_back = [j - jb0 - i for i in range(wslabs)]
    sw = jnp.concatenate(
        [sw_raw[i * qt:(i + 1) * qt, :]
         + wbias_ref[jnp.where(back < 0, 3, jnp.where(back == wslabs - 1, 0, jnp.where(back == 0, 2, 1)))]
         for i, back in enumerate(slabs_back)], axis=0)
    yield

    sc = _dot(kc_ref[0, :, kcols], qp)
    n_idx = lax.broadcasted_iota(jnp.int32, (nb, lanes), 0)
    cmask = (n_idx * NSA_CMP_STRIDE + NSA_CMP_LEN - 1 <= tq) & (n_idx < nb - 1)
    sc = jnp.where(cmask, sc, NEG_INF)
    e = jnp.where(cmask, jnp.exp(sc - jnp.max(sc, axis=0, keepdims=True)), 0.0)
    p_c = e / jnp.maximum(jnp.sum(e, axis=0, keepdims=True), 1e-30)
    yield
    acc_c = _dot(vct_ref[0, kcols, :], p_c.astype(BF16))
    o_c = [acc_c[g * dh:(g + 1) * dh, g * glanes:(g + 1) * glanes] for g in range(gl)]
    imps = []
    for g in range(gl):
        imp = p_c[:, g * glanes:g * glanes + qt]
        for h in range(1, NSA_HPG):
            imp = imp + p_c[:, g * glanes + h * qt:g * glanes + (h + 1) * qt]
        imps.append(imp)
    imp = jnp.concatenate(imps, axis=1)

    ratio = NSA_SEL_LEN // NSA_CMP_STRIDE
    dd = lax.broadcasted_iota(jnp.int32, (nsb, nb), 1) - ratio * lax.broadcasted_iota(jnp.int32, (nsb, nb), 0)
    wsel = jnp.where((dd == -1) | (dd == ratio - 1), 1.0,
                     jnp.where((dd >= 0) & (dd < ratio - 1), 2.0, 0.0)).astype(BF16)
    terms = _split3(imp)
    imp_sel = _dot(wsel, terms[0]) + _dot(wsel, terms[1]) + _dot(wsel, terms[2])
    jq = lax.broadcasted_iota(jnp.int32, (nsb, gl * qt), 0)
    cur = (t0 + lax.broadcasted_iota(jnp.int32, (nsb, gl * qt), 1) % qt) // NSA_SEL_LEN
    forced = (jq == 0) | (jq == cur) | (jq == cur - 1)
    score = jnp.where(jq <= cur, imp_sel + jnp.where(forced, NSA_FORCE_BONUS, 0.0), NEG_INF)
    yield

    pw = jnp.exp(sw - jnp.max(sw, axis=0, keepdims=True)).astype(BF16)
    yield
    o_w = normalise(weighted_values(values(v2_ref, jb0, wslabs), pw))

    jqf = jq.astype(F32)
    work = score
    sel_bias = jnp.full((nsb, gl * qt), NEG_INF, F32)
    for rnd in range(n_sel):
        top = jnp.max(work, axis=0, keepdims=True)
        first = jnp.min(jnp.where(work == top, jqf, float(nsb)), axis=0, keepdims=True)
        pick = jqf == first
        sel_bias = jnp.where(pick & (work > 0.5 * NEG_INF), 0.0, sel_bias)
        work = jnp.where(pick, 3.0 * NEG_INF, work)
        if rnd % 4 == 3:
            yield
    bias_ref[p] = jnp.concatenate([sel_bias[:, g * qt:(g + 1) * qt] for g in range(gl) for _ in range(NSA_HPG)],
                                  axis=1)

    blocks_per_chunk = kc // NSA_SEL_LEN
    slabs_per_chunk = kc // qt

    def online(carry, st, bias, v_aug):
        m, acc = carry
        blks = [st[r * NSA_SEL_LEN:(r + 1) * NSA_SEL_LEN, :] for r in range(blocks_per_chunk)]
        part = None
        for r in range(blocks_per_chunk):
            pr = blks[r][0:8, :]
            for i in range(1, NSA_SEL_LEN // 8):
                pr = jnp.maximum(pr, blks[r][8 * i:8 * (i + 1), :])
            pr = pr + bias[r:r + 1, :]
            part = pr if part is None else jnp.maximum(part, pr)
        m_new = jnp.maximum(m, jnp.max(part, axis=0, keepdims=True))
        pr = jnp.concatenate([jnp.exp(blks[r] - (m_new - bias[r:r + 1, :])) for r in range(blocks_per_chunk)],
                             axis=0).astype(BF16)
        return m_new, jnp.exp(m - m_new) * acc + weighted_values(v_aug, pr)

    def sel_scores(c):
        return _dot(k1_ref[0, pl.ds(pl.multiple_of(c * kc, kc), kc), kcols], qp)

    def sel_bias_rows(c):
        return bias_ref[p, pl.ds(pl.multiple_of(c * blocks_per_chunk, blocks_per_chunk), blocks_per_chunk), :]

    def sel_values(c):
        return values(v1_ref, c * slabs_per_chunk, slabs_per_chunk)

    init = (jnp.full((1, lanes), NEG_INF, F32), jnp.zeros((dh + 16, lanes), F32))
    c_hi = t0 // kc
    row_k = lax.broadcasted_iota(jnp.int32, (kc, lanes), 0)
    st = jnp.where(row_k <= (t0 - c_hi * kc) + qlane, sel_scores(c_hi), NEG_INF)
    yield
    chain_a = online(init, st, sel_bias_rows(c_hi), sel_values(c_hi))
    yield
    chain_a = yield ("cond", c_hi % 2 == 1,
                     lambda cr: online(cr, sel_scores(c_hi - 1), sel_bias_rows(c_hi - 1), sel_values(c_hi - 1)),
                     chain_a)
    chain_b = (chain_a[0], jnp.zeros_like(chain_a[1]))

    def pair_scores(i):
        return sel_scores(2 * i), sel_scores(2 * i + 1)

    def pair_update(i, chains, sts):
        return (online(chains[0], sts[0], sel_bias_rows(2 * i), sel_values(2 * i)),
                online(chains[1], sts[1], sel_bias_rows(2 * i + 1), sel_values(2 * i + 1)))

    (m_a, acc_a), (m_b, acc_b) = yield ("loop", c_hi // 2, pair_scores, pair_update, (chain_a, chain_b))
    m_s = jnp.maximum(m_a, m_b)
    o_s = normalise(jnp.exp(m_a - m_s) * acc_a + jnp.exp(m_b - m_s) * acc_b)

    grows = gl * 4 * NSA_HPG
    gates = jax.nn.sigmoid(gt_ref[0, 0, p * grows:(p + 1) * grows, :].astype(F32))
    outs = []
    for g in range(gl):
        def gate_row(br):
            r0 = (g * 4 + br) * NSA_HPG
            return jnp.concatenate([gates[r0 + h:r0 + h + 1, :] for h in range(NSA_HPG)], axis=1)
        o = gate_row(0) * o_c[g] + gate_row(1) * o_s[g] + gate_row(2) * o_w[g]
        outs.extend(o[:, h * qt:(h + 1) * qt] for h in range(NSA_HPG))
    ocols = gl * NSA_HPG * dh
    o_ref[0, :, p * ocols:(p + 1) * ocols] = jnp.transpose(jnp.concatenate(outs, axis=0)).astype(o_ref.dtype)


def _nsa_body(q_ref, gt_ref, kc_ref, vct_ref, k1_ref, k2_ref, v1_ref, v2_ref, o_ref, bias_ref, wbias_ref, *, s):
    qt = NSA_QT
    lanes = wbias_ref.shape[2]

    @pl.when(pl.program_id(1) == 0)
    def _():
        row_w = lax.broadcasted_iota(jnp.int32, (qt, lanes), 0)
        qlane = lax.broadcasted_iota(jnp.int32, (1, lanes), 1) % qt
        wbias_ref[0] = jnp.where(row_w > qlane, 0.0, NEG_INF)
        wbias_ref[1] = jnp.zeros((qt, lanes), F32)
        wbias_ref[2] = jnp.where(row_w <= qlane, 0.0, NEG_INF)
        wbias_ref[3] = jnp.full((qt, lanes), NEG_INF, F32)

    refs = (q_ref, gt_ref, kc_ref, vct_ref, k1_ref, k2_ref, v1_ref, v2_ref, o_ref, bias_ref, wbias_ref)
    _run_interleaved([_nsa_stream(p, *refs, s=s) for p in range(NSA_GROUPS // 2)])


def _nsa_attend(tr, k_cmp, v_cmp_t, k12, *, b, s):
    qt, dh = NSA_QT, NSA_DH
    gd = NSA_GROUPS * dh
    hq = NSA_GROUPS * NSA_HPG * dh
    pairs = NSA_GROUPS // 2
    pair_lanes = 2 * NSA_HPG * qt
    nslab = s // qt
    assert nslab > NSA_WINDOW // qt
    nb = k_cmp.shape[1]
    tr4 = tr.reshape(b, nslab, tr.shape[1], qt)
    gate_rows = NSA_GROUPS * 4 * NSA_HPG
    nsb = s // NSA_SEL_LEN
    once = pl.Buffered(1)
    return pl.pallas_call(
        functools.partial(_nsa_body, s=s),
        grid=(b, nslab),
        in_specs=[
            pl.BlockSpec((1, 1, hq, qt), lambda i, j: (i, j, 0, 0)),
            pl.BlockSpec((1, 1, gate_rows, qt), lambda i, j: (i, j, (hq + 2 * gd) // gate_rows, 0)),
            pl.BlockSpec((1, nb, gd), lambda i, j: (i, 0, 0), pipeline_mode=once),
            pl.BlockSpec((1, gd, nb), lambda i, j: (i, 0, 0), pipeline_mode=once),
            pl.BlockSpec((1, s, gd), lambda i, j: (i, 0, 0), pipeline_mode=once),
            pl.BlockSpec((1, s, gd), lambda i, j: (i, 0, 1), pipeline_mode=once),
            pl.BlockSpec((1, nslab, gd, qt), lambda i, j: (i, 0, hq // gd, 0), pipeline_mode=once),
            pl.BlockSpec((1, nslab, gd, qt), lambda i, j: (i, 0, hq // gd + 1, 0), pipeline_mode=once),
        ],
        out_specs=pl.BlockSpec((1, qt, hq), lambda i, j: (i, j, 0)),
        out_shape=jax.ShapeDtypeStruct((b, s, hq), BF16),
        scratch_shapes=[pltpu.VMEM((pairs, nsb, pair_lanes), F32), pltpu.VMEM((4, qt, pair_lanes), F32)],
        compiler_params=_params("parallel", "arbitrary"),
        name="nsa_attend",
    )(tr4, tr4, k_cmp, v_cmp_t, k12, k12, tr4, tr4)


def _nsa(h2, g_norm, w_in, cmp_pos, cmp_w1, cmp_w2, w_out, *, b, s):
    g, hpg, dh = NSA_GROUPS, NSA_HPG, NSA_DH
    hq, gd = g * hpg * dh, g * dh
    kv_w = lambda br, kv: w_in[:, hq + (2 * br + kv) * gd:hq + (2 * br + kv + 1) * gd]
    w_q = w_in[:, :hq] * (dh ** -0.5)
    w_g = w_in[:, hq + 6 * gd:].reshape(-1, 3, g, hpg).transpose(2, 1, 3, 0)
    w_g = jnp.pad(w_g, ((0, 0), (0, 1), (0, 0), (0, 0))).reshape(g * 4 * hpg, -1)
    w_t = jnp.concatenate([w_q.T, kv_w(1, 1).T, kv_w(2, 1).T, w_g], axis=0).astype(BF16)
    w_nat = jnp.concatenate([kv_w(0, 0), kv_w(0, 1), kv_w(1, 0), kv_w(2, 0)], axis=1).astype(BF16)
    k0, v0, k12, tr = _nsa_proj(h2, g_norm, w_nat, w_t)
    nb = s // NSA_CMP_STRIDE
    k_cmp = _nsa_compress(k0.reshape(b, nb, NSA_CMP_STRIDE * gd), cmp_pos[0], cmp_w1[0], cmp_w2[0], transposed=False)
    v_cmp_t = _nsa_compress(v0.reshape(b, nb, NSA_CMP_STRIDE * gd), cmp_pos[1], cmp_w1[1], cmp_w2[1], transposed=True)
    y = _nsa_attend(tr, k_cmp, v_cmp_t, k12.reshape(b, s, 2 * gd), b=b, s=s)
    return y.reshape(b * s, hq), w_out.astype(BF16)


def kernel(x, norm_ffn1, ffn1_w_in, ffn1_w_out, norm_mix, norm_ffn2, ffn2_w_in, ffn2_w_out, norm_final,
           ret_w_in, ret_gn_gain, ret_w_out,
           nsa_w_in, nsa_cmp_pos, nsa_cmp_w1, nsa_cmp_w2, nsa_w_out,
           ssd_w_in, ssd_conv_w, ssd_conv_b, ssd_dt_bias, ssd_a_log, ssd_d, ssd_norm, ssd_w_out,
           dil_w_in, dil_w_out):
    b, s, d = x.shape
    depth = norm_mix.shape[0]
    h = x.reshape(b * s, d)
    for i in range(depth):
        h = _ffn(h, norm_ffn1[i], ffn1_w_in[i].astype(BF16), ffn1_w_out[i].astype(BF16))
        m, j = i % 4, i // 4
        mix = None
        if m == 0:
            mix = _retention(h, norm_mix[i], ret_w_in[j], ret_gn_gain[j], ret_w_out[j], b=b, s=s)
        elif m == 1:
            mix = _nsa(h, norm_mix[i], nsa_w_in[j], nsa_cmp_pos[j], nsa_cmp_w1[j], nsa_cmp_w2[j], nsa_w_out[j], b=b, s=s)
        elif m == 2:
            mix = _ssd(h, norm_mix[i], ssd_w_in[j], ssd_conv_w[j], ssd_conv_b[j], ssd_dt_bias[j], ssd_a_log[j],
                       ssd_d[j], ssd_norm[j], ssd_w_out[j], b=b, s=s)
        else:
            h = _dilated(h, norm_mix[i], dil_w_in[j], dil_w_out[j], b=b, s=s)
        h = _ffn(h, norm_ffn2[i], ffn2_w_in[i].astype(BF16), ffn2_w_out[i].astype(BF16),
                 norm_final if i == depth - 1 else None, mix)
    return h.reshape(b, s, d)
```

```python
import functools
import math

import jax
import jax.numpy as jnp
from jax import lax
from jax.experimental import pallas as pl
from jax.experimental.pallas import tpu as pltpu

F32 = jnp.float32
BF16 = jnp.bfloat16
NORM_EPS = 1e-6
NEG_INF = -1e30
ROPE_BASE = 10000.0
VMEM_LIMIT_BYTES = 56 * 1024 * 1024
LANES = 128
MXU_WIDTH = 256
TOKEN_TILE = 512
MAX_DOT_COLS = 2048

RET_HEADS = 4
RET_CHUNK = 128

NSA_GROUPS = 4
NSA_HPG = 4
NSA_DH = 64
NSA_CMP_LEN = 32
NSA_CMP_STRIDE = 16
NSA_SEL_LEN = 64
NSA_N_SEL = 16
NSA_WINDOW = 512
NSA_FORCE_BONUS = 1e4
NSA_QT = 256
NSA_KC = 512

SSD_GROUPS = 4
SSD_HPG = 8
SSD_HEADDIM = 64
SSD_STATE = 128
SSD_CONV = 4
SSD_CHUNK = 256

DIL_PATTERN = ((128, 1), (512, 4), (2048, 16))
DIL_HEADS = 8
DIL_DH = 128
DIL_QB = 128


def _params(*sem):
    return pltpu.CompilerParams(dimension_semantics=sem, vmem_limit_bytes=VMEM_LIMIT_BYTES)


def _rms(x, g):
    return x * lax.rsqrt(jnp.mean(x * x, axis=-1, keepdims=True) + NORM_EPS) * g


def _silu(x):
    hx = 0.5 * x
    return hx + hx * jnp.tanh(hx)


def _dot(a, b):
    return jnp.dot(a, b, preferred_element_type=F32)


def _dot_nt(a, b):
    return lax.dot_general(a, b, (((1,), (1,)), ((), ())), preferred_element_type=F32)


def _split3(x):
    hi = x.astype(BF16)
    r1 = x - hi.astype(F32)
    mid = r1.astype(BF16)
    lo = (r1 - mid.astype(F32)).astype(BF16)
    return hi, mid, lo


def _dot3(terms, rhs):
    return _dot(terms[0], rhs) + _dot(terms[1], rhs) + _dot(terms[2], rhs)


def _col_slices(n):
    if n <= MAX_DOT_COLS:
        return [slice(0, n)]
    tiles = -(-n // MXU_WIDTH)
    parts = -(-tiles * MXU_WIDTH // MAX_DOT_COLS)
    bounds = [min(n, (tiles * p // parts) * MXU_WIDTH) for p in range(parts + 1)]
    return [slice(lo, hi) for lo, hi in zip(bounds[:-1], bounds[1:])]


def _resident(shape, index=None):
    index = index if index is not None else (0,) * len(shape)
    return pl.BlockSpec(shape, lambda *_: index, pipeline_mode=pl.Buffered(1))


def _ffn_body(h_ref, *refs, final_norm, mix):
    x = h_ref[...]
    if mix:
        y_ref, wm_ref, *refs = refs
        x = x + _dot(y_ref[...], wm_ref[...])
    g_ref, wa_ref, wb_ref, wo_ref, gf_ref, o_ref = refs
    xn = _rms(x, g_ref[...]).astype(BF16)
    acc = jnp.zeros(x.shape, F32)
    for sl in _col_slices(wa_ref.shape[1]):
        a = _dot(xn, wa_ref[:, sl])
        b = _dot(xn, wb_ref[:, sl])
        acc = acc + _dot((_silu(a) * b).astype(BF16), wo_ref[sl, :])
    y = x + 0.5 * acc
    if final_norm:
        y = _rms(y, gf_ref[...])
    o_ref[...] = y


def _ffn(h, g, w_in, w_out, g_final=None, mix=None, *, tm=TOKEN_TILE):
    m, d = h.shape
    f = w_out.shape[0]
    final_norm = g_final is not None
    gf = g_final if final_norm else g
    mix_args = list(mix) if mix is not None else []
    mix_specs = [pl.BlockSpec((tm, mix[0].shape[1]), lambda i: (i, 0)), _resident(mix[1].shape)] if mix_args else []
    return pl.pallas_call(
        functools.partial(_ffn_body, final_norm=final_norm, mix=bool(mix_args)),
        grid=(m // tm,),
        in_specs=[pl.BlockSpec((tm, d), lambda i: (i, 0))] + mix_specs + [
            _resident((1, d)),
            _resident((d, f)),
            _resident((d, f), (0, 1)),
            _resident((f, d)),
            _resident((1, d)),
        ],
        out_specs=pl.BlockSpec((tm, d), lambda i: (i, 0)),
        out_shape=jax.ShapeDtypeStruct((m, d), F32),
        compiler_params=_params("parallel"),
        name="ffn",
    )(h, *mix_args, g.reshape(1, d), w_in, w_in, w_out, gf.reshape(1, d))


def _norm_proj_body(h_ref, g_ref, *refs, n_out):
    xn = _rms(h_ref[...], g_ref[...]).astype(BF16)
    for w_ref, o_ref in zip(refs[:n_out], refs[n_out:]):
        for sl in _col_slices(w_ref.shape[1]):
            o_ref[:, sl] = _dot(xn, w_ref[:, sl]).astype(o_ref.dtype)


def _norm_proj(h, g, ws, out_dtypes, *, tm=TOKEN_TILE):
    m, d = h.shape
    return pl.pallas_call(
        functools.partial(_norm_proj_body, n_out=len(ws)),
        grid=(m // tm,),
        in_specs=[pl.BlockSpec((tm, d), lambda i: (i, 0)), pl.BlockSpec((1, d), lambda i: (0, 0))]
        + [_resident(w.shape) for w in ws],
        out_specs=[pl.BlockSpec((tm, w.shape[1]), lambda i: (i, 0)) for w in ws],
        out_shape=[jax.ShapeDtypeStruct((m, w.shape[1]), dt) for w, dt in zip(ws, out_dtypes)],
        compiler_params=_params("parallel"),
        name="norm_proj",
    )(h, g.reshape(1, d), *ws)


def _ret_body(q_ref, k_ref, v_ref, g_ref, cos_ref, sin_ref, gn_ref, o_ref, state_ref, *, ts):
    c_len = RET_CHUNK
    dk = q_ref.shape[2] // RET_HEADS
    dv = v_ref.shape[2] // RET_HEADS
    half = dk // 2

    @pl.when(pl.program_id(1) == 0)
    def _():
        state_ref[...] = jnp.zeros(state_ref.shape, F32)

    ii = lax.broadcasted_iota(jnp.int32, (c_len, c_len), 0)
    jj = lax.broadcasted_iota(jnp.int32, (c_len, c_len), 1)
    rel = (ii - jj).astype(F32)
    causal = ii >= jj
    idx = lax.broadcasted_iota(jnp.int32, (c_len, 1), 0).astype(F32)

    def rot(t, cos, sin):
        t1, t2 = t[:, :half], t[:, half:]
        return jnp.concatenate([t1 * cos - t2 * sin, t1 * sin + t2 * cos], axis=1)

    decays = []
    for h in range(RET_HEADS):
        log_gamma = math.log1p(-(2.0 ** (-5.0 - h)))
        decays.append((jnp.where(causal, jnp.exp(jnp.where(causal, rel, 0.0) * log_gamma), 0.0),
                       jnp.exp((idx + 1.0) * log_gamma),
                       jnp.exp((c_len - 1.0 - idx) * log_gamma),
                       math.exp(c_len * log_gamma)))

    def chunk(c, carry):
        r0 = pl.multiple_of(c * c_len, c_len)
        rows = pl.ds(r0, c_len)
        cos = cos_ref[rows, :]
        sin = sin_ref[rows, :]
        for h in range(RET_HEADS):
            inner, q_decay, k_decay, chunk_decay = decays[h]
            q = rot(q_ref[0, rows, h * dk:(h + 1) * dk].astype(F32), cos, sin)
            k = rot(k_ref[0, rows, h * dk:(h + 1) * dk].astype(F32), cos, sin) * (dk ** -0.5)
            v = v_ref[0, rows, h * dv:(h + 1) * dv]
            qb = q.astype(BF16)
            sc = _dot_nt(qb, k.astype(BF16)) * inner
            st = state_ref[h]
            o = _dot(sc.astype(BF16), v) + _dot(qb, st.astype(BF16)) * q_decay
            kd_t = jnp.transpose(k * k_decay).astype(BF16)
            state_ref[h] = st * chunk_decay + _dot(kd_t, v)
            gn = gn_ref[:, h * dv:(h + 1) * dv]
            on = _rms(o, gn)
            gate = g_ref[0, rows, h * dv:(h + 1) * dv].astype(F32)
            o_ref[0, rows, h * dv:(h + 1) * dv] = (_silu(gate) * on).astype(o_ref.dtype)
        return carry

    lax.fori_loop(0, ts // c_len, chunk, 0)


def _retention_core(proj, cos, sin, gn_gain, *, b, s, ts=TOKEN_TILE):
    n = proj.shape[2]
    hk = n // 6
    hv = 2 * hk
    dk = hk // RET_HEADS
    return pl.pallas_call(
        functools.partial(_ret_body, ts=ts),
        grid=(b, s // ts),
        in_specs=[
            pl.BlockSpec((1, ts, hk), lambda i, j: (i, j, 0)),
            pl.BlockSpec((1, ts, hk), lambda i, j: (i, j, 1)),
            pl.BlockSpec((1, ts, hv), lambda i, j: (i, j, 1)),
            pl.BlockSpec((1, ts, hv), lambda i, j: (i, j, 2)),
            pl.BlockSpec((ts, dk // 2), lambda i, j: (j, 0)),
            pl.BlockSpec((ts, dk // 2), lambda i, j: (j, 0)),
            pl.BlockSpec((1, hv), lambda i, j: (0, 0)),
        ],
        out_specs=pl.BlockSpec((1, ts, hv), lambda i, j: (i, j, 0)),
        out_shape=jax.ShapeDtypeStruct((b, s, hv), BF16),
        scratch_shapes=[pltpu.VMEM((RET_HEADS, dk, hv // RET_HEADS), F32)],
        compiler_params=_params("parallel", "arbitrary"),
        name="retention",
    )(proj, proj, proj, proj, cos, sin, gn_gain.reshape(1, hv))


def _rope_tables(s, half):
    inv = ROPE_BASE ** (-jnp.arange(half, dtype=F32) / half)
    ang = jnp.arange(s, dtype=F32)[:, None] * inv[None, :]
    return jnp.cos(ang), jnp.sin(ang)


def _retention(h2, g_norm, w_in, gn_gain, w_out, *, b, s):
    (proj,) = _norm_proj(h2, g_norm, [w_in.astype(BF16)], [BF16])
    n = proj.shape[1]
    cos, sin = _rope_tables(s, n // 6 // RET_HEADS // 2)
    y = _retention_core(proj.reshape(b, s, n), cos, sin, gn_gain, b=b, s=s)
    return y.reshape(b * s, -1), w_out.astype(BF16)


def _dil_proj_body(h_ref, g_ref, w_ref, o_ref, xn_ref, *, r):
    tm = h_ref.shape[0]
    n = tm // r
    xn = _rms(h_ref[...], g_ref[...])
    slabs = xn_ref.shape[0]
    for k in range(slabs):
        xn_ref[k] = xn[:, k * LANES:(k + 1) * LANES]
    xp = jnp.concatenate(
        [jnp.concatenate([xn_ref[k, pl.ds(c, n, stride=r), :] for c in range(r)], axis=0) for k in range(slabs)],
        axis=1).astype(BF16)
    for sl in _col_slices(w_ref.shape[1]):
        res = _dot(xp, w_ref[:, sl]).astype(o_ref.dtype)
        for c in range(r):
            o_ref[0, c, :, sl] = res[c * n:(c + 1) * n, :]


def _dil_proj(h2, g_norm, w, r, *, b, s, tm=TOKEN_TILE):
    d = h2.shape[1]
    n_out = w.shape[1]
    tiles = s // tm
    return pl.pallas_call(
        functools.partial(_dil_proj_body, r=r),
        grid=(b * tiles,),
        in_specs=[
            pl.BlockSpec((tm, d), lambda i: (i, 0)),
            pl.BlockSpec((1, d), lambda i: (0, 0)),
            _resident((d, n_out)),
        ],
        out_specs=pl.BlockSpec((1, r, tm // r, n_out), lambda i: (i // tiles, 0, i % tiles, 0)),
        out_shape=jax.ShapeDtypeStruct((b, r, s // r, n_out), BF16),
        scratch_shapes=[pltpu.VMEM((d // LANES, tm, LANES), F32)],
        compiler_params=_params("parallel"),
        name=f"dilated_proj_r{r}",
    )(h2, g_norm.reshape(1, d), w)


def _dil_body(q_ref, kc_ref, kp_ref, vc_ref, vp_ref, o_ref, lse_ref, kbuf, vbuf, *, rows, n_back):
    qb = DIL_QB
    dh = DIL_DH
    first_step = pl.program_id(2) == 0
    kbuf[0:qb, :] = kp_ref[0, 0]
    kbuf[qb:, :] = kc_ref[0, 0]
    vbuf[0:qb, :] = vp_ref[0, 0]
    vbuf[qb:, :] = vc_ref[0, 0]
    qi = lax.broadcasted_iota(jnp.int32, (qb, 2 * qb), 0)
    kj = lax.broadcasted_iota(jnp.int32, (qb, 2 * qb), 1)
    dist = qi + qb - kj
    band = (dist >= 0) & (dist <= n_back)
    lane = lax.broadcasted_iota(jnp.int32, (qb, LANES), 1)
    scale = dh ** -0.5
    ones = jnp.ones((2 * qb, dh), BF16)
    for i in range(rows // qb):
        mask = band & ((kj >= qb) | jnp.logical_not(first_step)) if i == 0 else band
        lse_tile = jnp.zeros((qb, LANES), F32)
        for h in range(DIL_HEADS):
            cols = slice(h * dh, (h + 1) * dh)
            q = q_ref[0, 0, i * qb:(i + 1) * qb, cols]
            k = kbuf[i * qb:(i + 2) * qb, cols]
            v = vbuf[i * qb:(i + 2) * qb, cols]
            s = jnp.where(mask, _dot_nt(q, k), NEG_INF)
            m = jnp.max(s, axis=1, keepdims=True)
            e = jnp.exp2((s - m) * (scale * math.log2(math.e))).astype(BF16)
            acc = _dot(e, jnp.concatenate([v, ones], axis=1))
            den = jnp.maximum(acc[:, dh:], 1e-30)
            o_ref[0, 0, i * qb:(i + 1) * qb, cols] = (acc[:, :dh] / den).astype(o_ref.dtype)
            lse_tile = jnp.where(lane == h, m * scale + jnp.log(den), lse_tile)
        lse_ref[0, 0, i * qb:(i + 1) * qb, :] = lse_tile


def _dilated_group(proj, win, r, *, b, s):
    hd = DIL_HEADS * DIL_DH
    length = s // r
    rows = min(length, TOKEN_TILE)
    qb = DIL_QB

    def cur(which):
        return pl.BlockSpec((1, 1, rows, hd), lambda i, c, n: (i, c, n, which))

    def prev(which):
        return pl.BlockSpec((1, 1, qb, hd), lambda i, c, n: (i, c, jnp.maximum(n * (rows // qb) - 1, 0), which))

    return pl.pallas_call(
        functools.partial(_dil_body, rows=rows, n_back=win // r),
        grid=(b, r, length // rows),
        in_specs=[cur(0), cur(1), prev(1), cur(2), prev(2)],
        out_specs=[
            pl.BlockSpec((1, 1, rows, hd), lambda i, c, n: (i, c, n, 0)),
            pl.BlockSpec((1, 1, rows, LANES), lambda i, c, n: (i, c, n, 0)),
        ],
        out_shape=[
            jax.ShapeDtypeStruct((b, r, length, hd), BF16),
            jax.ShapeDtypeStruct((b, r, length, LANES), F32),
        ],
        scratch_shapes=[pltpu.VMEM((rows + qb, hd), BF16), pltpu.VMEM((rows + qb, hd), BF16)],
        compiler_params=_params("parallel", "parallel", "arbitrary"),
        name=f"dilated_r{r}",
    )(proj, proj, proj, proj, proj)


def _dil_merge_body(h_ref, *refs, dilations):
    ng = len(dilations)
    o_refs, l_refs = refs[:ng], refs[ng:2 * ng]
    w_ref, out_ref = refs[2 * ng], refs[2 * ng + 1]
    o_bufs, l_bufs = refs[2 * ng + 2:3 * ng + 2], refs[3 * ng + 2:]
    tm = h_ref.shape[1]
    for o_ref, l_ref, o_buf, l_buf, r in zip(o_refs, l_refs, o_bufs, l_bufs, dilations):
        n = tm // r
        for c in range(r):
            rows = pl.ds(c, n, stride=r)
            l_buf[rows, :] = l_ref[0, c]
            blk = o_ref[0, c].astype(F32)
            for h in range(DIL_HEADS):
                o_buf[h, rows, :] = blk[:, h * DIL_DH:(h + 1) * DIL_DH]
    lses = [l[...] for l in l_bufs]
    mx = functools.reduce(jnp.maximum, lses)
    es = [jnp.exp(l - mx) for l in lses]
    tot = functools.reduce(jnp.add, es)
    wts = [e / tot for e in es]
    dh = DIL_DH
    parts = []
    for h in range(DIL_HEADS):
        acc = jnp.zeros((tm, dh), F32)
        for g in range(ng):
            wg = jnp.broadcast_to(wts[g][:, h:h + 1], (tm, dh))
            acc = acc + wg * o_bufs[g][h]
        parts.append(acc)
    o = jnp.concatenate(parts, axis=1).astype(BF16)
    out_ref[0] = h_ref[0] + _dot(o, w_ref[...])


def _dil_merge(h3, outs, lses, w, dilations, *, tm=TOKEN_TILE):
    b, s, d = h3.shape
    hd = w.shape[0]
    res_major = lambda r, width: pl.BlockSpec((1, r, tm // r, width), lambda i, n: (i, 0, n, 0))
    return pl.pallas_call(
        functools.partial(_dil_merge_body, dilations=tuple(dilations)),
        grid=(b, s // tm),
        in_specs=[pl.BlockSpec((1, tm, d), lambda i, n: (i, n, 0))]
        + [res_major(r, hd) for r in dilations] + [res_major(r, LANES) for r in dilations]
        + [pl.BlockSpec((hd, d), lambda i, n: (0, 0))],
        out_specs=pl.BlockSpec((1, tm, d), lambda i, n: (i, n, 0)),
        out_shape=jax.ShapeDtypeStruct((b, s, d), F32),
        scratch_shapes=[pltpu.VMEM((DIL_HEADS, tm, DIL_DH), F32) for _ in dilations]
        + [pltpu.VMEM((tm, LANES), F32) for _ in dilations],
        compiler_params=_params("parallel", "parallel"),
        name="dilated_merge",
    )(h3, *outs, *lses, w)


def _dilated(h2, g_norm, w_in, w_out, *, b, s):
    d = h2.shape[1]
    gw = 3 * DIL_HEADS * DIL_DH
    wb = w_in.astype(BF16)
    outs, lses, dilations = [], [], []
    for g, (win, r) in enumerate(DIL_PATTERN):
        proj = _dil_proj(h2, g_norm, wb[:, g * gw:(g + 1) * gw], r, b=b, s=s, tm=2 * TOKEN_TILE)
        o, lse = _dilated_group(proj, win, r, b=b, s=s)
        outs.append(o)
        lses.append(lse)
        dilations.append(r)
    return _dil_merge(h2.reshape(b, s, d), outs, lses, w_out.astype(BF16), dilations).reshape(b * s, d)


def _softplus(x):
    return jnp.maximum(x, 0.0) + jnp.log1p(jnp.exp(-jnp.abs(x)))


def _ssd_body(z_ref, x_ref, dt_ref, cw_ref, cb_ref, dtb_ref, alog_ref, dsk_ref, ng_ref, o_ref,
              xpad, state_ref):
    ln = SSD_CHUNK
    nst = SSD_STATE
    gw = SSD_HPG * SSD_HEADDIM
    d_inner = SSD_GROUPS * gw
    halo = 8

    @pl.when(pl.program_id(1) == 0)
    def _():
        xpad[0:halo, :] = jnp.zeros((halo, xpad.shape[1]), F32)
        state_ref[...] = jnp.zeros(state_ref.shape, F32)

    x_b = x_ref[0]
    x_f = x_b.astype(F32)
    ii = lax.broadcasted_iota(jnp.int32, (ln, ln), 0)
    jj = lax.broadcasted_iota(jnp.int32, (ln, ln), 1)
    conv = cb_ref[...] + x_f * cw_ref[SSD_CONV - 1:SSD_CONV, :]
    for shift in range(1, SSD_CONV):
        shifted = _dot(jnp.where(ii - jj == shift, 1.0, 0.0).astype(BF16), x_b)
        conv = conv + shifted * cw_ref[SSD_CONV - 1 - shift:SSD_CONV - shift, :]
    xpad[halo:2 * halo, :] = x_f[0:halo, :]
    head = cb_ref[...]
    for k in range(SSD_CONV):
        off = halo - (SSD_CONV - 1) + k
        head = head + xpad[off:off + halo, :] * cw_ref[k:k + 1, :]
    xpad[0:halo, :] = x_f[ln - halo:ln, :]
    xbc = _silu(jnp.concatenate([head, conv[halo:, :]], axis=0))
    xs = xbc[:, :d_inner]
    bm = xbc[:, d_inner:d_inner + SSD_GROUPS * nst]
    cm = xbc[:, d_inner + SSD_GROUPS * nst:]

    dt = _softplus(dt_ref[0] + dtb_ref[...])
    da = dt * (-jnp.exp(alog_ref[...]))
    ii = lax.broadcasted_iota(jnp.int32, (ln, ln), 0)
    jj = lax.broadcasted_iota(jnp.int32, (ln, ln), 1)
    causal = ii >= jj
    tril = jnp.where(causal, 1.0, 0.0).astype(BF16)
    da_terms = _split3(da)
    acs = _dot(tril, da_terms[0]) + _dot(tril, da_terms[1]) + _dot(tril, da_terms[2])
    acs2 = acs * math.log2(math.e)
    acs2_t = jnp.transpose(acs2)
    hl = ln // 2

    erow = lax.broadcasted_iota(jnp.int32, (LANES, d_inner), 0)
    ecol = lax.broadcasted_iota(jnp.int32, (LANES, d_inner), 1)
    expand = jnp.where(ecol // SSD_HEADDIM == erow, 1.0, 0.0).astype(BF16)
    acs_e = _dot3(_split3(acs), expand)
    dt_e = _dot3(_split3(dt), expand)
    last = acs_e[ln - 1:ln, :]
    decay_in = jnp.exp(acs_e)
    xs_dt = xs * dt_e
    xs_end = (xs_dt * jnp.exp(last - acs_e)).astype(BF16)
    xs_b = xs_dt.astype(BF16)
    lane = lax.broadcasted_iota(jnp.int32, (ln, LANES), 1)

    y_groups = []
    for g in range(SSD_GROUPS):
        bm_g = bm[:, g * nst:(g + 1) * nst]
        cm_g = cm[:, g * nst:(g + 1) * nst].astype(BF16)
        cb = jnp.where(causal, _dot_nt(cm_g, bm_g.astype(BF16)), 0.0)
        st = state_ref[g]
        gcols = slice(g * gw, (g + 1) * gw)
        y_state = _dot(cm_g, st.astype(BF16)) * decay_in[:, gcols]
        pairs = []
        for p in range(SSD_HPG // 2):
            pair_cols = slice(g * gw + p * LANES, g * gw + (p + 1) * LANES)
            halves = []
            for e in range(2):
                hd = g * SSD_HPG + 2 * p + e
                top = cb[:hl, :hl] * jnp.exp2(jnp.minimum(acs2[:hl, hd:hd + 1] - acs2_t[hd:hd + 1, :hl], 0.0))
                bot = cb[hl:, :] * jnp.exp2(jnp.minimum(acs2[hl:, hd:hd + 1] - acs2_t[hd:hd + 1, :], 0.0))
                halves.append(jnp.concatenate([_dot(top.astype(BF16), xs_b[:hl, pair_cols]),
                                               _dot(bot.astype(BF16), xs_b[:, pair_cols])], axis=0))
            pairs.append(jnp.where(lane < SSD_HEADDIM, halves[0], halves[1]))
        y_groups.append(jnp.concatenate(pairs, axis=1) + y_state)
        bm_t = jnp.transpose(bm_g).astype(BF16)
        state_ref[g] = st * jnp.exp(last[:, gcols]) + _dot(bm_t, xs_end[:, gcols])

    y = jnp.concatenate(y_groups, axis=1) + dsk_ref[...] * xs
    yz = y * _silu(z_ref[0].astype(F32))
    outs = [_rms(yz[:, g * gw:(g + 1) * gw], ng_ref[:, g * gw:(g + 1) * gw]) for g in range(SSD_GROUPS)]
    o_ref[0] = jnp.concatenate(outs, axis=1).astype(o_ref.dtype)


def _ssd_core(z, xbc, dt, conv_w, conv_b, dt_bias, a_log, d_skip, norm_g, *, b, s):
    ln = SSD_CHUNK
    d_inner = z.shape[2]
    conv_dim = xbc.shape[2]
    gw = SSD_HPG * SSD_HEADDIM
    full = lambda shape: pl.BlockSpec(shape, lambda i, j: (0,) * len(shape))
    return pl.pallas_call(
        _ssd_body,
        grid=(b, s // ln),
        in_specs=[
            pl.BlockSpec((1, ln, d_inner), lambda i, j: (i, j, 0)),
            pl.BlockSpec((1, ln, conv_dim), lambda i, j: (i, j, 0)),
            pl.BlockSpec((1, ln, LANES), lambda i, j: (i, j, 0)),
            full((SSD_CONV, conv_dim)),
            full((1, conv_dim)),
            full((1, LANES)),
            full((1, LANES)),
            full((1, d_inner)),
            full((1, d_inner)),
        ],
        out_specs=pl.BlockSpec((1, ln, d_inner), lambda i, j: (i, j, 0)),
        out_shape=jax.ShapeDtypeStruct((b, s, d_inner), BF16),
        scratch_shapes=[pltpu.VMEM((16, conv_dim), F32),
                        pltpu.VMEM((SSD_GROUPS, SSD_STATE, gw), F32)],
        compiler_params=_params("parallel", "arbitrary"),
        name="ssd",
    )(z, xbc, dt, conv_w, conv_b, dt_bias, a_log, d_skip, norm_g)


def _ssd(h2, g_norm, w_in, conv_w, conv_b, dt_bias, a_log, d_skip, norm_g, w_out, *, b, s):
    heads = SSD_GROUPS * SSD_HPG
    d_inner = heads * SSD_HEADDIM
    conv_dim = conv_w.shape[1]
    wb = w_in.astype(BF16)
    w_dt = jnp.pad(wb[:, d_inner + conv_dim:], ((0, 0), (0, LANES - heads)))
    z, xbc, dt = _norm_proj(h2, g_norm, [wb[:, :d_inner], wb[:, d_inner:d_inner + conv_dim], w_dt], [BF16, BF16, F32])
    pad_heads = lambda v: jnp.pad(v.astype(F32), (0, LANES - heads)).reshape(1, LANES)
    y = _ssd_core(z.reshape(b, s, -1), xbc.reshape(b, s, -1), dt.reshape(b, s, LANES),
                  conv_w.astype(F32), conv_b.reshape(1, -1).astype(F32), pad_heads(dt_bias), pad_heads(a_log),
                  jnp.repeat(d_skip.astype(F32), SSD_HEADDIM).reshape(1, d_inner), norm_g.reshape(1, d_inner),
                  b=b, s=s)
    return y.reshape(b * s, d_inner), w_out.astype(BF16)


def _nsa_proj_body(h_ref, g_ref, wn_ref, wt_ref, k0_ref, v0_ref, k12_ref, tr_ref):
    xn = _rms(h_ref[...], g_ref[...]).astype(BF16)
    nat = _dot(xn, wn_ref[...])
    gd = k0_ref.shape[1]
    k0_ref[...] = nat[:, :gd].astype(k0_ref.dtype)
    v0_ref[...] = nat[:, gd:2 * gd].astype(v0_ref.dtype)
    k12_ref[...] = nat[:, 2 * gd:].astype(k12_ref.dtype)
    res = _dot_nt(wt_ref[...], xn)
    for j in range(tr_ref.shape[0]):
        tr_ref[j] = res[:, j * NSA_QT:(j + 1) * NSA_QT].astype(tr_ref.dtype)


def _nsa_proj(h2, g_norm, w_nat, w_t, *, tm=TOKEN_TILE):
    m, d = h2.shape
    gd = NSA_GROUPS * NSA_DH
    nt = w_t.shape[0]
    slabs = tm // NSA_QT
    return pl.pallas_call(
        _nsa_proj_body,
        grid=(m // tm,),
        in_specs=[
            pl.BlockSpec((tm, d), lambda i: (i, 0)),
            pl.BlockSpec((1, d), lambda i: (0, 0)),
            _resident(w_nat.shape),
            _resident(w_t.shape),
        ],
        out_specs=[
            pl.BlockSpec((tm, gd), lambda i: (i, 0)),
            pl.BlockSpec((tm, gd), lambda i: (i, 0)),
            pl.BlockSpec((tm, 2 * gd), lambda i: (i, 0)),
            pl.BlockSpec((slabs, nt, NSA_QT), lambda i: (i, 0, 0)),
        ],
        out_shape=[
            jax.ShapeDtypeStruct((m, gd), BF16),
            jax.ShapeDtypeStruct((m, gd), BF16),
            jax.ShapeDtypeStruct((m, 2 * gd), BF16),
            jax.ShapeDtypeStruct((m // NSA_QT, nt, NSA_QT), BF16),
        ],
        compiler_params=_params("parallel"),
        name="nsa_proj",
    )(h2, g_norm.reshape(1, d), w_nat, w_t)


def _nsa_cmp_body(x_ref, pa_ref, pb_ref, w1a_ref, w1b_ref, w2_ref, o_ref, *, transposed):
    x = x_ref[0].astype(F32)
    nb = x.shape[0]
    ya = _dot((x + pa_ref[...]).astype(BF16), w1a_ref[...])
    yb = _dot((x + pb_ref[...]).astype(BF16), w1b_ref[...])
    hid = _silu(ya + pltpu.roll(yb, nb - 1, 0))
    out = _dot(hid.astype(BF16), w2_ref[...])
    o_ref[0] = (jnp.transpose(out) if transposed else out).astype(o_ref.dtype)


def _nsa_compress(x, pos, w1, w2, *, transposed):
    b, nb, width = x.shape
    g, dh = NSA_GROUPS, NSA_DH
    half = NSA_CMP_LEN // 2
    hid = w1.shape[1]
    eye = jnp.eye(g, dtype=F32)
    w1r = w1.reshape(NSA_CMP_LEN, dh, hid)
    big = jnp.einsum("ldj,gh->lgdhj", w1r, eye).reshape(NSA_CMP_LEN, g * dh, g * hid)
    w1a = big[:half].reshape(half * g * dh, g * hid).astype(BF16)
    w1b = big[half:].reshape(half * g * dh, g * hid).astype(BF16)
    w2big = jnp.einsum("jd,gh->gjhd", w2, eye).reshape(g * hid, g * dh).astype(BF16)
    posb = jnp.broadcast_to(pos[:, None, :], (NSA_CMP_LEN, g, dh)).reshape(NSA_CMP_LEN, g * dh)
    pa = posb[:half].reshape(1, width).astype(F32)
    pb = posb[half:].reshape(1, width).astype(F32)
    full = lambda shape: pl.BlockSpec(shape, lambda i: (0,) * len(shape))
    out_dims = (g * dh, nb) if transposed else (nb, g * dh)
    return pl.pallas_call(
        functools.partial(_nsa_cmp_body, transposed=transposed),
        grid=(b,),
        in_specs=[
            pl.BlockSpec((1, nb, width), lambda i: (i, 0, 0)),
            full((1, width)), full((1, width)),
            full((width, g * hid)), full((width, g * hid)), full((g * hid, g * dh)),
        ],
        out_specs=pl.BlockSpec((1,) + out_dims, lambda i: (i, 0, 0)),
        out_shape=jax.ShapeDtypeStruct((b,) + out_dims, BF16),
        compiler_params=_params("parallel"),
        name="nsa_compress",
    )(x, pa, pb, w1a, w1b, w2big)


def _nsa_pair_body(q_ref, gt_ref, kc_ref, vct_ref, k1_ref, k2_ref, v1_ref, v2_ref, o_ref, bias_ref, wbias_ref, *, s):
    qt, kc, dh = NSA_QT, NSA_KC, NSA_DH
    gl = 2
    glanes = NSA_HPG * qt
    lanes = gl * glanes
    vrows = gl * dh
    j = pl.program_id(2)
    t0 = j * qt
    nb = kc_ref.shape[1]
    nsb = s // NSA_SEL_LEN
    n_sel = min(NSA_N_SEL, nsb)

    q_t = q_ref[0, 0]
    qcat = [jnp.concatenate([q_t[(g * NSA_HPG + h) * dh:(g * NSA_HPG + h + 1) * dh, :] for h in range(NSA_HPG)],
                            axis=1) for g in range(gl)]
    zero = jnp.zeros_like(qcat[0])
    qp = jnp.concatenate([jnp.concatenate([qcat[0], zero], axis=1),
                          jnp.concatenate([zero, qcat[1]], axis=1)], axis=0)
    qlane = lax.broadcasted_iota(jnp.int32, (1, lanes), 1) % qt
    tq = t0 + qlane

    def values(v_ref, slab0, n_slabs):
        v = jnp.concatenate([v_ref[0, slab0 + i] for i in range(n_slabs)], axis=1)
        return jnp.concatenate([v, jnp.ones((16, n_slabs * qt), BF16)], axis=0)

    def weighted_values(v_aug, p):
        return jnp.concatenate(
            [_dot(jnp.concatenate([v_aug[g * dh:(g + 1) * dh, :], v_aug[vrows:, :]], axis=0),
                  p[:, g * glanes:(g + 1) * glanes]) for g in range(gl)], axis=1)

    def normalise(acc):
        o = acc[:dh, :] / jnp.maximum(acc[dh:dh + 1, :], 1e-30)
        return [o[:, g * glanes:(g + 1) * glanes] for g in range(gl)]

    wslabs = NSA_WINDOW // qt + 1
    row_w = lax.broadcasted_iota(jnp.int32, (qt, lanes), 0)

    @pl.when(j == 0)
    def _():
        wbias_ref[0] = jnp.where(row_w > qlane, 0.0, NEG_INF)
        wbias_ref[1] = jnp.zeros((qt, lanes), F32)
        wbias_ref[2] = jnp.where(row_w <= qlane, 0.0, NEG_INF)
        wbias_ref[3] = jnp.full((qt, lanes), NEG_INF, F32)

    jb0 = jnp.maximum(j + 1 - wslabs, 0)
    kw = k2_ref[0, pl.ds(pl.multiple_of(jb0 * qt, qt), wslabs * qt), :]
    sw_raw = _dot(kw, qp)
    slabs_back = [j - jb0 - i for i in range(wslabs)]
    sw = jnp.concatenate(
        [sw_raw[i * qt:(i + 1) * qt, :]
         + wbias_ref[jnp.where(back < 0, 3, jnp.where(back == wslabs - 1, 0, jnp.where(back == 0, 2, 1)))]
         for i, back in enumerate(slabs_back)], axis=0)

    sc = _dot(kc_ref[0], qp)
    n_idx = lax.broadcasted_iota(jnp.int32, (nb, lanes), 0)
    cmask = (n_idx * NSA_CMP_STRIDE + NSA_CMP_LEN - 1 <= tq) & (n_idx < nb - 1)
    sc = jnp.where(cmask, sc, NEG_INF)
    e = jnp.where(cmask, jnp.exp(sc - jnp.max(sc, axis=0, keepdims=True)), 0.0)
    p_c = e / jnp.maximum(jnp.sum(e, axis=0, keepdims=True), 1e-30)
    acc_c = _dot(vct_ref[0], p_c.astype(BF16))
    o_c = [acc_c[g * dh:(g + 1) * dh, g * glanes:(g + 1) * glanes] for g in range(gl)]
    imps = []
    for g in range(gl):
        imp = p_c[:, g * glanes:g * glanes + qt]
        for h in range(1, NSA_HPG):
            imp = imp + p_c[:, g * glanes + h * qt:g * glanes + (h + 1) * qt]
        imps.append(imp)
    imp = jnp.concatenate(imps, axis=1)

    ratio = NSA_SEL_LEN // NSA_CMP_STRIDE
    dd = lax.broadcasted_iota(jnp.int32, (nsb, nb), 1) - ratio * lax.broadcasted_iota(jnp.int32, (nsb, nb), 0)
    wsel = jnp.where((dd == -1) | (dd == ratio - 1), 1.0,
                     jnp.where((dd >= 0) & (dd < ratio - 1), 2.0, 0.0)).astype(BF16)
    terms = _split3(imp)
    imp_sel = _dot(wsel, terms[0]) + _dot(wsel, terms[1]) + _dot(wsel, terms[2])
    jq = lax.broadcasted_iota(jnp.int32, (nsb, gl * qt), 0)
    cur = (t0 + lax.broadcasted_iota(jnp.int32, (nsb, gl * qt), 1) % qt) // NSA_SEL_LEN
    forced = (jq == 0) | (jq == cur) | (jq == cur - 1)
    score = jnp.where(jq <= cur, imp_sel + jnp.where(forced, NSA_FORCE_BONUS, 0.0), NEG_INF)

    pw = jnp.exp(sw - jnp.max(sw, axis=0, keepdims=True)).astype(BF16)
    o_w = normalise(weighted_values(values(v2_ref, jb0, wslabs), pw))

    jqf = jq.astype(F32)
    work = score
    sel_bias = jnp.full((nsb, gl * qt), NEG_INF, F32)
    for _ in range(n_sel):
        top = jnp.max(work, axis=0, keepdims=True)
        first = jnp.min(jnp.where(work == top, jqf, float(nsb)), axis=0, keepdims=True)
        pick = jqf == first
        sel_bias = jnp.where(pick & (work > 0.5 * NEG_INF), 0.0, sel_bias)
        work = jnp.where(pick, 3.0 * NEG_INF, work)
    bias_ref[...] = jnp.concatenate([sel_bias[:, g * qt:(g + 1) * qt] for g in range(gl) for _ in range(NSA_HPG)],
                                    axis=1)

    blocks_per_chunk = kc // NSA_SEL_LEN
    slabs_per_chunk = kc // qt

    def online(carry, st, bias, v_aug):
        m, acc = carry
        blks = [st[r * NSA_SEL_LEN:(r + 1) * NSA_SEL_LEN, :] for r in range(blocks_per_chunk)]
        part = None
        for r in range(blocks_per_chunk):
            pr = blks[r][0:8, :]
            for i in range(1, NSA_SEL_LEN // 8):
                pr = jnp.maximum(pr, blks[r][8 * i:8 * (i + 1), :])
            pr = pr + bias[r:r + 1, :]
            part = pr if part is None else jnp.maximum(part, pr)
        m_new = jnp.maximum(m, jnp.max(part, axis=0, keepdims=True))
        p = jnp.concatenate([jnp.exp(blks[r] - (m_new - bias[r:r + 1, :])) for r in range(blocks_per_chunk)],
                            axis=0).astype(BF16)
        return m_new, jnp.exp(m - m_new) * acc + weighted_values(v_aug, p)

    def sel_scores(c):
        return _dot(k1_ref[0, pl.ds(pl.multiple_of(c * kc, kc), kc), :], qp)

    def sel_bias(c):
        return bias_ref[pl.ds(pl.multiple_of(c * blocks_per_chunk, blocks_per_chunk), blocks_per_chunk), :]

    def sel_values(c):
        return values(v1_ref, c * slabs_per_chunk, slabs_per_chunk)

    init = (jnp.full((1, lanes), NEG_INF, F32), jnp.zeros((dh + 16, lanes), F32))
    c_hi = t0 // kc
    own_slab = (t0 - c_hi * kc) // qt
    st_raw = sel_scores(c_hi)
    st = jnp.concatenate(
        [st_raw[i * qt:(i + 1) * qt, :] + wbias_ref[jnp.where(i < own_slab, 1, jnp.where(i == own_slab, 2, 3))]
         for i in range(slabs_per_chunk)], axis=0)
    chain_a = online(init, st, sel_bias(c_hi), sel_values(c_hi))
    chain_a = lax.cond(c_hi % 2 == 1,
                       lambda cr: online(cr, sel_scores(c_hi - 1), sel_bias(c_hi - 1), sel_values(c_hi - 1)),
                       lambda cr: cr, chain_a)
    chain_b = (chain_a[0], jnp.zeros_like(chain_a[1]))

    def pair(i, chains):
        ca, cb = chains
        st_a = sel_scores(2 * i)
        st_b = sel_scores(2 * i + 1)
        return (online(ca, st_a, sel_bias(2 * i), sel_values(2 * i)),
                online(cb, st_b, sel_bias(2 * i + 1), sel_values(2 * i + 1)))

    (m_a, acc_a), (m_b, acc_b) = lax.fori_loop(0, c_hi // 2, pair, (chain_a, chain_b))
    m_s = jnp.maximum(m_a, m_b)
    o_s = normalise(jnp.exp(m_a - m_s) * acc_a + jnp.exp(m_b - m_s) * acc_b)

    gates = jax.nn.sigmoid(gt_ref[0, 0].astype(F32))
    outs = []
    for g in range(gl):
        def gate_row(br):
            r0 = (g * 4 + br) * NSA_HPG
            return jnp.concatenate([gates[r0 + h:r0 + h + 1, :] for h in range(NSA_HPG)], axis=1)
        o = gate_row(0) * o_c[g] + gate_row(1) * o_s[g] + gate_row(2) * o_w[g]
        outs.extend(o[:, h * qt:(h + 1) * qt] for h in range(NSA_HPG))
    o_ref[0] = jnp.transpose(jnp.concatenate(outs, axis=0)).astype(o_ref.dtype)


def _nsa_attend_pairs(tr, k_cmp, v_cmp_t, k12, *, b, s):
    qt, dh = NSA_QT, NSA_DH
    gl = 2
    gd = NSA_GROUPS * dh
    hq = NSA_GROUPS * NSA_HPG * dh
    pairs = NSA_GROUPS // gl
    nslab = s // qt
    assert nslab > NSA_WINDOW // qt
    nb = k_cmp.shape[1]
    tr4 = tr.reshape(b, nslab, tr.shape[1], qt)
    q_rows = gl * NSA_HPG * dh
    v_rows = gl * dh
    gate_rows = gl * 4 * NSA_HPG
    v1_blk = hq // v_rows
    v2_blk = (hq + gd) // v_rows
    gate_blk = (hq + 2 * gd) // gate_rows
    nsb = s // NSA_SEL_LEN
    return pl.pallas_call(
        functools.partial(_nsa_pair_body, s=s),
        grid=(b, pairs, nslab),
        in_specs=[
            pl.BlockSpec((1, 1, q_rows, qt), lambda i, p, j: (i, j, p, 0)),
            pl.BlockSpec((1, 1, gate_rows, qt), lambda i, p, j: (i, j, gate_blk + p, 0)),
            pl.BlockSpec((1, nb, v_rows), lambda i, p, j: (i, 0, p)),
            pl.BlockSpec((1, v_rows, nb), lambda i, p, j: (i, p, 0)),
            pl.BlockSpec((1, s, v_rows), lambda i, p, j: (i, 0, p)),
            pl.BlockSpec((1, s, v_rows), lambda i, p, j: (i, 0, pairs + p)),
            pl.BlockSpec((1, nslab, v_rows, qt), lambda i, p, j: (i, 0, v1_blk + p, 0)),
            pl.BlockSpec((1, nslab, v_rows, qt), lambda i, p, j: (i, 0, v2_blk + p, 0)),
        ],
        out_specs=pl.BlockSpec((1, qt, q_rows), lambda i, p, j: (i, j, p)),
        out_shape=jax.ShapeDtypeStruct((b, s, hq), BF16),
        scratch_shapes=[pltpu.VMEM((nsb, gl * NSA_HPG * qt), F32), pltpu.VMEM((4, qt, gl * NSA_HPG * qt), F32)],
        compiler_params=_params("parallel", "parallel", "arbitrary"),
        name="nsa_attend",
    )(tr4, tr4, k_cmp, v_cmp_t, k12, k12, tr4, tr4)


def _nsa(h2, g_norm, w_in, cmp_pos, cmp_w1, cmp_w2, w_out, *, b, s):
    g, hpg, dh = NSA_GROUPS, NSA_HPG, NSA_DH
    hq, gd = g * hpg * dh, g * dh
    kv_w = lambda br, kv: w_in[:, hq + (2 * br + kv) * gd:hq + (2 * br + kv + 1) * gd]
    w_q = w_in[:, :hq] * (dh ** -0.5)
    w_g = w_in[:, hq + 6 * gd:].reshape(-1, 3, g, hpg).transpose(2, 1, 3, 0)
    w_g = jnp.pad(w_g, ((0, 0), (0, 1), (0, 0), (0, 0))).reshape(g * 4 * hpg, -1)
    w_t = jnp.concatenate([w_q.T, kv_w(1, 1).T, kv_w(2, 1).T, w_g], axis=0).astype(BF16)
    w_nat = jnp.concatenate([kv_w(0, 0), kv_w(0, 1), kv_w(1, 0), kv_w(2, 0)], axis=1).astype(BF16)
    k0, v0, k12, tr = _nsa_proj(h2, g_norm, w_nat, w_t)
    nb = s // NSA_CMP_STRIDE
    k_cmp = _nsa_compress(k0.reshape(b, nb, NSA_CMP_STRIDE * gd), cmp_pos[0], cmp_w1[0], cmp_w2[0], transposed=False)
    v_cmp_t = _nsa_compress(v0.reshape(b, nb, NSA_CMP_STRIDE * gd), cmp_pos[1], cmp_w1[1], cmp_w2[1], transposed=True)
    y = _nsa_attend_pairs(tr, k_cmp, v_cmp_t, k12.reshape(b, s, 2 * gd), b=b, s=s)
    return y.reshape(b * s, hq), w_out.astype(BF16)


def kernel(x, norm_ffn1, ffn1_w_in, ffn1_w_out, norm_mix, norm_ffn2, ffn2_w_in, ffn2_w_out, norm_final,
           ret_w_in, ret_gn_gain, ret_w_out,
           nsa_w_in, nsa_cmp_pos, nsa_cmp_w1, nsa_cmp_w2, nsa_w_out,
           ssd_w_in, ssd_conv_w, ssd_conv_b, ssd_dt_bias, ssd_a_log, ssd_d, ssd_norm, ssd_w_out,
           dil_w_in, dil_w_out):
    b, s, d = x.shape
    depth = norm_mix.shape[0]
    h = x.reshape(b * s, d)
    for i in range(depth):
        h = _ffn(h, norm_ffn1[i], ffn1_w_in[i].astype(BF16), ffn1_w_out[i].astype(BF16))
        m, j = i % 4, i // 4
        mix = None
        if m == 0:
            mix = _retention(h, norm_mix[i], ret_w_in[j], ret_gn_gain[j], ret_w_out[j], b=b, s=s)
        elif m == 1:
            mix = _nsa(h, norm_mix[i], nsa_w_in[j], nsa_cmp_pos[j], nsa_cmp_w1[j], nsa_cmp_w2[j], nsa_w_out[j], b=b, s=s)
        elif m == 2:
            mix = _ssd(h, norm_mix[i], ssd_w_in[j], ssd_conv_w[j], ssd_conv_b[j], ssd_dt_bias[j], ssd_a_log[j],
                       ssd_d[j], ssd_norm[j], ssd_w_out[j], b=b, s=s)
        else:
            h = _dilated(h, norm_mix[i], dil_w_in[j], dil_w_out[j], b=b, s=s)
        h = _ffn(h, norm_ffn2[i], ffn2_w_in[i].astype(BF16), ffn2_w_out[i].astype(BF16),
                 norm_final if i == depth - 1 else None, mix)
    return h.reshape(b, s, d)
```

```python
import functools
import math

import jax
import jax.numpy as jnp
from jax import lax
from jax.experimental import pallas as pl
from jax.experimental.pallas import tpu as pltpu

F32 = jnp.float32
BF16 = jnp.bfloat16
NORM_EPS = 1e-6
NEG_INF = -1e30
ROPE_BASE = 10000.0
VMEM_LIMIT_BYTES = 56 * 1024 * 1024
LANES = 128
MXU_WIDTH = 256
TOKEN_TILE = 512
MAX_DOT_COLS = 2048

RET_HEADS = 4
RET_CHUNK = 128

NSA_GROUPS = 4
NSA_HPG = 4
NSA_DH = 64
NSA_CMP_LEN = 32
NSA_CMP_STRIDE = 16
NSA_SEL_LEN = 64
NSA_N_SEL = 16
NSA_WINDOW = 512
NSA_FORCE_BONUS = 1e4
NSA_QT = 256
NSA_KC = 512

SSD_GROUPS = 4
SSD_HPG = 8
SSD_HEADDIM = 64
SSD_STATE = 128
SSD_CONV = 4
SSD_CHUNK = 256

DIL_PATTERN = ((128, 1), (512, 4), (2048, 16))
DIL_HEADS = 8
DIL_DH = 128
DIL_QB = 128


def _params(*sem):
    return pltpu.CompilerParams(dimension_semantics=sem, vmem_limit_bytes=VMEM_LIMIT_BYTES)


def _rms(x, g):
    return x * lax.rsqrt(jnp.mean(x * x, axis=-1, keepdims=True) + NORM_EPS) * g


def _silu(x):
    hx = 0.5 * x
    return hx + hx * jnp.tanh(hx)


def _dot(a, b):
    return jnp.dot(a, b, preferred_element_type=F32)


def _dot_nt(a, b):
    return lax.dot_general(a, b, (((1,), (1,)), ((), ())), preferred_element_type=F32)


def _split3(x):
    hi = x.astype(BF16)
    r1 = x - hi.astype(F32)
    mid = r1.astype(BF16)
    lo = (r1 - mid.astype(F32)).astype(BF16)
    return hi, mid, lo


def _dot3(terms, rhs):
    return _dot(terms[0], rhs) + _dot(terms[1], rhs) + _dot(terms[2], rhs)


def _col_slices(n):
    if n <= MAX_DOT_COLS:
        return [slice(0, n)]
    tiles = -(-n // MXU_WIDTH)
    parts = -(-tiles * MXU_WIDTH // MAX_DOT_COLS)
    bounds = [min(n, (tiles * p // parts) * MXU_WIDTH) for p in range(parts + 1)]
    return [slice(lo, hi) for lo, hi in zip(bounds[:-1], bounds[1:])]


def _resident(shape, index=None):
    index = index if index is not None else (0,) * len(shape)
    return pl.BlockSpec(shape, lambda *_: index, pipeline_mode=pl.Buffered(1))


def _ffn_body(h_ref, *refs, final_norm, mix):
    x = h_ref[...]
    if mix:
        y_ref, wm_ref, *refs = refs
        x = x + _dot(y_ref[...], wm_ref[...])
    g_ref, wa_ref, wb_ref, wo_ref, gf_ref, o_ref = refs
    xn = _rms(x, g_ref[...]).astype(BF16)
    acc = jnp.zeros(x.shape, F32)
    for sl in _col_slices(wa_ref.shape[1]):
        a = _dot(xn, wa_ref[:, sl])
        b = _dot(xn, wb_ref[:, sl])
        acc = acc + _dot((_silu(a) * b).astype(BF16), wo_ref[sl, :])
    y = x + 0.5 * acc
    if final_norm:
        y = _rms(y, gf_ref[...])
    o_ref[...] = y


def _ffn(h, g, w_in, w_out, g_final=None, mix=None, *, tm=TOKEN_TILE):
    m, d = h.shape
    f = w_out.shape[0]
    final_norm = g_final is not None
    gf = g_final if final_norm else g
    mix_args = list(mix) if mix is not None else []
    mix_specs = [pl.BlockSpec((tm, mix[0].shape[1]), lambda i: (i, 0)), _resident(mix[1].shape)] if mix_args else []
    return pl.pallas_call(
        functools.partial(_ffn_body, final_norm=final_norm, mix=bool(mix_args)),
        grid=(m // tm,),
        in_specs=[pl.BlockSpec((tm, d), lambda i: (i, 0))] + mix_specs + [
            _resident((1, d)),
            _resident((d, f)),
            _resident((d, f), (0, 1)),
            _resident((f, d)),
            _resident((1, d)),
        ],
        out_specs=pl.BlockSpec((tm, d), lambda i: (i, 0)),
        out_shape=jax.ShapeDtypeStruct((m, d), F32),
        compiler_params=_params("parallel"),
        name="ffn",
    )(h, *mix_args, g.reshape(1, d), w_in, w_in, w_out, gf.reshape(1, d))


def _norm_proj_body(h_ref, g_ref, *refs, n_out):
    xn = _rms(h_ref[...], g_ref[...]).astype(BF16)
    for w_ref, o_ref in zip(refs[:n_out], refs[n_out:]):
        for sl in _col_slices(w_ref.shape[1]):
            o_ref[:, sl] = _dot(xn, w_ref[:, sl]).astype(o_ref.dtype)


def _norm_proj(h, g, ws, out_dtypes, *, tm=TOKEN_TILE):
    m, d = h.shape
    return pl.pallas_call(
        functools.partial(_norm_proj_body, n_out=len(ws)),
        grid=(m // tm,),
        in_specs=[pl.BlockSpec((tm, d), lambda i: (i, 0)), pl.BlockSpec((1, d), lambda i: (0, 0))]
        + [_resident(w.shape) for w in ws],
        out_specs=[pl.BlockSpec((tm, w.shape[1]), lambda i: (i, 0)) for w in ws],
        out_shape=[jax.ShapeDtypeStruct((m, w.shape[1]), dt) for w, dt in zip(ws, out_dtypes)],
        compiler_params=_params("parallel"),
        name="norm_proj",
    )(h, g.reshape(1, d), *ws)


def _ret_body(q_ref, k_ref, v_ref, g_ref, cos_ref, sin_ref, gn_ref, o_ref, state_ref, *, ts):
    c_len = RET_CHUNK
    dk = q_ref.shape[2] // RET_HEADS
    dv = v_ref.shape[2] // RET_HEADS
    half = dk // 2

    @pl.when(pl.program_id(1) == 0)
    def _():
        state_ref[...] = jnp.zeros(state_ref.shape, F32)

    ii = lax.broadcasted_iota(jnp.int32, (c_len, c_len), 0)
    jj = lax.broadcasted_iota(jnp.int32, (c_len, c_len), 1)
    rel = (ii - jj).astype(F32)
    causal = ii >= jj
    idx = lax.broadcasted_iota(jnp.int32, (c_len, 1), 0).astype(F32)

    def rot(t, cos, sin):
        t1, t2 = t[:, :half], t[:, half:]
        return jnp.concatenate([t1 * cos - t2 * sin, t1 * sin + t2 * cos], axis=1)

    decays = []
    for h in range(RET_HEADS):
        log_gamma = math.log1p(-(2.0 ** (-5.0 - h)))
        decays.append((jnp.where(causal, jnp.exp(jnp.where(causal, rel, 0.0) * log_gamma), 0.0),
                       jnp.exp((idx + 1.0) * log_gamma),
                       jnp.exp((c_len - 1.0 - idx) * log_gamma),
                       math.exp(c_len * log_gamma)))

    def chunk(c, carry):
        r0 = pl.multiple_of(c * c_len, c_len)
        rows = pl.ds(r0, c_len)
        cos = cos_ref[rows, :]
        sin = sin_ref[rows, :]
        for h in range(RET_HEADS):
            inner, q_decay, k_decay, chunk_decay = decays[h]
            q = rot(q_ref[0, rows, h * dk:(h + 1) * dk].astype(F32), cos, sin)
            k = rot(k_ref[0, rows, h * dk:(h + 1) * dk].astype(F32), cos, sin) * (dk ** -0.5)
            v = v_ref[0, rows, h * dv:(h + 1) * dv]
            qb = q.astype(BF16)
            sc = _dot_nt(qb, k.astype(BF16)) * inner
            st = state_ref[h]
            o = _dot(sc.astype(BF16), v) + _dot(qb, st.astype(BF16)) * q_decay
            kd_t = jnp.transpose(k * k_decay).astype(BF16)
            state_ref[h] = st * chunk_decay + _dot(kd_t, v)
            gn = gn_ref[:, h * dv:(h + 1) * dv]
            on = _rms(o, gn)
            gate = g_ref[0, rows, h * dv:(h + 1) * dv].astype(F32)
            o_ref[0, rows, h * dv:(h + 1) * dv] = (_silu(gate) * on).astype(o_ref.dtype)
        return carry

    lax.fori_loop(0, ts // c_len, chunk, 0)


def _retention_core(proj, cos, sin, gn_gain, *, b, s, ts=TOKEN_TILE):
    n = proj.shape[2]
    hk = n // 6
    hv = 2 * hk
    dk = hk // RET_HEADS
    return pl.pallas_call(
        functools.partial(_ret_body, ts=ts),
        grid=(b, s // ts),
        in_specs=[
            pl.BlockSpec((1, ts, hk), lambda i, j: (i, j, 0)),
            pl.BlockSpec((1, ts, hk), lambda i, j: (i, j, 1)),
            pl.BlockSpec((1, ts, hv), lambda i, j: (i, j, 1)),
            pl.BlockSpec((1, ts, hv), lambda i, j: (i, j, 2)),
            pl.BlockSpec((ts, dk // 2), lambda i, j: (j, 0)),
            pl.BlockSpec((ts, dk // 2), lambda i, j: (j, 0)),
            pl.BlockSpec((1, hv), lambda i, j: (0, 0)),
        ],
        out_specs=pl.BlockSpec((1, ts, hv), lambda i, j: (i, j, 0)),
        out_shape=jax.ShapeDtypeStruct((b, s, hv), BF16),
        scratch_shapes=[pltpu.VMEM((RET_HEADS, dk, hv // RET_HEADS), F32)],
        compiler_params=_params("parallel", "arbitrary"),
        name="retention",
    )(proj, proj, proj, proj, cos, sin, gn_gain.reshape(1, hv))


def _rope_tables(s, half):
    inv = ROPE_BASE ** (-jnp.arange(half, dtype=F32) / half)
    ang = jnp.arange(s, dtype=F32)[:, None] * inv[None, :]
    return jnp.cos(ang), jnp.sin(ang)


def _retention(h2, g_norm, w_in, gn_gain, w_out, *, b, s):
    (proj,) = _norm_proj(h2, g_norm, [w_in.astype(BF16)], [BF16])
    n = proj.shape[1]
    cos, sin = _rope_tables(s, n // 6 // RET_HEADS // 2)
    y = _retention_core(proj.reshape(b, s, n), cos, sin, gn_gain, b=b, s=s)
    return y.reshape(b * s, -1), w_out.astype(BF16)


def _dil_proj_body(h_ref, g_ref, w_ref, o_ref, xn_ref, *, r):
    tm = h_ref.shape[0]
    n = tm // r
    xn = _rms(h_ref[...], g_ref[...])
    slabs = xn_ref.shape[0]
    for k in range(slabs):
        xn_ref[k] = xn[:, k * LANES:(k + 1) * LANES]
    xp = jnp.concatenate(
        [jnp.concatenate([xn_ref[k, pl.ds(c, n, stride=r), :] for c in range(r)], axis=0) for k in range(slabs)],
        axis=1).astype(BF16)
    for sl in _col_slices(w_ref.shape[1]):
        res = _dot(xp, w_ref[:, sl]).astype(o_ref.dtype)
        for c in range(r):
            o_ref[0, c, :, sl] = res[c * n:(c + 1) * n, :]


def _dil_proj(h2, g_norm, w, r, *, b, s, tm=TOKEN_TILE):
    d = h2.shape[1]
    n_out = w.shape[1]
    tiles = s // tm
    return pl.pallas_call(
        functools.partial(_dil_proj_body, r=r),
        grid=(b * tiles,),
        in_specs=[
            pl.BlockSpec((tm, d), lambda i: (i, 0)),
            pl.BlockSpec((1, d), lambda i: (0, 0)),
            _resident((d, n_out)),
        ],
        out_specs=pl.BlockSpec((1, r, tm // r, n_out), lambda i: (i // tiles, 0, i % tiles, 0)),
        out_shape=jax.ShapeDtypeStruct((b, r, s // r, n_out), BF16),
        scratch_shapes=[pltpu.VMEM((d // LANES, tm, LANES), F32)],
        compiler_params=_params("parallel"),
        name=f"dilated_proj_r{r}",
    )(h2, g_norm.reshape(1, d), w)


def _dil_body(q_ref, kc_ref, kp_ref, vc_ref, vp_ref, o_ref, lse_ref, kbuf, vbuf, *, rows, n_back):
    qb = DIL_QB
    dh = DIL_DH
    first_step = pl.program_id(2) == 0
    kbuf[0:qb, :] = kp_ref[0, 0]
    kbuf[qb:, :] = kc_ref[0, 0]
    vbuf[0:qb, :] = vp_ref[0, 0]
    vbuf[qb:, :] = vc_ref[0, 0]
    qi = lax.broadcasted_iota(jnp.int32, (qb, 2 * qb), 0)
    kj = lax.broadcasted_iota(jnp.int32, (qb, 2 * qb), 1)
    dist = qi + qb - kj
    band = (dist >= 0) & (dist <= n_back)
    lane = lax.broadcasted_iota(jnp.int32, (qb, LANES), 1)
    scale = dh ** -0.5
    ones = jnp.ones((2 * qb, dh), BF16)
    for i in range(rows // qb):
        mask = band & ((kj >= qb) | jnp.logical_not(first_step)) if i == 0 else band
        lse_tile = jnp.zeros((qb, LANES), F32)
        for h in range(DIL_HEADS):
            cols = slice(h * dh, (h + 1) * dh)
            q = q_ref[0, 0, i * qb:(i + 1) * qb, cols]
            k = kbuf[i * qb:(i + 2) * qb, cols]
            v = vbuf[i * qb:(i + 2) * qb, cols]
            s = jnp.where(mask, _dot_nt(q, k), NEG_INF)
            m = jnp.max(s, axis=1, keepdims=True)
            e = jnp.exp2((s - m) * (scale * math.log2(math.e))).astype(BF16)
            acc = _dot(e, jnp.concatenate([v, ones], axis=1))
            den = jnp.maximum(acc[:, dh:], 1e-30)
            o_ref[0, 0, i * qb:(i + 1) * qb, cols] = (acc[:, :dh] / den).astype(o_ref.dtype)
            lse_tile = jnp.where(lane == h, m * scale + jnp.log(den), lse_tile)
        lse_ref[0, 0, i * qb:(i + 1) * qb, :] = lse_tile


def _dilated_group(proj, win, r, *, b, s):
    hd = DIL_HEADS * DIL_DH
    length = s // r
    rows = min(length, TOKEN_TILE)
    qb = DIL_QB

    def cur(which):
        return pl.BlockSpec((1, 1, rows, hd), lambda i, c, n: (i, c, n, which))

    def prev(which):
        return pl.BlockSpec((1, 1, qb, hd), lambda i, c, n: (i, c, jnp.maximum(n * (rows // qb) - 1, 0), which))

    return pl.pallas_call(
        functools.partial(_dil_body, rows=rows, n_back=win // r),
        grid=(b, r, length // rows),
        in_specs=[cur(0), cur(1), prev(1), cur(2), prev(2)],
        out_specs=[
            pl.BlockSpec((1, 1, rows, hd), lambda i, c, n: (i, c, n, 0)),
            pl.BlockSpec((1, 1, rows, LANES), lambda i, c, n: (i, c, n, 0)),
        ],
        out_shape=[
            jax.ShapeDtypeStruct((b, r, length, hd), BF16),
            jax.ShapeDtypeStruct((b, r, length, LANES), F32),
        ],
        scratch_shapes=[pltpu.VMEM((rows + qb, hd), BF16), pltpu.VMEM((rows + qb, hd), BF16)],
        compiler_params=_params("parallel", "parallel", "arbitrary"),
        name=f"dilated_r{r}",
    )(proj, proj, proj, proj, proj)


def _dil_merge_body(h_ref, *refs, dilations):
    ng = len(dilations)
    o_refs, l_refs = refs[:ng], refs[ng:2 * ng]
    w_ref, out_ref = refs[2 * ng], refs[2 * ng + 1]
    o_bufs, l_bufs = refs[2 * ng + 2:3 * ng + 2], refs[3 * ng + 2:]
    tm = h_ref.shape[1]
    for o_ref, l_ref, o_buf, l_buf, r in zip(o_refs, l_refs, o_bufs, l_bufs, dilations):
        n = tm // r
        for c in range(r):
            rows = pl.ds(c, n, stride=r)
            l_buf[rows, :] = l_ref[0, c]
            blk = o_ref[0, c].astype(F32)
            for h in range(DIL_HEADS):
                o_buf[h, rows, :] = blk[:, h * DIL_DH:(h + 1) * DIL_DH]
    lses = [l[...] for l in l_bufs]
    mx = functools.reduce(jnp.maximum, lses)
    es = [jnp.exp(l - mx) for l in lses]
    tot = functools.reduce(jnp.add, es)
    wts = [e / tot for e in es]
    dh = DIL_DH
    parts = []
    for h in range(DIL_HEADS):
        acc = jnp.zeros((tm, dh), F32)
        for g in range(ng):
            wg = jnp.broadcast_to(wts[g][:, h:h + 1], (tm, dh))
            acc = acc + wg * o_bufs[g][h]
        parts.append(acc)
    o = jnp.concatenate(parts, axis=1).astype(BF16)
    out_ref[0] = h_ref[0] + _dot(o, w_ref[...])


def _dil_merge(h3, outs, lses, w, dilations, *, tm=TOKEN_TILE):
    b, s, d = h3.shape
    hd = w.shape[0]
    res_major = lambda r, width: pl.BlockSpec((1, r, tm // r, width), lambda i, n: (i, 0, n, 0))
    return pl.pallas_call(
        functools.partial(_dil_merge_body, dilations=tuple(dilations)),
        grid=(b, s // tm),
        in_specs=[pl.BlockSpec((1, tm, d), lambda i, n: (i, n, 0))]
        + [res_major(r, hd) for r in dilations] + [res_major(r, LANES) for r in dilations]
        + [pl.BlockSpec((hd, d), lambda i, n: (0, 0))],
        out_specs=pl.BlockSpec((1, tm, d), lambda i, n: (i, n, 0)),
        out_shape=jax.ShapeDtypeStruct((b, s, d), F32),
        scratch_shapes=[pltpu.VMEM((DIL_HEADS, tm, DIL_DH), F32) for _ in dilations]
        + [pltpu.VMEM((tm, LANES), F32) for _ in dilations],
        compiler_params=_params("parallel", "parallel"),
        name="dilated_merge",
    )(h3, *outs, *lses, w)


def _dilated(h2, g_norm, w_in, w_out, *, b, s):
    d = h2.shape[1]
    gw = 3 * DIL_HEADS * DIL_DH
    wb = w_in.astype(BF16)
    outs, lses, dilations = [], [], []
    for g, (win, r) in enumerate(DIL_PATTERN):
        proj = _dil_proj(h2, g_norm, wb[:, g * gw:(g + 1) * gw], r, b=b, s=s, tm=2 * TOKEN_TILE)
        o, lse = _dilated_group(proj, win, r, b=b, s=s)
        outs.append(o)
        lses.append(lse)
        dilations.append(r)
    return _dil_merge(h2.reshape(b, s, d), outs, lses, w_out.astype(BF16), dilations).reshape(b * s, d)


def _softplus(x):
    return jnp.maximum(x, 0.0) + jnp.log1p(jnp.exp(-jnp.abs(x)))


def _ssd_body(z_ref, x_ref, dt_ref, cw_ref, cb_ref, dtb_ref, alog_ref, dsk_ref, ng_ref, o_ref,
              xpad, state_ref):
    ln = SSD_CHUNK
    nst = SSD_STATE
    gw = SSD_HPG * SSD_HEADDIM
    d_inner = SSD_GROUPS * gw
    halo = 8

    @pl.when(pl.program_id(1) == 0)
    def _():
        xpad[0:halo, :] = jnp.zeros((halo, xpad.shape[1]), F32)
        state_ref[...] = jnp.zeros(state_ref.shape, F32)

    x_b = x_ref[0]
    x_f = x_b.astype(F32)
    ii = lax.broadcasted_iota(jnp.int32, (ln, ln), 0)
    jj = lax.broadcasted_iota(jnp.int32, (ln, ln), 1)
    conv = cb_ref[...] + x_f * cw_ref[SSD_CONV - 1:SSD_CONV, :]
    for shift in range(1, SSD_CONV):
        shifted = _dot(jnp.where(ii - jj == shift, 1.0, 0.0).astype(BF16), x_b)
        conv = conv + shifted * cw_ref[SSD_CONV - 1 - shift:SSD_CONV - shift, :]
    xpad[halo:2 * halo, :] = x_f[0:halo, :]
    head = cb_ref[...]
    for k in range(SSD_CONV):
        off = halo - (SSD_CONV - 1) + k
        head = head + xpad[off:off + halo, :] * cw_ref[k:k + 1, :]
    xpad[0:halo, :] = x_f[ln - halo:ln, :]
    xbc = _silu(jnp.concatenate([head, conv[halo:, :]], axis=0))
    xs = xbc[:, :d_inner]
    bm = xbc[:, d_inner:d_inner + SSD_GROUPS * nst]
    cm = xbc[:, d_inner + SSD_GROUPS * nst:]

    dt = _softplus(dt_ref[0] + dtb_ref[...])
    da = dt * (-jnp.exp(alog_ref[...]))
    ii = lax.broadcasted_iota(jnp.int32, (ln, ln), 0)
    jj = lax.broadcasted_iota(jnp.int32, (ln, ln), 1)
    causal = ii >= jj
    tril = jnp.where(causal, 1.0, 0.0).astype(BF16)
    da_terms = _split3(da)
    acs = _dot(tril, da_terms[0]) + _dot(tril, da_terms[1]) + _dot(tril, da_terms[2])
    acs2 = acs * math.log2(math.e)
    acs2_t = jnp.transpose(acs2)
    hl = ln // 2

    erow = lax.broadcasted_iota(jnp.int32, (LANES, d_inner), 0)
    ecol = lax.broadcasted_iota(jnp.int32, (LANES, d_inner), 1)
    expand = jnp.where(ecol // SSD_HEADDIM == erow, 1.0, 0.0).astype(BF16)
    acs_e = _dot3(_split3(acs), expand)
    dt_e = _dot3(_split3(dt), expand)
    last = acs_e[ln - 1:ln, :]
    decay_in = jnp.exp(acs_e)
    xs_dt = xs * dt_e
    xs_end = (xs_dt * jnp.exp(last - acs_e)).astype(BF16)
    xs_b = xs_dt.astype(BF16)
    lane = lax.broadcasted_iota(jnp.int32, (ln, LANES), 1)

    y_groups = []
    for g in range(SSD_GROUPS):
        bm_g = bm[:, g * nst:(g + 1) * nst]
        cm_g = cm[:, g * nst:(g + 1) * nst].astype(BF16)
        cb = jnp.where(causal, _dot_nt(cm_g, bm_g.astype(BF16)), 0.0)
        st = state_ref[g]
        gcols = slice(g * gw, (g + 1) * gw)
        y_state = _dot(cm_g, st.astype(BF16)) * decay_in[:, gcols]
        pairs = []
        for p in range(SSD_HPG // 2):
            pair_cols = slice(g * gw + p * LANES, g * gw + (p + 1) * LANES)
            halves = []
            for e in range(2):
                hd = g * SSD_HPG + 2 * p + e
                top = cb[:hl, :hl] * jnp.exp2(jnp.minimum(acs2[:hl, hd:hd + 1] - acs2_t[hd:hd + 1, :hl], 0.0))
                bot = cb[hl:, :] * jnp.exp2(jnp.minimum(acs2[hl:, hd:hd + 1] - acs2_t[hd:hd + 1, :], 0.0))
                halves.append(jnp.concatenate([_dot(top.astype(BF16), xs_b[:hl, pair_cols]),
                                               _dot(bot.astype(BF16), xs_b[:, pair_cols])], axis=0))
            pairs.append(jnp.where(lane < SSD_HEADDIM, halves[0], halves[1]))
        y_groups.append(jnp.concatenate(pairs, axis=1) + y_state)
        bm_t = jnp.transpose(bm_g).astype(BF16)
        state_ref[g] = st * jnp.exp(last[:, gcols]) + _dot(bm_t, xs_end[:, gcols])

    y = jnp.concatenate(y_groups, axis=1) + dsk_ref[...] * xs
    yz = y * _silu(z_ref[0].astype(F32))
    outs = [_rms(yz[:, g * gw:(g + 1) * gw], ng_ref[:, g * gw:(g + 1) * gw]) for g in range(SSD_GROUPS)]
    o_ref[0] = jnp.concatenate(outs, axis=1).astype(o_ref.dtype)


def _ssd_core(z, xbc, dt, conv_w, conv_b, dt_bias, a_log, d_skip, norm_g, *, b, s):
    ln = SSD_CHUNK
    d_inner = z.shape[2]
    conv_dim = xbc.shape[2]
    gw = SSD_HPG * SSD_HEADDIM
    full = lambda shape: pl.BlockSpec(shape, lambda i, j: (0,) * len(shape))
    return pl.pallas_call(
        _ssd_body,
        grid=(b, s // ln),
        in_specs=[
            pl.BlockSpec((1, ln, d_inner), lambda i, j: (i, j, 0)),
            pl.BlockSpec((1, ln, conv_dim), lambda i, j: (i, j, 0)),
            pl.BlockSpec((1, ln, LANES), lambda i, j: (i, j, 0)),
            full((SSD_CONV, conv_dim)),
            full((1, conv_dim)),
            full((1, LANES)),
            full((1, LANES)),
            full((1, d_inner)),
            full((1, d_inner)),
        ],
        out_specs=pl.BlockSpec((1, ln, d_inner), lambda i, j: (i, j, 0)),
        out_shape=jax.ShapeDtypeStruct((b, s, d_inner), BF16),
        scratch_shapes=[pltpu.VMEM((16, conv_dim), F32),
                        pltpu.VMEM((SSD_GROUPS, SSD_STATE, gw), F32)],
        compiler_params=_params("parallel", "arbitrary"),
        name="ssd",
    )(z, xbc, dt, conv_w, conv_b, dt_bias, a_log, d_skip, norm_g)


def _ssd(h2, g_norm, w_in, conv_w, conv_b, dt_bias, a_log, d_skip, norm_g, w_out, *, b, s):
    heads = SSD_GROUPS * SSD_HPG
    d_inner = heads * SSD_HEADDIM
    conv_dim = conv_w.shape[1]
    wb = w_in.astype(BF16)
    w_dt = jnp.pad(wb[:, d_inner + conv_dim:], ((0, 0), (0, LANES - heads)))
    z, xbc, dt = _norm_proj(h2, g_norm, [wb[:, :d_inner], wb[:, d_inner:d_inner + conv_dim], w_dt], [BF16, BF16, F32])
    pad_heads = lambda v: jnp.pad(v.astype(F32), (0, LANES - heads)).reshape(1, LANES)
    y = _ssd_core(z.reshape(b, s, -1), xbc.reshape(b, s, -1), dt.reshape(b, s, LANES),
                  conv_w.astype(F32), conv_b.reshape(1, -1).astype(F32), pad_heads(dt_bias), pad_heads(a_log),
                  jnp.repeat(d_skip.astype(F32), SSD_HEADDIM).reshape(1, d_inner), norm_g.reshape(1, d_inner),
                  b=b, s=s)
    return y.reshape(b * s, d_inner), w_out.astype(BF16)


def _nsa_proj_body(h_ref, g_ref, wn_ref, wt_ref, k0_ref, v0_ref, k12_ref, tr_ref):
    xn = _rms(h_ref[...], g_ref[...]).astype(BF16)
    nat = _dot(xn, wn_ref[...])
    gd = k0_ref.shape[1]
    k0_ref[...] = nat[:, :gd].astype(k0_ref.dtype)
    v0_ref[...] = nat[:, gd:2 * gd].astype(v0_ref.dtype)
    k12_ref[...] = nat[:, 2 * gd:].astype(k12_ref.dtype)
    res = _dot_nt(wt_ref[...], xn)
    for j in range(tr_ref.shape[0]):
        tr_ref[j] = res[:, j * NSA_QT:(j + 1) * NSA_QT].astype(tr_ref.dtype)


def _nsa_proj(h2, g_norm, w_nat, w_t, *, tm=TOKEN_TILE):
    m, d = h2.shape
    gd = NSA_GROUPS * NSA_DH
    nt = w_t.shape[0]
    slabs = tm // NSA_QT
    return pl.pallas_call(
        _nsa_proj_body,
        grid=(m // tm,),
        in_specs=[
            pl.BlockSpec((tm, d), lambda i: (i, 0)),
            pl.BlockSpec((1, d), lambda i: (0, 0)),
            _resident(w_nat.shape),
            _resident(w_t.shape),
        ],
        out_specs=[
            pl.BlockSpec((tm, gd), lambda i: (i, 0)),
            pl.BlockSpec((tm, gd), lambda i: (i, 0)),
            pl.BlockSpec((tm, 2 * gd), lambda i: (i, 0)),
            pl.BlockSpec((slabs, nt, NSA_QT), lambda i: (i, 0, 0)),
        ],
        out_shape=[
            jax.ShapeDtypeStruct((m, gd), BF16),
            jax.ShapeDtypeStruct((m, gd), BF16),
            jax.ShapeDtypeStruct((m, 2 * gd), BF16),
            jax.ShapeDtypeStruct((m // NSA_QT, nt, NSA_QT), BF16),
        ],
        compiler_params=_params("parallel"),
        name="nsa_proj",
    )(h2, g_norm.reshape(1, d), w_nat, w_t)


def _nsa_cmp_body(x_ref, pa_ref, pb_ref, w1a_ref, w1b_ref, w2_ref, o_ref, *, transposed):
    x = x_ref[0].astype(F32)
    nb = x.shape[0]
    ya = _dot((x + pa_ref[...]).astype(BF16), w1a_ref[...])
    yb = _dot((x + pb_ref[...]).astype(BF16), w1b_ref[...])
    hid = _silu(ya + pltpu.roll(yb, nb - 1, 0))
    out = _dot(hid.astype(BF16), w2_ref[...])
    o_ref[0] = (jnp.transpose(out) if transposed else out).astype(o_ref.dtype)


def _nsa_compress(x, pos, w1, w2, *, transposed):
    b, nb, width = x.shape
    g, dh = NSA_GROUPS, NSA_DH
    half = NSA_CMP_LEN // 2
    hid = w1.shape[1]
    eye = jnp.eye(g, dtype=F32)
    w1r = w1.reshape(NSA_CMP_LEN, dh, hid)
    big = jnp.einsum("ldj,gh->lgdhj", w1r, eye).reshape(NSA_CMP_LEN, g * dh, g * hid)
    w1a = big[:half].reshape(half * g * dh, g * hid).astype(BF16)
    w1b = big[half:].reshape(half * g * dh, g * hid).astype(BF16)
    w2big = jnp.einsum("jd,gh->gjhd", w2, eye).reshape(g * hid, g * dh).astype(BF16)
    posb = jnp.broadcast_to(pos[:, None, :], (NSA_CMP_LEN, g, dh)).reshape(NSA_CMP_LEN, g * dh)
    pa = posb[:half].reshape(1, width).astype(F32)
    pb = posb[half:].reshape(1, width).astype(F32)
    full = lambda shape: pl.BlockSpec(shape, lambda i: (0,) * len(shape))
    out_dims = (g * dh, nb) if transposed else (nb, g * dh)
    return pl.pallas_call(
        functools.partial(_nsa_cmp_body, transposed=transposed),
        grid=(b,),
        in_specs=[
            pl.BlockSpec((1, nb, width), lambda i: (i, 0, 0)),
            full((1, width)), full((1, width)),
            full((width, g * hid)), full((width, g * hid)), full((g * hid, g * dh)),
        ],
        out_specs=pl.BlockSpec((1,) + out_dims, lambda i: (i, 0, 0)),
        out_shape=jax.ShapeDtypeStruct((b,) + out_dims, BF16),
        compiler_params=_params("parallel"),
        name="nsa_compress",
    )(x, pa, pb, w1a, w1b, w2big)


def _nsa_pair_body(q_ref, gt_ref, kc_ref, vct_ref, k1_ref, k2_ref, v1_ref, v2_ref, o_ref, bias_ref, wbias_ref, *, s):
    qt, kc, dh = NSA_QT, NSA_KC, NSA_DH
    gl = 2
    glanes = NSA_HPG * qt
    lanes = gl * glanes
    vrows = gl * dh
    j = pl.program_id(2)
    t0 = j * qt
    nb = kc_ref.shape[1]
    nsb = s // NSA_SEL_LEN
    n_sel = min(NSA_N_SEL, nsb)

    q_t = q_ref[0, 0]
    qcat = [jnp.concatenate([q_t[(g * NSA_HPG + h) * dh:(g * NSA_HPG + h + 1) * dh, :] for h in range(NSA_HPG)],
                            axis=1) for g in range(gl)]
    zero = jnp.zeros_like(qcat[0])
    qp = jnp.concatenate([jnp.concatenate([qcat[0], zero], axis=1),
                          jnp.concatenate([zero, qcat[1]], axis=1)], axis=0)
    qlane = lax.broadcasted_iota(jnp.int32, (1, lanes), 1) % qt
    tq = t0 + qlane

    def values(v_ref, slab0, n_slabs):
        v = jnp.concatenate([v_ref[0, slab0 + i] for i in range(n_slabs)], axis=1)
        return jnp.concatenate([v, jnp.ones((16, n_slabs * qt), BF16)], axis=0)

    def weighted_values(v_aug, p):
        return jnp.concatenate(
            [_dot(jnp.concatenate([v_aug[g * dh:(g + 1) * dh, :], v_aug[vrows:, :]], axis=0),
                  p[:, g * glanes:(g + 1) * glanes]) for g in range(gl)], axis=1)

    def normalise(acc):
        o = acc[:dh, :] * (1.0 / jnp.maximum(acc[dh:dh + 1, :], 1e-30))
        return [o[:, g * glanes:(g + 1) * glanes] for g in range(gl)]

    wslabs = NSA_WINDOW // qt + 1
    row_w = lax.broadcasted_iota(jnp.int32, (qt, lanes), 0)

    @pl.when(j == 0)
    def _():
        wbias_ref[0] = jnp.where(row_w > qlane, 0.0, NEG_INF)
        wbias_ref[1] = jnp.zeros((qt, lanes), F32)
        wbias_ref[2] = jnp.where(row_w <= qlane, 0.0, NEG_INF)
        wbias_ref[3] = jnp.full((qt, lanes), NEG_INF, F32)

    jb0 = jnp.maximum(j + 1 - wslabs, 0)
    kw = k2_ref[0, pl.ds(pl.multiple_of(jb0 * qt, qt), wslabs * qt), :]
    sw_raw = _dot(kw, qp)
    slabs_back = [j - jb0 - i for i in range(wslabs)]
    sw = jnp.concatenate(
        [sw_raw[i * qt:(i + 1) * qt, :]
         + wbias_ref[jnp.where(back < 0, 3, jnp.where(back == wslabs - 1, 0, jnp.where(back == 0, 2, 1)))]
         for i, back in enumerate(slabs_back)], axis=0)

    sc = _dot(kc_ref[0], qp)
    assert NSA_CMP_STRIDE == 16
    last_visible = jnp.minimum(jnp.right_shift(tq - (NSA_CMP_LEN - 1), 4), nb - 2)
    sc = jnp.where(lax.broadcasted_iota(jnp.int32, (nb, lanes), 0) <= last_visible, sc, NEG_INF)
    m_c = jnp.max(sc, axis=0, keepdims=True)
    e = jnp.exp(sc - jnp.where(m_c > 0.5 * NEG_INF, m_c, 0.0))
    p_c = e * (1.0 / jnp.maximum(jnp.sum(e, axis=0, keepdims=True), 1e-30))
    acc_c = _dot(vct_ref[0], p_c.astype(BF16))
    o_c = [acc_c[g * dh:(g + 1) * dh, g * glanes:(g + 1) * glanes] for g in range(gl)]
    imps = []
    for g in range(gl):
        imp = p_c[:, g * glanes:g * glanes + qt]
        for h in range(1, NSA_HPG):
            imp = imp + p_c[:, g * glanes + h * qt:g * glanes + (h + 1) * qt]
        imps.append(imp)
    imp = jnp.concatenate(imps, axis=1)

    ratio = NSA_SEL_LEN // NSA_CMP_STRIDE
    dd = lax.broadcasted_iota(jnp.int32, (nsb, nb), 1) - ratio * lax.broadcasted_iota(jnp.int32, (nsb, nb), 0)
    wsel = jnp.where((dd == -1) | (dd == ratio - 1), 1.0,
                     jnp.where((dd >= 0) & (dd < ratio - 1), 2.0, 0.0)).astype(BF16)
    terms = _split3(imp)
    imp_sel = _dot(wsel, terms[0]) + _dot(wsel, terms[1]) + _dot(wsel, terms[2])
    jq = lax.broadcasted_iota(jnp.int32, (nsb, gl * qt), 0)
    cur = (t0 + lax.broadcasted_iota(jnp.int32, (nsb, gl * qt), 1) % qt) // NSA_SEL_LEN
    forced = (jq == 0) | (jq == cur) | (jq == cur - 1)
    score = jnp.where(jq <= cur, imp_sel + jnp.where(forced, NSA_FORCE_BONUS, 0.0), NEG_INF)

    pw = jnp.exp(sw - jnp.max(sw, axis=0, keepdims=True)).astype(BF16)
    o_w = normalise(weighted_values(values(v2_ref, jb0, wslabs), pw))

    jqf = jq.astype(F32)
    work = score
    picked = None
    for _ in range(n_sel):
        top = jnp.max(work, axis=0, keepdims=True)
        first = jnp.min(jnp.where(work == top, jqf, float(nsb)), axis=0, keepdims=True)
        pick = jqf == first
        picked = pick if picked is None else picked | pick
        work = jnp.where(pick, 3.0 * NEG_INF, work)
    sel_bias = jnp.where(picked & (score > 0.5 * NEG_INF), 0.0, NEG_INF)
    bias_ref[...] = jnp.concatenate([sel_bias[:, g * qt:(g + 1) * qt] for g in range(gl) for _ in range(NSA_HPG)],
                                    axis=1)

    blocks_per_chunk = kc // NSA_SEL_LEN
    slabs_per_chunk = kc // qt

    def online(carry, st, bias, v_aug):
        m, acc = carry
        blks = [st[r * NSA_SEL_LEN:(r + 1) * NSA_SEL_LEN, :] for r in range(blocks_per_chunk)]
        part = None
        for r in range(blocks_per_chunk):
            pr = blks[r][0:8, :]
            for i in range(1, NSA_SEL_LEN // 8):
                pr = jnp.maximum(pr, blks[r][8 * i:8 * (i + 1), :])
            pr = pr + bias[r:r + 1, :]
            part = pr if part is None else jnp.maximum(part, pr)
        m_new = jnp.maximum(m, jnp.max(part, axis=0, keepdims=True))
        p = jnp.concatenate([jnp.exp(blks[r] - (m_new - bias[r:r + 1, :])) for r in range(blocks_per_chunk)],
                            axis=0).astype(BF16)
        return m_new, jnp.exp(m - m_new) * acc + weighted_values(v_aug, p)

    def sel_scores(c):
        return _dot(k1_ref[0, pl.ds(pl.multiple_of(c * kc, kc), kc), :], qp)

    def sel_bias(c):
        return bias_ref[pl.ds(pl.multiple_of(c * blocks_per_chunk, blocks_per_chunk), blocks_per_chunk), :]

    def sel_values(c):
        return values(v1_ref, c * slabs_per_chunk, slabs_per_chunk)

    init = (jnp.full((1, lanes), NEG_INF, F32), jnp.zeros((dh + 16, lanes), F32))
    c_hi = t0 // kc
    own_slab = (t0 - c_hi * kc) // qt
    st_raw = sel_scores(c_hi)
    st = jnp.concatenate(
        [st_raw[i * qt:(i + 1) * qt, :] + wbias_ref[jnp.where(i < own_slab, 1, jnp.where(i == own_slab, 2, 3))]
         for i in range(slabs_per_chunk)], axis=0)
    chain_a = online(init, st, sel_bias(c_hi), sel_values(c_hi))
    chain_a = lax.cond(c_hi % 2 == 1,
                       lambda cr: online(cr, sel_scores(c_hi - 1), sel_bias(c_hi - 1), sel_values(c_hi - 1)),
                       lambda cr: cr, chain_a)
    chain_b = (chain_a[0], jnp.zeros_like(chain_a[1]))

    def pair(i, chains):
        ca, cb = chains
        st_a = sel_scores(2 * i)
        st_b = sel_scores(2 * i + 1)
        return (online(ca, st_a, sel_bias(2 * i), sel_values(2 * i)),
                online(cb, st_b, sel_bias(2 * i + 1), sel_values(2 * i + 1)))

    (m_a, acc_a), (m_b, acc_b) = lax.fori_loop(0, c_hi // 2, pair, (chain_a, chain_b))
    m_s = jnp.maximum(m_a, m_b)
    o_s = normalise(jnp.exp(m_a - m_s) * acc_a + jnp.exp(m_b - m_s) * acc_b)

    gates = jax.nn.sigmoid(gt_ref[0, 0].astype(F32))
    outs = []
    for g in range(gl):
        def gate_row(br):
            r0 = (g * 4 + br) * NSA_HPG
            return jnp.concatenate([gates[r0 + h:r0 + h + 1, :] for h in range(NSA_HPG)], axis=1)
        o = gate_row(0) * o_c[g] + gate_row(1) * o_s[g] + gate_row(2) * o_w[g]
        outs.extend(o[:, h * qt:(h + 1) * qt] for h in range(NSA_HPG))
    o_ref[0] = jnp.transpose(jnp.concatenate(outs, axis=0)).astype(o_ref.dtype)


def _nsa_attend_pairs(tr, k_cmp, v_cmp_t, k12, *, b, s):
    qt, dh = NSA_QT, NSA_DH
    gl = 2
    gd = NSA_GROUPS * dh
    hq = NSA_GROUPS * NSA_HPG * dh
    pairs = NSA_GROUPS // gl
    nslab = s // qt
    assert nslab > NSA_WINDOW // qt
    nb = k_cmp.shape[1]
    tr4 = tr.reshape(b, nslab, tr.shape[1], qt)
    q_rows = gl * NSA_HPG * dh
    v_rows = gl * dh
    gate_rows = gl * 4 * NSA_HPG
    v1_blk = hq // v_rows
    v2_blk = (hq + gd) // v_rows
    gate_blk = (hq + 2 * gd) // gate_rows
    nsb = s // NSA_SEL_LEN
    return pl.pallas_call(
        functools.partial(_nsa_pair_body, s=s),
        grid=(b, pairs, nslab),
        in_specs=[
            pl.BlockSpec((1, 1, q_rows, qt), lambda i, p, j: (i, j, p, 0)),
            pl.BlockSpec((1, 1, gate_rows, qt), lambda i, p, j: (i, j, gate_blk + p, 0)),
            pl.BlockSpec((1, nb, v_rows), lambda i, p, j: (i, 0, p)),
            pl.BlockSpec((1, v_rows, nb), lambda i, p, j: (i, p, 0)),
            pl.BlockSpec((1, s, v_rows), lambda i, p, j: (i, 0, p)),
            pl.BlockSpec((1, s, v_rows), lambda i, p, j: (i, 0, pairs + p)),
            pl.BlockSpec((1, nslab, v_rows, qt), lambda i, p, j: (i, 0, v1_blk + p, 0)),
            pl.BlockSpec((1, nslab, v_rows, qt), lambda i, p, j: (i, 0, v2_blk + p, 0)),
        ],
        out_specs=pl.BlockSpec((1, qt, q_rows), lambda i, p, j: (i, j, p)),
        out_shape=jax.ShapeDtypeStruct((b, s, hq), BF16),
        scratch_shapes=[pltpu.VMEM((nsb, gl * NSA_HPG * qt), F32), pltpu.VMEM((4, qt, gl * NSA_HPG * qt), F32)],
        compiler_params=_params("parallel", "parallel", "arbitrary"),
        name="nsa_attend",
    )(tr4, tr4, k_cmp, v_cmp_t, k12, k12, tr4, tr4)


def _nsa(h2, g_norm, w_in, cmp_pos, cmp_w1, cmp_w2, w_out, *, b, s):
    g, hpg, dh = NSA_GROUPS, NSA_HPG, NSA_DH
    hq, gd = g * hpg * dh, g * dh
    kv_w = lambda br, kv: w_in[:, hq + (2 * br + kv) * gd:hq + (2 * br + kv + 1) * gd]
    w_q = w_in[:, :hq] * (dh ** -0.5)
    w_g = w_in[:, hq + 6 * gd:].reshape(-1, 3, g, hpg).transpose(2, 1, 3, 0)
    w_g = jnp.pad(w_g, ((0, 0), (0, 1), (0, 0), (0, 0))).reshape(g * 4 * hpg, -1)
    w_t = jnp.concatenate([w_q.T, kv_w(1, 1).T, kv_w(2, 1).T, w_g], axis=0).astype(BF16)
    w_nat = jnp.concatenate([kv_w(0, 0), kv_w(0, 1), kv_w(1, 0), kv_w(2, 0)], axis=1).astype(BF16)
    k0, v0, k12, tr = _nsa_proj(h2, g_norm, w_nat, w_t)
    nb = s // NSA_CMP_STRIDE
    k_cmp = _nsa_compress(k0.reshape(b, nb, NSA_CMP_STRIDE * gd), cmp_pos[0], cmp_w1[0], cmp_w2[0], transposed=False)
    v_cmp_t = _nsa_compress(v0.reshape(b, nb, NSA_CMP_STRIDE * gd), cmp_pos[1], cmp_w1[1], cmp_w2[1], transposed=True)
    y = _nsa_attend_pairs(tr, k_cmp, v_cmp_t, k12.reshape(b, s, 2 * gd), b=b, s=s)
    return y.reshape(b * s, hq), w_out.astype(BF16)


def kernel(x, norm_ffn1, ffn1_w_in, ffn1_w_out, norm_mix, norm_ffn2, ffn2_w_in, ffn2_w_out, norm_final,
           ret_w_in, ret_gn_gain, ret_w_out,
           nsa_w_in, nsa_cmp_pos, nsa_cmp_w1, nsa_cmp_w2, nsa_w_out,
           ssd_w_in, ssd_conv_w, ssd_conv_b, ssd_dt_bias, ssd_a_log, ssd_d, ssd_norm, ssd_w_out,
           dil_w_in, dil_w_out):
    b, s, d = x.shape
    depth = norm_mix.shape[0]
    h = x.reshape(b * s, d)
    for i in range(depth):
        h = _ffn(h, norm_ffn1[i], ffn1_w_in[i].astype(BF16), ffn1_w_out[i].astype(BF16))
        m, j = i % 4, i // 4
        mix = None
        if m == 0:
            mix = _retention(h, norm_mix[i], ret_w_in[j], ret_gn_gain[j], ret_w_out[j], b=b, s=s)
        elif m == 1:
            mix = _nsa(h, norm_mix[i], nsa_w_in[j], nsa_cmp_pos[j], nsa_cmp_w1[j], nsa_cmp_w2[j], nsa_w_out[j], b=b, s=s)
        elif m == 2:
            mix = _ssd(h, norm_mix[i], ssd_w_in[j], ssd_conv_w[j], ssd_conv_b[j], ssd_dt_bias[j], ssd_a_log[j],
                       ssd_d[j], ssd_norm[j], ssd_w_out[j], b=b, s=s)
        else:
            h = _dilated(h, norm_mix[i], dil_w_in[j], dil_w_out[j], b=b, s=s)
        h = _ffn(h, norm_ffn2[i], ffn2_w_in[i].astype(BF16), ffn2_w_out[i].astype(BF16),
                 norm_final if i == depth - 1 else None, mix)
    return h.reshape(b, s, d)
```

```python
import functools
import math

import jax
import jax.numpy as jnp
from jax import lax
from jax.experimental import pallas as pl
from jax.experimental.pallas import tpu as pltpu

F32 = jnp.float32
BF16 = jnp.bfloat16
NORM_EPS = 1e-6
NEG_INF = -1e30
ROPE_BASE = 10000.0
VMEM_LIMIT_BYTES = 56 * 1024 * 1024
LANES = 128
MXU_WIDTH = 256
TOKEN_TILE = 512
MAX_DOT_COLS = 2048

RET_HEADS = 4
RET_CHUNK = 128

NSA_GROUPS = 4
NSA_HPG = 4
NSA_DH = 64
NSA_CMP_LEN = 32
NSA_CMP_STRIDE = 16
NSA_SEL_LEN = 64
NSA_N_SEL = 16
NSA_WINDOW = 512
NSA_FORCE_BONUS = 1e4
NSA_QT = 256
NSA_KC = 512

SSD_GROUPS = 4
SSD_HPG = 8
SSD_HEADDIM = 64
SSD_STATE = 128
SSD_CONV = 4
SSD_CHUNK = 256

DIL_PATTERN = ((128, 1), (512, 4), (2048, 16))
DIL_HEADS = 8
DIL_DH = 128
DIL_QB = 128


def _params(*sem):
    return pltpu.CompilerParams(dimension_semantics=sem, vmem_limit_bytes=VMEM_LIMIT_BYTES)


def _rms(x, g):
    return x * lax.rsqrt(jnp.mean(x * x, axis=-1, keepdims=True) + NORM_EPS) * g


def _silu(x):
    hx = 0.5 * x
    return hx + hx * jnp.tanh(hx)


def _dot(a, b):
    return jnp.dot(a, b, preferred_element_type=F32)


def _dot_nt(a, b):
    return lax.dot_general(a, b, (((1,), (1,)), ((), ())), preferred_element_type=F32)


def _split3(x):
    hi = x.astype(BF16)
    r1 = x - hi.astype(F32)
    mid = r1.astype(BF16)
    lo = (r1 - mid.astype(F32)).astype(BF16)
    return hi, mid, lo


def _dot3(terms, rhs):
    return _dot(terms[0], rhs) + _dot(terms[1], rhs) + _dot(terms[2], rhs)


def _col_slices(n):
    if n <= MAX_DOT_COLS:
        return [slice(0, n)]
    tiles = -(-n // MXU_WIDTH)
    parts = -(-tiles * MXU_WIDTH // MAX_DOT_COLS)
    bounds = [min(n, (tiles * p // parts) * MXU_WIDTH) for p in range(parts + 1)]
    return [slice(lo, hi) for lo, hi in zip(bounds[:-1], bounds[1:])]


def _resident(shape, index=None):
    index = index if index is not None else (0,) * len(shape)
    return pl.BlockSpec(shape, lambda *_: index, pipeline_mode=pl.Buffered(1))


def _ffn_body(h_ref, *refs, final_norm, mix):
    x = h_ref[...]
    if mix:
        y_ref, wm_ref, *refs = refs
        x = x + _dot(y_ref[...], wm_ref[...])
    g_ref, wa_ref, wb_ref, wo_ref, gf_ref, o_ref = refs
    xn = _rms(x, g_ref[...]).astype(BF16)
    acc = jnp.zeros(x.shape, F32)
    for sl in _col_slices(wa_ref.shape[1]):
        a = _dot(xn, wa_ref[:, sl])
        b = _dot(xn, wb_ref[:, sl])
        acc = acc + _dot((_silu(a) * b).astype(BF16), wo_ref[sl, :])
    y = x + 0.5 * acc
    if final_norm:
        y = _rms(y, gf_ref[...])
    o_ref[...] = y


def _ffn(h, g, w_in, w_out, layer, g_final=None, mix=None, *, tm=TOKEN_TILE):
    m, d = h.shape
    f = w_out.shape[1]
    final_norm = g_final is not None
    gf = g_final if final_norm else g
    mix_args = list(mix) if mix is not None else []
    mix_specs = [pl.BlockSpec((tm, mix[0].shape[1]), lambda i: (i, 0)), _resident(mix[1].shape)] if mix_args else []
    return pl.pallas_call(
        functools.partial(_ffn_body, final_norm=final_norm, mix=bool(mix_args)),
        grid=(m // tm,),
        in_specs=[pl.BlockSpec((tm, d), lambda i: (i, 0))] + mix_specs + [
            _resident((1, d)),
            _resident((None, d, f), (layer, 0, 0)),
            _resident((None, d, f), (layer, 0, 1)),
            _resident((None, f, d), (layer, 0, 0)),
            _resident((1, d)),
        ],
        out_specs=pl.BlockSpec((tm, d), lambda i: (i, 0)),
        out_shape=jax.ShapeDtypeStruct((m, d), F32),
        compiler_params=_params("parallel"),
        name="ffn",
    )(h, *mix_args, g.reshape(1, d), w_in, w_in, w_out, gf.reshape(1, d))


def _norm_proj_body(h_ref, g_ref, *refs, n_out):
    xn = _rms(h_ref[...], g_ref[...]).astype(BF16)
    for w_ref, o_ref in zip(refs[:n_out], refs[n_out:]):
        for sl in _col_slices(w_ref.shape[1]):
            o_ref[:, sl] = _dot(xn, w_ref[:, sl]).astype(o_ref.dtype)


def _norm_proj(h, g, ws, out_dtypes, *, tm=TOKEN_TILE):
    m, d = h.shape
    return pl.pallas_call(
        functools.partial(_norm_proj_body, n_out=len(ws)),
        grid=(m // tm,),
        in_specs=[pl.BlockSpec((tm, d), lambda i: (i, 0)), pl.BlockSpec((1, d), lambda i: (0, 0))]
        + [_resident(w.shape) for w in ws],
        out_specs=[pl.BlockSpec((tm, w.shape[1]), lambda i: (i, 0)) for w in ws],
        out_shape=[jax.ShapeDtypeStruct((m, w.shape[1]), dt) for w, dt in zip(ws, out_dtypes)],
        compiler_params=_params("parallel"),
        name="norm_proj",
    )(h, g.reshape(1, d), *ws)


def _ret_body(q_ref, k_ref, v_ref, g_ref, cos_ref, sin_ref, gn_ref, o_ref, state_ref, *, ts):
    c_len = RET_CHUNK
    dk = q_ref.shape[2] // RET_HEADS
    dv = v_ref.shape[2] // RET_HEADS
    half = dk // 2

    @pl.when(pl.program_id(1) == 0)
    def _():
        state_ref[...] = jnp.zeros(state_ref.shape, F32)

    ii = lax.broadcasted_iota(jnp.int32, (c_len, c_len), 0)
    jj = lax.broadcasted_iota(jnp.int32, (c_len, c_len), 1)
    rel = (ii - jj).astype(F32)
    causal = ii >= jj
    idx = lax.broadcasted_iota(jnp.int32, (c_len, 1), 0).astype(F32)

    def rot(t, cos, sin):
        t1, t2 = t[:, :half], t[:, half:]
        return jnp.concatenate([t1 * cos - t2 * sin, t1 * sin + t2 * cos], axis=1)

    decays = []
    for h in range(RET_HEADS):
        log_gamma = math.log1p(-(2.0 ** (-5.0 - h)))
        decays.append((jnp.where(causal, jnp.exp(jnp.where(causal, rel, 0.0) * log_gamma), 0.0),
                       jnp.exp((idx + 1.0) * log_gamma),
                       jnp.exp((c_len - 1.0 - idx) * log_gamma),
                       math.exp(c_len * log_gamma)))

    def chunk(c, carry):
        r0 = pl.multiple_of(c * c_len, c_len)
        rows = pl.ds(r0, c_len)
        cos = cos_ref[rows, :]
        sin = sin_ref[rows, :]
        for h in range(RET_HEADS):
            inner, q_decay, k_decay, chunk_decay = decays[h]
            q = rot(q_ref[0, rows, h * dk:(h + 1) * dk].astype(F32), cos, sin)
            k = rot(k_ref[0, rows, h * dk:(h + 1) * dk].astype(F32), cos, sin) * (dk ** -0.5)
            v = v_ref[0, rows, h * dv:(h + 1) * dv]
            qb = q.astype(BF16)
            sc = _dot_nt(qb, k.astype(BF16)) * inner
            st = state_ref[h]
            o = _dot(sc.astype(BF16), v) + _dot(qb, st.astype(BF16)) * q_decay
            kd_t = jnp.transpose(k * k_decay).astype(BF16)
            state_ref[h] = st * chunk_decay + _dot(kd_t, v)
            gn = gn_ref[:, h * dv:(h + 1) * dv]
            on = _rms(o, gn)
            gate = g_ref[0, rows, h * dv:(h + 1) * dv].astype(F32)
            o_ref[0, rows, h * dv:(h + 1) * dv] = (_silu(gate) * on).astype(o_ref.dtype)
        return carry

    lax.fori_loop(0, ts // c_len, chunk, 0)


def _retention_core(proj, cos, sin, gn_gain, *, b, s, ts=TOKEN_TILE):
    n = proj.shape[2]
    hk = n // 6
    hv = 2 * hk
    dk = hk // RET_HEADS
    return pl.pallas_call(
        functools.partial(_ret_body, ts=ts),
        grid=(b, s // ts),
        in_specs=[
            pl.BlockSpec((1, ts, hk), lambda i, j: (i, j, 0)),
            pl.BlockSpec((1, ts, hk), lambda i, j: (i, j, 1)),
            pl.BlockSpec((1, ts, hv), lambda i, j: (i, j, 1)),
            pl.BlockSpec((1, ts, hv), lambda i, j: (i, j, 2)),
            pl.BlockSpec((ts, dk // 2), lambda i, j: (j, 0)),
            pl.BlockSpec((ts, dk // 2), lambda i, j: (j, 0)),
            pl.BlockSpec((1, hv), lambda i, j: (0, 0)),
        ],
        out_specs=pl.BlockSpec((1, ts, hv), lambda i, j: (i, j, 0)),
        out_shape=jax.ShapeDtypeStruct((b, s, hv), BF16),
        scratch_shapes=[pltpu.VMEM((RET_HEADS, dk, hv // RET_HEADS), F32)],
        compiler_params=_params("parallel", "arbitrary"),
        name="retention",
    )(proj, proj, proj, proj, cos, sin, gn_gain.reshape(1, hv))


def _rope_tables(s, half):
    inv = ROPE_BASE ** (-jnp.arange(half, dtype=F32) / half)
    ang = jnp.arange(s, dtype=F32)[:, None] * inv[None, :]
    return jnp.cos(ang), jnp.sin(ang)


def _retention(h2, g_norm, w_in, gn_gain, w_out, *, b, s):
    (proj,) = _norm_proj(h2, g_norm, [w_in.astype(BF16)], [BF16])
    n = proj.shape[1]
    cos, sin = _rope_tables(s, n // 6 // RET_HEADS // 2)
    y = _retention_core(proj.reshape(b, s, n), cos, sin, gn_gain, b=b, s=s)
    return y.reshape(b * s, -1), w_out.astype(BF16)


def _dil_proj_body(h_ref, g_ref, w_ref, o_ref, xn_ref, *, r):
    tm = h_ref.shape[0]
    n = tm // r
    xn = _rms(h_ref[...], g_ref[...])
    slabs = xn_ref.shape[0]
    for k in range(slabs):
        xn_ref[k] = xn[:, k * LANES:(k + 1) * LANES]
    xp = jnp.concatenate(
        [jnp.concatenate([xn_ref[k, pl.ds(c, n, stride=r), :] for c in range(r)], axis=0) for k in range(slabs)],
        axis=1).astype(BF16)
    for sl in _col_slices(w_ref.shape[1]):
        res = _dot(xp, w_ref[:, sl]).astype(o_ref.dtype)
        for c in range(r):
            o_ref[0, c, :, sl] = res[c * n:(c + 1) * n, :]


def _dil_proj(h2, g_norm, w, r, *, b, s, tm=TOKEN_TILE):
    d = h2.shape[1]
    n_out = w.shape[1]
    tiles = s // tm
    return pl.pallas_call(
        functools.partial(_dil_proj_body, r=r),
        grid=(b * tiles,),
        in_specs=[
            pl.BlockSpec((tm, d), lambda i: (i, 0)),
            pl.BlockSpec((1, d), lambda i: (0, 0)),
            _resident((d, n_out)),
        ],
        out_specs=pl.BlockSpec((1, r, tm // r, n_out), lambda i: (i // tiles, 0, i % tiles, 0)),
        out_shape=jax.ShapeDtypeStruct((b, r, s // r, n_out), BF16),
        scratch_shapes=[pltpu.VMEM((d // LANES, tm, LANES), F32)],
        compiler_params=_params("parallel"),
        name=f"dilated_proj_r{r}",
    )(h2, g_norm.reshape(1, d), w)


def _dil_body(q_ref, kc_ref, kp_ref, vc_ref, vp_ref, o_ref, lse_ref, kbuf, vbuf, *, rows, n_back):
    qb = DIL_QB
    dh = DIL_DH
    first_step = pl.program_id(2) == 0
    kbuf[0:qb, :] = kp_ref[0, 0]
    kbuf[qb:, :] = kc_ref[0, 0]
    vbuf[0:qb, :] = vp_ref[0, 0]
    vbuf[qb:, :] = vc_ref[0, 0]
    qi = lax.broadcasted_iota(jnp.int32, (qb, 2 * qb), 0)
    kj = lax.broadcasted_iota(jnp.int32, (qb, 2 * qb), 1)
    dist = qi + qb - kj
    band = (dist >= 0) & (dist <= n_back)
    lane = lax.broadcasted_iota(jnp.int32, (qb, LANES), 1)
    scale = dh ** -0.5
    ones = jnp.ones((2 * qb, dh), BF16)
    for i in range(rows // qb):
        mask = band & ((kj >= qb) | jnp.logical_not(first_step)) if i == 0 else band
        lse_tile = jnp.zeros((qb, LANES), F32)
        for h in range(DIL_HEADS):
            cols = slice(h * dh, (h + 1) * dh)
            q = q_ref[0, 0, i * qb:(i + 1) * qb, cols]
            k = kbuf[i * qb:(i + 2) * qb, cols]
            v = vbuf[i * qb:(i + 2) * qb, cols]
            s = jnp.where(mask, _dot_nt(q, k), NEG_INF)
            m = jnp.max(s, axis=1, keepdims=True)
            e = jnp.exp2((s - m) * (scale * math.log2(math.e))).astype(BF16)
            acc = _dot(e, jnp.concatenate([v, ones], axis=1))
            den = jnp.maximum(acc[:, dh:], 1e-30)
            o_ref[0, 0, i * qb:(i + 1) * qb, cols] = (acc[:, :dh] / den).astype(o_ref.dtype)
            lse_tile = jnp.where(lane == h, m * scale + jnp.log(den), lse_tile)
        lse_ref[0, 0, i * qb:(i + 1) * qb, :] = lse_tile


def _dilated_group(proj, win, r, *, b, s):
    hd = DIL_HEADS * DIL_DH
    length = s // r
    rows = min(length, TOKEN_TILE)
    qb = DIL_QB

    def cur(which):
        return pl.BlockSpec((1, 1, rows, hd), lambda i, c, n: (i, c, n, which))

    def prev(which):
        return pl.BlockSpec((1, 1, qb, hd), lambda i, c, n: (i, c, jnp.maximum(n * (rows // qb) - 1, 0), which))

    return pl.pallas_call(
        functools.partial(_dil_body, rows=rows, n_back=win // r),
        grid=(b, r, length // rows),
        in_specs=[cur(0), cur(1), prev(1), cur(2), prev(2)],
        out_specs=[
            pl.BlockSpec((1, 1, rows, hd), lambda i, c, n: (i, c, n, 0)),
            pl.BlockSpec((1, 1, rows, LANES), lambda i, c, n: (i, c, n, 0)),
        ],
        out_shape=[
            jax.ShapeDtypeStruct((b, r, length, hd), BF16),
            jax.ShapeDtypeStruct((b, r, length, LANES), F32),
        ],
        scratch_shapes=[pltpu.VMEM((rows + qb, hd), BF16), pltpu.VMEM((rows + qb, hd), BF16)],
        compiler_params=_params("parallel", "parallel", "arbitrary"),
        name=f"dilated_r{r}",
    )(proj, proj, proj, proj, proj)


def _dil_merge_body(h_ref, *refs, dilations):
    ng = len(dilations)
    o_refs, l_refs = refs[:ng], refs[ng:2 * ng]
    w_ref, out_ref = refs[2 * ng], refs[2 * ng + 1]
    o_bufs, l_bufs = refs[2 * ng + 2:3 * ng + 2], refs[3 * ng + 2:]
    tm = h_ref.shape[1]
    for o_ref, l_ref, o_buf, l_buf, r in zip(o_refs, l_refs, o_bufs, l_bufs, dilations):
        n = tm // r
        for c in range(r):
            rows = pl.ds(c, n, stride=r)
            l_buf[rows, :] = l_ref[0, c]
            blk = o_ref[0, c].astype(F32)
            for h in range(DIL_HEADS):
                o_buf[h, rows, :] = blk[:, h * DIL_DH:(h + 1) * DIL_DH]
    lses = [l[...] for l in l_bufs]
    mx = functools.reduce(jnp.maximum, lses)
    es = [jnp.exp(l - mx) for l in lses]
    tot = functools.reduce(jnp.add, es)
    wts = [e / tot for e in es]
    dh = DIL_DH
    parts = []
    for h in range(DIL_HEADS):
        acc = jnp.zeros((tm, dh), F32)
        for g in range(ng):
            wg = jnp.broadcast_to(wts[g][:, h:h + 1], (tm, dh))
            acc = acc + wg * o_bufs[g][h]
        parts.append(acc)
    o = jnp.concatenate(parts, axis=1).astype(BF16)
    out_ref[0] = h_ref[0] + _dot(o, w_ref[...])


def _dil_merge(h3, outs, lses, w, dilations, *, tm=TOKEN_TILE):
    b, s, d = h3.shape
    hd = w.shape[0]
    res_major = lambda r, width: pl.BlockSpec((1, r, tm // r, width), lambda i, n: (i, 0, n, 0))
    return pl.pallas_call(
        functools.partial(_dil_merge_body, dilations=tuple(dilations)),
        grid=(b, s // tm),
        in_specs=[pl.BlockSpec((1, tm, d), lambda i, n: (i, n, 0))]
        + [res_major(r, hd) for r in dilations] + [res_major(r, LANES) for r in dilations]
        + [pl.BlockSpec((hd, d), lambda i, n: (0, 0))],
        out_specs=pl.BlockSpec((1, tm, d), lambda i, n: (i, n, 0)),
        out_shape=jax.ShapeDtypeStruct((b, s, d), F32),
        scratch_shapes=[pltpu.VMEM((DIL_HEADS, tm, DIL_DH), F32) for _ in dilations]
        + [pltpu.VMEM((tm, LANES), F32) for _ in dilations],
        compiler_params=_params("parallel", "parallel"),
        name="dilated_merge",
    )(h3, *outs, *lses, w)


def _dilated(h2, g_norm, w_in, w_out, *, b, s):
    d = h2.shape[1]
    gw = 3 * DIL_HEADS * DIL_DH
    wb = w_in.astype(BF16)
    outs, lses, dilations = [], [], []
    for g, (win, r) in enumerate(DIL_PATTERN):
        proj = _dil_proj(h2, g_norm, wb[:, g * gw:(g + 1) * gw], r, b=b, s=s, tm=2 * TOKEN_TILE)
        o, lse = _dilated_group(proj, win, r, b=b, s=s)
        outs.append(o)
        lses.append(lse)
        dilations.append(r)
    return _dil_merge(h2.reshape(b, s, d), outs, lses, w_out.astype(BF16), dilations).reshape(b * s, d)


def _softplus(x):
    return jnp.maximum(x, 0.0) + jnp.log1p(jnp.exp(-jnp.abs(x)))


def _ssd_body(z_ref, x_ref, dt_ref, cw_ref, cb_ref, dtb_ref, alog_ref, dsk_ref, ng_ref, o_ref,
              xpad, state_ref):
    ln = SSD_CHUNK
    nst = SSD_STATE
    gw = SSD_HPG * SSD_HEADDIM
    d_inner = SSD_GROUPS * gw
    halo = 8

    @pl.when(pl.program_id(1) == 0)
    def _():
        xpad[0:halo, :] = jnp.zeros((halo, xpad.shape[1]), F32)
        state_ref[...] = jnp.zeros(state_ref.shape, F32)

    x_b = x_ref[0]
    x_f = x_b.astype(F32)
    ii = lax.broadcasted_iota(jnp.int32, (ln, ln), 0)
    jj = lax.broadcasted_iota(jnp.int32, (ln, ln), 1)
    conv = cb_ref[...] + x_f * cw_ref[SSD_CONV - 1:SSD_CONV, :]
    for shift in range(1, SSD_CONV):
        shifted = _dot(jnp.where(ii - jj == shift, 1.0, 0.0).astype(BF16), x_b)
        conv = conv + shifted * cw_ref[SSD_CONV - 1 - shift:SSD_CONV - shift, :]
    xpad[halo:2 * halo, :] = x_f[0:halo, :]
    head = cb_ref[...]
    for k in range(SSD_CONV):
        off = halo - (SSD_CONV - 1) + k
        head = head + xpad[off:off + halo, :] * cw_ref[k:k + 1, :]
    xpad[0:halo, :] = x_f[ln - halo:ln, :]
    xbc = _silu(jnp.concatenate([head, conv[halo:, :]], axis=0))
    xs = xbc[:, :d_inner]
    bm = xbc[:, d_inner:d_inner + SSD_GROUPS * nst]
    cm = xbc[:, d_inner + SSD_GROUPS * nst:]

    dt = _softplus(dt_ref[0] + dtb_ref[...])
    da = dt * (-jnp.exp(alog_ref[...]))
    ii = lax.broadcasted_iota(jnp.int32, (ln, ln), 0)
    jj = lax.broadcasted_iota(jnp.int32, (ln, ln), 1)
    causal = ii >= jj
    tril = jnp.where(causal, 1.0, 0.0).astype(BF16)
    da_terms = _split3(da)
    acs = _dot(tril, da_terms[0]) + _dot(tril, da_terms[1]) + _dot(tril, da_terms[2])
    acs2 = acs * math.log2(math.e)
    acs2_t = jnp.transpose(acs2)
    hl = ln // 2

    erow = lax.broadcasted_iota(jnp.int32, (LANES, d_inner), 0)
    ecol = lax.broadcasted_iota(jnp.int32, (LANES, d_inner), 1)
    expand = jnp.where(ecol // SSD_HEADDIM == erow, 1.0, 0.0).astype(BF16)
    acs_e = _dot3(_split3(acs), expand)
    dt_e = _dot3(_split3(dt), expand)
    last = acs_e[ln - 1:ln, :]
    decay_in = jnp.exp(acs_e)
    xs_dt = xs * dt_e
    xs_end = (xs_dt * jnp.exp(last - acs_e)).astype(BF16)
    xs_b = xs_dt.astype(BF16)
    lane = lax.broadcasted_iota(jnp.int32, (ln, LANES), 1)

    y_groups = []
    for g in range(SSD_GROUPS):
        bm_g = bm[:, g * nst:(g + 1) * nst]
        cm_g = cm[:, g * nst:(g + 1) * nst].astype(BF16)
        cb = jnp.where(causal, _dot_nt(cm_g, bm_g.astype(BF16)), 0.0)
        st = state_ref[g]
        gcols = slice(g * gw, (g + 1) * gw)
        y_state = _dot(cm_g, st.astype(BF16)) * decay_in[:, gcols]
        pairs = []
        for p in range(SSD_HPG // 2):
            pair_cols = slice(g * gw + p * LANES, g * gw + (p + 1) * LANES)
            halves = []
            for e in range(2):
                hd = g * SSD_HPG + 2 * p + e
                top = cb[:hl, :hl] * jnp.exp2(jnp.minimum(acs2[:hl, hd:hd + 1] - acs2_t[hd:hd + 1, :hl], 0.0))
                bot = cb[hl:, :] * jnp.exp2(jnp.minimum(acs2[hl:, hd:hd + 1] - acs2_t[hd:hd + 1, :], 0.0))
                halves.append(jnp.concatenate([_dot(top.astype(BF16), xs_b[:hl, pair_cols]),
                                               _dot(bot.astype(BF16), xs_b[:, pair_cols])], axis=0))
            pairs.append(jnp.where(lane < SSD_HEADDIM, halves[0], halves[1]))
        y_groups.append(jnp.concatenate(pairs, axis=1) + y_state)
        bm_t = jnp.transpose(bm_g).astype(BF16)
        state_ref[g] = st * jnp.exp(last[:, gcols]) + _dot(bm_t, xs_end[:, gcols])

    y = jnp.concatenate(y_groups, axis=1) + dsk_ref[...] * xs
    yz = y * _silu(z_ref[0].astype(F32))
    outs = [_rms(yz[:, g * gw:(g + 1) * gw], ng_ref[:, g * gw:(g + 1) * gw]) for g in range(SSD_GROUPS)]
    o_ref[0] = jnp.concatenate(outs, axis=1).astype(o_ref.dtype)


def _ssd_core(z, xbc, dt, conv_w, conv_b, dt_bias, a_log, d_skip, norm_g, *, b, s):
    ln = SSD_CHUNK
    d_inner = z.shape[2]
    conv_dim = xbc.shape[2]
    gw = SSD_HPG * SSD_HEADDIM
    full = lambda shape: pl.BlockSpec(shape, lambda i, j: (0,) * len(shape))
    return pl.pallas_call(
        _ssd_body,
        grid=(b, s // ln),
        in_specs=[
            pl.BlockSpec((1, ln, d_inner), lambda i, j: (i, j, 0)),
            pl.BlockSpec((1, ln, conv_dim), lambda i, j: (i, j, 0)),
            pl.BlockSpec((1, ln, LANES), lambda i, j: (i, j, 0)),
            full((SSD_CONV, conv_dim)),
            full((1, conv_dim)),
            full((1, LANES)),
            full((1, LANES)),
            full((1, d_inner)),
            full((1, d_inner)),
        ],
        out_specs=pl.BlockSpec((1, ln, d_inner), lambda i, j: (i, j, 0)),
        out_shape=jax.ShapeDtypeStruct((b, s, d_inner), BF16),
        scratch_shapes=[pltpu.VMEM((16, conv_dim), F32),
                        pltpu.VMEM((SSD_GROUPS, SSD_STATE, gw), F32)],
        compiler_params=_params("parallel", "arbitrary"),
        name="ssd",
    )(z, xbc, dt, conv_w, conv_b, dt_bias, a_log, d_skip, norm_g)


def _ssd(h2, g_norm, w_in, conv_w, conv_b, dt_bias, a_log, d_skip, norm_g, w_out, *, b, s):
    heads = SSD_GROUPS * SSD_HPG
    d_inner = heads * SSD_HEADDIM
    conv_dim = conv_w.shape[1]
    wb = w_in.astype(BF16)
    w_dt = jnp.pad(wb[:, d_inner + conv_dim:], ((0, 0), (0, LANES - heads)))
    z, xbc, dt = _norm_proj(h2, g_norm, [wb[:, :d_inner], wb[:, d_inner:d_inner + conv_dim], w_dt], [BF16, BF16, F32])
    pad_heads = lambda v: jnp.pad(v.astype(F32), (0, LANES - heads)).reshape(1, LANES)
    y = _ssd_core(z.reshape(b, s, -1), xbc.reshape(b, s, -1), dt.reshape(b, s, LANES),
                  conv_w.astype(F32), conv_b.reshape(1, -1).astype(F32), pad_heads(dt_bias), pad_heads(a_log),
                  jnp.repeat(d_skip.astype(F32), SSD_HEADDIM).reshape(1, d_inner), norm_g.reshape(1, d_inner),
                  b=b, s=s)
    return y.reshape(b * s, d_inner), w_out.astype(BF16)


def _nsa_proj_body(h_ref, g_ref, wn_ref, wt_ref, k0_ref, v0_ref, k12_ref, tr_ref, stage_ref):
    xn = _rms(h_ref[...], g_ref[...]).astype(BF16)
    nat = _dot(xn, wn_ref[...])
    tm = nat.shape[0]
    gd = k12_ref.shape[1] // 2
    k12_ref[...] = nat[:, 2 * gd:].astype(k12_ref.dtype)
    slabs = gd // LANES
    rows = tm // NSA_CMP_STRIDE
    for sl in range(2 * slabs):
        stage_ref[sl] = nat[:, sl * LANES:(sl + 1) * LANES]
    for l in range(NSA_CMP_STRIDE):
        for sl in range(slabs):
            cols = slice(l * gd + sl * LANES, l * gd + (sl + 1) * LANES)
            k0_ref[:, cols] = stage_ref[sl, pl.ds(l, rows, stride=NSA_CMP_STRIDE), :].astype(k0_ref.dtype)
            v0_ref[:, cols] = stage_ref[slabs + sl, pl.ds(l, rows, stride=NSA_CMP_STRIDE), :].astype(v0_ref.dtype)
    res = _dot_nt(wt_ref[...], xn)
    for j in range(tr_ref.shape[0]):
        tr_ref[j] = res[:, j * NSA_QT:(j + 1) * NSA_QT].astype(tr_ref.dtype)


def _nsa_proj(h2, g_norm, w_nat, w_t, *, tm=TOKEN_TILE):
    m, d = h2.shape
    gd = NSA_GROUPS * NSA_DH
    nt = w_t.shape[0]
    slabs = tm // NSA_QT
    return pl.pallas_call(
        _nsa_proj_body,
        grid=(m // tm,),
        in_specs=[
            pl.BlockSpec((tm, d), lambda i: (i, 0)),
            pl.BlockSpec((1, d), lambda i: (0, 0)),
            _resident(w_nat.shape),
            _resident(w_t.shape),
        ],
        out_specs=[
            pl.BlockSpec((tm // NSA_CMP_STRIDE, NSA_CMP_STRIDE * gd), lambda i: (i, 0)),
            pl.BlockSpec((tm // NSA_CMP_STRIDE, NSA_CMP_STRIDE * gd), lambda i: (i, 0)),
            pl.BlockSpec((tm, 2 * gd), lambda i: (i, 0)),
            pl.BlockSpec((slabs, nt, NSA_QT), lambda i: (i, 0, 0)),
        ],
        out_shape=[
            jax.ShapeDtypeStruct((m // NSA_CMP_STRIDE, NSA_CMP_STRIDE * gd), BF16),
            jax.ShapeDtypeStruct((m // NSA_CMP_STRIDE, NSA_CMP_STRIDE * gd), BF16),
            jax.ShapeDtypeStruct((m, 2 * gd), BF16),
            jax.ShapeDtypeStruct((m // NSA_QT, nt, NSA_QT), BF16),
        ],
        scratch_shapes=[pltpu.VMEM((2 * gd // LANES, tm, LANES), F32)],
        compiler_params=_params("parallel"),
        name="nsa_proj",
    )(h2, g_norm.reshape(1, d), w_nat, w_t)


def _nsa_cmp_body(x_ref, pa_ref, pb_ref, w1a_ref, w1b_ref, w2_ref, o_ref, *, transposed):
    x = x_ref[0].astype(F32)
    nb = x.shape[0]
    ya = _dot((x + pa_ref[...]).astype(BF16), w1a_ref[...])
    yb = _dot((x + pb_ref[...]).astype(BF16), w1b_ref[...])
    hid = _silu(ya + pltpu.roll(yb, nb - 1, 0))
    out = _dot(hid.astype(BF16), w2_ref[...])
    o_ref[0] = (jnp.transpose(out) if transposed else out).astype(o_ref.dtype)


def _nsa_compress(x, pos, w1, w2, *, transposed):
    b, nb, width = x.shape
    g, dh = NSA_GROUPS, NSA_DH
    half = NSA_CMP_LEN // 2
    hid = w1.shape[1]
    eye = jnp.eye(g, dtype=F32)
    w1r = w1.reshape(NSA_CMP_LEN, dh, hid)
    big = jnp.einsum("ldj,gh->lgdhj", w1r, eye).reshape(NSA_CMP_LEN, g * dh, g * hid)
    w1a = big[:half].reshape(half * g * dh, g * hid).astype(BF16)
    w1b = big[half:].reshape(half * g * dh, g * hid).astype(BF16)
    w2big = jnp.einsum("jd,gh->gjhd", w2, eye).reshape(g * hid, g * dh).astype(BF16)
    posb = jnp.broadcast_to(pos[:, None, :], (NSA_CMP_LEN, g, dh)).reshape(NSA_CMP_LEN, g * dh)
    pa = posb[:half].reshape(1, width).astype(F32)
    pb = posb[half:].reshape(1, width).astype(F32)
    full = lambda shape: pl.BlockSpec(shape, lambda i: (0,) * len(shape))
    out_dims = (g * dh, nb) if transposed else (nb, g * dh)
    return pl.pallas_call(
        functools.partial(_nsa_cmp_body, transposed=transposed),
        grid=(b,),
        in_specs=[
            pl.BlockSpec((1, nb, width), lambda i: (i, 0, 0)),
            full((1, width)), full((1, width)),
            full((width, g * hid)), full((width, g * hid)), full((g * hid, g * dh)),
        ],
        out_specs=pl.BlockSpec((1,) + out_dims, lambda i: (i, 0, 0)),
        out_shape=jax.ShapeDtypeStruct((b,) + out_dims, BF16),
        compiler_params=_params("parallel"),
        name="nsa_compress",
    )(x, pa, pb, w1a, w1b, w2big)


def _nsa_pair_body(q_ref, gt_ref, kc_ref, vct_ref, k1_ref, k2_ref, v1_ref, v2_ref, o_ref, bias_ref, wbias_ref, *, s):
    qt, kc, dh = NSA_QT, NSA_KC, NSA_DH
    gl = 2
    glanes = NSA_HPG * qt
    lanes = gl * glanes
    vrows = gl * dh
    j = pl.program_id(2)
    t0 = j * qt
    nb = kc_ref.shape[1]
    nsb = s // NSA_SEL_LEN
    n_sel = min(NSA_N_SEL, nsb)

    q_t = q_ref[0, 0]
    qcat = [jnp.concatenate([q_t[(g * NSA_HPG + h) * dh:(g * NSA_HPG + h + 1) * dh, :] for h in range(NSA_HPG)],
                            axis=1) for g in range(gl)]
    zero = jnp.zeros_like(qcat[0])
    qp = jnp.concatenate([jnp.concatenate([qcat[0], zero], axis=1),
                          jnp.concatenate([zero, qcat[1]], axis=1)], axis=0)
    qlane = lax.broadcasted_iota(jnp.int32, (1, lanes), 1) % qt
    tq = t0 + qlane

    def values(v_ref, slab0, n_slabs):
        v = jnp.concatenate([v_ref[0, slab0 + i] for i in range(n_slabs)], axis=1)
        return jnp.concatenate([v, jnp.ones((16, n_slabs * qt), BF16)], axis=0)

    def weighted_values(v_aug, p):
        return jnp.concatenate(
            [_dot(jnp.concatenate([v_aug[g * dh:(g + 1) * dh, :], v_aug[vrows:, :]], axis=0),
                  p[:, g * glanes:(g + 1) * glanes]) for g in range(gl)], axis=1)

    def normalise(acc):
        o = acc[:dh, :] * (1.0 / jnp.maximum(acc[dh:dh + 1, :], 1e-30))
        return [o[:, g * glanes:(g + 1) * glanes] for g in range(gl)]

    wslabs = NSA_WINDOW // qt + 1
    row_w = lax.broadcasted_iota(jnp.int32, (qt, lanes), 0)

    @pl.when(j == 0)
    def _():
        wbias_ref[0] = jnp.where(row_w > qlane, 0.0, NEG_INF)
        wbias_ref[1] = jnp.zeros((qt, lanes), F32)
        wbias_ref[2] = jnp.where(row_w <= qlane, 0.0, NEG_INF)
        wbias_ref[3] = jnp.full((qt, lanes), NEG_INF, F32)

    jb0 = jnp.maximum(j + 1 - wslabs, 0)
    kw = k2_ref[0, pl.ds(pl.multiple_of(jb0 * qt, qt), wslabs * qt), :]
    sw_raw = _dot(kw, qp)
    slabs_back = [j - jb0 - i for i in range(wslabs)]
    sw = jnp.concatenate(
        [sw_raw[i * qt:(i + 1) * qt, :]
         + wbias_ref[jnp.where(back < 0, 3, jnp.where(back == wslabs - 1, 0, jnp.where(back == 0, 2, 1)))]
         for i, back in enumerate(slabs_back)], axis=0)

    sc = _dot(kc_ref[0], qp)
    assert NSA_CMP_STRIDE == 16
    last_visible = jnp.minimum(jnp.right_shift(tq - (NSA_CMP_LEN - 1), 4), nb - 2)
    sc = jnp.where(lax.broadcasted_iota(jnp.int32, (nb, lanes), 0) <= last_visible, sc, NEG_INF)
    m_c = jnp.max(sc, axis=0, keepdims=True)
    e = jnp.exp(sc - jnp.where(m_c > 0.5 * NEG_INF, m_c, 0.0))
    p_c = e * (1.0 / jnp.maximum(jnp.sum(e, axis=0, keepdims=True), 1e-30))
    acc_c = _dot(vct_ref[0], p_c.astype(BF16))
    o_c = [acc_c[g * dh:(g + 1) * dh, g * glanes:(g + 1) * glanes] for g in range(gl)]
    imps = []
    for g in range(gl):
        imp = p_c[:, g * glanes:g * glanes + qt]
        for h in range(1, NSA_HPG):
            imp = imp + p_c[:, g * glanes + h * qt:g * glanes + (h + 1) * qt]
        imps.append(imp)
    imp = jnp.concatenate(imps, axis=1)

    ratio = NSA_SEL_LEN // NSA_CMP_STRIDE
    dd = lax.broadcasted_iota(jnp.int32, (nsb, nb), 1) - ratio * lax.broadcasted_iota(jnp.int32, (nsb, nb), 0)
    wsel = jnp.where((dd == -1) | (dd == ratio - 1), 1.0,
                     jnp.where((dd >= 0) & (dd < ratio - 1), 2.0, 0.0)).astype(BF16)
    terms = _split3(imp)
    imp_sel = _dot(wsel, terms[0]) + _dot(wsel, terms[1]) + _dot(wsel, terms[2])
    jq = lax.broadcasted_iota(jnp.int32, (nsb, gl * qt), 0)
    cur = (t0 + lax.broadcasted_iota(jnp.int32, (nsb, gl * qt), 1) % qt) // NSA_SEL_LEN
    forced = (jq == 0) | (jq == cur) | (jq == cur - 1)
    score = jnp.where(jq <= cur, imp_sel + jnp.where(forced, NSA_FORCE_BONUS, 0.0), NEG_INF)

    pw = jnp.exp(sw - jnp.max(sw, axis=0, keepdims=True)).astype(BF16)
    o_w = normalise(weighted_values(values(v2_ref, jb0, wslabs), pw))

    jqf = jq.astype(F32)
    work = score
    picked = None
    for _ in range(n_sel):
        top = jnp.max(work, axis=0, keepdims=True)
        first = jnp.min(jnp.where(work == top, jqf, float(nsb)), axis=0, keepdims=True)
        pick = jqf == first
        picked = pick if picked is None else picked | pick
        work = jnp.where(pick, 3.0 * NEG_INF, work)
    sel_bias = jnp.where(picked & (score > 0.5 * NEG_INF), 0.0, NEG_INF)
    bias_ref[...] = jnp.concatenate([sel_bias[:, g * qt:(g + 1) * qt] for g in range(gl) for _ in range(NSA_HPG)],
                                    axis=1)

    blocks_per_chunk = kc // NSA_SEL_LEN
    slabs_per_chunk = kc // qt

    def online(carry, st, bias, v_aug):
        m, acc = carry
        blks = [st[r * NSA_SEL_LEN:(r + 1) * NSA_SEL_LEN, :] for r in range(blocks_per_chunk)]
        part = None
        for r in range(blocks_per_chunk):
            pr = blks[r][0:8, :]
            for i in range(1, NSA_SEL_LEN // 8):
                pr = jnp.maximum(pr, blks[r][8 * i:8 * (i + 1), :])
            pr = pr + bias[r:r + 1, :]
            part = pr if part is None else jnp.maximum(part, pr)
        m_new = jnp.maximum(m, jnp.max(part, axis=0, keepdims=True))
        p = jnp.concatenate([jnp.exp(blks[r] - (m_new - bias[r:r + 1, :])) for r in range(blocks_per_chunk)],
                            axis=0).astype(BF16)
        return m_new, jnp.exp(m - m_new) * acc + weighted_values(v_aug, p)

    def sel_scores(c):
        return _dot(k1_ref[0, pl.ds(pl.multiple_of(c * kc, kc), kc), :], qp)

    def sel_bias(c):
        return bias_ref[pl.ds(pl.multiple_of(c * blocks_per_chunk, blocks_per_chunk), blocks_per_chunk), :]

    def sel_values(c):
        return values(v1_ref, c * slabs_per_chunk, slabs_per_chunk)

    init = (jnp.full((1, lanes), NEG_INF, F32), jnp.zeros((dh + 16, lanes), F32))
    c_hi = t0 // kc
    own_slab = (t0 - c_hi * kc) // qt
    st_raw = sel_scores(c_hi)
    st = jnp.concatenate(
        [st_raw[i * qt:(i + 1) * qt, :] + wbias_ref[jnp.where(i < own_slab, 1, jnp.where(i == own_slab, 2, 3))]
         for i in range(slabs_per_chunk)], axis=0)
    chain_a = online(init, st, sel_bias(c_hi), sel_values(c_hi))
    chain_a = lax.cond(c_hi % 2 == 1,
                       lambda cr: online(cr, sel_scores(c_hi - 1), sel_bias(c_hi - 1), sel_values(c_hi - 1)),
                       lambda cr: cr, chain_a)
    chain_b = (chain_a[0], jnp.zeros_like(chain_a[1]))

    def pair(i, chains):
        ca, cb = chains
        st_a = sel_scores(2 * i)
        st_b = sel_scores(2 * i + 1)
        return (online(ca, st_a, sel_bias(2 * i), sel_values(2 * i)),
                online(cb, st_b, sel_bias(2 * i + 1), sel_values(2 * i + 1)))

    (m_a, acc_a), (m_b, acc_b) = lax.fori_loop(0, c_hi // 2, pair, (chain_a, chain_b))
    m_s = jnp.maximum(m_a, m_b)
    o_s = normalise(jnp.exp(m_a - m_s) * acc_a + jnp.exp(m_b - m_s) * acc_b)

    gates = jax.nn.sigmoid(gt_ref[0, 0].astype(F32))
    outs = []
    for g in range(gl):
        def gate_row(br):
            r0 = (g * 4 + br) * NSA_HPG
            return jnp.concatenate([gates[r0 + h:r0 + h + 1, :] for h in range(NSA_HPG)], axis=1)
        o = gate_row(0) * o_c[g] + gate_row(1) * o_s[g] + gate_row(2) * o_w[g]
        outs.extend(o[:, h * qt:(h + 1) * qt] for h in range(NSA_HPG))
    o_ref[0] = jnp.transpose(jnp.concatenate(outs, axis=0)).astype(o_ref.dtype)


def _nsa_attend_pairs(tr, k_cmp, v_cmp_t, k12, *, b, s):
    qt, dh = NSA_QT, NSA_DH
    gl = 2
    gd = NSA_GROUPS * dh
    hq = NSA_GROUPS * NSA_HPG * dh
    pairs = NSA_GROUPS // gl
    nslab = s // qt
    assert nslab > NSA_WINDOW // qt
    nb = k_cmp.shape[1]
    tr4 = tr.reshape(b, nslab, tr.shape[1], qt)
    q_rows = gl * NSA_HPG * dh
    v_rows = gl * dh
    gate_rows = gl * 4 * NSA_HPG
    v1_blk = hq // v_rows
    v2_blk = (hq + gd) // v_rows
    gate_blk = (hq + 2 * gd) // gate_rows
    nsb = s // NSA_SEL_LEN
    return pl.pallas_call(
        functools.partial(_nsa_pair_body, s=s),
        grid=(b, pairs, nslab),
        in_specs=[
            pl.BlockSpec((1, 1, q_rows, qt), lambda i, p, j: (i, j, p, 0)),
            pl.BlockSpec((1, 1, gate_rows, qt), lambda i, p, j: (i, j, gate_blk + p, 0)),
            pl.BlockSpec((1, nb, v_rows), lambda i, p, j: (i, 0, p)),
            pl.BlockSpec((1, v_rows, nb), lambda i, p, j: (i, p, 0)),
            pl.BlockSpec((1, s, v_rows), lambda i, p, j: (i, 0, p)),
            pl.BlockSpec((1, s, v_rows), lambda i, p, j: (i, 0, pairs + p)),
            pl.BlockSpec((1, nslab, v_rows, qt), lambda i, p, j: (i, 0, v1_blk + p, 0)),
            pl.BlockSpec((1, nslab, v_rows, qt), lambda i, p, j: (i, 0, v2_blk + p, 0)),
        ],
        out_specs=pl.BlockSpec((1, qt, q_rows), lambda i, p, j: (i, j, p)),
        out_shape=jax.ShapeDtypeStruct((b, s, hq), BF16),
        scratch_shapes=[pltpu.VMEM((nsb, gl * NSA_HPG * qt), F32), pltpu.VMEM((4, qt, gl * NSA_HPG * qt), F32)],
        compiler_params=_params("parallel", "parallel", "arbitrary"),
        name="nsa_attend",
    )(tr4, tr4, k_cmp, v_cmp_t, k12, k12, tr4, tr4)


def _nsa(h2, g_norm, w_in, cmp_pos, cmp_w1, cmp_w2, w_out, *, b, s):
    g, hpg, dh = NSA_GROUPS, NSA_HPG, NSA_DH
    hq, gd = g * hpg * dh, g * dh
    kv_w = lambda br, kv: w_in[:, hq + (2 * br + kv) * gd:hq + (2 * br + kv + 1) * gd]
    w_q = w_in[:, :hq] * (dh ** -0.5)
    w_g = w_in[:, hq + 6 * gd:].reshape(-1, 3, g, hpg).transpose(2, 1, 3, 0)
    w_g = jnp.pad(w_g, ((0, 0), (0, 1), (0, 0), (0, 0))).reshape(g * 4 * hpg, -1)
    w_t = jnp.concatenate([w_q.T, kv_w(1, 1).T, kv_w(2, 1).T, w_g], axis=0).astype(BF16)
    w_nat = jnp.concatenate([kv_w(0, 0), kv_w(0, 1), kv_w(1, 0), kv_w(2, 0)], axis=1).astype(BF16)
    k0, v0, k12, tr = _nsa_proj(h2, g_norm, w_nat, w_t)
    nb = s // NSA_CMP_STRIDE
    k_cmp = _nsa_compress(k0.reshape(b, nb, NSA_CMP_STRIDE * gd), cmp_pos[0], cmp_w1[0], cmp_w2[0], transposed=False)
    v_cmp_t = _nsa_compress(v0.reshape(b, nb, NSA_CMP_STRIDE * gd), cmp_pos[1], cmp_w1[1], cmp_w2[1], transposed=True)
    y = _nsa_attend_pairs(tr, k_cmp, v_cmp_t, k12.reshape(b, s, 2 * gd), b=b, s=s)
    return y.reshape(b * s, hq), w_out.astype(BF16)


def kernel(x, norm_ffn1, ffn1_w_in, ffn1_w_out, norm_mix, norm_ffn2, ffn2_w_in, ffn2_w_out, norm_final,
           ret_w_in, ret_gn_gain, ret_w_out,
           nsa_w_in, nsa_cmp_pos, nsa_cmp_w1, nsa_cmp_w2, nsa_w_out,
           ssd_w_in, ssd_conv_w, ssd_conv_b, ssd_dt_bias, ssd_a_log, ssd_d, ssd_norm, ssd_w_out,
           dil_w_in, dil_w_out):
    b, s, d = x.shape
    depth = norm_mix.shape[0]
    h = x.reshape(b * s, d)
    w1_in, w1_out = ffn1_w_in.astype(BF16), ffn1_w_out.astype(BF16)
    w2_in, w2_out = ffn2_w_in.astype(BF16), ffn2_w_out.astype(BF16)
    for i in range(depth):
        h = _ffn(h, norm_ffn1[i], w1_in, w1_out, i)
        m, j = i % 4, i // 4
        mix = None
        if m == 0:
            mix = _retention(h, norm_mix[i], ret_w_in[j], ret_gn_gain[j], ret_w_out[j], b=b, s=s)
        elif m == 1:
            mix = _nsa(h, norm_mix[i], nsa_w_in[j], nsa_cmp_pos[j], nsa_cmp_w1[j], nsa_cmp_w2[j], nsa_w_out[j], b=b, s=s)
        elif m == 2:
            mix = _ssd(h, norm_mix[i], ssd_w_in[j], ssd_conv_w[j], ssd_conv_b[j], ssd_dt_bias[j], ssd_a_log[j],
                       ssd_d[j], ssd_norm[j], ssd_w_out[j], b=b, s=s)
        else:
            h = _dilated(h, norm_mix[i], dil_w_in[j], dil_w_out[j], b=b, s=s)
        h = _ffn(h, norm_ffn2[i], w2_in, w2_out, i,
                 norm_final if i == depth - 1 else None, mix)
    return h.reshape(b, s, d)
```

```python
import functools
import math

import jax
import jax.numpy as jnp
from jax import lax
from jax.experimental import pallas as pl
from jax.experimental.pallas import tpu as pltpu

F32 = jnp.float32
BF16 = jnp.bfloat16
NORM_EPS = 1e-6
NEG_INF = -1e30
ROPE_BASE = 10000.0
VMEM_LIMIT_BYTES = 56 * 1024 * 1024
LANES = 128
MXU_WIDTH = 256
TOKEN_TILE = 512
MAX_DOT_COLS = 2048

RET_HEADS = 4
RET_CHUNK = 128

NSA_GROUPS = 4
NSA_HPG = 4
NSA_DH = 64
NSA_CMP_LEN = 32
NSA_CMP_STRIDE = 16
NSA_SEL_LEN = 64
NSA_N_SEL = 16
NSA_WINDOW = 512
NSA_FORCE_BONUS = 1e4
NSA_QT = 256
NSA_KC = 512

SSD_GROUPS = 4
SSD_HPG = 8
SSD_HEADDIM = 64
SSD_STATE = 128
SSD_CONV = 4
SSD_CHUNK = 256

DIL_PATTERN = ((128, 1), (512, 4), (2048, 16))
DIL_HEADS = 8
DIL_DH = 128
DIL_QB = 128


def _params(*sem):
    return pltpu.CompilerParams(dimension_semantics=sem, vmem_limit_bytes=VMEM_LIMIT_BYTES)


def _rms(x, g):
    return x * lax.rsqrt(jnp.mean(x * x, axis=-1, keepdims=True) + NORM_EPS) * g


def _silu(x):
    hx = 0.5 * x
    return hx + hx * jnp.tanh(hx)


def _dot(a, b):
    return jnp.dot(a, b, preferred_element_type=F32)


def _dot_nt(a, b):
    return lax.dot_general(a, b, (((1,), (1,)), ((), ())), preferred_element_type=F32)


def _split3(x):
    hi = x.astype(BF16)
    r1 = x - hi.astype(F32)
    mid = r1.astype(BF16)
    lo = (r1 - mid.astype(F32)).astype(BF16)
    return hi, mid, lo


def _dot3(terms, rhs):
    return _dot(terms[0], rhs) + _dot(terms[1], rhs) + _dot(terms[2], rhs)


def _col_slices(n):
    if n <= MAX_DOT_COLS:
        return [slice(0, n)]
    tiles = -(-n // MXU_WIDTH)
    parts = -(-tiles * MXU_WIDTH // MAX_DOT_COLS)
    bounds = [min(n, (tiles * p // parts) * MXU_WIDTH) for p in range(parts + 1)]
    return [slice(lo, hi) for lo, hi in zip(bounds[:-1], bounds[1:])]


def _resident(shape, index=None):
    index = index if index is not None else (0,) * len(shape)
    return pl.BlockSpec(shape, lambda *_: index, pipeline_mode=pl.Buffered(1))


def _ffn_body(h_ref, *refs, final_norm, mix):
    x = h_ref[...]
    if mix:
        y_ref, wm_ref, *refs = refs
        x = x + _dot(y_ref[...], wm_ref[...])
    g_ref, wa_ref, wb_ref, wo_ref, gf_ref, o_ref = refs
    xn = _rms(x, g_ref[...]).astype(BF16)
    acc = jnp.zeros(x.shape, F32)
    for sl in _col_slices(wa_ref.shape[1]):
        a = _dot(xn, wa_ref[:, sl])
        b = _dot(xn, wb_ref[:, sl])
        acc = acc + _dot((_silu(a) * b).astype(BF16), wo_ref[sl, :])
    y = x + 0.5 * acc
    if final_norm:
        y = _rms(y, gf_ref[...])
    o_ref[...] = y


def _ffn(h, g, w_in, w_out, layer, g_final=None, mix=None, *, tm=TOKEN_TILE):
    m, d = h.shape
    f = w_out.shape[1]
    final_norm = g_final is not None
    gf = g_final if final_norm else g
    mix_args = list(mix) if mix is not None else []
    mix_specs = [pl.BlockSpec((tm, mix[0].shape[1]), lambda i: (i, 0)), _resident(mix[1].shape)] if mix_args else []
    return pl.pallas_call(
        functools.partial(_ffn_body, final_norm=final_norm, mix=bool(mix_args)),
        grid=(m // tm,),
        in_specs=[pl.BlockSpec((tm, d), lambda i: (i, 0))] + mix_specs + [
            _resident((1, d)),
            _resident((None, d, f), (layer, 0, 0)),
            _resident((None, d, f), (layer, 0, 1)),
            _resident((None, f, d), (layer, 0, 0)),
            _resident((1, d)),
        ],
        out_specs=pl.BlockSpec((tm, d), lambda i: (i, 0)),
        out_shape=jax.ShapeDtypeStruct((m, d), F32),
        compiler_params=_params("parallel"),
        name="ffn",
    )(h, *mix_args, g.reshape(1, d), w_in, w_in, w_out, gf.reshape(1, d))


def _norm_proj_body(h_ref, g_ref, *refs, n_out):
    xn = _rms(h_ref[...], g_ref[...]).astype(BF16)
    for w_ref, o_ref in zip(refs[:n_out], refs[n_out:]):
        for sl in _col_slices(w_ref.shape[1]):
            o_ref[:, sl] = _dot(xn, w_ref[:, sl]).astype(o_ref.dtype)


def _norm_proj(h, g, ws, out_dtypes, *, tm=TOKEN_TILE):
    m, d = h.shape
    return pl.pallas_call(
        functools.partial(_norm_proj_body, n_out=len(ws)),
        grid=(m // tm,),
        in_specs=[pl.BlockSpec((tm, d), lambda i: (i, 0)), pl.BlockSpec((1, d), lambda i: (0, 0))]
        + [_resident(w.shape) for w in ws],
        out_specs=[pl.BlockSpec((tm, w.shape[1]), lambda i: (i, 0)) for w in ws],
        out_shape=[jax.ShapeDtypeStruct((m, w.shape[1]), dt) for w, dt in zip(ws, out_dtypes)],
        compiler_params=_params("parallel"),
        name="norm_proj",
    )(h, g.reshape(1, d), *ws)


def _ret_body(q_ref, k_ref, v_ref, g_ref, cos_ref, sin_ref, gn_ref, o_ref, state_ref, *, ts):
    c_len = RET_CHUNK
    dk = q_ref.shape[2] // RET_HEADS
    dv = v_ref.shape[2] // RET_HEADS
    half = dk // 2

    @pl.when(pl.program_id(1) == 0)
    def _():
        state_ref[...] = jnp.zeros(state_ref.shape, F32)

    ii = lax.broadcasted_iota(jnp.int32, (c_len, c_len), 0)
    jj = lax.broadcasted_iota(jnp.int32, (c_len, c_len), 1)
    rel = (ii - jj).astype(F32)
    causal = ii >= jj
    idx = lax.broadcasted_iota(jnp.int32, (c_len, 1), 0).astype(F32)

    def rot(t, cos, sin):
        t1, t2 = t[:, :half], t[:, half:]
        return jnp.concatenate([t1 * cos - t2 * sin, t1 * sin + t2 * cos], axis=1)

    decays = []
    for h in range(RET_HEADS):
        log_gamma = math.log1p(-(2.0 ** (-5.0 - h)))
        decays.append((jnp.where(causal, jnp.exp(jnp.where(causal, rel, 0.0) * log_gamma), 0.0),
                       jnp.exp((idx + 1.0) * log_gamma),
                       jnp.exp((c_len - 1.0 - idx) * log_gamma),
                       math.exp(c_len * log_gamma)))

    def chunk(c, carry):
        r0 = pl.multiple_of(c * c_len, c_len)
        rows = pl.ds(r0, c_len)
        cos = cos_ref[rows, :]
        sin = sin_ref[rows, :]
        for h in range(RET_HEADS):
            inner, q_decay, k_decay, chunk_decay = decays[h]
            q = rot(q_ref[0, rows, h * dk:(h + 1) * dk].astype(F32), cos, sin)
            k = rot(k_ref[0, rows, h * dk:(h + 1) * dk].astype(F32), cos, sin) * (dk ** -0.5)
            v = v_ref[0, rows, h * dv:(h + 1) * dv]
            qb = q.astype(BF16)
            sc = _dot_nt(qb, k.astype(BF16)) * inner
            st = state_ref[h]
            o = _dot(sc.astype(BF16), v) + _dot(qb, st.astype(BF16)) * q_decay
            kd_t = jnp.transpose(k * k_decay).astype(BF16)
            state_ref[h] = st * chunk_decay + _dot(kd_t, v)
            gn = gn_ref[:, h * dv:(h + 1) * dv]
            on = _rms(o, gn)
            gate = g_ref[0, rows, h * dv:(h + 1) * dv].astype(F32)
            o_ref[0, rows, h * dv:(h + 1) * dv] = (_silu(gate) * on).astype(o_ref.dtype)
        return carry

    lax.fori_loop(0, ts // c_len, chunk, 0)


def _retention_core(proj, cos, sin, gn_gain, *, b, s, ts=TOKEN_TILE):
    n = proj.shape[2]
    hk = n // 6
    hv = 2 * hk
    dk = hk // RET_HEADS
    return pl.pallas_call(
        functools.partial(_ret_body, ts=ts),
        grid=(b, s // ts),
        in_specs=[
            pl.BlockSpec((1, ts, hk), lambda i, j: (i, j, 0)),
            pl.BlockSpec((1, ts, hk), lambda i, j: (i, j, 1)),
            pl.BlockSpec((1, ts, hv), lambda i, j: (i, j, 1)),
            pl.BlockSpec((1, ts, hv), lambda i, j: (i, j, 2)),
            pl.BlockSpec((ts, dk // 2), lambda i, j: (j, 0)),
            pl.BlockSpec((ts, dk // 2), lambda i, j: (j, 0)),
            pl.BlockSpec((1, hv), lambda i, j: (0, 0)),
        ],
        out_specs=pl.BlockSpec((1, ts, hv), lambda i, j: (i, j, 0)),
        out_shape=jax.ShapeDtypeStruct((b, s, hv), BF16),
        scratch_shapes=[pltpu.VMEM((RET_HEADS, dk, hv // RET_HEADS), F32)],
        compiler_params=_params("parallel", "arbitrary"),
        name="retention",
    )(proj, proj, proj, proj, cos, sin, gn_gain.reshape(1, hv))


def _rope_tables(s, half):
    inv = ROPE_BASE ** (-jnp.arange(half, dtype=F32) / half)
    ang = jnp.arange(s, dtype=F32)[:, None] * inv[None, :]
    return jnp.cos(ang), jnp.sin(ang)


def _retention(h2, g_norm, w_in, gn_gain, w_out, *, b, s):
    (proj,) = _norm_proj(h2, g_norm, [w_in.astype(BF16)], [BF16])
    n = proj.shape[1]
    cos, sin = _rope_tables(s, n // 6 // RET_HEADS // 2)
    y = _retention_core(proj.reshape(b, s, n), cos, sin, gn_gain, b=b, s=s)
    return y.reshape(b * s, -1), w_out.astype(BF16)


def _dil_proj_body(h_ref, g_ref, w_ref, o_ref, xn_ref, *, r):
    tm = h_ref.shape[0]
    n = tm // r
    xn = _rms(h_ref[...], g_ref[...])
    slabs = xn_ref.shape[0]
    for k in range(slabs):
        xn_ref[k] = xn[:, k * LANES:(k + 1) * LANES]
    xp = jnp.concatenate(
        [jnp.concatenate([xn_ref[k, pl.ds(c, n, stride=r), :] for c in range(r)], axis=0) for k in range(slabs)],
        axis=1).astype(BF16)
    for sl in _col_slices(w_ref.shape[1]):
        res = _dot(xp, w_ref[:, sl]).astype(o_ref.dtype)
        for c in range(r):
            o_ref[0, c, :, sl] = res[c * n:(c + 1) * n, :]


def _dil_proj(h2, g_norm, w, r, *, b, s, tm=TOKEN_TILE):
    d = h2.shape[1]
    n_out = w.shape[1]
    tiles = s // tm
    return pl.pallas_call(
        functools.partial(_dil_proj_body, r=r),
        grid=(b * tiles,),
        in_specs=[
            pl.BlockSpec((tm, d), lambda i: (i, 0)),
            pl.BlockSpec((1, d), lambda i: (0, 0)),
            _resident((d, n_out)),
        ],
        out_specs=pl.BlockSpec((1, r, tm // r, n_out), lambda i: (i // tiles, 0, i % tiles, 0)),
        out_shape=jax.ShapeDtypeStruct((b, r, s // r, n_out), BF16),
        scratch_shapes=[pltpu.VMEM((d // LANES, tm, LANES), F32)],
        compiler_params=_params("parallel"),
        name=f"dilated_proj_r{r}",
    )(h2, g_norm.reshape(1, d), w)


def _dil_body(q_ref, kc_ref, kp_ref, vc_ref, vp_ref, o_ref, lse_ref, kbuf, vbuf, *, rows, n_back):
    qb = DIL_QB
    dh = DIL_DH
    first_step = pl.program_id(2) == 0
    kbuf[0:qb, :] = kp_ref[0, 0]
    kbuf[qb:, :] = kc_ref[0, 0]
    vbuf[0:qb, :] = vp_ref[0, 0]
    vbuf[qb:, :] = vc_ref[0, 0]
    qi = lax.broadcasted_iota(jnp.int32, (qb, 2 * qb), 0)
    kj = lax.broadcasted_iota(jnp.int32, (qb, 2 * qb), 1)
    dist = qi + qb - kj
    band = (dist >= 0) & (dist <= n_back)
    lane = lax.broadcasted_iota(jnp.int32, (qb, LANES), 1)
    scale = dh ** -0.5
    ones = jnp.ones((2 * qb, dh), BF16)
    for i in range(rows // qb):
        mask = band & ((kj >= qb) | jnp.logical_not(first_step)) if i == 0 else band
        lse_tile = jnp.zeros((qb, LANES), F32)
        for h in range(DIL_HEADS):
            cols = slice(h * dh, (h + 1) * dh)
            q = q_ref[0, 0, i * qb:(i + 1) * qb, cols]
            k = kbuf[i * qb:(i + 2) * qb, cols]
            v = vbuf[i * qb:(i + 2) * qb, cols]
            s = jnp.where(mask, _dot_nt(q, k), NEG_INF)
            m = jnp.max(s, axis=1, keepdims=True)
            e = jnp.exp2((s - m) * (scale * math.log2(math.e))).astype(BF16)
            acc = _dot(e, jnp.concatenate([v, ones], axis=1))
            den = jnp.maximum(acc[:, dh:], 1e-30)
            o_ref[0, 0, i * qb:(i + 1) * qb, cols] = (acc[:, :dh] / den).astype(o_ref.dtype)
            lse_tile = jnp.where(lane == h, m * scale + jnp.log(den), lse_tile)
        lse_ref[0, 0, i * qb:(i + 1) * qb, :] = lse_tile


def _dilated_group(proj, win, r, *, b, s):
    hd = DIL_HEADS * DIL_DH
    length = s // r
    rows = min(length, TOKEN_TILE)
    qb = DIL_QB

    def cur(which):
        return pl.BlockSpec((1, 1, rows, hd), lambda i, c, n: (i, c, n, which))

    def prev(which):
        return pl.BlockSpec((1, 1, qb, hd), lambda i, c, n: (i, c, jnp.maximum(n * (rows // qb) - 1, 0), which))

    return pl.pallas_call(
        functools.partial(_dil_body, rows=rows, n_back=win // r),
        grid=(b, r, length // rows),
        in_specs=[cur(0), cur(1), prev(1), cur(2), prev(2)],
        out_specs=[
            pl.BlockSpec((1, 1, rows, hd), lambda i, c, n: (i, c, n, 0)),
            pl.BlockSpec((1, 1, rows, LANES), lambda i, c, n: (i, c, n, 0)),
        ],
        out_shape=[
            jax.ShapeDtypeStruct((b, r, length, hd), BF16),
            jax.ShapeDtypeStruct((b, r, length, LANES), F32),
        ],
        scratch_shapes=[pltpu.VMEM((rows + qb, hd), BF16), pltpu.VMEM((rows + qb, hd), BF16)],
        compiler_params=_params("parallel", "parallel", "arbitrary"),
        name=f"dilated_r{r}",
    )(proj, proj, proj, proj, proj)


def _dil_merge_body(h_ref, *refs, dilations):
    ng = len(dilations)
    o_refs, l_refs = refs[:ng], refs[ng:2 * ng]
    w_ref, out_ref = refs[2 * ng], refs[2 * ng + 1]
    o_bufs, l_bufs = refs[2 * ng + 2:3 * ng + 2], refs[3 * ng + 2:]
    tm = h_ref.shape[1]
    for o_ref, l_ref, o_buf, l_buf, r in zip(o_refs, l_refs, o_bufs, l_bufs, dilations):
        n = tm // r
        for c in range(r):
            rows = pl.ds(c, n, stride=r)
            l_buf[rows, :] = l_ref[0, c]
            blk = o_ref[0, c].astype(F32)
            for h in range(DIL_HEADS):
                o_buf[h, rows, :] = blk[:, h * DIL_DH:(h + 1) * DIL_DH]
    lses = [l[...] for l in l_bufs]
    mx = functools.reduce(jnp.maximum, lses)
    es = [jnp.exp(l - mx) for l in lses]
    tot = functools.reduce(jnp.add, es)
    wts = [e / tot for e in es]
    dh = DIL_DH
    parts = []
    for h in range(DIL_HEADS):
        acc = jnp.zeros((tm, dh), F32)
        for g in range(ng):
            wg = jnp.broadcast_to(wts[g][:, h:h + 1], (tm, dh))
            acc = acc + wg * o_bufs[g][h]
        parts.append(acc)
    o = jnp.concatenate(parts, axis=1).astype(BF16)
    out_ref[0] = h_ref[0] + _dot(o, w_ref[...])


def _dil_merge(h3, outs, lses, w, dilations, *, tm=TOKEN_TILE):
    b, s, d = h3.shape
    hd = w.shape[0]
    res_major = lambda r, width: pl.BlockSpec((1, r, tm // r, width), lambda i, n: (i, 0, n, 0))
    return pl.pallas_call(
        functools.partial(_dil_merge_body, dilations=tuple(dilations)),
        grid=(b, s // tm),
        in_specs=[pl.BlockSpec((1, tm, d), lambda i, n: (i, n, 0))]
        + [res_major(r, hd) for r in dilations] + [res_major(r, LANES) for r in dilations]
        + [pl.BlockSpec((hd, d), lambda i, n: (0, 0))],
        out_specs=pl.BlockSpec((1, tm, d), lambda i, n: (i, n, 0)),
        out_shape=jax.ShapeDtypeStruct((b, s, d), F32),
        scratch_shapes=[pltpu.VMEM((DIL_HEADS, tm, DIL_DH), F32) for _ in dilations]
        + [pltpu.VMEM((tm, LANES), F32) for _ in dilations],
        compiler_params=_params("parallel", "parallel"),
        name="dilated_merge",
    )(h3, *outs, *lses, w)


def _dilated(h2, g_norm, w_in, w_out, *, b, s):
    d = h2.shape[1]
    gw = 3 * DIL_HEADS * DIL_DH
    wb = w_in.astype(BF16)
    outs, lses, dilations = [], [], []
    for g, (win, r) in enumerate(DIL_PATTERN):
        proj = _dil_proj(h2, g_norm, wb[:, g * gw:(g + 1) * gw], r, b=b, s=s, tm=2 * TOKEN_TILE)
        o, lse = _dilated_group(proj, win, r, b=b, s=s)
        outs.append(o)
        lses.append(lse)
        dilations.append(r)
    return _dil_merge(h2.reshape(b, s, d), outs, lses, w_out.astype(BF16), dilations).reshape(b * s, d)


def _softplus(x):
    return jnp.maximum(x, 0.0) + jnp.log1p(jnp.exp(-jnp.abs(x)))


def _ssd_body(z_ref, x_ref, dt_ref, cw_ref, cb_ref, dtb_ref, alog_ref, dsk_ref, ng_ref, o_ref,
              xpad, state_ref):
    ln = SSD_CHUNK
    nst = SSD_STATE
    gw = SSD_HPG * SSD_HEADDIM
    d_inner = SSD_GROUPS * gw
    halo = 8

    @pl.when(pl.program_id(1) == 0)
    def _():
        xpad[0:halo, :] = jnp.zeros((halo, xpad.shape[1]), F32)
        state_ref[...] = jnp.zeros(state_ref.shape, F32)

    x_b = x_ref[0]
    x_f = x_b.astype(F32)
    ii = lax.broadcasted_iota(jnp.int32, (ln, ln), 0)
    jj = lax.broadcasted_iota(jnp.int32, (ln, ln), 1)
    conv = cb_ref[...] + x_f * cw_ref[SSD_CONV - 1:SSD_CONV, :]
    for shift in range(1, SSD_CONV):
        shifted = _dot(jnp.where(ii - jj == shift, 1.0, 0.0).astype(BF16), x_b)
        conv = conv + shifted * cw_ref[SSD_CONV - 1 - shift:SSD_CONV - shift, :]
    xpad[halo:2 * halo, :] = x_f[0:halo, :]
    head = cb_ref[...]
    for k in range(SSD_CONV):
        off = halo - (SSD_CONV - 1) + k
        head = head + xpad[off:off + halo, :] * cw_ref[k:k + 1, :]
    xpad[0:halo, :] = x_f[ln - halo:ln, :]
    xbc = _silu(jnp.concatenate([head, conv[halo:, :]], axis=0))
    xs = xbc[:, :d_inner]
    bm = xbc[:, d_inner:d_inner + SSD_GROUPS * nst]
    cm = xbc[:, d_inner + SSD_GROUPS * nst:]

    dt = _softplus(dt_ref[0] + dtb_ref[...])
    da = dt * (-jnp.exp(alog_ref[...]))
    ii = lax.broadcasted_iota(jnp.int32, (ln, ln), 0)
    jj = lax.broadcasted_iota(jnp.int32, (ln, ln), 1)
    causal = ii >= jj
    tril = jnp.where(causal, 1.0, 0.0).astype(BF16)
    da_terms = _split3(da)
    acs = _dot(tril, da_terms[0]) + _dot(tril, da_terms[1]) + _dot(tril, da_terms[2])
    acs2 = acs * math.log2(math.e)
    acs2_t = jnp.transpose(acs2)
    hl = ln // 2

    erow = lax.broadcasted_iota(jnp.int32, (LANES, d_inner), 0)
    ecol = lax.broadcasted_iota(jnp.int32, (LANES, d_inner), 1)
    expand = jnp.where(ecol // SSD_HEADDIM == erow, 1.0, 0.0).astype(BF16)
    acs_e = _dot3(_split3(acs), expand)
    dt_e = _dot3(_split3(dt), expand)
    last = acs_e[ln - 1:ln, :]
    decay_in = jnp.exp(acs_e)
    xs_dt = xs * dt_e
    xs_end = (xs_dt * jnp.exp(last - acs_e)).astype(BF16)
    xs_b = xs_dt.astype(BF16)
    lane = lax.broadcasted_iota(jnp.int32, (ln, LANES), 1)

    y_groups = []
    for g in range(SSD_GROUPS):
        bm_g = bm[:, g * nst:(g + 1) * nst]
        cm_g = cm[:, g * nst:(g + 1) * nst].astype(BF16)
        cb = jnp.where(causal, _dot_nt(cm_g, bm_g.astype(BF16)), 0.0)
        st = state_ref[g]
        gcols = slice(g * gw, (g + 1) * gw)
        y_state = _dot(cm_g, st.astype(BF16)) * decay_in[:, gcols]
        pairs = []
        for p in range(SSD_HPG // 2):
            pair_cols = slice(g * gw + p * LANES, g * gw + (p + 1) * LANES)
            halves = []
            for e in range(2):
                hd = g * SSD_HPG + 2 * p + e
                top = cb[:hl, :hl] * jnp.exp2(jnp.minimum(acs2[:hl, hd:hd + 1] - acs2_t[hd:hd + 1, :hl], 0.0))
                bot = cb[hl:, :] * jnp.exp2(jnp.minimum(acs2[hl:, hd:hd + 1] - acs2_t[hd:hd + 1, :], 0.0))
                halves.append(jnp.concatenate([_dot(top.astype(BF16), xs_b[:hl, pair_cols]),
                                               _dot(bot.astype(BF16), xs_b[:, pair_cols])], axis=0))
            pairs.append(jnp.where(lane < SSD_HEADDIM, halves[0], halves[1]))
        y_groups.append(jnp.concatenate(pairs, axis=1) + y_state)
        bm_t = jnp.transpose(bm_g).astype(BF16)
        state_ref[g] = st * jnp.exp(last[:, gcols]) + _dot(bm_t, xs_end[:, gcols])

    y = jnp.concatenate(y_groups, axis=1) + dsk_ref[...] * xs
    yz = y * _silu(z_ref[0].astype(F32))
    outs = [_rms(yz[:, g * gw:(g + 1) * gw], ng_ref[:, g * gw:(g + 1) * gw]) for g in range(SSD_GROUPS)]
    o_ref[0] = jnp.concatenate(outs, axis=1).astype(o_ref.dtype)


def _ssd_core(z, xbc, dt, conv_w, conv_b, dt_bias, a_log, d_skip, norm_g, *, b, s):
    ln = SSD_CHUNK
    d_inner = z.shape[2]
    conv_dim = xbc.shape[2]
    gw = SSD_HPG * SSD_HEADDIM
    full = lambda shape: pl.BlockSpec(shape, lambda i, j: (0,) * len(shape))
    return pl.pallas_call(
        _ssd_body,
        grid=(b, s // ln),
        in_specs=[
            pl.BlockSpec((1, ln, d_inner), lambda i, j: (i, j, 0)),
            pl.BlockSpec((1, ln, conv_dim), lambda i, j: (i, j, 0)),
            pl.BlockSpec((1, ln, LANES), lambda i, j: (i, j, 0)),
            full((SSD_CONV, conv_dim)),
            full((1, conv_dim)),
            full((1, LANES)),
            full((1, LANES)),
            full((1, d_inner)),
            full((1, d_inner)),
        ],
        out_specs=pl.BlockSpec((1, ln, d_inner), lambda i, j: (i, j, 0)),
        out_shape=jax.ShapeDtypeStruct((b, s, d_inner), BF16),
        scratch_shapes=[pltpu.VMEM((16, conv_dim), F32),
                        pltpu.VMEM((SSD_GROUPS, SSD_STATE, gw), F32)],
        compiler_params=_params("parallel", "arbitrary"),
        name="ssd",
    )(z, xbc, dt, conv_w, conv_b, dt_bias, a_log, d_skip, norm_g)


def _ssd(h2, g_norm, w_in, conv_w, conv_b, dt_bias, a_log, d_skip, norm_g, w_out, *, b, s):
    heads = SSD_GROUPS * SSD_HPG
    d_inner = heads * SSD_HEADDIM
    conv_dim = conv_w.shape[1]
    wb = w_in.astype(BF16)
    w_dt = jnp.pad(wb[:, d_inner + conv_dim:], ((0, 0), (0, LANES - heads)))
    z, xbc, dt = _norm_proj(h2, g_norm, [wb[:, :d_inner], wb[:, d_inner:d_inner + conv_dim], w_dt], [BF16, BF16, F32])
    pad_heads = lambda v: jnp.pad(v.astype(F32), (0, LANES - heads)).reshape(1, LANES)
    y = _ssd_core(z.reshape(b, s, -1), xbc.reshape(b, s, -1), dt.reshape(b, s, LANES),
                  conv_w.astype(F32), conv_b.reshape(1, -1).astype(F32), pad_heads(dt_bias), pad_heads(a_log),
                  jnp.repeat(d_skip.astype(F32), SSD_HEADDIM).reshape(1, d_inner), norm_g.reshape(1, d_inner),
                  b=b, s=s)
    return y.reshape(b * s, d_inner), w_out.astype(BF16)


def _nsa_proj_body(h_ref, g_ref, wn_ref, wt_ref, k0_ref, v0_ref, k12_ref, tr_ref, stage_ref):
    xn = _rms(h_ref[...], g_ref[...]).astype(BF16)
    nat = _dot(xn, wn_ref[...])
    tm = nat.shape[0]
    gd = k12_ref.shape[1] // 2
    k12_ref[...] = nat[:, 2 * gd:].astype(k12_ref.dtype)
    slabs = gd // LANES
    rows = tm // NSA_CMP_STRIDE
    for sl in range(2 * slabs):
        stage_ref[sl] = nat[:, sl * LANES:(sl + 1) * LANES]
    for l in range(NSA_CMP_STRIDE):
        for sl in range(slabs):
            cols = slice(l * gd + sl * LANES, l * gd + (sl + 1) * LANES)
            k0_ref[:, cols] = stage_ref[sl, pl.ds(l, rows, stride=NSA_CMP_STRIDE), :].astype(k0_ref.dtype)
            v0_ref[:, cols] = stage_ref[slabs + sl, pl.ds(l, rows, stride=NSA_CMP_STRIDE), :].astype(v0_ref.dtype)
    res = _dot_nt(wt_ref[...], xn)
    for j in range(tr_ref.shape[0]):
        tr_ref[j] = res[:, j * NSA_QT:(j + 1) * NSA_QT].astype(tr_ref.dtype)


def _nsa_proj(h2, g_norm, w_nat, w_t, *, tm=TOKEN_TILE):
    m, d = h2.shape
    gd = NSA_GROUPS * NSA_DH
    nt = w_t.shape[0]
    slabs = tm // NSA_QT
    return pl.pallas_call(
        _nsa_proj_body,
        grid=(m // tm,),
        in_specs=[
            pl.BlockSpec((tm, d), lambda i: (i, 0)),
            pl.BlockSpec((1, d), lambda i: (0, 0)),
            _resident(w_nat.shape),
            _resident(w_t.shape),
        ],
        out_specs=[
            pl.BlockSpec((tm // NSA_CMP_STRIDE, NSA_CMP_STRIDE * gd), lambda i: (i, 0)),
            pl.BlockSpec((tm // NSA_CMP_STRIDE, NSA_CMP_STRIDE * gd), lambda i: (i, 0)),
            pl.BlockSpec((tm, 2 * gd), lambda i: (i, 0)),
            pl.BlockSpec((slabs, nt, NSA_QT), lambda i: (i, 0, 0)),
        ],
        out_shape=[
            jax.ShapeDtypeStruct((m // NSA_CMP_STRIDE, NSA_CMP_STRIDE * gd), BF16),
            jax.ShapeDtypeStruct((m // NSA_CMP_STRIDE, NSA_CMP_STRIDE * gd), BF16),
            jax.ShapeDtypeStruct((m, 2 * gd), BF16),
            jax.ShapeDtypeStruct((m // NSA_QT, nt, NSA_QT), BF16),
        ],
        scratch_shapes=[pltpu.VMEM((2 * gd // LANES, tm, LANES), F32)],
        compiler_params=_params("parallel"),
        name="nsa_proj",
    )(h2, g_norm.reshape(1, d), w_nat, w_t)


def _nsa_cmp_body(x_ref, pa_ref, pb_ref, w1_ref, w2_ref, o_ref, w1a_big, w1b_big, w2_big, *, transposed):
    g, dh = NSA_GROUPS, NSA_DH
    hid = w1_ref.shape[1]
    half = NSA_CMP_LEN // 2

    @pl.when(pl.program_id(0) == 0)
    def _():
        w1a_big[...] = jnp.zeros(w1a_big.shape, BF16)
        w1b_big[...] = jnp.zeros(w1b_big.shape, BF16)
        w2_big[...] = jnp.zeros(w2_big.shape, BF16)
        for gi in range(g):
            cols = slice(gi * hid, (gi + 1) * hid)
            for l in range(half):
                rows = slice((l * g + gi) * dh, (l * g + gi + 1) * dh)
                w1a_big[rows, cols] = w1_ref[l * dh:(l + 1) * dh, :]
                w1b_big[rows, cols] = w1_ref[(half + l) * dh:(half + l + 1) * dh, :]
            w2_big[cols, gi * dh:(gi + 1) * dh] = w2_ref[...]

    x = x_ref[0].astype(F32)
    nb = x.shape[0]
    ya = _dot((x + pa_ref[...]).astype(BF16), w1a_big[...])
    yb = _dot((x + pb_ref[...]).astype(BF16), w1b_big[...])
    hidden = _silu(ya + pltpu.roll(yb, nb - 1, 0))
    out = _dot(hidden.astype(BF16), w2_big[...])
    o_ref[0] = (jnp.transpose(out) if transposed else out).astype(o_ref.dtype)


def _nsa_compress(x, pos, w1, w2, *, transposed):
    b, nb, width = x.shape
    g, dh = NSA_GROUPS, NSA_DH
    half = NSA_CMP_LEN // 2
    hid = w1.shape[1]
    posb = jnp.broadcast_to(pos[:, None, :], (NSA_CMP_LEN, g, dh)).reshape(NSA_CMP_LEN, g * dh)
    pa = posb[:half].reshape(1, width).astype(F32)
    pb = posb[half:].reshape(1, width).astype(F32)
    out_dims = (g * dh, nb) if transposed else (nb, g * dh)
    return pl.pallas_call(
        functools.partial(_nsa_cmp_body, transposed=transposed),
        grid=(b,),
        in_specs=[
            pl.BlockSpec((1, nb, width), lambda i: (i, 0, 0)),
            _resident((1, width)), _resident((1, width)),
            _resident(w1.shape), _resident(w2.shape),
        ],
        out_specs=pl.BlockSpec((1,) + out_dims, lambda i: (i, 0, 0)),
        out_shape=jax.ShapeDtypeStruct((b,) + out_dims, BF16),
        scratch_shapes=[pltpu.VMEM((width, g * hid), BF16), pltpu.VMEM((width, g * hid), BF16),
                        pltpu.VMEM((g * hid, g * dh), BF16)],
        compiler_params=_params("arbitrary"),
        name="nsa_compress",
    )(x, pa, pb, w1.astype(BF16), w2.astype(BF16))


def _nsa_pair_body(q_ref, gt_ref, kc_ref, vct_ref, k1_ref, k2_ref, v1_ref, v2_ref, o_ref, bias_ref, wbias_ref, *, s):
    qt, kc, dh = NSA_QT, NSA_KC, NSA_DH
    gl = 2
    glanes = NSA_HPG * qt
    lanes = gl * glanes
    vrows = gl * dh
    j = pl.program_id(2)
    t0 = j * qt
    nb = kc_ref.shape[1]
    nsb = s // NSA_SEL_LEN
    n_sel = min(NSA_N_SEL, nsb)

    q_t = q_ref[0, 0]
    qcat = [jnp.concatenate([q_t[(g * NSA_HPG + h) * dh:(g * NSA_HPG + h + 1) * dh, :] for h in range(NSA_HPG)],
                            axis=1) for g in range(gl)]
    zero = jnp.zeros_like(qcat[0])
    qp = jnp.concatenate([jnp.concatenate([qcat[0], zero], axis=1),
                          jnp.concatenate([zero, qcat[1]], axis=1)], axis=0)
    qlane = lax.broadcasted_iota(jnp.int32, (1, lanes), 1) % qt
    tq = t0 + qlane

    def values(v_ref, slab0, n_slabs):
        v = jnp.concatenate([v_ref[0, slab0 + i] for i in range(n_slabs)], axis=1)
        return jnp.concatenate([v, jnp.ones((16, n_slabs * qt), BF16)], axis=0)

    def weighted_values(v_aug, p):
        return jnp.concatenate(
            [_dot(jnp.concatenate([v_aug[g * dh:(g + 1) * dh, :], v_aug[vrows:, :]], axis=0),
                  p[:, g * glanes:(g + 1) * glanes]) for g in range(gl)], axis=1)

    def normalise(acc):
        o = acc[:dh, :] * (1.0 / jnp.maximum(acc[dh:dh + 1, :], 1e-30))
        return [o[:, g * glanes:(g + 1) * glanes] for g in range(gl)]

    wslabs = NSA_WINDOW // qt + 1
    row_w = lax.broadcasted_iota(jnp.int32, (qt, lanes), 0)

    @pl.when(j == 0)
    def _():
        wbias_ref[0] = jnp.where(row_w > qlane, 0.0, NEG_INF)
        wbias_ref[1] = jnp.zeros((qt, lanes), F32)
        wbias_ref[2] = jnp.where(row_w <= qlane, 0.0, NEG_INF)
        wbias_ref[3] = jnp.full((qt, lanes), NEG_INF, F32)

    jb0 = jnp.maximum(j + 1 - wslabs, 0)
    kw = k2_ref[0, pl.ds(pl.multiple_of(jb0 * qt, qt), wslabs * qt), :]
    sw_raw = _dot(kw, qp)
    slabs_back = [j - jb0 - i for i in range(wslabs)]
    sw = jnp.concatenate(
        [sw_raw[i * qt:(i + 1) * qt, :]
         + wbias_ref[jnp.where(back < 0, 3, jnp.where(back == wslabs - 1, 0, jnp.where(back == 0, 2, 1)))]
         for i, back in enumerate(slabs_back)], axis=0)

    sc = _dot(kc_ref[0], qp)
    assert NSA_CMP_STRIDE == 16
    last_visible = jnp.minimum(jnp.right_shift(tq - (NSA_CMP_LEN - 1), 4), nb - 2)
    sc = jnp.where(lax.broadcasted_iota(jnp.int32, (nb, lanes), 0) <= last_visible, sc, NEG_INF)
    m_c = jnp.max(sc, axis=0, keepdims=True)
    e = jnp.exp(sc - jnp.where(m_c > 0.5 * NEG_INF, m_c, 0.0))
    p_c = e * (1.0 / jnp.maximum(jnp.sum(e, axis=0, keepdims=True), 1e-30))
    acc_c = _dot(vct_ref[0], p_c.astype(BF16))
    o_c = [acc_c[g * dh:(g + 1) * dh, g * glanes:(g + 1) * glanes] for g in range(gl)]
    imps = []
    for g in range(gl):
        imp = p_c[:, g * glanes:g * glanes + qt]
        for h in range(1, NSA_HPG):
            imp = imp + p_c[:, g * glanes + h * qt:g * glanes + (h + 1) * qt]
        imps.append(imp)
    imp = jnp.concatenate(imps, axis=1)

    ratio = NSA_SEL_LEN // NSA_CMP_STRIDE
    dd = lax.broadcasted_iota(jnp.int32, (nsb, nb), 1) - ratio * lax.broadcasted_iota(jnp.int32, (nsb, nb), 0)
    wsel = jnp.where((dd == -1) | (dd == ratio - 1), 1.0,
                     jnp.where((dd >= 0) & (dd < ratio - 1), 2.0, 0.0)).astype(BF16)
    terms = _split3(imp)
    imp_sel = _dot(wsel, terms[0]) + _dot(wsel, terms[1]) + _dot(wsel, terms[2])
    jq = lax.broadcasted_iota(jnp.int32, (nsb, gl * qt), 0)
    cur = (t0 + lax.broadcasted_iota(jnp.int32, (nsb, gl * qt), 1) % qt) // NSA_SEL_LEN
    forced = (jq == 0) | (jq == cur) | (jq == cur - 1)
    score = jnp.where(jq <= cur, imp_sel + jnp.where(forced, NSA_FORCE_BONUS, 0.0), NEG_INF)

    pw = jnp.exp(sw - jnp.max(sw, axis=0, keepdims=True)).astype(BF16)
    o_w = normalise(weighted_values(values(v2_ref, jb0, wslabs), pw))

    jqf = jq.astype(F32)
    work = score
    picked = None
    for _ in range(n_sel):
        top = jnp.max(work, axis=0, keepdims=True)
        first = jnp.min(jnp.where(work == top, jqf, float(nsb)), axis=0, keepdims=True)
        pick = jqf == first
        picked = pick if picked is None else picked | pick
        work = jnp.where(pick, 3.0 * NEG_INF, work)
    sel_bias = jnp.where(picked & (score > 0.5 * NEG_INF), 0.0, NEG_INF)
    bias_ref[...] = jnp.concatenate([sel_bias[:, g * qt:(g + 1) * qt] for g in range(gl) for _ in range(NSA_HPG)],
                                    axis=1)

    blocks_per_chunk = kc // NSA_SEL_LEN
    slabs_per_chunk = kc // qt

    def online(carry, st, bias, v_aug):
        m, acc = carry
        blks = [st[r * NSA_SEL_LEN:(r + 1) * NSA_SEL_LEN, :] for r in range(blocks_per_chunk)]
        part = None
        for r in range(blocks_per_chunk):
            pr = blks[r][0:8, :]
            for i in range(1, NSA_SEL_LEN // 8):
                pr = jnp.maximum(pr, blks[r][8 * i:8 * (i + 1), :])
            pr = pr + bias[r:r + 1, :]
            part = pr if part is None else jnp.maximum(part, pr)
        m_new = jnp.maximum(m, jnp.max(part, axis=0, keepdims=True))
        p = jnp.concatenate([jnp.exp(blks[r] - (m_new - bias[r:r + 1, :])) for r in range(blocks_per_chunk)],
                            axis=0).astype(BF16)
        return m_new, jnp.exp(m - m_new) * acc + weighted_values(v_aug, p)

    def sel_scores(c):
        return _dot(k1_ref[0, pl.ds(pl.multiple_of(c * kc, kc), kc), :], qp)

    def sel_bias(c):
        return bias_ref[pl.ds(pl.multiple_of(c * blocks_per_chunk, blocks_per_chunk), blocks_per_chunk), :]

    def sel_values(c):
        return values(v1_ref, c * slabs_per_chunk, slabs_per_chunk)

    init = (jnp.full((1, lanes), NEG_INF, F32), jnp.zeros((dh + 16, lanes), F32))
    c_hi = t0 // kc
    own_slab = (t0 - c_hi * kc) // qt
    st_raw = sel_scores(c_hi)
    st = jnp.concatenate(
        [st_raw[i * qt:(i + 1) * qt, :] + wbias_ref[jnp.where(i < own_slab, 1, jnp.where(i == own_slab, 2, 3))]
         for i in range(slabs_per_chunk)], axis=0)
    chain_a = online(init, st, sel_bias(c_hi), sel_values(c_hi))
    chain_a = lax.cond(c_hi % 2 == 1,
                       lambda cr: online(cr, sel_scores(c_hi - 1), sel_bias(c_hi - 1), sel_values(c_hi - 1)),
                       lambda cr: cr, chain_a)
    chain_b = (chain_a[0], jnp.zeros_like(chain_a[1]))

    def pair(i, chains):
        ca, cb = chains
        st_a = sel_scores(2 * i)
        st_b = sel_scores(2 * i + 1)
        return (online(ca, st_a, sel_bias(2 * i), sel_values(2 * i)),
                online(cb, st_b, sel_bias(2 * i + 1), sel_values(2 * i + 1)))

    (m_a, acc_a), (m_b, acc_b) = lax.fori_loop(0, c_hi // 2, pair, (chain_a, chain_b))
    m_s = jnp.maximum(m_a, m_b)
    o_s = normalise(jnp.exp(m_a - m_s) * acc_a + jnp.exp(m_b - m_s) * acc_b)

    gates = jax.nn.sigmoid(gt_ref[0, 0].astype(F32))
    outs = []
    for g in range(gl):
        def gate_row(br):
            r0 = (g * 4 + br) * NSA_HPG
            return jnp.concatenate([gates[r0 + h:r0 + h + 1, :] for h in range(NSA_HPG)], axis=1)
        o = gate_row(0) * o_c[g] + gate_row(1) * o_s[g] + gate_row(2) * o_w[g]
        outs.extend(o[:, h * qt:(h + 1) * qt] for h in range(NSA_HPG))
    o_ref[0] = jnp.transpose(jnp.concatenate(outs, axis=0)).astype(o_ref.dtype)


def _nsa_attend_pairs(tr, k_cmp, v_cmp_t, k12, *, b, s):
    qt, dh = NSA_QT, NSA_DH
    gl = 2
    gd = NSA_GROUPS * dh
    hq = NSA_GROUPS * NSA_HPG * dh
    pairs = NSA_GROUPS // gl
    nslab = s // qt
    assert nslab > NSA_WINDOW // qt
    nb = k_cmp.shape[1]
    tr4 = tr.reshape(b, nslab, tr.shape[1], qt)
    q_rows = gl * NSA_HPG * dh
    v_rows = gl * dh
    gate_rows = gl * 4 * NSA_HPG
    v1_blk = hq // v_rows
    v2_blk = (hq + gd) // v_rows
    gate_blk = (hq + 2 * gd) // gate_rows
    nsb = s // NSA_SEL_LEN
    return pl.pallas_call(
        functools.partial(_nsa_pair_body, s=s),
        grid=(b, pairs, nslab),
        in_specs=[
            pl.BlockSpec((1, 1, q_rows, qt), lambda i, p, j: (i, j, p, 0)),
            pl.BlockSpec((1, 1, gate_rows, qt), lambda i, p, j: (i, j, gate_blk + p, 0)),
            pl.BlockSpec((1, nb, v_rows), lambda i, p, j: (i, 0, p)),
            pl.BlockSpec((1, v_rows, nb), lambda i, p, j: (i, p, 0)),
            pl.BlockSpec((1, s, v_rows), lambda i, p, j: (i, 0, p)),
            pl.BlockSpec((1, s, v_rows), lambda i, p, j: (i, 0, pairs + p)),
            pl.BlockSpec((1, nslab, v_rows, qt), lambda i, p, j: (i, 0, v1_blk + p, 0)),
            pl.BlockSpec((1, nslab, v_rows, qt), lambda i, p, j: (i, 0, v2_blk + p, 0)),
        ],
        out_specs=pl.BlockSpec((1, qt, q_rows), lambda i, p, j: (i, j, p)),
        out_shape=jax.ShapeDtypeStruct((b, s, hq), BF16),
        scratch_shapes=[pltpu.VMEM((nsb, gl * NSA_HPG * qt), F32), pltpu.VMEM((4, qt, gl * NSA_HPG * qt), F32)],
        compiler_params=_params("parallel", "parallel", "arbitrary"),
        name="nsa_attend",
    )(tr4, tr4, k_cmp, v_cmp_t, k12, k12, tr4, tr4)


def _nsa(h2, g_norm, w_in, cmp_pos, cmp_w1, cmp_w2, w_out, *, b, s):
    g, hpg, dh = NSA_GROUPS, NSA_HPG, NSA_DH
    hq, gd = g * hpg * dh, g * dh
    kv_w = lambda br, kv: w_in[:, hq + (2 * br + kv) * gd:hq + (2 * br + kv + 1) * gd]
    w_q = w_in[:, :hq] * (dh ** -0.5)
    w_g = w_in[:, hq + 6 * gd:].reshape(-1, 3, g, hpg).transpose(2, 1, 3, 0)
    w_g = jnp.pad(w_g, ((0, 0), (0, 1), (0, 0), (0, 0))).reshape(g * 4 * hpg, -1)
    w_t = jnp.concatenate([w_q.T, kv_w(1, 1).T, kv_w(2, 1).T, w_g], axis=0).astype(BF16)
    w_nat = jnp.concatenate([kv_w(0, 0), kv_w(0, 1), kv_w(1, 0), kv_w(2, 0)], axis=1).astype(BF16)
    k0, v0, k12, tr = _nsa_proj(h2, g_norm, w_nat, w_t)
    nb = s // NSA_CMP_STRIDE
    k_cmp = _nsa_compress(k0.reshape(b, nb, NSA_CMP_STRIDE * gd), cmp_pos[0], cmp_w1[0], cmp_w2[0], transposed=False)
    v_cmp_t = _nsa_compress(v0.reshape(b, nb, NSA_CMP_STRIDE * gd), cmp_pos[1], cmp_w1[1], cmp_w2[1], transposed=True)
    y = _nsa_attend_pairs(tr, k_cmp, v_cmp_t, k12.reshape(b, s, 2 * gd), b=b, s=s)
    return y.reshape(b * s, hq), w_out.astype(BF16)


def kernel(x, norm_ffn1, ffn1_w_in, ffn1_w_out, norm_mix, norm_ffn2, ffn2_w_in, ffn2_w_out, norm_final,
           ret_w_in, ret_gn_gain, ret_w_out,
           nsa_w_in, nsa_cmp_pos, nsa_cmp_w1, nsa_cmp_w2, nsa_w_out,
           ssd_w_in, ssd_conv_w, ssd_conv_b, ssd_dt_bias, ssd_a_log, ssd_d, ssd_norm, ssd_w_out,
           dil_w_in, dil_w_out):
    b, s, d = x.shape
    depth = norm_mix.shape[0]
    h = x.reshape(b * s, d)
    w1_in, w1_out = ffn1_w_in.astype(BF16), ffn1_w_out.astype(BF16)
    w2_in, w2_out = ffn2_w_in.astype(BF16), ffn2_w_out.astype(BF16)
    for i in range(depth):
        h = _ffn(h, norm_ffn1[i], w1_in, w1_out, i)
        m, j = i % 4, i // 4
        mix = None
        if m == 0:
            mix = _retention(h, norm_mix[i], ret_w_in[j], ret_gn_gain[j], ret_w_out[j], b=b, s=s)
        elif m == 1:
            mix = _nsa(h, norm_mix[i], nsa_w_in[j], nsa_cmp_pos[j], nsa_cmp_w1[j], nsa_cmp_w2[j], nsa_w_out[j], b=b, s=s)
        elif m == 2:
            mix = _ssd(h, norm_mix[i], ssd_w_in[j], ssd_conv_w[j], ssd_conv_b[j], ssd_dt_bias[j], ssd_a_log[j],
                       ssd_d[j], ssd_norm[j], ssd_w_out[j], b=b, s=s)
        else:
            h = _dilated(h, norm_mix[i], dil_w_in[j], dil_w_out[j], b=b, s=s)
        h = _ffn(h, norm_ffn2[i], w2_in, w2_out, i,
                 norm_final if i == depth - 1 else None, mix)
    return h.reshape(b, s, d)
```

```python
import functools
import math

import jax
import jax.numpy as jnp
from jax import lax
from jax.experimental import pallas as pl
from jax.experimental.pallas import tpu as pltpu

F32 = jnp.float32
BF16 = jnp.bfloat16
NORM_EPS = 1e-6
NEG_INF = -1e30
ROPE_BASE = 10000.0
VMEM_LIMIT_BYTES = 56 * 1024 * 1024
LANES = 128
MXU_WIDTH = 256
TOKEN_TILE = 512
MAX_DOT_COLS = 2048

RET_HEADS = 4
RET_CHUNK = 128

NSA_GROUPS = 4
NSA_HPG = 4
NSA_DH = 64
NSA_CMP_LEN = 32
NSA_CMP_STRIDE = 16
NSA_SEL_LEN = 64
NSA_N_SEL = 16
NSA_WINDOW = 512
NSA_FORCE_BONUS = 1e4
NSA_QT = 256
NSA_KC = 512

SSD_GROUPS = 4
SSD_HPG = 8
SSD_HEADDIM = 64
SSD_STATE = 128
SSD_CONV = 4
SSD_CHUNK = 256

DIL_PATTERN = ((128, 1), (512, 4), (2048, 16))
DIL_HEADS = 8
DIL_DH = 128
DIL_QB = 128


def _params(*sem):
    return pltpu.CompilerParams(dimension_semantics=sem, vmem_limit_bytes=VMEM_LIMIT_BYTES)


def _rms(x, g):
    return x * lax.rsqrt(jnp.mean(x * x, axis=-1, keepdims=True) + NORM_EPS) * g


def _silu(x):
    hx = 0.5 * x
    return hx + hx * jnp.tanh(hx)


def _dot(a, b):
    return jnp.dot(a, b, preferred_element_type=F32)


def _dot_nt(a, b):
    return lax.dot_general(a, b, (((1,), (1,)), ((), ())), preferred_element_type=F32)


def _split3(x):
    hi = x.astype(BF16)
    r1 = x - hi.astype(F32)
    mid = r1.astype(BF16)
    lo = (r1 - mid.astype(F32)).astype(BF16)
    return hi, mid, lo


def _dot3(terms, rhs):
    return _dot(terms[0], rhs) + _dot(terms[1], rhs) + _dot(terms[2], rhs)


def _col_slices(n):
    if n <= MAX_DOT_COLS:
        return [slice(0, n)]
    tiles = -(-n // MXU_WIDTH)
    parts = -(-tiles * MXU_WIDTH // MAX_DOT_COLS)
    bounds = [min(n, (tiles * p // parts) * MXU_WIDTH) for p in range(parts + 1)]
    return [slice(lo, hi) for lo, hi in zip(bounds[:-1], bounds[1:])]


def _resident(shape, index=None):
    index = index if index is not None else (0,) * len(shape)
    return pl.BlockSpec(shape, lambda *_: index, pipeline_mode=pl.Buffered(1))


def _ffn_body(h_ref, *refs, final_norm, mix):
    x = h_ref[...]
    if mix:
        y_ref, wm_ref, *refs = refs
        x = x + _dot(y_ref[...], wm_ref[...])
    g_ref, wa_ref, wb_ref, wo_ref, gf_ref, o_ref = refs
    xn = _rms(x, g_ref[...]).astype(BF16)
    acc = jnp.zeros(x.shape, F32)
    for sl in _col_slices(wa_ref.shape[1]):
        a = _dot(xn, wa_ref[:, sl])
        b = _dot(xn, wb_ref[:, sl])
        acc = acc + _dot((_silu(a) * b).astype(BF16), wo_ref[sl, :])
    y = x + 0.5 * acc
    if final_norm:
        y = _rms(y, gf_ref[...])
    o_ref[...] = y


def _ffn(h, g, w_in, w_out, layer, g_final=None, mix=None, *, tm=TOKEN_TILE):
    m, d = h.shape
    f = w_out.shape[1]
    final_norm = g_final is not None
    gf = g_final if final_norm else g
    mix_args = list(mix) if mix is not None else []
    mix_specs = [pl.BlockSpec((tm, mix[0].shape[1]), lambda i: (i, 0)), _resident(mix[1].shape)] if mix_args else []
    return pl.pallas_call(
        functools.partial(_ffn_body, final_norm=final_norm, mix=bool(mix_args)),
        grid=(m // tm,),
        in_specs=[pl.BlockSpec((tm, d), lambda i: (i, 0))] + mix_specs + [
            _resident((1, d)),
            _resident((None, d, f), (layer, 0, 0)),
            _resident((None, d, f), (layer, 0, 1)),
            _resident((None, f, d), (layer, 0, 0)),
            _resident((1, d)),
        ],
        out_specs=pl.BlockSpec((tm, d), lambda i: (i, 0)),
        out_shape=jax.ShapeDtypeStruct((m, d), F32),
        compiler_params=_params("parallel"),
        name="ffn",
    )(h, *mix_args, g.reshape(1, d), w_in, w_in, w_out, gf.reshape(1, d))


def _norm_proj_body(h_ref, g_ref, *refs, n_out):
    xn = _rms(h_ref[...], g_ref[...]).astype(BF16)
    for w_ref, o_ref in zip(refs[:n_out], refs[n_out:]):
        for sl in _col_slices(w_ref.shape[1]):
            o_ref[:, sl] = _dot(xn, w_ref[:, sl]).astype(o_ref.dtype)


def _norm_proj(h, g, ws, out_dtypes, *, tm=TOKEN_TILE):
    m, d = h.shape
    return pl.pallas_call(
        functools.partial(_norm_proj_body, n_out=len(ws)),
        grid=(m // tm,),
        in_specs=[pl.BlockSpec((tm, d), lambda i: (i, 0)), pl.BlockSpec((1, d), lambda i: (0, 0))]
        + [_resident(w.shape) for w in ws],
        out_specs=[pl.BlockSpec((tm, w.shape[1]), lambda i: (i, 0)) for w in ws],
        out_shape=[jax.ShapeDtypeStruct((m, w.shape[1]), dt) for w, dt in zip(ws, out_dtypes)],
        compiler_params=_params("parallel"),
        name="norm_proj",
    )(h, g.reshape(1, d), *ws)


def _ret_body(q_ref, k_ref, v_ref, g_ref, cos_ref, sin_ref, gn_ref, o_ref, state_ref, *, ts):
    c_len = RET_CHUNK
    dk = q_ref.shape[2] // RET_HEADS
    dv = v_ref.shape[2] // RET_HEADS
    half = dk // 2

    @pl.when(pl.program_id(1) == 0)
    def _():
        state_ref[...] = jnp.zeros(state_ref.shape, F32)

    ii = lax.broadcasted_iota(jnp.int32, (c_len, c_len), 0)
    jj = lax.broadcasted_iota(jnp.int32, (c_len, c_len), 1)
    rel = (ii - jj).astype(F32)
    causal = ii >= jj
    idx = lax.broadcasted_iota(jnp.int32, (c_len, 1), 0).astype(F32)

    def rot(t, cos, sin):
        t1, t2 = t[:, :half], t[:, half:]
        return jnp.concatenate([t1 * cos - t2 * sin, t1 * sin + t2 * cos], axis=1)

    decays = []
    for h in range(RET_HEADS):
        log_gamma = math.log1p(-(2.0 ** (-5.0 - h)))
        decays.append((jnp.where(causal, jnp.exp(jnp.where(causal, rel, 0.0) * log_gamma), 0.0),
                       jnp.exp((idx + 1.0) * log_gamma),
                       jnp.exp((c_len - 1.0 - idx) * log_gamma),
                       math.exp(c_len * log_gamma)))

    def chunk(c, carry):
        r0 = pl.multiple_of(c * c_len, c_len)
        rows = pl.ds(r0, c_len)
        cos = cos_ref[rows, :]
        sin = sin_ref[rows, :]
        for h in range(RET_HEADS):
            inner, q_decay, k_decay, chunk_decay = decays[h]
            q = rot(q_ref[0, rows, h * dk:(h + 1) * dk].astype(F32), cos, sin)
            k = rot(k_ref[0, rows, h * dk:(h + 1) * dk].astype(F32), cos, sin) * (dk ** -0.5)
            v = v_ref[0, rows, h * dv:(h + 1) * dv]
            qb = q.astype(BF16)
            sc = _dot_nt(qb, k.astype(BF16)) * inner
            st = state_ref[h]
            o = _dot(sc.astype(BF16), v) + _dot(qb, st.astype(BF16)) * q_decay
            kd_t = jnp.transpose(k * k_decay).astype(BF16)
            state_ref[h] = st * chunk_decay + _dot(kd_t, v)
            gn = gn_ref[:, h * dv:(h + 1) * dv]
            on = _rms(o, gn)
            gate = g_ref[0, rows, h * dv:(h + 1) * dv].astype(F32)
            o_ref[0, rows, h * dv:(h + 1) * dv] = (_silu(gate) * on).astype(o_ref.dtype)
        return carry

    lax.fori_loop(0, ts // c_len, chunk, 0)


def _retention_core(proj, cos, sin, gn_gain, *, b, s, ts=TOKEN_TILE):
    n = proj.shape[2]
    hk = n // 6
    hv = 2 * hk
    dk = hk // RET_HEADS
    return pl.pallas_call(
        functools.partial(_ret_body, ts=ts),
        grid=(b, s // ts),
        in_specs=[
            pl.BlockSpec((1, ts, hk), lambda i, j: (i, j, 0)),
            pl.BlockSpec((1, ts, hk), lambda i, j: (i, j, 1)),
            pl.BlockSpec((1, ts, hv), lambda i, j: (i, j, 1)),
            pl.BlockSpec((1, ts, hv), lambda i, j: (i, j, 2)),
            pl.BlockSpec((ts, dk // 2), lambda i, j: (j, 0)),
            pl.BlockSpec((ts, dk // 2), lambda i, j: (j, 0)),
            pl.BlockSpec((1, hv), lambda i, j: (0, 0)),
        ],
        out_specs=pl.BlockSpec((1, ts, hv), lambda i, j: (i, j, 0)),
        out_shape=jax.ShapeDtypeStruct((b, s, hv), BF16),
        scratch_shapes=[pltpu.VMEM((RET_HEADS, dk, hv // RET_HEADS), F32)],
        compiler_params=_params("parallel", "arbitrary"),
        name="retention",
    )(proj, proj, proj, proj, cos, sin, gn_gain.reshape(1, hv))


def _rope_tables(s, half):
    inv = ROPE_BASE ** (-jnp.arange(half, dtype=F32) / half)
    ang = jnp.arange(s, dtype=F32)[:, None] * inv[None, :]
    return jnp.cos(ang), jnp.sin(ang)


def _retention(h2, g_norm, w_in, gn_gain, w_out, *, b, s):
    (proj,) = _norm_proj(h2, g_norm, [w_in.astype(BF16)], [BF16])
    n = proj.shape[1]
    cos, sin = _rope_tables(s, n // 6 // RET_HEADS // 2)
    y = _retention_core(proj.reshape(b, s, n), cos, sin, gn_gain, b=b, s=s)
    return y.reshape(b * s, -1), w_out.astype(BF16)


def _dil_proj_body(h_ref, g_ref, w_ref, o_ref, xn_ref, *, r):
    tm = h_ref.shape[0]
    n = tm // r
    xn = _rms(h_ref[...], g_ref[...])
    slabs = xn_ref.shape[0]
    for k in range(slabs):
        xn_ref[k] = xn[:, k * LANES:(k + 1) * LANES]
    xp = jnp.concatenate(
        [jnp.concatenate([xn_ref[k, pl.ds(c, n, stride=r), :] for c in range(r)], axis=0) for k in range(slabs)],
        axis=1).astype(BF16)
    for sl in _col_slices(w_ref.shape[1]):
        res = _dot(xp, w_ref[:, sl]).astype(o_ref.dtype)
        for c in range(r):
            o_ref[0, c, :, sl] = res[c * n:(c + 1) * n, :]


def _dil_proj(h2, g_norm, w, r, *, b, s, tm=TOKEN_TILE):
    d = h2.shape[1]
    n_out = w.shape[1]
    tiles = s // tm
    return pl.pallas_call(
        functools.partial(_dil_proj_body, r=r),
        grid=(b * tiles,),
        in_specs=[
            pl.BlockSpec((tm, d), lambda i: (i, 0)),
            pl.BlockSpec((1, d), lambda i: (0, 0)),
            _resident((d, n_out)),
        ],
        out_specs=pl.BlockSpec((1, r, tm // r, n_out), lambda i: (i // tiles, 0, i % tiles, 0)),
        out_shape=jax.ShapeDtypeStruct((b, r, s // r, n_out), BF16),
        scratch_shapes=[pltpu.VMEM((d // LANES, tm, LANES), F32)],
        compiler_params=_params("parallel"),
        name=f"dilated_proj_r{r}",
    )(h2, g_norm.reshape(1, d), w)


def _dil_body(q_ref, kc_ref, kp_ref, vc_ref, vp_ref, o_ref, lse_ref, kbuf, vbuf, *, rows, n_back):
    qb = DIL_QB
    dh = DIL_DH
    first_step = pl.program_id(2) == 0
    kbuf[0:qb, :] = kp_ref[0, 0]
    kbuf[qb:, :] = kc_ref[0, 0]
    vbuf[0:qb, :] = vp_ref[0, 0]
    vbuf[qb:, :] = vc_ref[0, 0]
    qi = lax.broadcasted_iota(jnp.int32, (qb, 2 * qb), 0)
    kj = lax.broadcasted_iota(jnp.int32, (qb, 2 * qb), 1)
    dist = qi + qb - kj
    band = (dist >= 0) & (dist <= n_back)
    lane = lax.broadcasted_iota(jnp.int32, (qb, LANES), 1)
    scale = dh ** -0.5
    ones = jnp.ones((2 * qb, dh), BF16)
    for i in range(rows // qb):
        mask = band & ((kj >= qb) | jnp.logical_not(first_step)) if i == 0 else band
        m_tile = jnp.zeros((qb, LANES), F32)
        den_tile = jnp.ones((qb, LANES), F32)
        for h in range(DIL_HEADS):
            cols = slice(h * dh, (h + 1) * dh)
            q = q_ref[0, 0, i * qb:(i + 1) * qb, cols]
            k = kbuf[i * qb:(i + 2) * qb, cols]
            v = vbuf[i * qb:(i + 2) * qb, cols]
            s = jnp.where(mask, _dot_nt(q, k), NEG_INF)
            m = jnp.max(s, axis=1, keepdims=True)
            e = jnp.exp2((s - m) * (scale * math.log2(math.e))).astype(BF16)
            acc = _dot(e, jnp.concatenate([v, ones], axis=1))
            den = jnp.maximum(acc[:, dh:], 1e-30)
            o_ref[0, 0, i * qb:(i + 1) * qb, cols] = (acc[:, :dh] / den).astype(o_ref.dtype)
            m_tile = jnp.where(lane == h, m, m_tile)
            den_tile = jnp.where(lane == h, den, den_tile)
        lse_ref[0, 0, i * qb:(i + 1) * qb, :] = m_tile * scale + jnp.log(den_tile)


def _dilated_group(proj, win, r, *, b, s):
    hd = DIL_HEADS * DIL_DH
    length = s // r
    rows = min(length, TOKEN_TILE)
    qb = DIL_QB

    def cur(which):
        return pl.BlockSpec((1, 1, rows, hd), lambda i, c, n: (i, c, n, which))

    def prev(which):
        return pl.BlockSpec((1, 1, qb, hd), lambda i, c, n: (i, c, jnp.maximum(n * (rows // qb) - 1, 0), which))

    return pl.pallas_call(
        functools.partial(_dil_body, rows=rows, n_back=win // r),
        grid=(b, r, length // rows),
        in_specs=[cur(0), cur(1), prev(1), cur(2), prev(2)],
        out_specs=[
            pl.BlockSpec((1, 1, rows, hd), lambda i, c, n: (i, c, n, 0)),
            pl.BlockSpec((1, 1, rows, LANES), lambda i, c, n: (i, c, n, 0)),
        ],
        out_shape=[
            jax.ShapeDtypeStruct((b, r, length, hd), BF16),
            jax.ShapeDtypeStruct((b, r, length, LANES), F32),
        ],
        scratch_shapes=[pltpu.VMEM((rows + qb, hd), BF16), pltpu.VMEM((rows + qb, hd), BF16)],
        compiler_params=_params("parallel", "parallel", "arbitrary"),
        name=f"dilated_r{r}",
    )(proj, proj, proj, proj, proj)


def _dil_merge_body(h_ref, *refs, dilations):
    ng = len(dilations)
    o_refs, l_refs = refs[:ng], refs[ng:2 * ng]
    w_ref, out_ref = refs[2 * ng], refs[2 * ng + 1]
    o_bufs, l_bufs = refs[2 * ng + 2:3 * ng + 2], refs[3 * ng + 2:]
    tm = h_ref.shape[1]
    for o_ref, l_ref, o_buf, l_buf, r in zip(o_refs, l_refs, o_bufs, l_bufs, dilations):
        n = tm // r
        for c in range(r):
            rows = pl.ds(c, n, stride=r)
            l_buf[rows, :] = l_ref[0, c]
            blk = o_ref[0, c].astype(F32)
            for h in range(DIL_HEADS):
                o_buf[h, rows, :] = blk[:, h * DIL_DH:(h + 1) * DIL_DH]
    lses = [l[...] for l in l_bufs]
    mx = functools.reduce(jnp.maximum, lses)
    es = [jnp.exp(l - mx) for l in lses]
    tot = functools.reduce(jnp.add, es)
    wts = [e / tot for e in es]
    dh = DIL_DH
    parts = []
    for h in range(DIL_HEADS):
        acc = jnp.zeros((tm, dh), F32)
        for g in range(ng):
            wg = jnp.broadcast_to(wts[g][:, h:h + 1], (tm, dh))
            acc = acc + wg * o_bufs[g][h]
        parts.append(acc)
    o = jnp.concatenate(parts, axis=1).astype(BF16)
    out_ref[0] = h_ref[0] + _dot(o, w_ref[...])


def _dil_merge(h3, outs, lses, w, dilations, *, tm=TOKEN_TILE):
    b, s, d = h3.shape
    hd = w.shape[0]
    res_major = lambda r, width: pl.BlockSpec((1, r, tm // r, width), lambda i, n: (i, 0, n, 0))
    return pl.pallas_call(
        functools.partial(_dil_merge_body, dilations=tuple(dilations)),
        grid=(b, s // tm),
        in_specs=[pl.BlockSpec((1, tm, d), lambda i, n: (i, n, 0))]
        + [res_major(r, hd) for r in dilations] + [res_major(r, LANES) for r in dilations]
        + [pl.BlockSpec((hd, d), lambda i, n: (0, 0))],
        out_specs=pl.BlockSpec((1, tm, d), lambda i, n: (i, n, 0)),
        out_shape=jax.ShapeDtypeStruct((b, s, d), F32),
        scratch_shapes=[pltpu.VMEM((DIL_HEADS, tm, DIL_DH), F32) for _ in dilations]
        + [pltpu.VMEM((tm, LANES), F32) for _ in dilations],
        compiler_params=_params("parallel", "parallel"),
        name="dilated_merge",
    )(h3, *outs, *lses, w)


def _dilated(h2, g_norm, w_in, w_out, *, b, s):
    d = h2.shape[1]
    gw = 3 * DIL_HEADS * DIL_DH
    wb = w_in.astype(BF16)
    outs, lses, dilations = [], [], []
    for g, (win, r) in enumerate(DIL_PATTERN):
        proj = _dil_proj(h2, g_norm, wb[:, g * gw:(g + 1) * gw], r, b=b, s=s, tm=2 * TOKEN_TILE)
        o, lse = _dilated_group(proj, win, r, b=b, s=s)
        outs.append(o)
        lses.append(lse)
        dilations.append(r)
    return _dil_merge(h2.reshape(b, s, d), outs, lses, w_out.astype(BF16), dilations).reshape(b * s, d)


def _softplus(x):
    return jnp.maximum(x, 0.0) + jnp.log1p(jnp.exp(-jnp.abs(x)))


def _ssd_body(z_ref, x_ref, dt_ref, cw_ref, cb_ref, dtb_ref, alog_ref, dsk_ref, ng_ref, o_ref,
              xpad, state_ref):
    ln = SSD_CHUNK
    nst = SSD_STATE
    gw = SSD_HPG * SSD_HEADDIM
    d_inner = SSD_GROUPS * gw
    halo = 8

    @pl.when(pl.program_id(1) == 0)
    def _():
        xpad[0:halo, :] = jnp.zeros((halo, xpad.shape[1]), F32)
        state_ref[...] = jnp.zeros(state_ref.shape, F32)

    x_b = x_ref[0]
    x_f = x_b.astype(F32)
    ii = lax.broadcasted_iota(jnp.int32, (ln, ln), 0)
    jj = lax.broadcasted_iota(jnp.int32, (ln, ln), 1)
    conv = cb_ref[...] + x_f * cw_ref[SSD_CONV - 1:SSD_CONV, :]
    for shift in range(1, SSD_CONV):
        shifted = _dot(jnp.where(ii - jj == shift, 1.0, 0.0).astype(BF16), x_b)
        conv = conv + shifted * cw_ref[SSD_CONV - 1 - shift:SSD_CONV - shift, :]
    xpad[halo:2 * halo, :] = x_f[0:halo, :]
    head = cb_ref[...]
    for k in range(SSD_CONV):
        off = halo - (SSD_CONV - 1) + k
        head = head + xpad[off:off + halo, :] * cw_ref[k:k + 1, :]
    xpad[0:halo, :] = x_f[ln - halo:ln, :]
    xbc = _silu(jnp.concatenate([head, conv[halo:, :]], axis=0))
    xs = xbc[:, :d_inner]
    bm = xbc[:, d_inner:d_inner + SSD_GROUPS * nst]
    cm = xbc[:, d_inner + SSD_GROUPS * nst:]

    dt = _softplus(dt_ref[0] + dtb_ref[...])
    da = dt * (-jnp.exp(alog_ref[...]))
    ii = lax.broadcasted_iota(jnp.int32, (ln, ln), 0)
    jj = lax.broadcasted_iota(jnp.int32, (ln, ln), 1)
    causal = ii >= jj
    tril = jnp.where(causal, 1.0, 0.0).astype(BF16)
    da_terms = _split3(da)
    acs = _dot(tril, da_terms[0]) + _dot(tril, da_terms[1]) + _dot(tril, da_terms[2])
    acs2 = acs * math.log2(math.e)
    acs2_t = jnp.transpose(acs2)
    hl = ln // 2

    erow = lax.broadcasted_iota(jnp.int32, (LANES, d_inner), 0)
    ecol = lax.broadcasted_iota(jnp.int32, (LANES, d_inner), 1)
    expand = jnp.where(ecol // SSD_HEADDIM == erow, 1.0, 0.0).astype(BF16)
    acs_e = _dot3(_split3(acs), expand)
    dt_e = _dot3(_split3(dt), expand)
    last = acs_e[ln - 1:ln, :]
    decay_in = jnp.exp(acs_e)
    xs_dt = xs * dt_e
    xs_end = (xs_dt * jnp.exp(last - acs_e)).astype(BF16)
    xs_b = xs_dt.astype(BF16)
    lane = lax.broadcasted_iota(jnp.int32, (ln, LANES), 1)

    y_groups = []
    for g in range(SSD_GROUPS):
        bm_g = bm[:, g * nst:(g + 1) * nst]
        cm_g = cm[:, g * nst:(g + 1) * nst].astype(BF16)
        cb = jnp.where(causal, _dot_nt(cm_g, bm_g.astype(BF16)), 0.0)
        st = state_ref[g]
        gcols = slice(g * gw, (g + 1) * gw)
        y_state = _dot(cm_g, st.astype(BF16)) * decay_in[:, gcols]
        pairs = []
        for p in range(SSD_HPG // 2):
            pair_cols = slice(g * gw + p * LANES, g * gw + (p + 1) * LANES)
            halves = []
            for e in range(2):
                hd = g * SSD_HPG + 2 * p + e
                top = cb[:hl, :hl] * jnp.exp2(jnp.minimum(acs2[:hl, hd:hd + 1] - acs2_t[hd:hd + 1, :hl], 0.0))
                bot = cb[hl:, :] * jnp.exp2(jnp.minimum(acs2[hl:, hd:hd + 1] - acs2_t[hd:hd + 1, :], 0.0))
                halves.append(jnp.concatenate([_dot(top.astype(BF16), xs_b[:hl, pair_cols]),
                                               _dot(bot.astype(BF16), xs_b[:, pair_cols])], axis=0))
            pairs.append(jnp.where(lane < SSD_HEADDIM, halves[0], halves[1]))
        y_groups.append(jnp.concatenate(pairs, axis=1) + y_state)
        bm_t = jnp.transpose(bm_g).astype(BF16)
        state_ref[g] = st * jnp.exp(last[:, gcols]) + _dot(bm_t, xs_end[:, gcols])

    y = jnp.concatenate(y_groups, axis=1) + dsk_ref[...] * xs
    yz = y * _silu(z_ref[0].astype(F32))
    outs = [_rms(yz[:, g * gw:(g + 1) * gw], ng_ref[:, g * gw:(g + 1) * gw]) for g in range(SSD_GROUPS)]
    o_ref[0] = jnp.concatenate(outs, axis=1).astype(o_ref.dtype)


def _ssd_core(z, xbc, dt, conv_w, conv_b, dt_bias, a_log, d_skip, norm_g, *, b, s):
    ln = SSD_CHUNK
    d_inner = z.shape[2]
    conv_dim = xbc.shape[2]
    gw = SSD_HPG * SSD_HEADDIM
    full = lambda shape: pl.BlockSpec(shape, lambda i, j: (0,) * len(shape))
    return pl.pallas_call(
        _ssd_body,
        grid=(b, s // ln),
        in_specs=[
            pl.BlockSpec((1, ln, d_inner), lambda i, j: (i, j, 0)),
            pl.BlockSpec((1, ln, conv_dim), lambda i, j: (i, j, 0)),
            pl.BlockSpec((1, ln, LANES), lambda i, j: (i, j, 0)),
            full((SSD_CONV, conv_dim)),
            full((1, conv_dim)),
            full((1, LANES)),
            full((1, LANES)),
            full((1, d_inner)),
            full((1, d_inner)),
        ],
        out_specs=pl.BlockSpec((1, ln, d_inner), lambda i, j: (i, j, 0)),
        out_shape=jax.ShapeDtypeStruct((b, s, d_inner), BF16),
        scratch_shapes=[pltpu.VMEM((16, conv_dim), F32),
                        pltpu.VMEM((SSD_GROUPS, SSD_STATE, gw), F32)],
        compiler_params=_params("parallel", "arbitrary"),
        name="ssd",
    )(z, xbc, dt, conv_w, conv_b, dt_bias, a_log, d_skip, norm_g)


def _ssd(h2, g_norm, w_in, conv_w, conv_b, dt_bias, a_log, d_skip, norm_g, w_out, *, b, s):
    heads = SSD_GROUPS * SSD_HPG
    d_inner = heads * SSD_HEADDIM
    conv_dim = conv_w.shape[1]
    wb = w_in.astype(BF16)
    w_dt = jnp.pad(wb[:, d_inner + conv_dim:], ((0, 0), (0, LANES - heads)))
    z, xbc, dt = _norm_proj(h2, g_norm, [wb[:, :d_inner], wb[:, d_inner:d_inner + conv_dim], w_dt], [BF16, BF16, F32])
    pad_heads = lambda v: jnp.pad(v.astype(F32), (0, LANES - heads)).reshape(1, LANES)
    y = _ssd_core(z.reshape(b, s, -1), xbc.reshape(b, s, -1), dt.reshape(b, s, LANES),
                  conv_w.astype(F32), conv_b.reshape(1, -1).astype(F32), pad_heads(dt_bias), pad_heads(a_log),
                  jnp.repeat(d_skip.astype(F32), SSD_HEADDIM).reshape(1, d_inner), norm_g.reshape(1, d_inner),
                  b=b, s=s)
    return y.reshape(b * s, d_inner), w_out.astype(BF16)


def _nsa_proj_body(h_ref, g_ref, wn_ref, wt_ref, k0_ref, v0_ref, k12_ref, tr_ref, stage_ref):
    xn = _rms(h_ref[...], g_ref[...]).astype(BF16)
    nat = _dot(xn, wn_ref[...])
    tm = nat.shape[0]
    gd = k12_ref.shape[1] // 2
    k12_ref[...] = nat[:, 2 * gd:].astype(k12_ref.dtype)
    slabs = gd // LANES
    rows = tm // NSA_CMP_STRIDE
    for sl in range(2 * slabs):
        stage_ref[sl] = nat[:, sl * LANES:(sl + 1) * LANES]
    for l in range(NSA_CMP_STRIDE):
        for sl in range(slabs):
            cols = slice(l * gd + sl * LANES, l * gd + (sl + 1) * LANES)
            k0_ref[:, cols] = stage_ref[sl, pl.ds(l, rows, stride=NSA_CMP_STRIDE), :].astype(k0_ref.dtype)
            v0_ref[:, cols] = stage_ref[slabs + sl, pl.ds(l, rows, stride=NSA_CMP_STRIDE), :].astype(v0_ref.dtype)
    res = _dot_nt(wt_ref[...], xn)
    for j in range(tr_ref.shape[0]):
        tr_ref[j] = res[:, j * NSA_QT:(j + 1) * NSA_QT].astype(tr_ref.dtype)


def _nsa_proj(h2, g_norm, w_nat, w_t, *, tm=TOKEN_TILE):
    m, d = h2.shape
    gd = NSA_GROUPS * NSA_DH
    nt = w_t.shape[0]
    slabs = tm // NSA_QT
    return pl.pallas_call(
        _nsa_proj_body,
        grid=(m // tm,),
        in_specs=[
            pl.BlockSpec((tm, d), lambda i: (i, 0)),
            pl.BlockSpec((1, d), lambda i: (0, 0)),
            _resident(w_nat.shape),
            _resident(w_t.shape),
        ],
        out_specs=[
            pl.BlockSpec((tm // NSA_CMP_STRIDE, NSA_CMP_STRIDE * gd), lambda i: (i, 0)),
            pl.BlockSpec((tm // NSA_CMP_STRIDE, NSA_CMP_STRIDE * gd), lambda i: (i, 0)),
            pl.BlockSpec((tm, 2 * gd), lambda i: (i, 0)),
            pl.BlockSpec((slabs, nt, NSA_QT), lambda i: (i, 0, 0)),
        ],
        out_shape=[
            jax.ShapeDtypeStruct((m // NSA_CMP_STRIDE, NSA_CMP_STRIDE * gd), BF16),
            jax.ShapeDtypeStruct((m // NSA_CMP_STRIDE, NSA_CMP_STRIDE * gd), BF16),
            jax.ShapeDtypeStruct((m, 2 * gd), BF16),
            jax.ShapeDtypeStruct((m // NSA_QT, nt, NSA_QT), BF16),
        ],
        scratch_shapes=[pltpu.VMEM((2 * gd // LANES, tm, LANES), F32)],
        compiler_params=_params("parallel"),
        name="nsa_proj",
    )(h2, g_norm.reshape(1, d), w_nat, w_t)


def _nsa_cmp_body(x_ref, pa_ref, pb_ref, w1_ref, w2_ref, o_ref, w1a_big, w1b_big, w2_big, *, transposed):
    g, dh = NSA_GROUPS, NSA_DH
    hid = w1_ref.shape[1]
    half = NSA_CMP_LEN // 2

    @pl.when(pl.program_id(0) == 0)
    def _():
        w1a_big[...] = jnp.zeros(w1a_big.shape, BF16)
        w1b_big[...] = jnp.zeros(w1b_big.shape, BF16)
        w2_big[...] = jnp.zeros(w2_big.shape, BF16)
        for gi in range(g):
            cols = slice(gi * hid, (gi + 1) * hid)
            for l in range(half):
                rows = slice((l * g + gi) * dh, (l * g + gi + 1) * dh)
                w1a_big[rows, cols] = w1_ref[l * dh:(l + 1) * dh, :]
                w1b_big[rows, cols] = w1_ref[(half + l) * dh:(half + l + 1) * dh, :]
            w2_big[cols, gi * dh:(gi + 1) * dh] = w2_ref[...]

    x = x_ref[0].astype(F32)
    nb = x.shape[0]
    ya = _dot((x + pa_ref[...]).astype(BF16), w1a_big[...])
    yb = _dot((x + pb_ref[...]).astype(BF16), w1b_big[...])
    hidden = _silu(ya + pltpu.roll(yb, nb - 1, 0))
    out = _dot(hidden.astype(BF16), w2_big[...])
    o_ref[0] = (jnp.transpose(out) if transposed else out).astype(o_ref.dtype)


def _nsa_compress(x, pos, w1, w2, *, transposed):
    b, nb, width = x.shape
    g, dh = NSA_GROUPS, NSA_DH
    half = NSA_CMP_LEN // 2
    hid = w1.shape[1]
    posb = jnp.broadcast_to(pos[:, None, :], (NSA_CMP_LEN, g, dh)).reshape(NSA_CMP_LEN, g * dh)
    pa = posb[:half].reshape(1, width).astype(F32)
    pb = posb[half:].reshape(1, width).astype(F32)
    out_dims = (g * dh, nb) if transposed else (nb, g * dh)
    return pl.pallas_call(
        functools.partial(_nsa_cmp_body, transposed=transposed),
        grid=(b,),
        in_specs=[
            pl.BlockSpec((1, nb, width), lambda i: (i, 0, 0)),
            _resident((1, width)), _resident((1, width)),
            _resident(w1.shape), _resident(w2.shape),
        ],
        out_specs=pl.BlockSpec((1,) + out_dims, lambda i: (i, 0, 0)),
        out_shape=jax.ShapeDtypeStruct((b,) + out_dims, BF16),
        scratch_shapes=[pltpu.VMEM((width, g * hid), BF16), pltpu.VMEM((width, g * hid), BF16),
                        pltpu.VMEM((g * hid, g * dh), BF16)],
        compiler_params=_params("arbitrary"),
        name="nsa_compress",
    )(x, pa, pb, w1.astype(BF16), w2.astype(BF16))


def _nsa_pair_body(q_ref, gt_ref, kc_ref, vct_ref, k1_ref, k2_ref, v1_ref, v2_ref, o_ref, bias_ref, wbias_ref, *, s):
    qt, kc, dh = NSA_QT, NSA_KC, NSA_DH
    gl = 2
    glanes = NSA_HPG * qt
    lanes = gl * glanes
    vrows = gl * dh
    j = pl.program_id(2)
    t0 = j * qt
    nb = kc_ref.shape[1]
    nsb = s // NSA_SEL_LEN
    n_sel = min(NSA_N_SEL, nsb)

    q_t = q_ref[0, 0]
    qcat = [jnp.concatenate([q_t[(g * NSA_HPG + h) * dh:(g * NSA_HPG + h + 1) * dh, :] for h in range(NSA_HPG)],
                            axis=1) for g in range(gl)]
    zero = jnp.zeros_like(qcat[0])
    qp = jnp.concatenate([jnp.concatenate([qcat[0], zero], axis=1),
                          jnp.concatenate([zero, qcat[1]], axis=1)], axis=0)
    qlane = lax.broadcasted_iota(jnp.int32, (1, lanes), 1) % qt
    tq = t0 + qlane

    def values(v_ref, slab0, n_slabs):
        v = jnp.concatenate([v_ref[0, slab0 + i] for i in range(n_slabs)], axis=1)
        return jnp.concatenate([v, jnp.ones((16, n_slabs * qt), BF16)], axis=0)

    def weighted_values(v_aug, p):
        return jnp.concatenate(
            [_dot(jnp.concatenate([v_aug[g * dh:(g + 1) * dh, :], v_aug[vrows:, :]], axis=0),
                  p[:, g * glanes:(g + 1) * glanes]) for g in range(gl)], axis=1)

    def normalise(acc):
        o = acc[:dh, :] * (1.0 / jnp.maximum(acc[dh:dh + 1, :], 1e-30))
        return [o[:, g * glanes:(g + 1) * glanes] for g in range(gl)]

    wslabs = NSA_WINDOW // qt + 1
    row_w = lax.broadcasted_iota(jnp.int32, (qt, lanes), 0)

    @pl.when(j == 0)
    def _():
        wbias_ref[0] = jnp.where(row_w > qlane, 0.0, NEG_INF)
        wbias_ref[1] = jnp.zeros((qt, lanes), F32)
        wbias_ref[2] = jnp.where(row_w <= qlane, 0.0, NEG_INF)
        wbias_ref[3] = jnp.full((qt, lanes), NEG_INF, F32)

    jb0 = jnp.maximum(j + 1 - wslabs, 0)
    kw = k2_ref[0, pl.ds(pl.multiple_of(jb0 * qt, qt), wslabs * qt), :]
    sw_raw = _dot(kw, qp)
    slabs_back = [j - jb0 - i for i in range(wslabs)]
    sw = jnp.concatenate(
        [sw_raw[i * qt:(i + 1) * qt, :]
         + wbias_ref[jnp.where(back < 0, 3, jnp.where(back == wslabs - 1, 0, jnp.where(back == 0, 2, 1)))]
         for i, back in enumerate(slabs_back)], axis=0)

    sc = _dot(kc_ref[0], qp)
    assert NSA_CMP_STRIDE == 16
    last_visible = jnp.minimum(jnp.right_shift(tq - (NSA_CMP_LEN - 1), 4), nb - 2)
    sc = jnp.where(lax.broadcasted_iota(jnp.int32, (nb, lanes), 0) <= last_visible, sc, NEG_INF)
    m_c = jnp.max(sc, axis=0, keepdims=True)
    e = jnp.exp2(sc - jnp.where(m_c > 0.5 * NEG_INF, m_c, 0.0))
    p_c = e * (1.0 / jnp.maximum(jnp.sum(e, axis=0, keepdims=True), 1e-30))
    acc_c = _dot(vct_ref[0], p_c.astype(BF16))
    o_c = [acc_c[g * dh:(g + 1) * dh, g * glanes:(g + 1) * glanes] for g in range(gl)]
    imps = []
    for g in range(gl):
        imp = p_c[:, g * glanes:g * glanes + qt]
        for h in range(1, NSA_HPG):
            imp = imp + p_c[:, g * glanes + h * qt:g * glanes + (h + 1) * qt]
        imps.append(imp)
    imp = jnp.concatenate(imps, axis=1)

    ratio = NSA_SEL_LEN // NSA_CMP_STRIDE
    dd = lax.broadcasted_iota(jnp.int32, (nsb, nb), 1) - ratio * lax.broadcasted_iota(jnp.int32, (nsb, nb), 0)
    wsel = jnp.where((dd == -1) | (dd == ratio - 1), 1.0,
                     jnp.where((dd >= 0) & (dd < ratio - 1), 2.0, 0.0)).astype(BF16)
    terms = _split3(imp)
    imp_sel = _dot(wsel, terms[0]) + _dot(wsel, terms[1]) + _dot(wsel, terms[2])
    jq = lax.broadcasted_iota(jnp.int32, (nsb, gl * qt), 0)
    cur = (t0 + lax.broadcasted_iota(jnp.int32, (nsb, gl * qt), 1) % qt) // NSA_SEL_LEN
    forced = (jq == 0) | (jq == cur) | (jq == cur - 1)
    score = jnp.where(jq <= cur, imp_sel + jnp.where(forced, NSA_FORCE_BONUS, 0.0), NEG_INF)

    pw = jnp.exp2(sw - jnp.max(sw, axis=0, keepdims=True)).astype(BF16)
    o_w = normalise(weighted_values(values(v2_ref, jb0, wslabs), pw))

    jqf = jq.astype(F32)
    work = score
    picked = None
    for _ in range(n_sel):
        top = jnp.max(work, axis=0, keepdims=True)
        first = jnp.min(jnp.where(work == top, jqf, float(nsb)), axis=0, keepdims=True)
        pick = jqf == first
        picked = pick if picked is None else picked | pick
        work = jnp.where(pick, 3.0 * NEG_INF, work)
    sel_bias = jnp.where(picked & (score > 0.5 * NEG_INF), 0.0, NEG_INF)
    bias_ref[...] = jnp.concatenate([sel_bias[:, g * qt:(g + 1) * qt] for g in range(gl) for _ in range(NSA_HPG)],
                                    axis=1)

    blocks_per_chunk = kc // NSA_SEL_LEN
    slabs_per_chunk = kc // qt

    def online(carry, st, bias, v_aug):
        m, acc = carry
        blks = [st[r * NSA_SEL_LEN:(r + 1) * NSA_SEL_LEN, :] for r in range(blocks_per_chunk)]
        part = None
        for r in range(blocks_per_chunk):
            pr = blks[r][0:8, :]
            for i in range(1, NSA_SEL_LEN // 8):
                pr = jnp.maximum(pr, blks[r][8 * i:8 * (i + 1), :])
            pr = pr + bias[r:r + 1, :]
            part = pr if part is None else jnp.maximum(part, pr)
        m_new = jnp.maximum(m, jnp.max(part, axis=0, keepdims=True))
        p = jnp.concatenate([jnp.exp2(blks[r] - (m_new - bias[r:r + 1, :])) for r in range(blocks_per_chunk)],
                            axis=0).astype(BF16)
        return m_new, jnp.exp2(m - m_new) * acc + weighted_values(v_aug, p)

    def sel_scores(c):
        return _dot(k1_ref[0, pl.ds(pl.multiple_of(c * kc, kc), kc), :], qp)

    def sel_bias(c):
        return bias_ref[pl.ds(pl.multiple_of(c * blocks_per_chunk, blocks_per_chunk), blocks_per_chunk), :]

    def sel_values(c):
        return values(v1_ref, c * slabs_per_chunk, slabs_per_chunk)

    init = (jnp.full((1, lanes), NEG_INF, F32), jnp.zeros((dh + 16, lanes), F32))
    c_hi = t0 // kc
    own_slab = (t0 - c_hi * kc) // qt
    st_raw = sel_scores(c_hi)
    st = jnp.concatenate(
        [st_raw[i * qt:(i + 1) * qt, :] + wbias_ref[jnp.where(i < own_slab, 1, jnp.where(i == own_slab, 2, 3))]
         for i in range(slabs_per_chunk)], axis=0)
    chain_a = online(init, st, sel_bias(c_hi), sel_values(c_hi))
    chain_a = lax.cond(c_hi % 2 == 1,
                       lambda cr: online(cr, sel_scores(c_hi - 1), sel_bias(c_hi - 1), sel_values(c_hi - 1)),
                       lambda cr: cr, chain_a)
    chain_b = (chain_a[0], jnp.zeros_like(chain_a[1]))

    def pair(i, chains):
        ca, cb = chains
        st_a = sel_scores(2 * i)
        st_b = sel_scores(2 * i + 1)
        return (online(ca, st_a, sel_bias(2 * i), sel_values(2 * i)),
                online(cb, st_b, sel_bias(2 * i + 1), sel_values(2 * i + 1)))

    (m_a, acc_a), (m_b, acc_b) = lax.fori_loop(0, c_hi // 2, pair, (chain_a, chain_b))
    m_s = jnp.maximum(m_a, m_b)
    o_s = normalise(jnp.exp2(m_a - m_s) * acc_a + jnp.exp2(m_b - m_s) * acc_b)

    gates = jax.nn.sigmoid(gt_ref[0, 0].astype(F32))
    outs = []
    for g in range(gl):
        def gate_row(br):
            r0 = (g * 4 + br) * NSA_HPG
            return jnp.concatenate([gates[r0 + h:r0 + h + 1, :] for h in range(NSA_HPG)], axis=1)
        o = gate_row(0) * o_c[g] + gate_row(1) * o_s[g] + gate_row(2) * o_w[g]
        outs.extend(o[:, h * qt:(h + 1) * qt] for h in range(NSA_HPG))
    o_ref[0] = jnp.transpose(jnp.concatenate(outs, axis=0)).astype(o_ref.dtype)


def _nsa_attend_pairs(tr, k_cmp, v_cmp_t, k12, *, b, s):
    qt, dh = NSA_QT, NSA_DH
    gl = 2
    gd = NSA_GROUPS * dh
    hq = NSA_GROUPS * NSA_HPG * dh
    pairs = NSA_GROUPS // gl
    nslab = s // qt
    assert nslab > NSA_WINDOW // qt
    nb = k_cmp.shape[1]
    tr4 = tr.reshape(b, nslab, tr.shape[1], qt)
    q_rows = gl * NSA_HPG * dh
    v_rows = gl * dh
    gate_rows = gl * 4 * NSA_HPG
    v1_blk = hq // v_rows
    v2_blk = (hq + gd) // v_rows
    gate_blk = (hq + 2 * gd) // gate_rows
    nsb = s // NSA_SEL_LEN
    return pl.pallas_call(
        functools.partial(_nsa_pair_body, s=s),
        grid=(b, pairs, nslab),
        in_specs=[
            pl.BlockSpec((1, 1, q_rows, qt), lambda i, p, j: (i, j, p, 0)),
            pl.BlockSpec((1, 1, gate_rows, qt), lambda i, p, j: (i, j, gate_blk + p, 0)),
            pl.BlockSpec((1, nb, v_rows), lambda i, p, j: (i, 0, p)),
            pl.BlockSpec((1, v_rows, nb), lambda i, p, j: (i, p, 0)),
            pl.BlockSpec((1, s, v_rows), lambda i, p, j: (i, 0, p)),
            pl.BlockSpec((1, s, v_rows), lambda i, p, j: (i, 0, pairs + p)),
            pl.BlockSpec((1, nslab, v_rows, qt), lambda i, p, j: (i, 0, v1_blk + p, 0)),
            pl.BlockSpec((1, nslab, v_rows, qt), lambda i, p, j: (i, 0, v2_blk + p, 0)),
        ],
        out_specs=pl.BlockSpec((1, qt, q_rows), lambda i, p, j: (i, j, p)),
        out_shape=jax.ShapeDtypeStruct((b, s, hq), BF16),
        scratch_shapes=[pltpu.VMEM((nsb, gl * NSA_HPG * qt), F32), pltpu.VMEM((4, qt, gl * NSA_HPG * qt), F32)],
        compiler_params=_params("parallel", "parallel", "arbitrary"),
        name="nsa_attend",
    )(tr4, tr4, k_cmp, v_cmp_t, k12, k12, tr4, tr4)


def _nsa(h2, g_norm, w_in, cmp_pos, cmp_w1, cmp_w2, w_out, *, b, s):
    g, hpg, dh = NSA_GROUPS, NSA_HPG, NSA_DH
    hq, gd = g * hpg * dh, g * dh
    kv_w = lambda br, kv: w_in[:, hq + (2 * br + kv) * gd:hq + (2 * br + kv + 1) * gd]
    w_q = w_in[:, :hq] * (dh ** -0.5 * math.log2(math.e))
    w_g = w_in[:, hq + 6 * gd:].reshape(-1, 3, g, hpg).transpose(2, 1, 3, 0)
    w_g = jnp.pad(w_g, ((0, 0), (0, 1), (0, 0), (0, 0))).reshape(g * 4 * hpg, -1)
    w_t = jnp.concatenate([w_q.T, kv_w(1, 1).T, kv_w(2, 1).T, w_g], axis=0).astype(BF16)
    w_nat = jnp.concatenate([kv_w(0, 0), kv_w(0, 1), kv_w(1, 0), kv_w(2, 0)], axis=1).astype(BF16)
    k0, v0, k12, tr = _nsa_proj(h2, g_norm, w_nat, w_t)
    nb = s // NSA_CMP_STRIDE
    k_cmp = _nsa_compress(k0.reshape(b, nb, NSA_CMP_STRIDE * gd), cmp_pos[0], cmp_w1[0], cmp_w2[0], transposed=False)
    v_cmp_t = _nsa_compress(v0.reshape(b, nb, NSA_CMP_STRIDE * gd), cmp_pos[1], cmp_w1[1], cmp_w2[1], transposed=True)
    y = _nsa_attend_pairs(tr, k_cmp, v_cmp_t, k12.reshape(b, s, 2 * gd), b=b, s=s)
    return y.reshape(b * s, hq), w_out.astype(BF16)


def kernel(x, norm_ffn1, ffn1_w_in, ffn1_w_out, norm_mix, norm_ffn2, ffn2_w_in, ffn2_w_out, norm_final,
           ret_w_in, ret_gn_gain, ret_w_out,
           nsa_w_in, nsa_cmp_pos, nsa_cmp_w1, nsa_cmp_w2, nsa_w_out,
           ssd_w_in, ssd_conv_w, ssd_conv_b, ssd_dt_bias, ssd_a_log, ssd_d, ssd_norm, ssd_w_out,
           dil_w_in, dil_w_out):
    b, s, d = x.shape
    depth = norm_mix.shape[0]
    h = x.reshape(b * s, d)
    w1_in, w1_out = ffn1_w_in.astype(BF16), ffn1_w_out.astype(BF16)
    w2_in, w2_out = ffn2_w_in.astype(BF16), ffn2_w_out.astype(BF16)
    for i in range(depth):
        h = _ffn(h, norm_ffn1[i], w1_in, w1_out, i)
        m, j = i % 4, i // 4
        mix = None
        if m == 0:
            mix = _retention(h, norm_mix[i], ret_w_in[j], ret_gn_gain[j], ret_w_out[j], b=b, s=s)
        elif m == 1:
            mix = _nsa(h, norm_mix[i], nsa_w_in[j], nsa_cmp_pos[j], nsa_cmp_w1[j], nsa_cmp_w2[j], nsa_w_out[j], b=b, s=s)
        elif m == 2:
            mix = _ssd(h, norm_mix[i], ssd_w_in[j], ssd_conv_w[j], ssd_conv_b[j], ssd_dt_bias[j], ssd_a_log[j],
                       ssd_d[j], ssd_norm[j], ssd_w_out[j], b=b, s=s)
        else:
            h = _dilated(h, norm_mix[i], dil_w_in[j], dil_w_out[j], b=b, s=s)
        h = _ffn(h, norm_ffn2[i], w2_in, w2_out, i,
                 norm_final if i == depth - 1 else None, mix)
    return h.reshape(b, s, d)
```

```python
import functools
import math

import jax
import jax.numpy as jnp
from jax import lax
from jax.experimental import pallas as pl
from jax.experimental.pallas import tpu as pltpu

F32 = jnp.float32
BF16 = jnp.bfloat16
NORM_EPS = 1e-6
NEG_INF = -1e30
ROPE_BASE = 10000.0
VMEM_LIMIT_BYTES = 56 * 1024 * 1024
LANES = 128
MXU_WIDTH = 256
TOKEN_TILE = 512
MAX_DOT_COLS = 2048

RET_HEADS = 4
RET_CHUNK = 128

NSA_GROUPS = 4
NSA_HPG = 4
NSA_DH = 64
NSA_CMP_LEN = 32
NSA_CMP_STRIDE = 16
NSA_SEL_LEN = 64
NSA_N_SEL = 16
NSA_WINDOW = 512
NSA_FORCE_BONUS = 1e4
NSA_QT = 256
NSA_KC = 512

SSD_GROUPS = 4
SSD_HPG = 8
SSD_HEADDIM = 64
SSD_STATE = 128
SSD_CONV = 4
SSD_CHUNK = 256

DIL_PATTERN = ((128, 1), (512, 4), (2048, 16))
DIL_HEADS = 8
DIL_DH = 128
DIL_QB = 128


def _params(*sem):
    return pltpu.CompilerParams(dimension_semantics=sem, vmem_limit_bytes=VMEM_LIMIT_BYTES)


def _rms(x, g):
    return x * lax.rsqrt(jnp.mean(x * x, axis=-1, keepdims=True) + NORM_EPS) * g


def _silu(x):
    hx = 0.5 * x
    return hx + hx * jnp.tanh(hx)


def _dot(a, b):
    return jnp.dot(a, b, preferred_element_type=F32)


def _dot_nt(a, b):
    return lax.dot_general(a, b, (((1,), (1,)), ((), ())), preferred_element_type=F32)


def _split3(x):
    hi = x.astype(BF16)
    r1 = x - hi.astype(F32)
    mid = r1.astype(BF16)
    lo = (r1 - mid.astype(F32)).astype(BF16)
    return hi, mid, lo


def _dot3(terms, rhs):
    return _dot(terms[0], rhs) + _dot(terms[1], rhs) + _dot(terms[2], rhs)


def _col_slices(n, max_cols=MAX_DOT_COLS):
    if n <= max_cols:
        return [slice(0, n)]
    tiles = -(-n // MXU_WIDTH)
    parts = -(-tiles * MXU_WIDTH // max_cols)
    bounds = [min(n, (tiles * p // parts) * MXU_WIDTH) for p in range(parts + 1)]
    return [slice(lo, hi) for lo, hi in zip(bounds[:-1], bounds[1:])]


def _resident(shape, index=None):
    index = index if index is not None else (0,) * len(shape)
    return pl.BlockSpec(shape, lambda *_: index, pipeline_mode=pl.Buffered(1))


def _ffn_body(h_ref, *refs, final_norm, mix):
    x = h_ref[...]
    if mix:
        y_ref, wm_ref, *refs = refs
        x = x + _dot(y_ref[...], wm_ref[...])
    g_ref, wa_ref, wb_ref, wo_ref, gf_ref, o_ref = refs
    xn = _rms(x, g_ref[...]).astype(BF16)
    acc = jnp.zeros(x.shape, F32)
    for sl in _col_slices(wa_ref.shape[1], MAX_DOT_COLS * TOKEN_TILE // x.shape[0]):
        a = _dot(xn, wa_ref[:, sl])
        b = _dot(xn, wb_ref[:, sl])
        acc = acc + _dot((_silu(a) * b).astype(BF16), wo_ref[sl, :])
    y = x + 0.5 * acc
    if final_norm:
        y = _rms(y, gf_ref[...])
    o_ref[...] = y


def _ffn(h, g, w_in, w_out, layer, g_final=None, mix=None, *, tm=None):
    m, d = h.shape
    if tm is None:
        tm = TOKEN_TILE if mix is not None else 2 * TOKEN_TILE
    f = w_out.shape[1]
    final_norm = g_final is not None
    gf = g_final if final_norm else g
    mix_args = list(mix) if mix is not None else []
    mix_specs = [pl.BlockSpec((tm, mix[0].shape[1]), lambda i: (i, 0)), _resident(mix[1].shape)] if mix_args else []
    return pl.pallas_call(
        functools.partial(_ffn_body, final_norm=final_norm, mix=bool(mix_args)),
        grid=(m // tm,),
        in_specs=[pl.BlockSpec((tm, d), lambda i: (i, 0))] + mix_specs + [
            _resident((1, d)),
            _resident((None, d, f), (layer, 0, 0)),
            _resident((None, d, f), (layer, 0, 1)),
            _resident((None, f, d), (layer, 0, 0)),
            _resident((1, d)),
        ],
        out_specs=pl.BlockSpec((tm, d), lambda i: (i, 0)),
        out_shape=jax.ShapeDtypeStruct((m, d), F32),
        compiler_params=_params("parallel"),
        name="ffn",
    )(h, *mix_args, g.reshape(1, d), w_in, w_in, w_out, gf.reshape(1, d))


def _norm_proj_body(h_ref, g_ref, *refs, n_out):
    xn = _rms(h_ref[...], g_ref[...]).astype(BF16)
    for w_ref, o_ref in zip(refs[:n_out], refs[n_out:]):
        for sl in _col_slices(w_ref.shape[1]):
            o_ref[:, sl] = _dot(xn, w_ref[:, sl]).astype(o_ref.dtype)


def _norm_proj(h, g, ws, out_dtypes, *, tm=TOKEN_TILE):
    m, d = h.shape
    return pl.pallas_call(
        functools.partial(_norm_proj_body, n_out=len(ws)),
        grid=(m // tm,),
        in_specs=[pl.BlockSpec((tm, d), lambda i: (i, 0)), pl.BlockSpec((1, d), lambda i: (0, 0))]
        + [_resident(w.shape) for w in ws],
        out_specs=[pl.BlockSpec((tm, w.shape[1]), lambda i: (i, 0)) for w in ws],
        out_shape=[jax.ShapeDtypeStruct((m, w.shape[1]), dt) for w, dt in zip(ws, out_dtypes)],
        compiler_params=_params("parallel"),
        name="norm_proj",
    )(h, g.reshape(1, d), *ws)


def _ret_body(q_ref, k_ref, v_ref, g_ref, cos_ref, sin_ref, gn_ref, o_ref, state_ref, *, ts):
    c_len = RET_CHUNK
    dk = q_ref.shape[2] // RET_HEADS
    dv = v_ref.shape[2] // RET_HEADS
    half = dk // 2

    @pl.when(pl.program_id(1) == 0)
    def _():
        state_ref[...] = jnp.zeros(state_ref.shape, F32)

    ii = lax.broadcasted_iota(jnp.int32, (c_len, c_len), 0)
    jj = lax.broadcasted_iota(jnp.int32, (c_len, c_len), 1)
    rel = (ii - jj).astype(F32)
    causal = ii >= jj
    idx = lax.broadcasted_iota(jnp.int32, (c_len, 1), 0).astype(F32)

    def rot(t, cos, sin):
        t1, t2 = t[:, :half], t[:, half:]
        return jnp.concatenate([t1 * cos - t2 * sin, t1 * sin + t2 * cos], axis=1)

    decays = []
    for h in range(RET_HEADS):
        log_gamma = math.log1p(-(2.0 ** (-5.0 - h)))
        decays.append((jnp.where(causal, jnp.exp(jnp.where(causal, rel, 0.0) * log_gamma), 0.0),
                       jnp.exp((idx + 1.0) * log_gamma),
                       jnp.exp((c_len - 1.0 - idx) * log_gamma),
                       math.exp(c_len * log_gamma)))

    def chunk(c, carry):
        r0 = pl.multiple_of(c * c_len, c_len)
        rows = pl.ds(r0, c_len)
        cos = cos_ref[rows, :]
        sin = sin_ref[rows, :]
        for h in range(RET_HEADS):
            inner, q_decay, k_decay, chunk_decay = decays[h]
            q = rot(q_ref[0, rows, h * dk:(h + 1) * dk].astype(F32), cos, sin)
            k = rot(k_ref[0, rows, h * dk:(h + 1) * dk].astype(F32), cos, sin) * (dk ** -0.5)
            v = v_ref[0, rows, h * dv:(h + 1) * dv]
            qb = q.astype(BF16)
            sc = _dot_nt(qb, k.astype(BF16)) * inner
            st = state_ref[h]
            o = _dot(sc.astype(BF16), v) + _dot(qb, st.astype(BF16)) * q_decay
            kd_t = jnp.transpose(k * k_decay).astype(BF16)
            state_ref[h] = st * chunk_decay + _dot(kd_t, v)
            gn = gn_ref[:, h * dv:(h + 1) * dv]
            on = _rms(o, gn)
            gate = g_ref[0, rows, h * dv:(h + 1) * dv].astype(F32)
            o_ref[0, rows, h * dv:(h + 1) * dv] = (_silu(gate) * on).astype(o_ref.dtype)
        return carry

    lax.fori_loop(0, ts // c_len, chunk, 0)


def _retention_core(proj, cos, sin, gn_gain, *, b, s, ts=TOKEN_TILE):
    n = proj.shape[2]
    hk = n // 6
    hv = 2 * hk
    dk = hk // RET_HEADS
    return pl.pallas_call(
        functools.partial(_ret_body, ts=ts),
        grid=(b, s // ts),
        in_specs=[
            pl.BlockSpec((1, ts, hk), lambda i, j: (i, j, 0)),
            pl.BlockSpec((1, ts, hk), lambda i, j: (i, j, 1)),
            pl.BlockSpec((1, ts, hv), lambda i, j: (i, j, 1)),
            pl.BlockSpec((1, ts, hv), lambda i, j: (i, j, 2)),
            pl.BlockSpec((ts, dk // 2), lambda i, j: (j, 0)),
            pl.BlockSpec((ts, dk // 2), lambda i, j: (j, 0)),
            pl.BlockSpec((1, hv), lambda i, j: (0, 0)),
        ],
        out_specs=pl.BlockSpec((1, ts, hv), lambda i, j: (i, j, 0)),
        out_shape=jax.ShapeDtypeStruct((b, s, hv), BF16),
        scratch_shapes=[pltpu.VMEM((RET_HEADS, dk, hv // RET_HEADS), F32)],
        compiler_params=_params("parallel", "arbitrary"),
        name="retention",
    )(proj, proj, proj, proj, cos, sin, gn_gain.reshape(1, hv))


def _rope_tables(s, half):
    inv = ROPE_BASE ** (-jnp.arange(half, dtype=F32) / half)
    ang = jnp.arange(s, dtype=F32)[:, None] * inv[None, :]
    return jnp.cos(ang), jnp.sin(ang)


def _retention(h2, g_norm, w_in, gn_gain, w_out, *, b, s):
    (proj,) = _norm_proj(h2, g_norm, [w_in.astype(BF16)], [BF16])
    n = proj.shape[1]
    cos, sin = _rope_tables(s, n // 6 // RET_HEADS // 2)
    y = _retention_core(proj.reshape(b, s, n), cos, sin, gn_gain, b=b, s=s)
    return y.reshape(b * s, -1), w_out.astype(BF16)


def _dil_proj_body(h_ref, g_ref, w_ref, o_ref, xn_ref, *, r):
    tm = h_ref.shape[0]
    n = tm // r
    xn = _rms(h_ref[...], g_ref[...])
    slabs = xn_ref.shape[0]
    for k in range(slabs):
        xn_ref[k] = xn[:, k * LANES:(k + 1) * LANES]
    xp = jnp.concatenate(
        [jnp.concatenate([xn_ref[k, pl.ds(c, n, stride=r), :] for c in range(r)], axis=0) for k in range(slabs)],
        axis=1).astype(BF16)
    for sl in _col_slices(w_ref.shape[1]):
        res = _dot(xp, w_ref[:, sl]).astype(o_ref.dtype)
        for c in range(r):
            o_ref[0, c, :, sl] = res[c * n:(c + 1) * n, :]


def _dil_proj(h2, g_norm, w, r, *, b, s, tm=TOKEN_TILE):
    d = h2.shape[1]
    n_out = w.shape[1]
    tiles = s // tm
    return pl.pallas_call(
        functools.partial(_dil_proj_body, r=r),
        grid=(b * tiles,),
        in_specs=[
            pl.BlockSpec((tm, d), lambda i: (i, 0)),
            pl.BlockSpec((1, d), lambda i: (0, 0)),
            _resident((d, n_out)),
        ],
        out_specs=pl.BlockSpec((1, r, tm // r, n_out), lambda i: (i // tiles, 0, i % tiles, 0)),
        out_shape=jax.ShapeDtypeStruct((b, r, s // r, n_out), BF16),
        scratch_shapes=[pltpu.VMEM((d // LANES, tm, LANES), F32)],
        compiler_params=_params("parallel"),
        name=f"dilated_proj_r{r}",
    )(h2, g_norm.reshape(1, d), w)


def _dil_attend(get_q, kbuf, vbuf, put_o, put_lse, *, rows, n_back):
    qb = DIL_QB
    dh = DIL_DH
    first_step = pl.program_id(2) == 0
    qi = lax.broadcasted_iota(jnp.int32, (qb, 2 * qb), 0)
    kj = lax.broadcasted_iota(jnp.int32, (qb, 2 * qb), 1)
    dist = qi + qb - kj
    band = (dist >= 0) & (dist <= n_back)
    lane = lax.broadcasted_iota(jnp.int32, (qb, LANES), 1)
    scale = dh ** -0.5
    ones = jnp.ones((2 * qb, dh), BF16)
    for i in range(rows // qb):
        mask = band & ((kj >= qb) | jnp.logical_not(first_step)) if i == 0 else band
        lse_tile = jnp.zeros((qb, LANES), F32)
        for h in range(DIL_HEADS):
            cols = slice(h * dh, (h + 1) * dh)
            q = get_q(slice(i * qb, (i + 1) * qb), cols)
            k = kbuf[i * qb:(i + 2) * qb, cols]
            v = vbuf[i * qb:(i + 2) * qb, cols]
            s = jnp.where(mask, _dot_nt(q, k), NEG_INF)
            m = jnp.max(s, axis=1, keepdims=True)
            e = jnp.exp2((s - m) * (scale * math.log2(math.e))).astype(BF16)
            acc = _dot(e, jnp.concatenate([v, ones], axis=1))
            den = jnp.maximum(acc[:, dh:], 1e-30)
            put_o(slice(i * qb, (i + 1) * qb), h, acc[:, :dh] / den)
            lse_tile = jnp.where(lane == h, m * scale + jnp.log(den), lse_tile)
        put_lse(slice(i * qb, (i + 1) * qb), lse_tile)


def _dil_body(q_ref, kc_ref, kp_ref, vc_ref, vp_ref, o_ref, lse_ref, kbuf, vbuf, *, rows, n_back):
    dh = DIL_DH

    def put_o(rs, h, o):
        o_ref[0, 0, rs, h * dh:(h + 1) * dh] = o.astype(o_ref.dtype)

    def put_lse(rs, lse_tile):
        lse_ref[0, 0, rs, :] = lse_tile

    qb = DIL_QB
    kbuf[0:qb, :] = kp_ref[0, 0]
    kbuf[qb:, :] = kc_ref[0, 0]
    vbuf[0:qb, :] = vp_ref[0, 0]
    vbuf[qb:, :] = vc_ref[0, 0]
    _dil_attend(lambda rs, cols: q_ref[0, 0, rs, cols], kbuf, vbuf, put_o, put_lse, rows=rows, n_back=n_back)


def _dilated_group(proj, win, r, *, b, s):
    hd = DIL_HEADS * DIL_DH
    length = s // r
    rows = min(length, TOKEN_TILE)
    qb = DIL_QB

    def cur(which):
        return pl.BlockSpec((1, 1, rows, hd), lambda i, c, n: (i, c, n, which))

    def prev(which):
        return pl.BlockSpec((1, 1, qb, hd), lambda i, c, n: (i, c, jnp.maximum(n * (rows // qb) - 1, 0), which))

    return pl.pallas_call(
        functools.partial(_dil_body, rows=rows, n_back=win // r),
        grid=(b, r, length // rows),
        in_specs=[cur(0), cur(1), prev(1), cur(2), prev(2)],
        out_specs=[
            pl.BlockSpec((1, 1, rows, hd), lambda i, c, n: (i, c, n, 0)),
            pl.BlockSpec((1, 1, rows, LANES), lambda i, c, n: (i, c, n, 0)),
        ],
        out_shape=[
            jax.ShapeDtypeStruct((b, r, length, hd), BF16),
            jax.ShapeDtypeStruct((b, r, length, LANES), F32),
        ],
        scratch_shapes=[pltpu.VMEM((rows + qb, hd), BF16), pltpu.VMEM((rows + qb, hd), BF16)],
        compiler_params=_params("parallel", "parallel", "arbitrary"),
        name=f"dilated_r{r}",
    )(proj, proj, proj, proj, proj)


def _dil_merge_body(h_ref, g_ref, wg_ref, *refs, rows, n_back, dilations):
    ng = len(dilations)
    o_refs, l_refs = refs[:ng - 1], refs[ng - 1:2 * ng - 2]
    w_ref, out_ref, qbuf, kbuf, vbuf = refs[2 * ng - 2:2 * ng + 3]
    o_bufs, l_bufs = refs[2 * ng + 3:3 * ng + 3], refs[3 * ng + 3:]
    tm = h_ref.shape[1]
    qb = DIL_QB
    hd = qbuf.shape[1]

    @pl.when(pl.program_id(2) == 0)
    def _():
        kbuf[0:qb, :] = jnp.zeros((qb, hd), BF16)
        vbuf[0:qb, :] = jnp.zeros((qb, hd), BF16)

    @pl.when(pl.program_id(2) > 0)
    def _():
        kbuf[0:qb, :] = kbuf[rows:rows + qb, :]
        vbuf[0:qb, :] = vbuf[rows:rows + qb, :]

    xn = _rms(h_ref[0], g_ref[...]).astype(BF16)
    for part, (dst, row0) in enumerate(((qbuf, 0), (kbuf, qb), (vbuf, qb))):
        dst[row0:row0 + rows, :] = _dot(xn, wg_ref[:, part * hd:(part + 1) * hd]).astype(BF16)

    def put_o(rs, h, o):
        o_bufs[0][h, rs, :] = o

    def put_lse(rs, lse_tile):
        l_bufs[0][rs, :] = lse_tile

    _dil_attend(lambda rs, cols: qbuf[rs, cols], kbuf, vbuf, put_o, put_lse, rows=rows, n_back=n_back)
    for o_ref, l_ref, o_buf, l_buf, r in zip(o_refs, l_refs, o_bufs[1:], l_bufs[1:], dilations[1:]):
        n = tm // r
        for c in range(r):
            rs = pl.ds(c, n, stride=r)
            l_buf[rs, :] = l_ref[0, c]
            blk = o_ref[0, c].astype(F32)
            for h in range(DIL_HEADS):
                o_buf[h, rs, :] = blk[:, h * DIL_DH:(h + 1) * DIL_DH]
    lses = [l[...] for l in l_bufs]
    mx = functools.reduce(jnp.maximum, lses)
    es = [jnp.exp(l - mx) for l in lses]
    tot = functools.reduce(jnp.add, es)
    wts = [e / tot for e in es]
    dh = DIL_DH
    parts = []
    for h in range(DIL_HEADS):
        acc = jnp.zeros((tm, dh), F32)
        for g in range(ng):
            wg = jnp.broadcast_to(wts[g][:, h:h + 1], (tm, dh))
            acc = acc + wg * o_bufs[g][h]
        parts.append(acc)
    o = jnp.concatenate(parts, axis=1).astype(BF16)
    out_ref[0] = h_ref[0] + _dot(o, w_ref[...])


def _dil_merge(h3, g_norm, w_g, win, outs, lses, w, dilations, *, tm=TOKEN_TILE):
    b, s, d = h3.shape
    hd = w.shape[0]
    qb = DIL_QB
    assert dilations[0] == 1
    res_major = lambda r, width: pl.BlockSpec((1, r, tm // r, width), lambda i, c, n: (i, 0, n, 0))
    return pl.pallas_call(
        functools.partial(_dil_merge_body, rows=tm, n_back=win, dilations=tuple(dilations)),
        grid=(b, 1, s // tm),
        in_specs=[pl.BlockSpec((1, tm, d), lambda i, c, n: (i, n, 0)), _resident((1, d)), _resident(w_g.shape)]
        + [res_major(r, hd) for r in dilations[1:]] + [res_major(r, LANES) for r in dilations[1:]]
        + [_resident((hd, d))],
        out_specs=pl.BlockSpec((1, tm, d), lambda i, c, n: (i, n, 0)),
        out_shape=jax.ShapeDtypeStruct((b, s, d), F32),
        scratch_shapes=[pltpu.VMEM((tm, hd), BF16), pltpu.VMEM((tm + qb, hd), BF16), pltpu.VMEM((tm + qb, hd), BF16)]
        + [pltpu.VMEM((DIL_HEADS, tm, DIL_DH), F32) for _ in dilations]
        + [pltpu.VMEM((tm, LANES), F32) for _ in dilations],
        compiler_params=_params("parallel", "parallel", "arbitrary"),
        name="dilated_r1_merge",
    )(h3, g_norm.reshape(1, d), w_g, *outs, *lses, w)


def _dilated(h2, g_norm, w_in, w_out, *, b, s):
    d = h2.shape[1]
    gw = 3 * DIL_HEADS * DIL_DH
    wb = w_in.astype(BF16)
    outs, lses, dilations = [], [], []
    order = sorted(range(len(DIL_PATTERN)), key=lambda g: DIL_PATTERN[g][1] == 1)
    for g in order:
        win, r = DIL_PATTERN[g]
        w_g = wb[:, g * gw:(g + 1) * gw]
        if r == 1:
            break
        proj = _dil_proj(h2, g_norm, w_g, r, b=b, s=s, tm=2 * TOKEN_TILE)
        o, lse = _dilated_group(proj, win, r, b=b, s=s)
        outs.append(o)
        lses.append(lse)
        dilations.append(r)
    return _dil_merge(h2.reshape(b, s, d), g_norm, w_g, win, outs, lses, w_out.astype(BF16),
                      [1] + dilations).reshape(b * s, d)


def _softplus(x):
    return jnp.maximum(x, 0.0) + jnp.log1p(jnp.exp(-jnp.abs(x)))


def _ssd_body(z_ref, x_ref, dt_ref, cw_ref, cb_ref, dtb_ref, alog_ref, dsk_ref, ng_ref, o_ref,
              xpad, state_ref):
    ln = SSD_CHUNK
    nst = SSD_STATE
    gw = SSD_HPG * SSD_HEADDIM
    d_inner = SSD_GROUPS * gw
    halo = 8

    @pl.when(pl.program_id(1) == 0)
    def _():
        xpad[0:halo, :] = jnp.zeros((halo, xpad.shape[1]), F32)
        state_ref[...] = jnp.zeros(state_ref.shape, F32)

    x_b = x_ref[0]
    x_f = x_b.astype(F32)
    ii = lax.broadcasted_iota(jnp.int32, (ln, ln), 0)
    jj = lax.broadcasted_iota(jnp.int32, (ln, ln), 1)
    conv = cb_ref[...] + x_f * cw_ref[SSD_CONV - 1:SSD_CONV, :]
    for shift in range(1, SSD_CONV):
        shifted = _dot(jnp.where(ii - jj == shift, 1.0, 0.0).astype(BF16), x_b)
        conv = conv + shifted * cw_ref[SSD_CONV - 1 - shift:SSD_CONV - shift, :]
    xpad[halo:2 * halo, :] = x_f[0:halo, :]
    head = cb_ref[...]
    for k in range(SSD_CONV):
        off = halo - (SSD_CONV - 1) + k
        head = head + xpad[off:off + halo, :] * cw_ref[k:k + 1, :]
    xpad[0:halo, :] = x_f[ln - halo:ln, :]
    xbc = _silu(jnp.concatenate([head, conv[halo:, :]], axis=0))
    xs = xbc[:, :d_inner]
    bm = xbc[:, d_inner:d_inner + SSD_GROUPS * nst]
    cm = xbc[:, d_inner + SSD_GROUPS * nst:]

    dt = _softplus(dt_ref[0] + dtb_ref[...])
    da = dt * (-jnp.exp(alog_ref[...]))
    ii = lax.broadcasted_iota(jnp.int32, (ln, ln), 0)
    jj = lax.broadcasted_iota(jnp.int32, (ln, ln), 1)
    causal = ii >= jj
    tril = jnp.where(causal, 1.0, 0.0).astype(BF16)
    da_terms = _split3(da)
    acs = _dot(tril, da_terms[0]) + _dot(tril, da_terms[1]) + _dot(tril, da_terms[2])
    acs2 = acs * math.log2(math.e)
    acs2_t = jnp.transpose(acs2)
    hl = ln // 2

    erow = lax.broadcasted_iota(jnp.int32, (LANES, d_inner), 0)
    ecol = lax.broadcasted_iota(jnp.int32, (LANES, d_inner), 1)
    expand = jnp.where(ecol // SSD_HEADDIM == erow, 1.0, 0.0).astype(BF16)
    acs_e = _dot3(_split3(acs), expand)
    dt_e = _dot3(_split3(dt), expand)
    last = acs_e[ln - 1:ln, :]
    decay_in = jnp.exp(acs_e)
    xs_dt = xs * dt_e
    xs_end = (xs_dt * jnp.exp(last - acs_e)).astype(BF16)
    xs_b = xs_dt.astype(BF16)
    lane = lax.broadcasted_iota(jnp.int32, (ln, LANES), 1)

    y_groups = []
    for g in range(SSD_GROUPS):
        bm_g = bm[:, g * nst:(g + 1) * nst]
        cm_g = cm[:, g * nst:(g + 1) * nst].astype(BF16)
        cb = jnp.where(causal, _dot_nt(cm_g, bm_g.astype(BF16)), 0.0)
        st = state_ref[g]
        gcols = slice(g * gw, (g + 1) * gw)
        y_state = _dot(cm_g, st.astype(BF16)) * decay_in[:, gcols]
        pairs = []
        for p in range(SSD_HPG // 2):
            pair_cols = slice(g * gw + p * LANES, g * gw + (p + 1) * LANES)
            halves = []
            for e in range(2):
                hd = g * SSD_HPG + 2 * p + e
                top = cb[:hl, :hl] * jnp.exp2(jnp.minimum(acs2[:hl, hd:hd + 1] - acs2_t[hd:hd + 1, :hl], 0.0))
                bot = cb[hl:, :] * jnp.exp2(jnp.minimum(acs2[hl:, hd:hd + 1] - acs2_t[hd:hd + 1, :], 0.0))
                halves.append(jnp.concatenate([_dot(top.astype(BF16), xs_b[:hl, pair_cols]),
                                               _dot(bot.astype(BF16), xs_b[:, pair_cols])], axis=0))
            pairs.append(jnp.where(lane < SSD_HEADDIM, halves[0], halves[1]))
        y_groups.append(jnp.concatenate(pairs, axis=1) + y_state)
        bm_t = jnp.transpose(bm_g).astype(BF16)
        state_ref[g] = st * jnp.exp(last[:, gcols]) + _dot(bm_t, xs_end[:, gcols])

    y = jnp.concatenate(y_groups, axis=1) + dsk_ref[...] * xs
    yz = y * _silu(z_ref[0].astype(F32))
    outs = [_rms(yz[:, g * gw:(g + 1) * gw], ng_ref[:, g * gw:(g + 1) * gw]) for g in range(SSD_GROUPS)]
    o_ref[0] = jnp.concatenate(outs, axis=1).astype(o_ref.dtype)


def _ssd_core(z, xbc, dt, conv_w, conv_b, dt_bias, a_log, d_skip, norm_g, *, b, s):
    ln = SSD_CHUNK
    d_inner = z.shape[2]
    conv_dim = xbc.shape[2]
    gw = SSD_HPG * SSD_HEADDIM
    full = lambda shape: pl.BlockSpec(shape, lambda i, j: (0,) * len(shape))
    return pl.pallas_call(
        _ssd_body,
        grid=(b, s // ln),
        in_specs=[
            pl.BlockSpec((1, ln, d_inner), lambda i, j: (i, j, 0)),
            pl.BlockSpec((1, ln, conv_dim), lambda i, j: (i, j, 0)),
            pl.BlockSpec((1, ln, LANES), lambda i, j: (i, j, 0)),
            full((SSD_CONV, conv_dim)),
            full((1, conv_dim)),
            full((1, LANES)),
            full((1, LANES)),
            full((1, d_inner)),
            full((1, d_inner)),
        ],
        out_specs=pl.BlockSpec((1, ln, d_inner), lambda i, j: (i, j, 0)),
        out_shape=jax.ShapeDtypeStruct((b, s, d_inner), BF16),
        scratch_shapes=[pltpu.VMEM((16, conv_dim), F32),
                        pltpu.VMEM((SSD_GROUPS, SSD_STATE, gw), F32)],
        compiler_params=_params("parallel", "arbitrary"),
        name="ssd",
    )(z, xbc, dt, conv_w, conv_b, dt_bias, a_log, d_skip, norm_g)


def _ssd(h2, g_norm, w_in, conv_w, conv_b, dt_bias, a_log, d_skip, norm_g, w_out, *, b, s):
    heads = SSD_GROUPS * SSD_HPG
    d_inner = heads * SSD_HEADDIM
    conv_dim = conv_w.shape[1]
    wb = w_in.astype(BF16)
    w_dt = jnp.pad(wb[:, d_inner + conv_dim:], ((0, 0), (0, LANES - heads)))
    z, xbc, dt = _norm_proj(h2, g_norm, [wb[:, :d_inner], wb[:, d_inner:d_inner + conv_dim], w_dt], [BF16, BF16, F32])
    pad_heads = lambda v: jnp.pad(v.astype(F32), (0, LANES - heads)).reshape(1, LANES)
    y = _ssd_core(z.reshape(b, s, -1), xbc.reshape(b, s, -1), dt.reshape(b, s, LANES),
                  conv_w.astype(F32), conv_b.reshape(1, -1).astype(F32), pad_heads(dt_bias), pad_heads(a_log),
                  jnp.repeat(d_skip.astype(F32), SSD_HEADDIM).reshape(1, d_inner), norm_g.reshape(1, d_inner),
                  b=b, s=s)
    return y.reshape(b * s, d_inner), w_out.astype(BF16)


def _nsa_proj_body(h_ref, g_ref, wn_ref, wt_ref, k0_ref, v0_ref, k12_ref, tr_ref, stage_ref):
    xn = _rms(h_ref[...], g_ref[...]).astype(BF16)
    nat = _dot(xn, wn_ref[...])
    tm = nat.shape[0]
    gd = k12_ref.shape[1] // 2
    k12_ref[...] = nat[:, 2 * gd:].astype(k12_ref.dtype)
    slabs = gd // LANES
    rows = tm // NSA_CMP_STRIDE
    for sl in range(2 * slabs):
        stage_ref[sl] = nat[:, sl * LANES:(sl + 1) * LANES]
    for l in range(NSA_CMP_STRIDE):
        for sl in range(slabs):
            cols = slice(l * gd + sl * LANES, l * gd + (sl + 1) * LANES)
            k0_ref[:, cols] = stage_ref[sl, pl.ds(l, rows, stride=NSA_CMP_STRIDE), :].astype(k0_ref.dtype)
            v0_ref[:, cols] = stage_ref[slabs + sl, pl.ds(l, rows, stride=NSA_CMP_STRIDE), :].astype(v0_ref.dtype)
    res = _dot_nt(wt_ref[...], xn)
    for j in range(tr_ref.shape[0]):
        tr_ref[j] = res[:, j * NSA_QT:(j + 1) * NSA_QT].astype(tr_ref.dtype)


def _nsa_proj(h2, g_norm, w_nat, w_t, *, tm=TOKEN_TILE):
    m, d = h2.shape
    gd = NSA_GROUPS * NSA_DH
    nt = w_t.shape[0]
    slabs = tm // NSA_QT
    return pl.pallas_call(
        _nsa_proj_body,
        grid=(m // tm,),
        in_specs=[
            pl.BlockSpec((tm, d), lambda i: (i, 0)),
            pl.BlockSpec((1, d), lambda i: (0, 0)),
            _resident(w_nat.shape),
            _resident(w_t.shape),
        ],
        out_specs=[
            pl.BlockSpec((tm // NSA_CMP_STRIDE, NSA_CMP_STRIDE * gd), lambda i: (i, 0)),
            pl.BlockSpec((tm // NSA_CMP_STRIDE, NSA_CMP_STRIDE * gd), lambda i: (i, 0)),
            pl.BlockSpec((tm, 2 * gd), lambda i: (i, 0)),
            pl.BlockSpec((slabs, nt, NSA_QT), lambda i: (i, 0, 0)),
        ],
        out_shape=[
            jax.ShapeDtypeStruct((m // NSA_CMP_STRIDE, NSA_CMP_STRIDE * gd), BF16),
            jax.ShapeDtypeStruct((m // NSA_CMP_STRIDE, NSA_CMP_STRIDE * gd), BF16),
            jax.ShapeDtypeStruct((m, 2 * gd), BF16),
            jax.ShapeDtypeStruct((m // NSA_QT, nt, NSA_QT), BF16),
        ],
        scratch_shapes=[pltpu.VMEM((2 * gd // LANES, tm, LANES), F32)],
        compiler_params=_params("parallel"),
        name="nsa_proj",
    )(h2, g_norm.reshape(1, d), w_nat, w_t)


def _nsa_cmp_body(x_ref, pa_ref, pb_ref, w1_ref, w2_ref, o_ref, w1a_big, w1b_big, w2_big, *, transposed):
    g, dh = NSA_GROUPS, NSA_DH
    hid = w1_ref.shape[1]
    half = NSA_CMP_LEN // 2

    @pl.when(pl.program_id(0) == 0)
    def _():
        w1a_big[...] = jnp.zeros(w1a_big.shape, BF16)
        w1b_big[...] = jnp.zeros(w1b_big.shape, BF16)
        w2_big[...] = jnp.zeros(w2_big.shape, BF16)
        for gi in range(g):
            cols = slice(gi * hid, (gi + 1) * hid)
            for l in range(half):
                rows = slice((l * g + gi) * dh, (l * g + gi + 1) * dh)
                w1a_big[rows, cols] = w1_ref[l * dh:(l + 1) * dh, :]
                w1b_big[rows, cols] = w1_ref[(half + l) * dh:(half + l + 1) * dh, :]
            w2_big[cols, gi * dh:(gi + 1) * dh] = w2_ref[...]

    x = x_ref[0].astype(F32)
    nb = x.shape[0]
    ya = _dot((x + pa_ref[...]).astype(BF16), w1a_big[...])
    yb = _dot((x + pb_ref[...]).astype(BF16), w1b_big[...])
    hidden = _silu(ya + pltpu.roll(yb, nb - 1, 0))
    out = _dot(hidden.astype(BF16), w2_big[...])
    o_ref[0] = (jnp.transpose(out) if transposed else out).astype(o_ref.dtype)


def _nsa_compress(x, pos, w1, w2, *, transposed):
    b, nb, width = x.shape
    g, dh = NSA_GROUPS, NSA_DH
    half = NSA_CMP_LEN // 2
    hid = w1.shape[1]
    posb = jnp.broadcast_to(pos[:, None, :], (NSA_CMP_LEN, g, dh)).reshape(NSA_CMP_LEN, g * dh)
    pa = posb[:half].reshape(1, width).astype(F32)
    pb = posb[half:].reshape(1, width).astype(F32)
    out_dims = (g * dh, nb) if transposed else (nb, g * dh)
    return pl.pallas_call(
        functools.partial(_nsa_cmp_body, transposed=transposed),
        grid=(b,),
        in_specs=[
            pl.BlockSpec((1, nb, width), lambda i: (i, 0, 0)),
            _resident((1, width)), _resident((1, width)),
            _resident(w1.shape), _resident(w2.shape),
        ],
        out_specs=pl.BlockSpec((1,) + out_dims, lambda i: (i, 0, 0)),
        out_shape=jax.ShapeDtypeStruct((b,) + out_dims, BF16),
        scratch_shapes=[pltpu.VMEM((width, g * hid), BF16), pltpu.VMEM((width, g * hid), BF16),
                        pltpu.VMEM((g * hid, g * dh), BF16)],
        compiler_params=_params("arbitrary"),
        name="nsa_compress",
    )(x, pa, pb, w1.astype(BF16), w2.astype(BF16))


def _nsa_pair_body(q_ref, gt_ref, kc_ref, vct_ref, k1_ref, k2_ref, v1_ref, v2_ref, o_ref, bias_ref, wbias_ref, *, s):
    qt, kc, dh = NSA_QT, NSA_KC, NSA_DH
    gl = 2
    glanes = NSA_HPG * qt
    lanes = gl * glanes
    vrows = gl * dh
    j = pl.program_id(2)
    t0 = j * qt
    nb = kc_ref.shape[1]
    nsb = s // NSA_SEL_LEN
    n_sel = min(NSA_N_SEL, nsb)

    q_t = q_ref[0, 0]
    qcat = [jnp.concatenate([q_t[(g * NSA_HPG + h) * dh:(g * NSA_HPG + h + 1) * dh, :] for h in range(NSA_HPG)],
                            axis=1) for g in range(gl)]
    zero = jnp.zeros_like(qcat[0])
    qp = jnp.concatenate([jnp.concatenate([qcat[0], zero], axis=1),
                          jnp.concatenate([zero, qcat[1]], axis=1)], axis=0)
    qlane = lax.broadcasted_iota(jnp.int32, (1, lanes), 1) % qt
    tq = t0 + qlane

    def values(v_ref, slab0, n_slabs):
        v = jnp.concatenate([v_ref[0, slab0 + i] for i in range(n_slabs)], axis=1)
        return jnp.concatenate([v, jnp.ones((16, n_slabs * qt), BF16)], axis=0)

    def weighted_values(v_aug, p):
        return jnp.concatenate(
            [_dot(jnp.concatenate([v_aug[g * dh:(g + 1) * dh, :], v_aug[vrows:, :]], axis=0),
                  p[:, g * glanes:(g + 1) * glanes]) for g in range(gl)], axis=1)

    def normalise(acc):
        o = acc[:dh, :] * (1.0 / jnp.maximum(acc[dh:dh + 1, :], 1e-30))
        return [o[:, g * glanes:(g + 1) * glanes] for g in range(gl)]

    wslabs = NSA_WINDOW // qt + 1
    row_w = lax.broadcasted_iota(jnp.int32, (qt, lanes), 0)

    @pl.when(j == 0)
    def _():
        wbias_ref[0] = jnp.where(row_w > qlane, 0.0, NEG_INF)
        wbias_ref[1] = jnp.zeros((qt, lanes), F32)
        wbias_ref[2] = jnp.where(row_w <= qlane, 0.0, NEG_INF)
        wbias_ref[3] = jnp.full((qt, lanes), NEG_INF, F32)

    jb0 = jnp.maximum(j + 1 - wslabs, 0)
    kw = k2_ref[0, pl.ds(pl.multiple_of(jb0 * qt, qt), wslabs * qt), :]
    sw_raw = _dot(kw, qp)
    slabs_back = [j - jb0 - i for i in range(wslabs)]
    sw = jnp.concatenate(
        [sw_raw[i * qt:(i + 1) * qt, :]
         + wbias_ref[jnp.where(back < 0, 3, jnp.where(back == wslabs - 1, 0, jnp.where(back == 0, 2, 1)))]
         for i, back in enumerate(slabs_back)], axis=0)

    sc = _dot(kc_ref[0], qp)
    assert NSA_CMP_STRIDE == 16
    last_visible = jnp.minimum(jnp.right_shift(tq - (NSA_CMP_LEN - 1), 4), nb - 2)
    sc = jnp.where(lax.broadcasted_iota(jnp.int32, (nb, lanes), 0) <= last_visible, sc, NEG_INF)
    m_c = jnp.max(sc, axis=0, keepdims=True)
    e = jnp.exp2(sc - jnp.where(m_c > 0.5 * NEG_INF, m_c, 0.0))
    p_c = e * (1.0 / jnp.maximum(jnp.sum(e, axis=0, keepdims=True), 1e-30))
    acc_c = _dot(vct_ref[0], p_c.astype(BF16))
    o_c = [acc_c[g * dh:(g + 1) * dh, g * glanes:(g + 1) * glanes] for g in range(gl)]
    imps = []
    for g in range(gl):
        imp = p_c[:, g * glanes:g * glanes + qt]
        for h in range(1, NSA_HPG):
            imp = imp + p_c[:, g * glanes + h * qt:g * glanes + (h + 1) * qt]
        imps.append(imp)
    imp = jnp.concatenate(imps, axis=1)

    ratio = NSA_SEL_LEN // NSA_CMP_STRIDE
    dd = lax.broadcasted_iota(jnp.int32, (nsb, nb), 1) - ratio * lax.broadcasted_iota(jnp.int32, (nsb, nb), 0)
    wsel = jnp.where((dd == -1) | (dd == ratio - 1), 1.0,
                     jnp.where((dd >= 0) & (dd < ratio - 1), 2.0, 0.0)).astype(BF16)
    terms = _split3(imp)
    imp_sel = _dot(wsel, terms[0]) + _dot(wsel, terms[1]) + _dot(wsel, terms[2])
    jq = lax.broadcasted_iota(jnp.int32, (nsb, gl * qt), 0)
    cur = (t0 + lax.broadcasted_iota(jnp.int32, (nsb, gl * qt), 1) % qt) // NSA_SEL_LEN
    forced = (jq == 0) | (jq == cur) | (jq == cur - 1)
    score = jnp.where(jq <= cur, imp_sel + jnp.where(forced, NSA_FORCE_BONUS, 0.0), NEG_INF)

    pw = jnp.exp2(sw - jnp.max(sw, axis=0, keepdims=True)).astype(BF16)
    o_w = normalise(weighted_values(values(v2_ref, jb0, wslabs), pw))

    jqf = jq.astype(F32)
    work = score
    picked = None
    for _ in range(n_sel):
        top = jnp.max(work, axis=0, keepdims=True)
        first = jnp.min(jnp.where(work == top, jqf, float(nsb)), axis=0, keepdims=True)
        pick = jqf == first
        picked = pick if picked is None else picked | pick
        work = jnp.where(pick, 3.0 * NEG_INF, work)
    sel_bias = jnp.where(picked & (score > 0.5 * NEG_INF), 0.0, NEG_INF)
    bias_ref[...] = jnp.concatenate([sel_bias[:, g * qt:(g + 1) * qt] for g in range(gl) for _ in range(NSA_HPG)],
                                    axis=1)

    blocks_per_chunk = kc // NSA_SEL_LEN
    slabs_per_chunk = kc // qt

    def online(carry, st, bias, v_aug):
        m, acc = carry
        blks = [st[r * NSA_SEL_LEN:(r + 1) * NSA_SEL_LEN, :] for r in range(blocks_per_chunk)]
        part = None
        for r in range(blocks_per_chunk):
            pr = blks[r][0:8, :]
            for i in range(1, NSA_SEL_LEN // 8):
                pr = jnp.maximum(pr, blks[r][8 * i:8 * (i + 1), :])
            pr = pr + bias[r:r + 1, :]
            part = pr if part is None else jnp.maximum(part, pr)
        m_new = jnp.maximum(m, jnp.max(part, axis=0, keepdims=True))
        p = jnp.concatenate([jnp.exp2(blks[r] - (m_new - bias[r:r + 1, :])) for r in range(blocks_per_chunk)],
                            axis=0).astype(BF16)
        return m_new, jnp.exp2(m - m_new) * acc + weighted_values(v_aug, p)

    def sel_scores(c):
        return _dot(k1_ref[0, pl.ds(pl.multiple_of(c * kc, kc), kc), :], qp)

    def sel_bias(c):
        return bias_ref[pl.ds(pl.multiple_of(c * blocks_per_chunk, blocks_per_chunk), blocks_per_chunk), :]

    def sel_values(c):
        return values(v1_ref, c * slabs_per_chunk, slabs_per_chunk)

    init = (jnp.full((1, lanes), NEG_INF, F32), jnp.zeros((dh + 16, lanes), F32))
    c_hi = t0 // kc
    own_slab = (t0 - c_hi * kc) // qt
    st_raw = sel_scores(c_hi)
    st = jnp.concatenate(
        [st_raw[i * qt:(i + 1) * qt, :] + wbias_ref[jnp.where(i < own_slab, 1, jnp.where(i == own_slab, 2, 3))]
         for i in range(slabs_per_chunk)], axis=0)
    chain_a = online(init, st, sel_bias(c_hi), sel_values(c_hi))
    chain_a = lax.cond(c_hi % 2 == 1,
                       lambda cr: online(cr, sel_scores(c_hi - 1), sel_bias(c_hi - 1), sel_values(c_hi - 1)),
                       lambda cr: cr, chain_a)
    chain_b = (chain_a[0], jnp.zeros_like(chain_a[1]))

    def pair(i, chains):
        ca, cb = chains
        st_a = sel_scores(2 * i)
        st_b = sel_scores(2 * i + 1)
        return (online(ca, st_a, sel_bias(2 * i), sel_values(2 * i)),
                online(cb, st_b, sel_bias(2 * i + 1), sel_values(2 * i + 1)))

    (m_a, acc_a), (m_b, acc_b) = lax.fori_loop(0, c_hi // 2, pair, (chain_a, chain_b))
    m_s = jnp.maximum(m_a, m_b)
    o_s = normalise(jnp.exp2(m_a - m_s) * acc_a + jnp.exp2(m_b - m_s) * acc_b)

    gates = jax.nn.sigmoid(gt_ref[0, 0].astype(F32))
    outs = []
    for g in range(gl):
        def gate_row(br):
            r0 = (g * 4 + br) * NSA_HPG
            return jnp.concatenate([gates[r0 + h:r0 + h + 1, :] for h in range(NSA_HPG)], axis=1)
        o = gate_row(0) * o_c[g] + gate_row(1) * o_s[g] + gate_row(2) * o_w[g]
        outs.extend(o[:, h * qt:(h + 1) * qt] for h in range(NSA_HPG))
    o_ref[0] = jnp.transpose(jnp.concatenate(outs, axis=0)).astype(o_ref.dtype)


def _nsa_attend_pairs(tr, k_cmp, v_cmp_t, k12, *, b, s):
    qt, dh = NSA_QT, NSA_DH
    gl = 2
    gd = NSA_GROUPS * dh
    hq = NSA_GROUPS * NSA_HPG * dh
    pairs = NSA_GROUPS // gl
    nslab = s // qt
    assert nslab > NSA_WINDOW // qt
    nb = k_cmp.shape[1]
    tr4 = tr.reshape(b, nslab, tr.shape[1], qt)
    q_rows = gl * NSA_HPG * dh
    v_rows = gl * dh
    gate_rows = gl * 4 * NSA_HPG
    v1_blk = hq // v_rows
    v2_blk = (hq + gd) // v_rows
    gate_blk = (hq + 2 * gd) // gate_rows
    nsb = s // NSA_SEL_LEN
    return pl.pallas_call(
        functools.partial(_nsa_pair_body, s=s),
        grid=(b, pairs, nslab),
        in_specs=[
            pl.BlockSpec((1, 1, q_rows, qt), lambda i, p, j: (i, j, p, 0)),
            pl.BlockSpec((1, 1, gate_rows, qt), lambda i, p, j: (i, j, gate_blk + p, 0)),
            pl.BlockSpec((1, nb, v_rows), lambda i, p, j: (i, 0, p)),
            pl.BlockSpec((1, v_rows, nb), lambda i, p, j: (i, p, 0)),
            pl.BlockSpec((1, s, v_rows), lambda i, p, j: (i, 0, p)),
            pl.BlockSpec((1, s, v_rows), lambda i, p, j: (i, 0, pairs + p)),
            pl.BlockSpec((1, nslab, v_rows, qt), lambda i, p, j: (i, 0, v1_blk + p, 0)),
            pl.BlockSpec((1, nslab, v_rows, qt), lambda i, p, j: (i, 0, v2_blk + p, 0)),
        ],
        out_specs=pl.BlockSpec((1, qt, q_rows), lambda i, p, j: (i, j, p)),
        out_shape=jax.ShapeDtypeStruct((b, s, hq), BF16),
        scratch_shapes=[pltpu.VMEM((nsb, gl * NSA_HPG * qt), F32), pltpu.VMEM((4, qt, gl * NSA_HPG * qt), F32)],
        compiler_params=_params("parallel", "parallel", "arbitrary"),
        name="nsa_attend",
    )(tr4, tr4, k_cmp, v_cmp_t, k12, k12, tr4, tr4)


def _nsa(h2, g_norm, w_in, cmp_pos, cmp_w1, cmp_w2, w_out, *, b, s):
    g, hpg, dh = NSA_GROUPS, NSA_HPG, NSA_DH
    hq, gd = g * hpg * dh, g * dh
    kv_w = lambda br, kv: w_in[:, hq + (2 * br + kv) * gd:hq + (2 * br + kv + 1) * gd]
    w_q = w_in[:, :hq] * (dh ** -0.5 * math.log2(math.e))
    w_g = w_in[:, hq + 6 * gd:].reshape(-1, 3, g, hpg).transpose(2, 1, 3, 0)
    w_g = jnp.pad(w_g, ((0, 0), (0, 1), (0, 0), (0, 0))).reshape(g * 4 * hpg, -1)
    w_t = jnp.concatenate([w_q.T, kv_w(1, 1).T, kv_w(2, 1).T, w_g], axis=0).astype(BF16)
    w_nat = jnp.concatenate([kv_w(0, 0), kv_w(0, 1), kv_w(1, 0), kv_w(2, 0)], axis=1).astype(BF16)
    k0, v0, k12, tr = _nsa_proj(h2, g_norm, w_nat, w_t)
    nb = s // NSA_CMP_STRIDE
    k_cmp = _nsa_compress(k0.reshape(b, nb, NSA_CMP_STRIDE * gd), cmp_pos[0], cmp_w1[0], cmp_w2[0], transposed=False)
    v_cmp_t = _nsa_compress(v0.reshape(b, nb, NSA_CMP_STRIDE * gd), cmp_pos[1], cmp_w1[1], cmp_w2[1], transposed=True)
    y = _nsa_attend_pairs(tr, k_cmp, v_cmp_t, k12.reshape(b, s, 2 * gd), b=b, s=s)
    return y.reshape(b * s, hq), w_out.astype(BF16)


def kernel(x, norm_ffn1, ffn1_w_in, ffn1_w_out, norm_mix, norm_ffn2, ffn2_w_in, ffn2_w_out, norm_final,
           ret_w_in, ret_gn_gain, ret_w_out,
           nsa_w_in, nsa_cmp_pos, nsa_cmp_w1, nsa_cmp_w2, nsa_w_out,
           ssd_w_in, ssd_conv_w, ssd_conv_b, ssd_dt_bias, ssd_a_log, ssd_d, ssd_norm, ssd_w_out,
           dil_w_in, dil_w_out):
    b, s, d = x.shape
    depth = norm_mix.shape[0]
    h = x.reshape(b * s, d)
    w1_in, w1_out = ffn1_w_in.astype(BF16), ffn1_w_out.astype(BF16)
    w2_in, w2_out = ffn2_w_in.astype(BF16), ffn2_w_out.astype(BF16)
    for i in range(depth):
        h = _ffn(h, norm_ffn1[i], w1_in, w1_out, i)
        m, j = i % 4, i // 4
        mix = None
        if m == 0:
            mix = _retention(h, norm_mix[i], ret_w_in[j], ret_gn_gain[j], ret_w_out[j], b=b, s=s)
        elif m == 1:
            mix = _nsa(h, norm_mix[i], nsa_w_in[j], nsa_cmp_pos[j], nsa_cmp_w1[j], nsa_cmp_w2[j], nsa_w_out[j], b=b, s=s)
        elif m == 2:
            mix = _ssd(h, norm_mix[i], ssd_w_in[j], ssd_conv_w[j], ssd_conv_b[j], ssd_dt_bias[j], ssd_a_log[j],
                       ssd_d[j], ssd_norm[j], ssd_w_out[j], b=b, s=s)
        else:
            h = _dilated(h, norm_mix[i], dil_w_in[j], dil_w_out[j], b=b, s=s)
        h = _ffn(h, norm_ffn2[i], w2_in, w2_out, i,
                 norm_final if i == depth - 1 else None, mix)
    return h.reshape(b, s, d)
```

```python
import functools
import math

import jax
import jax.numpy as jnp
from jax import lax
from jax.experimental import pallas as pl
from jax.experimental.pallas import tpu as pltpu

F32 = jnp.float32
BF16 = jnp.bfloat16
NORM_EPS = 1e-6
NEG_INF = -1e30
ROPE_BASE = 10000.0
VMEM_LIMIT_BYTES = 56 * 1024 * 1024
LANES = 128
MXU_WIDTH = 256
TOKEN_TILE = 512
MAX_DOT_COLS = 2048

RET_HEADS = 4
RET_CHUNK = 128

NSA_GROUPS = 4
NSA_HPG = 4
NSA_DH = 64
NSA_CMP_LEN = 32
NSA_CMP_STRIDE = 16
NSA_SEL_LEN = 64
NSA_N_SEL = 16
NSA_WINDOW = 512
NSA_FORCE_BONUS = 1e4
NSA_QT = 256
NSA_KC = 512

SSD_GROUPS = 4
SSD_HPG = 8
SSD_HEADDIM = 64
SSD_STATE = 128
SSD_CONV = 4
SSD_CHUNK = 256

DIL_PATTERN = ((128, 1), (512, 4), (2048, 16))
DIL_HEADS = 8
DIL_DH = 128
DIL_QB = 128


def _params(*sem):
    return pltpu.CompilerParams(dimension_semantics=sem, vmem_limit_bytes=VMEM_LIMIT_BYTES)


def _rms(x, g):
    return x * lax.rsqrt(jnp.mean(x * x, axis=-1, keepdims=True) + NORM_EPS) * g


def _silu(x):
    hx = 0.5 * x
    return hx + hx * jnp.tanh(hx)


def _dot(a, b):
    return jnp.dot(a, b, preferred_element_type=F32)


def _dot_nt(a, b):
    return lax.dot_general(a, b, (((1,), (1,)), ((), ())), preferred_element_type=F32)


def _split3(x):
    hi = x.astype(BF16)
    r1 = x - hi.astype(F32)
    mid = r1.astype(BF16)
    lo = (r1 - mid.astype(F32)).astype(BF16)
    return hi, mid, lo


def _dot3(terms, rhs):
    return _dot(terms[0], rhs) + _dot(terms[1], rhs) + _dot(terms[2], rhs)


def _col_slices(n, max_cols=MAX_DOT_COLS):
    if n <= max_cols:
        return [slice(0, n)]
    tiles = -(-n // MXU_WIDTH)
    parts = -(-tiles * MXU_WIDTH // max_cols)
    bounds = [min(n, (tiles * p // parts) * MXU_WIDTH) for p in range(parts + 1)]
    return [slice(lo, hi) for lo, hi in zip(bounds[:-1], bounds[1:])]


def _resident(shape, index=None):
    index = index if index is not None else (0,) * len(shape)
    return pl.BlockSpec(shape, lambda *_: index, pipeline_mode=pl.Buffered(1))


def _ffn_body(h_ref, *refs, final_norm, mix):
    x = h_ref[...]
    if mix:
        y_ref, wm_ref, *refs = refs
        x = x + _dot(y_ref[...], wm_ref[...])
    g_ref, wa_ref, wb_ref, wo_ref, gf_ref, o_ref = refs
    xn = _rms(x, g_ref[...]).astype(BF16)
    acc = jnp.zeros(x.shape, F32)
    for sl in _col_slices(wa_ref.shape[1], MAX_DOT_COLS * TOKEN_TILE // x.shape[0]):
        a = _dot(xn, wa_ref[:, sl])
        b = _dot(xn, wb_ref[:, sl])
        acc = acc + _dot((_silu(a) * b).astype(BF16), wo_ref[sl, :])
    y = x + 0.5 * acc
    if final_norm:
        y = _rms(y, gf_ref[...])
    o_ref[...] = y


def _ffn(h, g, w_in, w_out, layer, g_final=None, mix=None, *, tm=None):
    m, d = h.shape
    if tm is None:
        tm = TOKEN_TILE if mix is not None else 2 * TOKEN_TILE
    f = w_out.shape[1]
    final_norm = g_final is not None
    gf = g_final if final_norm else g
    mix_args = list(mix) if mix is not None else []
    mix_specs = [pl.BlockSpec((tm, mix[0].shape[1]), lambda i: (i, 0)), _resident(mix[1].shape)] if mix_args else []
    return pl.pallas_call(
        functools.partial(_ffn_body, final_norm=final_norm, mix=bool(mix_args)),
        grid=(m // tm,),
        in_specs=[pl.BlockSpec((tm, d), lambda i: (i, 0))] + mix_specs + [
            _resident((1, d)),
            _resident((None, d, f), (layer, 0, 0)),
            _resident((None, d, f), (layer, 0, 1)),
            _resident((None, f, d), (layer, 0, 0)),
            _resident((1, d)),
        ],
        out_specs=pl.BlockSpec((tm, d), lambda i: (i, 0)),
        out_shape=jax.ShapeDtypeStruct((m, d), F32),
        compiler_params=_params("parallel"),
        name="ffn",
    )(h, *mix_args, g.reshape(1, d), w_in, w_in, w_out, gf.reshape(1, d))


def _norm_proj_body(h_ref, g_ref, *refs, n_out):
    xn = _rms(h_ref[...], g_ref[...]).astype(BF16)
    for w_ref, o_ref in zip(refs[:n_out], refs[n_out:]):
        for sl in _col_slices(w_ref.shape[1]):
            o_ref[:, sl] = _dot(xn, w_ref[:, sl]).astype(o_ref.dtype)


def _norm_proj(h, g, ws, out_dtypes, *, tm=TOKEN_TILE):
    m, d = h.shape
    return pl.pallas_call(
        functools.partial(_norm_proj_body, n_out=len(ws)),
        grid=(m // tm,),
        in_specs=[pl.BlockSpec((tm, d), lambda i: (i, 0)), pl.BlockSpec((1, d), lambda i: (0, 0))]
        + [_resident(w.shape) for w in ws],
        out_specs=[pl.BlockSpec((tm, w.shape[1]), lambda i: (i, 0)) for w in ws],
        out_shape=[jax.ShapeDtypeStruct((m, w.shape[1]), dt) for w, dt in zip(ws, out_dtypes)],
        compiler_params=_params("parallel"),
        name="norm_proj",
    )(h, g.reshape(1, d), *ws)


def _ret_body(q_ref, k_ref, v_ref, g_ref, cos_ref, sin_ref, gn_ref, o_ref, state_ref, *, ts):
    c_len = RET_CHUNK
    dk = q_ref.shape[2] // RET_HEADS
    dv = v_ref.shape[2] // RET_HEADS
    half = dk // 2

    @pl.when(pl.program_id(1) == 0)
    def _():
        state_ref[...] = jnp.zeros(state_ref.shape, F32)

    ii = lax.broadcasted_iota(jnp.int32, (c_len, c_len), 0)
    jj = lax.broadcasted_iota(jnp.int32, (c_len, c_len), 1)
    rel = (ii - jj).astype(F32)
    causal = ii >= jj
    idx = lax.broadcasted_iota(jnp.int32, (c_len, 1), 0).astype(F32)

    def rot(t, cos, sin):
        t1, t2 = t[:, :half], t[:, half:]
        return jnp.concatenate([t1 * cos - t2 * sin, t1 * sin + t2 * cos], axis=1)

    decays = []
    for h in range(RET_HEADS):
        log_gamma = math.log1p(-(2.0 ** (-5.0 - h)))
        decays.append((jnp.where(causal, jnp.exp(jnp.where(causal, rel, 0.0) * log_gamma), 0.0),
                       jnp.exp((idx + 1.0) * log_gamma),
                       jnp.exp((c_len - 1.0 - idx) * log_gamma),
                       math.exp(c_len * log_gamma)))

    def chunk(c, carry):
        r0 = pl.multiple_of(c * c_len, c_len)
        rows = pl.ds(r0, c_len)
        cos = cos_ref[rows, :]
        sin = sin_ref[rows, :]
        heads = range(RET_HEADS)
        ks = [rot(k_ref[0, rows, h * dk:(h + 1) * dk].astype(F32), cos, sin) * (dk ** -0.5) for h in heads]
        qbs = [rot(q_ref[0, rows, h * dk:(h + 1) * dk].astype(F32), cos, sin).astype(BF16) for h in heads]
        vs = [v_ref[0, rows, h * dv:(h + 1) * dv] for h in heads]
        scs = [_dot_nt(qbs[h], ks[h].astype(BF16)) * decays[h][0] for h in heads]
        sts = [state_ref[h] for h in heads]
        cross = [_dot(qbs[h], sts[h].astype(BF16)) * decays[h][1] for h in heads]
        for h in heads:
            kd_t = jnp.transpose(ks[h] * decays[h][2]).astype(BF16)
            state_ref[h] = sts[h] * decays[h][3] + _dot(kd_t, vs[h])
        outs = [_dot(scs[h].astype(BF16), vs[h]) + cross[h] for h in heads]
        for h in heads:
            on = _rms(outs[h], gn_ref[:, h * dv:(h + 1) * dv])
            gate = g_ref[0, rows, h * dv:(h + 1) * dv].astype(F32)
            o_ref[0, rows, h * dv:(h + 1) * dv] = (_silu(gate) * on).astype(o_ref.dtype)
        return carry

    lax.fori_loop(0, ts // c_len, chunk, 0)


def _retention_core(proj, cos, sin, gn_gain, *, b, s, ts=TOKEN_TILE):
    n = proj.shape[2]
    hk = n // 6
    hv = 2 * hk
    dk = hk // RET_HEADS
    return pl.pallas_call(
        functools.partial(_ret_body, ts=ts),
        grid=(b, s // ts),
        in_specs=[
            pl.BlockSpec((1, ts, hk), lambda i, j: (i, j, 0)),
            pl.BlockSpec((1, ts, hk), lambda i, j: (i, j, 1)),
            pl.BlockSpec((1, ts, hv), lambda i, j: (i, j, 1)),
            pl.BlockSpec((1, ts, hv), lambda i, j: (i, j, 2)),
            pl.BlockSpec((ts, dk // 2), lambda i, j: (j, 0)),
            pl.BlockSpec((ts, dk // 2), lambda i, j: (j, 0)),
            pl.BlockSpec((1, hv), lambda i, j: (0, 0)),
        ],
        out_specs=pl.BlockSpec((1, ts, hv), lambda i, j: (i, j, 0)),
        out_shape=jax.ShapeDtypeStruct((b, s, hv), BF16),
        scratch_shapes=[pltpu.VMEM((RET_HEADS, dk, hv // RET_HEADS), F32)],
        compiler_params=_params("parallel", "arbitrary"),
        name="retention",
    )(proj, proj, proj, proj, cos, sin, gn_gain.reshape(1, hv))


def _rope_tables(s, half):
    inv = ROPE_BASE ** (-jnp.arange(half, dtype=F32) / half)
    ang = jnp.arange(s, dtype=F32)[:, None] * inv[None, :]
    return jnp.cos(ang), jnp.sin(ang)


def _retention(h2, g_norm, w_in, gn_gain, w_out, *, b, s):
    (proj,) = _norm_proj(h2, g_norm, [w_in.astype(BF16)], [BF16])
    n = proj.shape[1]
    cos, sin = _rope_tables(s, n // 6 // RET_HEADS // 2)
    y = _retention_core(proj.reshape(b, s, n), cos, sin, gn_gain, b=b, s=s)
    return y.reshape(b * s, -1), w_out.astype(BF16)


def _dil_proj_body(h_ref, g_ref, w_ref, o_ref, xn_ref, *, r):
    tm = h_ref.shape[0]
    n = tm // r
    xn = _rms(h_ref[...], g_ref[...])
    slabs = xn_ref.shape[0]
    for k in range(slabs):
        xn_ref[k] = xn[:, k * LANES:(k + 1) * LANES]
    xp = jnp.concatenate(
        [jnp.concatenate([xn_ref[k, pl.ds(c, n, stride=r), :] for c in range(r)], axis=0) for k in range(slabs)],
        axis=1).astype(BF16)
    for sl in _col_slices(w_ref.shape[1]):
        res = _dot(xp, w_ref[:, sl]).astype(o_ref.dtype)
        for c in range(r):
            o_ref[0, c, :, sl] = res[c * n:(c + 1) * n, :]


def _dil_proj(h2, g_norm, w, r, *, b, s, tm=TOKEN_TILE):
    d = h2.shape[1]
    n_out = w.shape[1]
    tiles = s // tm
    return pl.pallas_call(
        functools.partial(_dil_proj_body, r=r),
        grid=(b * tiles,),
        in_specs=[
            pl.BlockSpec((tm, d), lambda i: (i, 0)),
            pl.BlockSpec((1, d), lambda i: (0, 0)),
            _resident((d, n_out)),
        ],
        out_specs=pl.BlockSpec((1, r, tm // r, n_out), lambda i: (i // tiles, 0, i % tiles, 0)),
        out_shape=jax.ShapeDtypeStruct((b, r, s // r, n_out), BF16),
        scratch_shapes=[pltpu.VMEM((d // LANES, tm, LANES), F32)],
        compiler_params=_params("parallel"),
        name=f"dilated_proj_r{r}",
    )(h2, g_norm.reshape(1, d), w)


def _dil_attend(get_q, kbuf, vbuf, put_o, put_lse, *, rows, n_back):
    qb = DIL_QB
    dh = DIL_DH
    first_step = pl.program_id(2) == 0
    qi = lax.broadcasted_iota(jnp.int32, (qb, 2 * qb), 0)
    kj = lax.broadcasted_iota(jnp.int32, (qb, 2 * qb), 1)
    dist = qi + qb - kj
    band = (dist >= 0) & (dist <= n_back)
    lane = lax.broadcasted_iota(jnp.int32, (qb, LANES), 1)
    scale = dh ** -0.5
    ones = jnp.ones((2 * qb, dh), BF16)
    for i in range(rows // qb):
        mask = band & ((kj >= qb) | jnp.logical_not(first_step)) if i == 0 else band
        lse_tile = jnp.zeros((qb, LANES), F32)
        for h in range(DIL_HEADS):
            cols = slice(h * dh, (h + 1) * dh)
            q = get_q(slice(i * qb, (i + 1) * qb), cols)
            k = kbuf[i * qb:(i + 2) * qb, cols]
            v = vbuf[i * qb:(i + 2) * qb, cols]
            s = jnp.where(mask, _dot_nt(q, k), NEG_INF)
            m = jnp.max(s, axis=1, keepdims=True)
            e = jnp.exp2((s - m) * (scale * math.log2(math.e))).astype(BF16)
            acc = _dot(e, jnp.concatenate([v, ones], axis=1))
            den = jnp.maximum(acc[:, dh:], 1e-30)
            put_o(slice(i * qb, (i + 1) * qb), h, acc[:, :dh] / den)
            lse_tile = jnp.where(lane == h, m * scale + jnp.log(den), lse_tile)
        put_lse(slice(i * qb, (i + 1) * qb), lse_tile)


def _dil_body(q_ref, kc_ref, kp_ref, vc_ref, vp_ref, o_ref, lse_ref, kbuf, vbuf, *, rows, n_back):
    dh = DIL_DH

    def put_o(rs, h, o):
        o_ref[0, 0, rs, h * dh:(h + 1) * dh] = o.astype(o_ref.dtype)

    def put_lse(rs, lse_tile):
        lse_ref[0, 0, rs, :] = lse_tile

    qb = DIL_QB
    kbuf[0:qb, :] = kp_ref[0, 0]
    kbuf[qb:, :] = kc_ref[0, 0]
    vbuf[0:qb, :] = vp_ref[0, 0]
    vbuf[qb:, :] = vc_ref[0, 0]
    _dil_attend(lambda rs, cols: q_ref[0, 0, rs, cols], kbuf, vbuf, put_o, put_lse, rows=rows, n_back=n_back)


def _dilated_group(proj, win, r, *, b, s):
    hd = DIL_HEADS * DIL_DH
    length = s // r
    rows = min(length, TOKEN_TILE)
    qb = DIL_QB

    def cur(which):
        return pl.BlockSpec((1, 1, rows, hd), lambda i, c, n: (i, c, n, which))

    def prev(which):
        return pl.BlockSpec((1, 1, qb, hd), lambda i, c, n: (i, c, jnp.maximum(n * (rows // qb) - 1, 0), which))

    return pl.pallas_call(
        functools.partial(_dil_body, rows=rows, n_back=win // r),
        grid=(b, r, length // rows),
        in_specs=[cur(0), cur(1), prev(1), cur(2), prev(2)],
        out_specs=[
            pl.BlockSpec((1, 1, rows, hd), lambda i, c, n: (i, c, n, 0)),
            pl.BlockSpec((1, 1, rows, LANES), lambda i, c, n: (i, c, n, 0)),
        ],
        out_shape=[
            jax.ShapeDtypeStruct((b, r, length, hd), BF16),
            jax.ShapeDtypeStruct((b, r, length, LANES), F32),
        ],
        scratch_shapes=[pltpu.VMEM((rows + qb, hd), BF16), pltpu.VMEM((rows + qb, hd), BF16)],
        compiler_params=_params("parallel", "parallel", "arbitrary"),
        name=f"dilated_r{r}",
    )(proj, proj, proj, proj, proj)


def _dil_merge_body(h_ref, g_ref, wg_ref, *refs, rows, n_back, dilations):
    ng = len(dilations)
    o_refs, l_refs = refs[:ng - 1], refs[ng - 1:2 * ng - 2]
    w_ref, out_ref, qbuf, kbuf, vbuf = refs[2 * ng - 2:2 * ng + 3]
    o_bufs, l_bufs = refs[2 * ng + 3:3 * ng + 3], refs[3 * ng + 3:]
    tm = h_ref.shape[1]
    qb = DIL_QB
    hd = qbuf.shape[1]

    @pl.when(pl.program_id(2) == 0)
    def _():
        kbuf[0:qb, :] = jnp.zeros((qb, hd), BF16)
        vbuf[0:qb, :] = jnp.zeros((qb, hd), BF16)

    @pl.when(pl.program_id(2) > 0)
    def _():
        kbuf[0:qb, :] = kbuf[rows:rows + qb, :]
        vbuf[0:qb, :] = vbuf[rows:rows + qb, :]

    xn = _rms(h_ref[0], g_ref[...]).astype(BF16)
    for part, (dst, row0) in enumerate(((qbuf, 0), (kbuf, qb), (vbuf, qb))):
        dst[row0:row0 + rows, :] = _dot(xn, wg_ref[:, part * hd:(part + 1) * hd]).astype(BF16)

    def put_o(rs, h, o):
        o_bufs[0][h, rs, :] = o

    def put_lse(rs, lse_tile):
        l_bufs[0][rs, :] = lse_tile

    _dil_attend(lambda rs, cols: qbuf[rs, cols], kbuf, vbuf, put_o, put_lse, rows=rows, n_back=n_back)
    for o_ref, l_ref, o_buf, l_buf, r in zip(o_refs, l_refs, o_bufs[1:], l_bufs[1:], dilations[1:]):
        n = tm // r
        for c in range(r):
            rs = pl.ds(c, n, stride=r)
            l_buf[rs, :] = l_ref[0, c]
            blk = o_ref[0, c].astype(F32)
            for h in range(DIL_HEADS):
                o_buf[h, rs, :] = blk[:, h * DIL_DH:(h + 1) * DIL_DH]
    lses = [l[...] for l in l_bufs]
    mx = functools.reduce(jnp.maximum, lses)
    es = [jnp.exp(l - mx) for l in lses]
    tot = functools.reduce(jnp.add, es)
    wts = [e / tot for e in es]
    dh = DIL_DH
    parts = []
    for h in range(DIL_HEADS):
        acc = jnp.zeros((tm, dh), F32)
        for g in range(ng):
            wg = jnp.broadcast_to(wts[g][:, h:h + 1], (tm, dh))
            acc = acc + wg * o_bufs[g][h]
        parts.append(acc)
    o = jnp.concatenate(parts, axis=1).astype(BF16)
    out_ref[0] = h_ref[0] + _dot(o, w_ref[...])


def _dil_merge(h3, g_norm, w_g, win, outs, lses, w, dilations, *, tm=TOKEN_TILE):
    b, s, d = h3.shape
    hd = w.shape[0]
    qb = DIL_QB
    assert dilations[0] == 1
    res_major = lambda r, width: pl.BlockSpec((1, r, tm // r, width), lambda i, c, n: (i, 0, n, 0))
    return pl.pallas_call(
        functools.partial(_dil_merge_body, rows=tm, n_back=win, dilations=tuple(dilations)),
        grid=(b, 1, s // tm),
        in_specs=[pl.BlockSpec((1, tm, d), lambda i, c, n: (i, n, 0)), _resident((1, d)), _resident(w_g.shape)]
        + [res_major(r, hd) for r in dilations[1:]] + [res_major(r, LANES) for r in dilations[1:]]
        + [_resident((hd, d))],
        out_specs=pl.BlockSpec((1, tm, d), lambda i, c, n: (i, n, 0)),
        out_shape=jax.ShapeDtypeStruct((b, s, d), F32),
        scratch_shapes=[pltpu.VMEM((tm, hd), BF16), pltpu.VMEM((tm + qb, hd), BF16), pltpu.VMEM((tm + qb, hd), BF16)]
        + [pltpu.VMEM((DIL_HEADS, tm, DIL_DH), F32) for _ in dilations]
        + [pltpu.VMEM((tm, LANES), F32) for _ in dilations],
        compiler_params=_params("parallel", "parallel", "arbitrary"),
        name="dilated_r1_merge",
    )(h3, g_norm.reshape(1, d), w_g, *outs, *lses, w)


def _dilated(h2, g_norm, w_in, w_out, *, b, s):
    d = h2.shape[1]
    gw = 3 * DIL_HEADS * DIL_DH
    wb = w_in.astype(BF16)
    outs, lses, dilations = [], [], []
    order = sorted(range(len(DIL_PATTERN)), key=lambda g: DIL_PATTERN[g][1] == 1)
    for g in order:
        win, r = DIL_PATTERN[g]
        w_g = wb[:, g * gw:(g + 1) * gw]
        if r == 1:
            break
        proj = _dil_proj(h2, g_norm, w_g, r, b=b, s=s, tm=2 * TOKEN_TILE)
        o, lse = _dilated_group(proj, win, r, b=b, s=s)
        outs.append(o)
        lses.append(lse)
        dilations.append(r)
    return _dil_merge(h2.reshape(b, s, d), g_norm, w_g, win, outs, lses, w_out.astype(BF16),
                      [1] + dilations).reshape(b * s, d)


def _softplus(x):
    return jnp.maximum(x, 0.0) + jnp.log1p(jnp.exp(-jnp.abs(x)))


def _ssd_body(z_ref, x_ref, dt_ref, cw_ref, cb_ref, dtb_ref, alog_ref, dsk_ref, ng_ref, o_ref,
              xpad, state_ref):
    ln = SSD_CHUNK
    nst = SSD_STATE
    gw = SSD_HPG * SSD_HEADDIM
    d_inner = SSD_GROUPS * gw
    halo = 8

    @pl.when(pl.program_id(1) == 0)
    def _():
        xpad[0:halo, :] = jnp.zeros((halo, xpad.shape[1]), F32)
        state_ref[...] = jnp.zeros(state_ref.shape, F32)

    x_b = x_ref[0]
    x_f = x_b.astype(F32)
    ii = lax.broadcasted_iota(jnp.int32, (ln, ln), 0)
    jj = lax.broadcasted_iota(jnp.int32, (ln, ln), 1)
    conv = cb_ref[...] + x_f * cw_ref[SSD_CONV - 1:SSD_CONV, :]
    for shift in range(1, SSD_CONV):
        shifted = _dot(jnp.where(ii - jj == shift, 1.0, 0.0).astype(BF16), x_b)
        conv = conv + shifted * cw_ref[SSD_CONV - 1 - shift:SSD_CONV - shift, :]
    xpad[halo:2 * halo, :] = x_f[0:halo, :]
    head = cb_ref[...]
    for k in range(SSD_CONV):
        off = halo - (SSD_CONV - 1) + k
        head = head + xpad[off:off + halo, :] * cw_ref[k:k + 1, :]
    xpad[0:halo, :] = x_f[ln - halo:ln, :]
    xbc = _silu(jnp.concatenate([head, conv[halo:, :]], axis=0))
    xs = xbc[:, :d_inner]
    bm = xbc[:, d_inner:d_inner + SSD_GROUPS * nst]
    cm = xbc[:, d_inner + SSD_GROUPS * nst:]

    dt = _softplus(dt_ref[0] + dtb_ref[...])
    da = dt * (-jnp.exp(alog_ref[...]))
    ii = lax.broadcasted_iota(jnp.int32, (ln, ln), 0)
    jj = lax.broadcasted_iota(jnp.int32, (ln, ln), 1)
    causal = ii >= jj
    tril = jnp.where(causal, 1.0, 0.0).astype(BF16)
    da_terms = _split3(da)
    acs = _dot(tril, da_terms[0]) + _dot(tril, da_terms[1]) + _dot(tril, da_terms[2])
    acs2 = acs * math.log2(math.e)
    acs2_t = jnp.transpose(acs2)
    hl = ln // 2

    erow = lax.broadcasted_iota(jnp.int32, (LANES, d_inner), 0)
    ecol = lax.broadcasted_iota(jnp.int32, (LANES, d_inner), 1)
    expand = jnp.where(ecol // SSD_HEADDIM == erow, 1.0, 0.0).astype(BF16)
    acs_e = _dot3(_split3(acs), expand)
    dt_e = _dot3(_split3(dt), expand)
    last = acs_e[ln - 1:ln, :]
    decay_in = jnp.exp(acs_e)
    xs_dt = xs * dt_e
    xs_end = (xs_dt * jnp.exp(last - acs_e)).astype(BF16)
    xs_b = xs_dt.astype(BF16)
    lane = lax.broadcasted_iota(jnp.int32, (ln, LANES), 1)

    y_groups = []
    for g in range(SSD_GROUPS):
        bm_g = bm[:, g * nst:(g + 1) * nst]
        cm_g = cm[:, g * nst:(g + 1) * nst].astype(BF16)
        cb = jnp.where(causal, _dot_nt(cm_g, bm_g.astype(BF16)), 0.0)
        st = state_ref[g]
        gcols = slice(g * gw, (g + 1) * gw)
        y_state = _dot(cm_g, st.astype(BF16)) * decay_in[:, gcols]
        pairs = []
        for p in range(SSD_HPG // 2):
            pair_cols = slice(g * gw + p * LANES, g * gw + (p + 1) * LANES)
            halves = []
            for e in range(2):
                hd = g * SSD_HPG + 2 * p + e
                top = cb[:hl, :hl] * jnp.exp2(jnp.minimum(acs2[:hl, hd:hd + 1] - acs2_t[hd:hd + 1, :hl], 0.0))
                bot = cb[hl:, :] * jnp.exp2(jnp.minimum(acs2[hl:, hd:hd + 1] - acs2_t[hd:hd + 1, :], 0.0))
                halves.append(jnp.concatenate([_dot(top.astype(BF16), xs_b[:hl, pair_cols]),
                                               _dot(bot.astype(BF16), xs_b[:, pair_cols])], axis=0))
            pairs.append(jnp.where(lane < SSD_HEADDIM, halves[0], halves[1]))
        y_groups.append(jnp.concatenate(pairs, axis=1) + y_state)
        bm_t = jnp.transpose(bm_g).astype(BF16)
        state_ref[g] = st * jnp.exp(last[:, gcols]) + _dot(bm_t, xs_end[:, gcols])

    y = jnp.concatenate(y_groups, axis=1) + dsk_ref[...] * xs
    yz = y * _silu(z_ref[0].astype(F32))
    outs = [_rms(yz[:, g * gw:(g + 1) * gw], ng_ref[:, g * gw:(g + 1) * gw]) for g in range(SSD_GROUPS)]
    o_ref[0] = jnp.concatenate(outs, axis=1).astype(o_ref.dtype)


def _ssd_core(z, xbc, dt, conv_w, conv_b, dt_bias, a_log, d_skip, norm_g, *, b, s):
    ln = SSD_CHUNK
    d_inner = z.shape[2]
    conv_dim = xbc.shape[2]
    gw = SSD_HPG * SSD_HEADDIM
    full = lambda shape: pl.BlockSpec(shape, lambda i, j: (0,) * len(shape))
    return pl.pallas_call(
        _ssd_body,
        grid=(b, s // ln),
        in_specs=[
            pl.BlockSpec((1, ln, d_inner), lambda i, j: (i, j, 0)),
            pl.BlockSpec((1, ln, conv_dim), lambda i, j: (i, j, 0)),
            pl.BlockSpec((1, ln, LANES), lambda i, j: (i, j, 0)),
            full((SSD_CONV, conv_dim)),
            full((1, conv_dim)),
            full((1, LANES)),
            full((1, LANES)),
            full((1, d_inner)),
            full((1, d_inner)),
        ],
        out_specs=pl.BlockSpec((1, ln, d_inner), lambda i, j: (i, j, 0)),
        out_shape=jax.ShapeDtypeStruct((b, s, d_inner), BF16),
        scratch_shapes=[pltpu.VMEM((16, conv_dim), F32),
                        pltpu.VMEM((SSD_GROUPS, SSD_STATE, gw), F32)],
        compiler_params=_params("parallel", "arbitrary"),
        name="ssd",
    )(z, xbc, dt, conv_w, conv_b, dt_bias, a_log, d_skip, norm_g)


def _ssd(h2, g_norm, w_in, conv_w, conv_b, dt_bias, a_log, d_skip, norm_g, w_out, *, b, s):
    heads = SSD_GROUPS * SSD_HPG
    d_inner = heads * SSD_HEADDIM
    conv_dim = conv_w.shape[1]
    wb = w_in.astype(BF16)
    w_dt = jnp.pad(wb[:, d_inner + conv_dim:], ((0, 0), (0, LANES - heads)))
    z, xbc, dt = _norm_proj(h2, g_norm, [wb[:, :d_inner], wb[:, d_inner:d_inner + conv_dim], w_dt], [BF16, BF16, F32])
    pad_heads = lambda v: jnp.pad(v.astype(F32), (0, LANES - heads)).reshape(1, LANES)
    y = _ssd_core(z.reshape(b, s, -1), xbc.reshape(b, s, -1), dt.reshape(b, s, LANES),
                  conv_w.astype(F32), conv_b.reshape(1, -1).astype(F32), pad_heads(dt_bias), pad_heads(a_log),
                  jnp.repeat(d_skip.astype(F32), SSD_HEADDIM).reshape(1, d_inner), norm_g.reshape(1, d_inner),
                  b=b, s=s)
    return y.reshape(b * s, d_inner), w_out.astype(BF16)


def _nsa_proj_body(h_ref, g_ref, wn_ref, wt_ref, k0_ref, v0_ref, k12_ref, tr_ref, stage_ref):
    xn = _rms(h_ref[...], g_ref[...]).astype(BF16)
    nat = _dot(xn, wn_ref[...])
    tm = nat.shape[0]
    gd = k12_ref.shape[1] // 2
    k12_ref[...] = nat[:, 2 * gd:].astype(k12_ref.dtype)
    slabs = gd // LANES
    rows = tm // NSA_CMP_STRIDE
    for sl in range(2 * slabs):
        stage_ref[sl] = nat[:, sl * LANES:(sl + 1) * LANES]
    for l in range(NSA_CMP_STRIDE):
        for sl in range(slabs):
            cols = slice(l * gd + sl * LANES, l * gd + (sl + 1) * LANES)
            k0_ref[:, cols] = stage_ref[sl, pl.ds(l, rows, stride=NSA_CMP_STRIDE), :].astype(k0_ref.dtype)
            v0_ref[:, cols] = stage_ref[slabs + sl, pl.ds(l, rows, stride=NSA_CMP_STRIDE), :].astype(v0_ref.dtype)
    res = _dot_nt(wt_ref[...], xn)
    for j in range(tr_ref.shape[0]):
        tr_ref[j] = res[:, j * NSA_QT:(j + 1) * NSA_QT].astype(tr_ref.dtype)


def _nsa_proj(h2, g_norm, w_nat, w_t, *, tm=TOKEN_TILE):
    m, d = h2.shape
    gd = NSA_GROUPS * NSA_DH
    nt = w_t.shape[0]
    slabs = tm // NSA_QT
    return pl.pallas_call(
        _nsa_proj_body,
        grid=(m // tm,),
        in_specs=[
            pl.BlockSpec((tm, d), lambda i: (i, 0)),
            pl.BlockSpec((1, d), lambda i: (0, 0)),
            _resident(w_nat.shape),
            _resident(w_t.shape),
        ],
        out_specs=[
            pl.BlockSpec((tm // NSA_CMP_STRIDE, NSA_CMP_STRIDE * gd), lambda i: (i, 0)),
            pl.BlockSpec((tm // NSA_CMP_STRIDE, NSA_CMP_STRIDE * gd), lambda i: (i, 0)),
            pl.BlockSpec((tm, 2 * gd), lambda i: (i, 0)),
            pl.BlockSpec((slabs, nt, NSA_QT), lambda i: (i, 0, 0)),
        ],
        out_shape=[
            jax.ShapeDtypeStruct((m // NSA_CMP_STRIDE, NSA_CMP_STRIDE * gd), BF16),
            jax.ShapeDtypeStruct((m // NSA_CMP_STRIDE, NSA_CMP_STRIDE * gd), BF16),
            jax.ShapeDtypeStruct((m, 2 * gd), BF16),
            jax.ShapeDtypeStruct((m // NSA_QT, nt, NSA_QT), BF16),
        ],
        scratch_shapes=[pltpu.VMEM((2 * gd // LANES, tm, LANES), F32)],
        compiler_params=_params("parallel"),
        name="nsa_proj",
    )(h2, g_norm.reshape(1, d), w_nat, w_t)


def _nsa_cmp_body(x_ref, pa_ref, pb_ref, w1_ref, w2_ref, o_ref, w1a_big, w1b_big, w2_big, *, transposed):
    g, dh = NSA_GROUPS, NSA_DH
    hid = w1_ref.shape[1]
    half = NSA_CMP_LEN // 2

    @pl.when(pl.program_id(0) == 0)
    def _():
        w1a_big[...] = jnp.zeros(w1a_big.shape, BF16)
        w1b_big[...] = jnp.zeros(w1b_big.shape, BF16)
        w2_big[...] = jnp.zeros(w2_big.shape, BF16)
        for gi in range(g):
            cols = slice(gi * hid, (gi + 1) * hid)
            for l in range(half):
                rows = slice((l * g + gi) * dh, (l * g + gi + 1) * dh)
                w1a_big[rows, cols] = w1_ref[l * dh:(l + 1) * dh, :]
                w1b_big[rows, cols] = w1_ref[(half + l) * dh:(half + l + 1) * dh, :]
            w2_big[cols, gi * dh:(gi + 1) * dh] = w2_ref[...]

    x = x_ref[0].astype(F32)
    nb = x.shape[0]
    ya = _dot((x + pa_ref[...]).astype(BF16), w1a_big[...])
    yb = _dot((x + pb_ref[...]).astype(BF16), w1b_big[...])
    hidden = _silu(ya + pltpu.roll(yb, nb - 1, 0))
    out = _dot(hidden.astype(BF16), w2_big[...])
    o_ref[0] = (jnp.transpose(out) if transposed else out).astype(o_ref.dtype)


def _nsa_compress(x, pos, w1, w2, *, transposed):
    b, nb, width = x.shape
    g, dh = NSA_GROUPS, NSA_DH
    half = NSA_CMP_LEN // 2
    hid = w1.shape[1]
    posb = jnp.broadcast_to(pos[:, None, :], (NSA_CMP_LEN, g, dh)).reshape(NSA_CMP_LEN, g * dh)
    pa = posb[:half].reshape(1, width).astype(F32)
    pb = posb[half:].reshape(1, width).astype(F32)
    out_dims = (g * dh, nb) if transposed else (nb, g * dh)
    return pl.pallas_call(
        functools.partial(_nsa_cmp_body, transposed=transposed),
        grid=(b,),
        in_specs=[
            pl.BlockSpec((1, nb, width), lambda i: (i, 0, 0)),
            _resident((1, width)), _resident((1, width)),
            _resident(w1.shape), _resident(w2.shape),
        ],
        out_specs=pl.BlockSpec((1,) + out_dims, lambda i: (i, 0, 0)),
        out_shape=jax.ShapeDtypeStruct((b,) + out_dims, BF16),
        scratch_shapes=[pltpu.VMEM((width, g * hid), BF16), pltpu.VMEM((width, g * hid), BF16),
                        pltpu.VMEM((g * hid, g * dh), BF16)],
        compiler_params=_params("arbitrary"),
        name="nsa_compress",
    )(x, pa, pb, w1.astype(BF16), w2.astype(BF16))


def _nsa_pair_body(q_ref, gt_ref, kc_ref, vct_ref, k1_ref, k2_ref, v1_ref, v2_ref, o_ref, bias_ref, wbias_ref, *, s):
    qt, kc, dh = NSA_QT, NSA_KC, NSA_DH
    gl = 2
    glanes = NSA_HPG * qt
    lanes = gl * glanes
    vrows = gl * dh
    j = pl.program_id(2)
    t0 = j * qt
    nb = kc_ref.shape[1]
    nsb = s // NSA_SEL_LEN
    n_sel = min(NSA_N_SEL, nsb)

    q_t = q_ref[0, 0]
    qcat = [jnp.concatenate([q_t[(g * NSA_HPG + h) * dh:(g * NSA_HPG + h + 1) * dh, :] for h in range(NSA_HPG)],
                            axis=1) for g in range(gl)]
    zero = jnp.zeros_like(qcat[0])
    qp = jnp.concatenate([jnp.concatenate([qcat[0], zero], axis=1),
                          jnp.concatenate([zero, qcat[1]], axis=1)], axis=0)
    qlane = lax.broadcasted_iota(jnp.int32, (1, lanes), 1) % qt
    tq = t0 + qlane

    def values(v_ref, slab0, n_slabs):
        v = jnp.concatenate([v_ref[0, slab0 + i] for i in range(n_slabs)], axis=1)
        return jnp.concatenate([v, jnp.ones((16, n_slabs * qt), BF16)], axis=0)

    def weighted_values(v_aug, p):
        return jnp.concatenate(
            [_dot(jnp.concatenate([v_aug[g * dh:(g + 1) * dh, :], v_aug[vrows:, :]], axis=0),
                  p[:, g * glanes:(g + 1) * glanes]) for g in range(gl)], axis=1)

    def normalise(acc):
        o = acc[:dh, :] * (1.0 / jnp.maximum(acc[dh:dh + 1, :], 1e-30))
        return [o[:, g * glanes:(g + 1) * glanes] for g in range(gl)]

    wslabs = NSA_WINDOW // qt + 1
    row_w = lax.broadcasted_iota(jnp.int32, (qt, lanes), 0)

    @pl.when(j == 0)
    def _():
        wbias_ref[0] = jnp.where(row_w > qlane, 0.0, NEG_INF)
        wbias_ref[1] = jnp.zeros((qt, lanes), F32)
        wbias_ref[2] = jnp.where(row_w <= qlane, 0.0, NEG_INF)
        wbias_ref[3] = jnp.full((qt, lanes), NEG_INF, F32)

    jb0 = jnp.maximum(j + 1 - wslabs, 0)
    kw = k2_ref[0, pl.ds(pl.multiple_of(jb0 * qt, qt), wslabs * qt), :]
    sw_raw = _dot(kw, qp)
    slabs_back = [j - jb0 - i for i in range(wslabs)]
    sw = jnp.concatenate(
        [sw_raw[i * qt:(i + 1) * qt, :]
         + wbias_ref[jnp.where(back < 0, 3, jnp.where(back == wslabs - 1, 0, jnp.where(back == 0, 2, 1)))]
         for i, back in enumerate(slabs_back)], axis=0)

    sc = _dot(kc_ref[0], qp)
    assert NSA_CMP_STRIDE == 16
    last_visible = jnp.minimum(jnp.right_shift(tq - (NSA_CMP_LEN - 1), 4), nb - 2)
    sc = jnp.where(lax.broadcasted_iota(jnp.int32, (nb, lanes), 0) <= last_visible, sc, NEG_INF)
    m_c = jnp.max(sc, axis=0, keepdims=True)
    e = jnp.exp2(sc - jnp.where(m_c > 0.5 * NEG_INF, m_c, 0.0))
    p_c = e * (1.0 / jnp.maximum(jnp.sum(e, axis=0, keepdims=True), 1e-30))
    acc_c = _dot(vct_ref[0], p_c.astype(BF16))
    o_c = [acc_c[g * dh:(g + 1) * dh, g * glanes:(g + 1) * glanes] for g in range(gl)]
    imps = []
    for g in range(gl):
        imp = p_c[:, g * glanes:g * glanes + qt]
        for h in range(1, NSA_HPG):
            imp = imp + p_c[:, g * glanes + h * qt:g * glanes + (h + 1) * qt]
        imps.append(imp)
    imp = jnp.concatenate(imps, axis=1)

    ratio = NSA_SEL_LEN // NSA_CMP_STRIDE
    dd = lax.broadcasted_iota(jnp.int32, (nsb, nb), 1) - ratio * lax.broadcasted_iota(jnp.int32, (nsb, nb), 0)
    wsel = jnp.where((dd == -1) | (dd == ratio - 1), 1.0,
                     jnp.where((dd >= 0) & (dd < ratio - 1), 2.0, 0.0)).astype(BF16)
    terms = _split3(imp)
    imp_sel = _dot(wsel, terms[0]) + _dot(wsel, terms[1]) + _dot(wsel, terms[2])
    jq = lax.broadcasted_iota(jnp.int32, (nsb, gl * qt), 0)
    cur = (t0 + lax.broadcasted_iota(jnp.int32, (nsb, gl * qt), 1) % qt) // NSA_SEL_LEN
    forced = (jq == 0) | (jq == cur) | (jq == cur - 1)
    score = jnp.where(jq <= cur, imp_sel + jnp.where(forced, NSA_FORCE_BONUS, 0.0), NEG_INF)

    pw = jnp.exp2(sw - jnp.max(sw, axis=0, keepdims=True)).astype(BF16)
    o_w = normalise(weighted_values(values(v2_ref, jb0, wslabs), pw))

    jqf = jq.astype(F32)
    work = score
    picked = None
    for _ in range(n_sel):
        top = jnp.max(work, axis=0, keepdims=True)
        first = jnp.min(jnp.where(work == top, jqf, float(nsb)), axis=0, keepdims=True)
        pick = jqf == first
        picked = pick if picked is None else picked | pick
        work = jnp.where(pick, 3.0 * NEG_INF, work)
    sel_bias = jnp.where(picked & (score > 0.5 * NEG_INF), 0.0, NEG_INF)
    bias_ref[...] = jnp.concatenate([sel_bias[:, g * qt:(g + 1) * qt] for g in range(gl) for _ in range(NSA_HPG)],
                                    axis=1)

    blocks_per_chunk = kc // NSA_SEL_LEN
    slabs_per_chunk = kc // qt

    def online(carry, st, bias, v_aug):
        m, acc = carry
        blks = [st[r * NSA_SEL_LEN:(r + 1) * NSA_SEL_LEN, :] for r in range(blocks_per_chunk)]
        part = None
        for r in range(blocks_per_chunk):
            pr = blks[r][0:8, :]
            for i in range(1, NSA_SEL_LEN // 8):
                pr = jnp.maximum(pr, blks[r][8 * i:8 * (i + 1), :])
            pr = pr + bias[r:r + 1, :]
            part = pr if part is None else jnp.maximum(part, pr)
        m_new = jnp.maximum(m, jnp.max(part, axis=0, keepdims=True))
        p = jnp.concatenate([jnp.exp2(blks[r] - (m_new - bias[r:r + 1, :])) for r in range(blocks_per_chunk)],
                            axis=0).astype(BF16)
        return m_new, jnp.exp2(m - m_new) * acc + weighted_values(v_aug, p)

    def sel_scores(c):
        return _dot(k1_ref[0, pl.ds(pl.multiple_of(c * kc, kc), kc), :], qp)

    def sel_bias(c):
        return bias_ref[pl.ds(pl.multiple_of(c * blocks_per_chunk, blocks_per_chunk), blocks_per_chunk), :]

    def sel_values(c):
        return values(v1_ref, c * slabs_per_chunk, slabs_per_chunk)

    init = (jnp.full((1, lanes), NEG_INF, F32), jnp.zeros((dh + 16, lanes), F32))
    c_hi = t0 // kc
    own_slab = (t0 - c_hi * kc) // qt
    st_raw = sel_scores(c_hi)
    st = jnp.concatenate(
        [st_raw[i * qt:(i + 1) * qt, :] + wbias_ref[jnp.where(i < own_slab, 1, jnp.where(i == own_slab, 2, 3))]
         for i in range(slabs_per_chunk)], axis=0)
    chain_a = online(init, st, sel_bias(c_hi), sel_values(c_hi))
    chain_a = lax.cond(c_hi % 2 == 1,
                       lambda cr: online(cr, sel_scores(c_hi - 1), sel_bias(c_hi - 1), sel_values(c_hi - 1)),
                       lambda cr: cr, chain_a)
    chain_b = (chain_a[0], jnp.zeros_like(chain_a[1]))

    def pair(i, chains):
        ca, cb = chains
        st_a = sel_scores(2 * i)
        st_b = sel_scores(2 * i + 1)
        return (online(ca, st_a, sel_bias(2 * i), sel_values(2 * i)),
                online(cb, st_b, sel_bias(2 * i + 1), sel_values(2 * i + 1)))

    (m_a, acc_a), (m_b, acc_b) = lax.fori_loop(0, c_hi // 2, pair, (chain_a, chain_b))
    m_s = jnp.maximum(m_a, m_b)
    o_s = normalise(jnp.exp2(m_a - m_s) * acc_a + jnp.exp2(m_b - m_s) * acc_b)

    gates = jax.nn.sigmoid(gt_ref[0, 0].astype(F32))
    outs = []
    for g in range(gl):
        def gate_row(br):
            r0 = (g * 4 + br) * NSA_HPG
            return jnp.concatenate([gates[r0 + h:r0 + h + 1, :] for h in range(NSA_HPG)], axis=1)
        o = gate_row(0) * o_c[g] + gate_row(1) * o_s[g] + gate_row(2) * o_w[g]
        outs.extend(o[:, h * qt:(h + 1) * qt] for h in range(NSA_HPG))
    o_ref[0] = jnp.transpose(jnp.concatenate(outs, axis=0)).astype(o_ref.dtype)


def _nsa_attend_pairs(tr, k_cmp, v_cmp_t, k12, *, b, s):
    qt, dh = NSA_QT, NSA_DH
    gl = 2
    gd = NSA_GROUPS * dh
    hq = NSA_GROUPS * NSA_HPG * dh
    pairs = NSA_GROUPS // gl
    nslab = s // qt
    assert nslab > NSA_WINDOW // qt
    nb = k_cmp.shape[1]
    tr4 = tr.reshape(b, nslab, tr.shape[1], qt)
    q_rows = gl * NSA_HPG * dh
    v_rows = gl * dh
    gate_rows = gl * 4 * NSA_HPG
    v1_blk = hq // v_rows
    v2_blk = (hq + gd) // v_rows
    gate_blk = (hq + 2 * gd) // gate_rows
    nsb = s // NSA_SEL_LEN
    return pl.pallas_call(
        functools.partial(_nsa_pair_body, s=s),
        grid=(b, pairs, nslab),
        in_specs=[
            pl.BlockSpec((1, 1, q_rows, qt), lambda i, p, j: (i, j, p, 0)),
            pl.BlockSpec((1, 1, gate_rows, qt), lambda i, p, j: (i, j, gate_blk + p, 0)),
            pl.BlockSpec((1, nb, v_rows), lambda i, p, j: (i, 0, p)),
            pl.BlockSpec((1, v_rows, nb), lambda i, p, j: (i, p, 0)),
            pl.BlockSpec((1, s, v_rows), lambda i, p, j: (i, 0, p)),
            pl.BlockSpec((1, s, v_rows), lambda i, p, j: (i, 0, pairs + p)),
            pl.BlockSpec((1, nslab, v_rows, qt), lambda i, p, j: (i, 0, v1_blk + p, 0)),
            pl.BlockSpec((1, nslab, v_rows, qt), lambda i, p, j: (i, 0, v2_blk + p, 0)),
        ],
        out_specs=pl.BlockSpec((1, qt, q_rows), lambda i, p, j: (i, j, p)),
        out_shape=jax.ShapeDtypeStruct((b, s, hq), BF16),
        scratch_shapes=[pltpu.VMEM((nsb, gl * NSA_HPG * qt), F32), pltpu.VMEM((4, qt, gl * NSA_HPG * qt), F32)],
        compiler_params=_params("parallel", "parallel", "arbitrary"),
        name="nsa_attend",
    )(tr4, tr4, k_cmp, v_cmp_t, k12, k12, tr4, tr4)


def _nsa(h2, g_norm, w_in, cmp_pos, cmp_w1, cmp_w2, w_out, *, b, s):
    g, hpg, dh = NSA_GROUPS, NSA_HPG, NSA_DH
    hq, gd = g * hpg * dh, g * dh
    kv_w = lambda br, kv: w_in[:, hq + (2 * br + kv) * gd:hq + (2 * br + kv + 1) * gd]
    w_q = w_in[:, :hq] * (dh ** -0.5 * math.log2(math.e))
    w_g = w_in[:, hq + 6 * gd:].reshape(-1, 3, g, hpg).transpose(2, 1, 3, 0)
    w_g = jnp.pad(w_g, ((0, 0), (0, 1), (0, 0), (0, 0))).reshape(g * 4 * hpg, -1)
    w_t = jnp.concatenate([w_q.T, kv_w(1, 1).T, kv_w(2, 1).T, w_g], axis=0).astype(BF16)
    w_nat = jnp.concatenate([kv_w(0, 0), kv_w(0, 1), kv_w(1, 0), kv_w(2, 0)], axis=1).astype(BF16)
    k0, v0, k12, tr = _nsa_proj(h2, g_norm, w_nat, w_t)
    nb = s // NSA_CMP_STRIDE
    k_cmp = _nsa_compress(k0.reshape(b, nb, NSA_CMP_STRIDE * gd), cmp_pos[0], cmp_w1[0], cmp_w2[0], transposed=False)
    v_cmp_t = _nsa_compress(v0.reshape(b, nb, NSA_CMP_STRIDE * gd), cmp_pos[1], cmp_w1[1], cmp_w2[1], transposed=True)
    y = _nsa_attend_pairs(tr, k_cmp, v_cmp_t, k12.reshape(b, s, 2 * gd), b=b, s=s)
    return y.reshape(b * s, hq), w_out.astype(BF16)


def kernel(x, norm_ffn1, ffn1_w_in, ffn1_w_out, norm_mix, norm_ffn2, ffn2_w_in, ffn2_w_out, norm_final,
           ret_w_in, ret_gn_gain, ret_w_out,
           nsa_w_in, nsa_cmp_pos, nsa_cmp_w1, nsa_cmp_w2, nsa_w_out,
           ssd_w_in, ssd_conv_w, ssd_conv_b, ssd_dt_bias, ssd_a_log, ssd_d, ssd_norm, ssd_w_out,
           dil_w_in, dil_w_out):
    b, s, d = x.shape
    depth = norm_mix.shape[0]
    h = x.reshape(b * s, d)
    w1_in, w1_out = ffn1_w_in.astype(BF16), ffn1_w_out.astype(BF16)
    w2_in, w2_out = ffn2_w_in.astype(BF16), ffn2_w_out.astype(BF16)
    for i in range(depth):
        h = _ffn(h, norm_ffn1[i], w1_in, w1_out, i)
        m, j = i % 4, i // 4
        mix = None
        if m == 0:
            mix = _retention(h, norm_mix[i], ret_w_in[j], ret_gn_gain[j], ret_w_out[j], b=b, s=s)
        elif m == 1:
            mix = _nsa(h, norm_mix[i], nsa_w_in[j], nsa_cmp_pos[j], nsa_cmp_w1[j], nsa_cmp_w2[j], nsa_w_out[j], b=b, s=s)
        elif m == 2:
            mix = _ssd(h, norm_mix[i], ssd_w_in[j], ssd_conv_w[j], ssd_conv_b[j], ssd_dt_bias[j], ssd_a_log[j],
                       ssd_d[j], ssd_norm[j], ssd_w_out[j], b=b, s=s)
        else:
            h = _dilated(h, norm_mix[i], dil_w_in[j], dil_w_out[j], b=b, s=s)
        h = _ffn(h, norm_ffn2[i], w2_in, w2_out, i,
                 norm_final if i == depth - 1 else None, mix)
    return h.reshape(b, s, d)
```
